```python
import math
import jax, jax.numpy as jnp
from jax import lax
import numpy as np

D_MODEL = 4096
BATCH = 2
SEQ = 4096
DEPTH = 4

GRID_W = 64
CTX_LEN = 256

A_HEAD_DIM = 128
A_WIDTH = D_MODEL // 2
A_HEADS = A_WIDTH // A_HEAD_DIM
CHUNK = 64

HY_WIDTH = D_MODEL // 2
HY_EMB = 33
HY_BANDS = (HY_EMB - 1) // 2
HY_FILTER_HIDDEN = 64
HY_N_FILT = 4
HY_FILTER_SCALE = 0.05
HY_DECAY_MIN = math.log(1e2) / 1.5
HY_DECAY_MAX = math.log(1e2) / 0.3

ADA_RANK = 256

N_GROUPS = 4
EXPERTS_PER_GROUP = 8
N_EXPERTS = N_GROUPS * EXPERTS_PER_GROUP
TOP_K = 2
D_EXPERT = 192

NORM_EPS = 1e-6

Z_OFF = 3 * A_WIDTH
DG_OFF = 4 * A_WIDTH
HY_OFF = DG_OFF + 4 * A_HEADS
GA_OFF = HY_OFF + 3 * HY_WIDTH
GB_OFF = GA_OFF + D_MODEL
D_IN = GB_OFF + D_MODEL

kernel_name = "hybrid_deltanet_hyena_hmoe_dit"

F32 = jnp.float32


def rmsnorm(x, g):
    xf = x.astype(F32)
    y = xf * lax.rsqrt(jnp.mean(xf * xf, axis=-1, keepdims=True) + NORM_EPS)
    return (y * g.astype(F32)).astype(x.dtype)


def modulate(n, shift, scale):
    return n * (1.0 + scale) + shift


def adaln(cond, w_down, w_up, b):
    return (jax.nn.silu(cond) @ w_down) @ w_up + b


def l2norm(t):
    return t * lax.rsqrt(jnp.sum(t * t, axis=-1, keepdims=True) + NORM_EPS)


def grid_dwconv3x3(x, w, width):
    b, length, ch = x.shape
    rows = length // width
    img = jnp.pad(x.reshape(b, rows, width, ch), ((0, 0), (1, 1), (1, 1), (0, 0)))
    y = sum(img[:, i:i + rows, j:j + width] * w[i, j] for i in range(3) for j in range(3))
    return y.reshape(b, length, ch)


def conv1d_centred3(x, w):
    xp = jnp.pad(x, ((0, 0), (1, 1), (0, 0)))
    return xp[:, :-2] * w[0] + xp[:, 1:-1] * w[1] + xp[:, 2:] * w[2]


def chunked_gated_delta(q, k, v, g, beta, s0):
    b, h, length, _ = q.shape
    dv = v.shape[-1]
    n = length // CHUNK
    ch = lambda t: t.reshape(b, h, n, CHUNK, *t.shape[3:])
    q, k, v, g, beta = (ch(t) for t in (q, k, v, g, beta))
    gc = jnp.cumsum(g, axis=-1)
    incl = jnp.tril(jnp.ones((CHUNK, CHUNK), bool))
    strict = jnp.tril(jnp.ones((CHUNK, CHUNK), bool), -1)
    diff = gc[..., :, None] - gc[..., None, :]
    decay = jnp.where(incl, jnp.exp(jnp.where(incl, diff, 0.0)), 0.0)
    k_beta = k * beta[..., None]
    eye = jnp.eye(CHUNK, dtype=F32)
    lmat = jnp.where(strict, jnp.einsum('bhncd,bhnmd->bhncm', k_beta, k) * decay, 0.0)
    tinv = lax.linalg.triangular_solve(eye + lmat, jnp.broadcast_to(eye, lmat.shape),
                                       left_side=True, lower=True, unit_diagonal=True)
    u = tinv @ (v * beta[..., None])
    w = tinv @ (k_beta * jnp.exp(gc)[..., None])
    attn = jnp.einsum('bhncd,bhnmd->bhncm', q, k) * decay
    q_dec = q * jnp.exp(gc)[..., None]
    k_dec = k * jnp.exp(gc[..., -1:] - gc)[..., None]
    g_last = jnp.exp(gc[..., -1])

    def step(s, inp):
        q_i, k_i, w_i, u_i, a_i, gl_i = inp
        v_new = u_i - w_i @ s
        o_i = q_i @ s + a_i @ v_new
        s = s * gl_i[..., None, None] + jnp.swapaxes(k_i, -1, -2) @ v_new
        return s, o_i

    xs = tuple(jnp.moveaxis(t, 2, 0) for t in (q_dec, k_dec, w, u, attn, g_last))
    s_final, o = lax.scan(step, s0, xs)
    return jnp.moveaxis(o, 0, 2).reshape(b, h, length, dv), s_final


def delta_mixer(p, width, conv_w, a_log, dt_bias, s0_f, s0_b):
    b, length, _ = p.shape
    qkv = jax.nn.silu(grid_dwconv3x3(p[..., :3 * A_WIDTH], conv_w, width)).astype(F32)
    heads = lambda t: t.reshape(b, length, A_HEADS, A_HEAD_DIM).transpose(0, 2, 1, 3)
    q, k, v = (heads(t) for t in jnp.split(qkv, 3, axis=-1))
    q = l2norm(q) * A_HEAD_DIM ** -0.5
    k = l2norm(k)
    dg = p[..., DG_OFF:DG_OFF + 4 * A_HEADS].astype(F32).reshape(b, length, 4, A_HEADS).transpose(2, 0, 3, 1)
    g = -jnp.exp(a_log.astype(F32))[:, None, :, None] * jax.nn.softplus(
        dg[:2] + dt_bias.astype(F32)[:, None, :, None])
    beta = jax.nn.sigmoid(dg[2:])
    o_f, s_f = chunked_gated_delta(q, k, v, g[0], beta[0], s0_f)
    rev = lambda t: jnp.flip(t, axis=2)
    o_b, s_b = chunked_gated_delta(rev(q), rev(k), rev(v), rev(g[1]), rev(beta[1]), s0_b)
    o = (o_f + rev(o_b)).transpose(0, 2, 1, 3)
    return o, s_f, s_b


def hyena_filters(length, w1, b1, f1, w2, b2, f2, w3, deltas):
    t = jnp.linspace(0.0, 1.0, length, dtype=F32)
    omega = 2.0 * math.pi * jnp.arange(length, dtype=F32) / length
    bands = jnp.linspace(1e-4, HY_BANDS - 1, HY_BANDS, dtype=F32)
    ang = omega[:, None] * bands[None, :]
    feats = jnp.concatenate([t[:, None], jnp.cos(ang), -jnp.sin(ang)], axis=-1)
    hid = jnp.sin(f1.astype(F32) * (feats @ w1.astype(F32) + b1.astype(F32)))
    hid = jnp.sin(f2.astype(F32) * (hid @ w2.astype(F32) + b2.astype(F32)))
    filt = (hid @ w3.astype(F32)).reshape(length, HY_N_FILT, HY_WIDTH)
    window = jnp.exp(-t[:, None, None] * jnp.abs(deltas.astype(F32))[None])
    return filt * window


def long_conv_bidir(z, h_fwd, h_bwd, skip):
    length = z.shape[1]
    kern = jnp.concatenate([h_fwd, h_bwd[::-1]], axis=0)
    y = jnp.fft.irfft(jnp.fft.rfft(z, n=2 * length, axis=1) * jnp.fft.rfft(kern, axis=0)[None],
                      n=2 * length, axis=1)[:, :length]
    return y + z * skip.astype(F32)


def hyena_branch(p_hy, conv_w, filt, skip):
    u = conv1d_centred3(p_hy, conv_w).astype(F32)
    v, x1, x2 = jnp.split(u, 3, axis=-1)
    z = x1 * long_conv_bidir(v, filt[:, 0], filt[:, 1], skip[0])
    y = x2 * long_conv_bidir(z, filt[:, 2], filt[:, 3], skip[1])
    return y.astype(p_hy.dtype)


def merge_branches(p, o_delta, filt, o_gain, hy_conv, hy_skip, w_br_a, w_br_b, w_out):
    b, length, _ = p.shape
    z = p[..., Z_OFF:Z_OFF + A_WIDTH].astype(F32).reshape(b, length, A_HEADS, A_HEAD_DIM)
    o = o_delta * lax.rsqrt(jnp.mean(o_delta * o_delta, axis=-1, keepdims=True) + NORM_EPS)
    o = o * o_gain.astype(F32) * jax.nn.silu(z)
    branch_a = o.reshape(b, length, A_WIDTH).astype(p.dtype) @ w_br_a
    branch_b = hyena_branch(p[..., HY_OFF:HY_OFF + 3 * HY_WIDTH], hy_conv, filt, hy_skip) @ w_br_b
    gate_a = jax.nn.sigmoid(p[..., GA_OFF:GA_OFF + D_MODEL])
    gate_b = jax.nn.sigmoid(p[..., GB_OFF:GB_OFF + D_MODEL])
    return (gate_a * branch_a + gate_b * branch_b) @ w_out


def hier_moe(u, w_rg, b_rg, w_re, b_re, w_gate, w_up, w_down):
    uf = u.astype(F32)
    pg = jax.nn.softmax(uf @ w_rg.astype(F32) + b_rg.astype(F32), axis=-1)
    p_top, g_idx = lax.top_k(pg, 1)
    le = (uf @ w_re.astype(F32) + b_re.astype(F32)).reshape(*u.shape[:2], N_GROUPS, EXPERTS_PER_GROUP)
    le_g = jnp.einsum('blge,blg->ble', le, jax.nn.one_hot(g_idx[..., 0], N_GROUPS, dtype=F32))
    v_top, i_top = lax.top_k(le_g, TOP_K)
    w_top = jax.nn.softmax(v_top, axis=-1) * p_top
    e_idx = g_idx * EXPERTS_PER_GROUP + i_top
    combine = jnp.einsum('blk,blke->ble', w_top, jax.nn.one_hot(e_idx, N_EXPERTS, dtype=F32)).astype(u.dtype)
    hid = jax.nn.silu(jnp.einsum('bld,edf->blef', u, w_gate)) * jnp.einsum('bld,edf->blef', u, w_up)
    return jnp.einsum('blef,efd->bld', hid * combine[..., None], w_down)


def setup_inputs(seed: int = 0) -> dict:
    key = jax.random.key(seed)
    ks = iter(jax.random.split(key, 48))
    nrm = lambda shape, scale: jax.random.normal(next(ks), shape, F32) * scale
    L, D, FH = DEPTH, D_MODEL, HY_FILTER_HIDDEN
    dt = jnp.exp(jax.random.uniform(next(ks), (L, 2, A_HEADS), F32, math.log(1e-3), math.log(1e-1)))
    return {
        "x": nrm((BATCH, SEQ, D), 1.0),
        "c": nrm((BATCH, D), 1.0),
        "ctx": nrm((BATCH, CTX_LEN, D), 1.0),
        "c_ctx": nrm((D,), 1.0),
        "ada_down": nrm((L, D, ADA_RANK), D ** -0.5),
        "ada_up": nrm((L, ADA_RANK, 6 * D), 0.5 * ADA_RANK ** -0.5),
        "ada_bias": nrm((L, 6 * D), 0.02),
        "norm_mix": 1.0 + nrm((L, D), 0.02),
        "norm_ffn": 1.0 + nrm((L, D), 0.02),
        "w_in": nrm((L, D, D_IN), D ** -0.5),
        "qkv_conv": nrm((L, 3, 3, 3 * A_WIDTH), 1.0 / 3.0),
        "a_log": jnp.log(jax.random.uniform(next(ks), (L, 2, A_HEADS), F32, 1.0, 16.0)),
        "dt_bias": dt + jnp.log(-jnp.expm1(-dt)),
        "o_norm": 1.0 + nrm((L, A_HEAD_DIM), 0.02),
        "hy_conv": nrm((L, 3, 3 * HY_WIDTH), 3 ** -0.5),
        "hy_w1": nrm((L, HY_EMB, FH), HY_EMB ** -0.5),
        "hy_b1": nrm((L, FH), 0.02),
        "hy_f1": 1.0 + nrm((L, FH), 0.1),
        "hy_w2": nrm((L, FH, FH), FH ** -0.5),
        "hy_b2": nrm((L, FH), 0.02),
        "hy_f2": 1.0 + nrm((L, FH), 0.1),
        "hy_w3": nrm((L, FH, HY_N_FILT * HY_WIDTH), HY_FILTER_SCALE * FH ** -0.5),
        "hy_decay": jax.random.uniform(next(ks), (L, HY_N_FILT, HY_WIDTH), F32, HY_DECAY_MIN, HY_DECAY_MAX),
        "hy_skip": nrm((L, 2, HY_WIDTH), 0.1),
        "w_br_a": nrm((L, A_WIDTH, D), A_WIDTH ** -0.5),
        "w_br_b": nrm((L, HY_WIDTH, D), HY_WIDTH ** -0.5),
        "w_out": nrm((L, D, D), D ** -0.5),
        "router_group": nrm((L, D, N_GROUPS), D ** -0.5),
        "router_group_bias": nrm((L, N_GROUPS), 0.01),
        "router_expert": nrm((L, D, N_EXPERTS), D ** -0.5),
        "router_expert_bias": nrm((L, N_EXPERTS), 0.01),
        "exp_gate": nrm((L, N_EXPERTS, D, D_EXPERT), D ** -0.5),
        "exp_up": nrm((L, N_EXPERTS, D, D_EXPERT), D ** -0.5),
        "exp_down": nrm((L, N_EXPERTS, D_EXPERT, D), D_EXPERT ** -0.5),
        "final_norm": 1.0 + nrm((D,), 0.02),
    }


def reference(x, c, ctx, c_ctx, ada_down, ada_up, ada_bias, norm_mix, norm_ffn, w_in,
              qkv_conv, a_log, dt_bias, o_norm, hy_conv, hy_w1, hy_b1, hy_f1, hy_w2, hy_b2, hy_f2,
              hy_w3, hy_decay, hy_skip, w_br_a, w_br_b, w_out, router_group, router_group_bias,
              router_expert, router_expert_bias, exp_gate, exp_up, exp_down, final_norm):
    h_x, h_c = x, ctx
    batch, len_x, len_c = x.shape[0], x.shape[1], ctx.shape[1]
    for l in range(DEPTH):
        last = l == DEPTH - 1
        mod_x = jnp.split(adaln(c, ada_down[l], ada_up[l], ada_bias[l])[:, None, :], 6, axis=-1)
        mod_c = jnp.split(adaln(c_ctx, ada_down[l], ada_up[l], ada_bias[l])[None, None, :], 6, axis=-1)
        p_c = modulate(rmsnorm(h_c, norm_mix[l]), mod_c[0], mod_c[1]) @ w_in[l]
        p_x = modulate(rmsnorm(h_x, norm_mix[l]), mod_x[0], mod_x[1]) @ w_in[l]
        s0 = jnp.zeros((batch, A_HEADS, A_HEAD_DIM, A_HEAD_DIM), F32)
        o_c, s_f, s_b = delta_mixer(p_c, len_c, qkv_conv[l], a_log[l], dt_bias[l], s0, s0)
        o_x, _, _ = delta_mixer(p_x, GRID_W, qkv_conv[l], a_log[l], dt_bias[l], s_f, s_b)
        filt_args = (hy_w1[l], hy_b1[l], hy_f1[l], hy_w2[l], hy_b2[l], hy_f2[l], hy_w3[l], hy_decay[l])
        mix_args = (o_norm[l], hy_conv[l], hy_skip[l], w_br_a[l], w_br_b[l], w_out[l])
        moe_args = (router_group[l], router_group_bias[l], router_expert[l], router_expert_bias[l],
                    exp_gate[l], exp_up[l], exp_down[l])
        h_x = h_x + mod_x[2] * merge_branches(p_x, o_x, hyena_filters(len_x, *filt_args), *mix_args)
        h_x = h_x + mod_x[5] * hier_moe(modulate(rmsnorm(h_x, norm_ffn[l]), mod_x[3], mod_x[4]), *moe_args)
        if not last:
            h_c = h_c + mod_c[2] * merge_branches(p_c, o_c, hyena_filters(len_c, *filt_args), *mix_args)
            h_c = h_c + mod_c[5] * hier_moe(modulate(rmsnorm(h_c, norm_ffn[l]), mod_c[3], mod_c[4]), *moe_args)
    return rmsnorm(h_x, final_norm)
```

```python
import functools
import math

import jax
import jax.numpy as jnp
from jax import lax
from jax.experimental import pallas as pl
from jax.experimental.pallas import tpu as pltpu

F32 = jnp.float32
BF = jnp.bfloat16

GRID_W = 64
CHUNK = 64
N_GROUPS = 4
EXPERTS_PER_GROUP = 8
HY_N_FILT = 4
NORM_EPS = 1e-6
TOKEN_TILE = 256
LANES = 128
CONV_PAD = 72
VMEM_LIMIT = 56 * 1024 * 1024


def _cparams(n_grid):
    return pltpu.CompilerParams(dimension_semantics=("arbitrary",) * n_grid,
                                vmem_limit_bytes=VMEM_LIMIT)


def _split_bf16(a):
    hi = a.astype(BF)
    lo = (a - hi.astype(F32)).astype(BF)
    return hi, lo


def _dot3(a, b):
    ah, al = _split_bf16(a)
    bh, bl = _split_bf16(b)
    d = functools.partial(jnp.dot, preferred_element_type=F32)
    return d(ah, bh) + (d(ah, bl) + d(al, bh))


def _dotx(a, b):
    return jnp.dot(a, b, precision=lax.Precision.HIGHEST, preferred_element_type=F32)


def _bdot(a, b):
    return jnp.dot(a.astype(BF), b.astype(BF), preferred_element_type=F32)


def _mm_kernel(*refs, n_extra, n_out, cast_w, prologue, epilogue):
    x_ref, w_ref = refs[0], refs[1]
    extras = refs[2:2 + n_extra]
    outs = refs[2 + n_extra:2 + n_extra + n_out]
    scratch = refs[2 + n_extra + n_out:]
    if cast_w:
        wbf = scratch[0]

        @pl.when(pl.program_id(2) == 0)
        def _():
            k = w_ref.shape[0]
            step = 512 if k % 512 == 0 else k
            if k == step:
                wbf[...] = w_ref[...].astype(BF)
            else:
                def it(r, c):
                    r0 = pl.multiple_of(r * step, step)
                    wbf[pl.ds(r0, step), :] = w_ref[pl.ds(r0, step), :].astype(BF)
                    return c
                lax.fori_loop(0, k // step, it, 0)
        w = wbf[...]
    else:
        w = w_ref[...]
    x = x_ref[...]
    if prologue is not None:
        x = prologue(x)
    if x.dtype != BF:
        x = x.astype(BF)
    acc = jnp.dot(x, w, preferred_element_type=F32)
    epilogue(acc, extras, outs)


def _matmul(x, w, x_spec, w_spec, grid, outs, epilogue, extras=(), prologue=None, name=None):
    cast_w = w.dtype != BF
    kw, tn = [d for d in w_spec.block_shape if d is not None]
    scratch = [pltpu.VMEM((kw, tn), BF)] if cast_w else []
    body = functools.partial(_mm_kernel, n_extra=len(extras), n_out=len(outs), cast_w=cast_w,
                             prologue=prologue, epilogue=epilogue)
    res = pl.pallas_call(
        body,
        grid=grid,
        in_specs=[x_spec, w_spec] + [s for _, s in extras],
        out_specs=[s for _, s in outs],
        out_shape=[o for o, _ in outs],
        scratch_shapes=scratch,
        compiler_params=_cparams(3),
        name=name,
    )(x, w, *[a for a, _ in extras])
    return res


def _ep_store(dtype):
    def ep(acc, extras, outs):
        outs[0][...] = acc.astype(dtype)
    return ep


def _pick(n, prefs):
    for p in prefs:
        if n % p == 0:
            return p
    return n


def _adaln_all(conds, ada_down, ada_up, ada_bias):
    nl, d, r = ada_down.shape
    n6 = ada_up.shape[2]
    (t,) = _matmul(
        conds, ada_down,
        pl.BlockSpec((8, d), lambda g, j, i: (0, 0)),
        pl.BlockSpec((None, d, r), lambda g, j, i: (g, 0, 0)),
        (nl, 1, 1),
        [(jax.ShapeDtypeStruct((nl, 8, r), F32), pl.BlockSpec((None, 8, r), lambda g, j, i: (g, 0, 0)))],
        _ep_store(F32), prologue=lambda v: v * jax.nn.sigmoid(v), name="adaln_down")
    tn = _pick(n6, (2048, 1024, 512, 256, 128))

    def ep(acc, extras, outs):
        outs[0][...] = acc + extras[0][...]

    (mod,) = _matmul(
        t, ada_up,
        pl.BlockSpec((None, 8, r), lambda g, j, i: (g, 0, 0)),
        pl.BlockSpec((None, r, tn), lambda g, j, i: (g, 0, j)),
        (nl, n6 // tn, 1),
        [(jax.ShapeDtypeStruct((nl, 8, n6), F32), pl.BlockSpec((None, 8, tn), lambda g, j, i: (g, 0, j)))],
        ep, extras=[(ada_bias.reshape(nl, 1, n6), pl.BlockSpec((None, 1, tn), lambda g, j, i: (g, 0, j)))],
        name="adaln_up")
    return mod


def _mod_row_map(n_t, n_c):
    def m(i):
        return (jnp.where(i % n_t < n_c, 0, 1 + i // n_t), 0, 0)
    return m


def _norm_mod_kernel(h_ref, g_ref, mod_ref, u_ref, *, off):
    xf = h_ref[...]
    y = xf * lax.rsqrt(jnp.mean(xf * xf, axis=-1, keepdims=True) + NORM_EPS) * g_ref[...]
    sh = mod_ref[off:off + 1, :]
    sc = mod_ref[off + 1:off + 2, :]
    u_ref[...] = (y * (1.0 + sc) + sh).astype(u_ref.dtype)


def _norm_mod(h, gain, mod_l, off, n_t, n_c):
    m, d = h.shape
    tt = TOKEN_TILE
    return pl.pallas_call(
        functools.partial(_norm_mod_kernel, off=off),
        grid=(m // tt,),
        in_specs=[pl.BlockSpec((tt, d), lambda i: (i, 0)),
                  pl.BlockSpec((1, d), lambda i: (0, 0)),
                  pl.BlockSpec((None, 8, d), _mod_row_map(n_t, n_c))],
        out_specs=pl.BlockSpec((tt, d), lambda i: (i, 0)),
        out_shape=jax.ShapeDtypeStruct((m, d), BF),
        compiler_params=_cparams(1),
        name="norm_mod",
    )(h, gain.reshape(1, d), mod_l)


def _router_kernel(h_ref, g_ref, mod_ref, wr_ref, br_ref, u_ref, comb_ref, *, off, n_exp, n_grp):
    xf = h_ref[...]
    y = xf * lax.rsqrt(jnp.mean(xf * xf, axis=-1, keepdims=True) + NORM_EPS) * g_ref[...]
    sh = mod_ref[off:off + 1, :]
    sc = mod_ref[off + 1:off + 2, :]
    u = y * (1.0 + sc) + sh
    u_ref[...] = u.astype(u_ref.dtype)
    logits = _dot3(u, wr_ref[...]) + br_ref[...]
    lane = lax.broadcasted_iota(jnp.int32, logits.shape, 1)
    neg = jnp.float32(-jnp.inf)
    big = jnp.int32(1 << 20)
    is_g = (lane >= n_exp) & (lane < n_exp + n_grp)
    lg = jnp.where(is_g, logits, neg)
    eg = jnp.exp(lg - jnp.max(lg, axis=1, keepdims=True))
    pg = eg / jnp.sum(eg, axis=1, keepdims=True)
    p_top = jnp.max(pg, axis=1, keepdims=True)
    g_idx = jnp.min(jnp.where(is_g & (pg == p_top), lane, big), axis=1, keepdims=True) - n_exp
    per = n_exp // n_grp
    in_grp = (lane >= g_idx * per) & (lane < (g_idx + 1) * per)
    le = jnp.where(in_grp, logits, neg)
    v1 = jnp.max(le, axis=1, keepdims=True)
    i1 = jnp.min(jnp.where(le == v1, lane, big), axis=1, keepdims=True)
    le2 = jnp.where(lane == i1, neg, le)
    v2 = jnp.max(le2, axis=1, keepdims=True)
    i2 = jnp.min(jnp.where(le2 == v2, lane, big), axis=1, keepdims=True)
    e2 = jnp.exp(v2 - v1)
    w1 = p_top / (1.0 + e2)
    w2 = p_top * e2 / (1.0 + e2)
    comb_ref[...] = jnp.where(lane == i1, w1, 0.0) + jnp.where(lane == i2, w2, 0.0)


def _norm_mod_router(h, gain, mod_l, off, n_t, n_c, w_router, b_router, n_exp, n_grp):
    m, d = h.shape
    tt = TOKEN_TILE
    return pl.pallas_call(
        functools.partial(_router_kernel, off=off, n_exp=n_exp, n_grp=n_grp),
        grid=(m // tt,),
        in_specs=[pl.BlockSpec((tt, d), lambda i: (i, 0)),
                  pl.BlockSpec((1, d), lambda i: (0, 0)),
                  pl.BlockSpec((None, 8, d), _mod_row_map(n_t, n_c)),
                  pl.BlockSpec((d, LANES), lambda i: (0, 0)),
                  pl.BlockSpec((1, LANES), lambda i: (0, 0))],
        out_specs=[pl.BlockSpec((tt, d), lambda i: (i, 0)),
                   pl.BlockSpec((tt, LANES), lambda i: (i, 0))],
        out_shape=[jax.ShapeDtypeStruct((m, d), BF), jax.ShapeDtypeStruct((m, LANES), F32)],
        compiler_params=_cparams(1),
        name="norm_mod_router",
    )(h, gain.reshape(1, d), mod_l, w_router, b_router)


def _gates_kernel(w_ref, u_ref, aexp_ref, dtb_ref, o_ref, *, n_decay):
    acc = lax.dot_general(w_ref[...].astype(BF), u_ref[...], (((1,), (1,)), ((), ())),
                          preferred_element_type=F32)
    row = lax.broadcasted_iota(jnp.int32, acc.shape, 0)
    z = acc + dtb_ref[...]
    softplus = jnp.maximum(z, 0.0) + jnp.log(1.0 + jnp.exp(-jnp.abs(z)))
    g = -aexp_ref[...] * softplus
    o_ref[...] = jnp.where(row < n_decay, g, jax.nn.sigmoid(acc))


def _gates(u, w_dg_t, a_exp, dt_b):
    m, d = u.shape
    r = w_dg_t.shape[0]
    tm = _pick(m, (512, 256))
    return pl.pallas_call(
        functools.partial(_gates_kernel, n_decay=r // 2),
        grid=(m // tm,),
        in_specs=[pl.BlockSpec((r, d), lambda i: (0, 0)),
                  pl.BlockSpec((tm, d), lambda i: (i, 0)),
                  pl.BlockSpec((r, 1), lambda i: (0, 0)),
                  pl.BlockSpec((r, 1), lambda i: (0, 0))],
        out_specs=pl.BlockSpec((r, tm), lambda i: (0, i)),
        out_shape=jax.ShapeDtypeStruct((r, m), F32),
        compiler_params=_cparams(1),
        name="delta_gates",
    )(w_dg_t, u, a_exp, dt_b)


def _conv_silu_tile(pad_ref, cw_ref, t0, n, seq_start, seq_len, width):
    one_row = width == seq_len
    r = lax.broadcasted_iota(jnp.int32, (n, 1), 0) + (t0 - seq_start)
    col = lax.rem(r, width)
    acc = jnp.zeros((n, LANES), F32)
    for dj in (-1, 0, 1):
        part = jnp.zeros((n, LANES), F32)
        for di in (-1, 0, 1):
            if one_row and di != 0:
                continue
            off = di * width + dj
            tap = pad_ref[pl.ds(CONV_PAD + t0 + off, n), :]
            wrow = cw_ref[(di + 1) * 3 + (dj + 1):(di + 1) * 3 + (dj + 1) + 1, :]
            ok = (r + off >= 0) & (r + off < seq_len)
            part = part + jnp.where(ok, tap, 0.0) * wrow
        ok_c = (col + dj >= 0) & (col + dj < width)
        acc = acc + jnp.where(ok_c, part, 0.0)
    return acc * jax.nn.sigmoid(acc)


def _neumann_inverse(lmat, eye):
    x = -lmat
    p = eye + x
    n = lmat.shape[0]
    k = 1
    while 2 * k < n:
        x = _dot3(x, x)
        p = p + _dot3(p, x)
        k *= 2
    return p


def _delta_prep_kernel(pq_ref, pk_ref, pv_ref, cq_ref, ck_ref, cv_ref, g_ref, gt_ref,
                       u_ref, w_ref, qd_ref, kd_ref, a_ref, gl_ref,
                       pad_ref, q_s, k_s, v_s, gcr_s, gct_s,
                       *, t_len, ctx_len, n_chunks, group):
    hd = q_s.shape[1]
    tt = TOKEN_TILE
    zpad = jnp.zeros((CONV_PAD, LANES), F32)
    pad_ref[pl.ds(0, CONV_PAD), :] = zpad
    pad_ref[pl.ds(CONV_PAD + t_len, CONV_PAD), :] = zpad
    for src, cw, dst, kind in ((pq_ref, cq_ref, q_s, "q"), (pk_ref, ck_ref, k_s, "k"), (pv_ref, cv_ref, v_s, "v")):
        pad_ref[pl.ds(CONV_PAD, t_len), :] = src[...].astype(F32)
        for ti in range(t_len // tt):
            t0 = ti * tt
            if t0 < ctx_len:
                a = _conv_silu_tile(pad_ref, cw, t0, tt, 0, ctx_len, ctx_len)
            else:
                a = _conv_silu_tile(pad_ref, cw, t0, tt, ctx_len, t_len - ctx_len, GRID_W)
            if kind != "v":
                a = a * lax.rsqrt(jnp.sum(a * a, axis=-1, keepdims=True) + NORM_EPS)
            if kind == "q":
                a = a * (hd ** -0.5)
            dst[pl.ds(t0, tt), :] = a

    ii = lax.broadcasted_iota(jnp.int32, (CHUNK, CHUNK), 0)
    jj = lax.broadcasted_iota(jnp.int32, (CHUNK, CHUNK), 1)
    eye = (ii == jj).astype(F32)
    le = (ii <= jj).astype(F32)
    ge = (ii >= jj).astype(F32)
    gcr_s[0] = _dotx(g_ref[0], le)
    gcr_s[1] = _dotx(g_ref[1], ge)
    gct_s[0] = _dotx(ge, gt_ref[0])
    gct_s[1] = _dotx(le, gt_ref[1])

    lane_c = lax.broadcasted_iota(jnp.int32, (CHUNK, LANES), 1)

    def chunk_dir(c, d):
        t0 = pl.multiple_of(c * CHUNK, CHUNK)
        q = q_s[pl.ds(t0, CHUNK), :]
        k = k_s[pl.ds(t0, CHUNK), :]
        v = v_s[pl.ds(t0, CHUNK), :]
        sel = lane_c == c
        gcol = jnp.sum(jnp.where(sel, gct_s[d], 0.0), axis=1, keepdims=True)
        bcol = jnp.sum(jnp.where(sel, gt_ref[2 + d], 0.0), axis=1, keepdims=True)
        grow = gcr_s[d, pl.ds(c, 1), :]
        if d == 0:
            incl, strict = ii >= jj, ii > jj
            glast = grow[:, CHUNK - 1:CHUNK]
        else:
            incl, strict = ii <= jj, ii < jj
            glast = grow[:, 0:1]
        decay = jnp.where(incl, jnp.exp(jnp.where(incl, gcol - grow, 0.0)), 0.0)
        kb = k * bcol
        kbf = k.astype(BF)
        kk = lax.dot_general(kb.astype(BF), kbf, (((1,), (1,)), ((), ())), preferred_element_type=F32)
        lmat = jnp.where(strict, kk * decay, 0.0)
        tinv = _neumann_inverse(lmat, eye)
        egc = jnp.exp(gcol)
        rhs = jnp.concatenate([v * bcol, kb * egc], axis=1)
        uw = _bdot(tinv, rhs)
        attn = lax.dot_general(q.astype(BF), kbf, (((1,), (1,)), ((), ())), preferred_element_type=F32) * decay
        u_ref[d, pl.ds(t0, CHUNK), :] = uw[:, :hd]
        w_ref[d, pl.ds(t0, CHUNK), :] = uw[:, hd:].astype(BF)
        qd_ref[d, pl.ds(t0, CHUNK), :] = (q * egc).astype(BF)
        kd_ref[d, pl.ds(t0, CHUNK), :] = (k * jnp.exp(glast - gcol)).astype(BF)
        a_ref[d, pl.ds(t0, CHUNK), :] = attn.astype(BF)
        gl_ref[d, pl.ds(c, 1), :] = jnp.broadcast_to(jnp.exp(glast), (1, LANES))

    def body(it, carry):
        for g in range(group):
            for d in range(2):
                chunk_dir(it * group + g, d)
        return carry

    lax.fori_loop(0, n_chunks // group, body, 0)


def _delta_prep(p1, conv_w9, g_rows, g_cols, n_heads, ctx_len):
    b, t, _ = p1.shape
    hd = LANES
    nc = t // CHUNK
    ncp = g_rows.shape[3]
    group = _pick(nc, (4, 2, 1))
    h = n_heads
    kern = functools.partial(_delta_prep_kernel, t_len=t, ctx_len=ctx_len, n_chunks=nc, group=group)
    tok = lambda off: pl.BlockSpec((None, t, hd), lambda bi, hi: (bi, 0, off + hi))
    cws = lambda off: pl.BlockSpec((9, hd), lambda bi, hi: (0, off + hi))
    o5 = lambda last: pl.BlockSpec((2, None, None, t, last), lambda bi, hi: (0, bi, hi, 0, 0))
    return pl.pallas_call(
        kern,
        grid=(b, h),
        in_specs=[tok(0), tok(h), tok(2 * h), cws(0), cws(h), cws(2 * h),
                  pl.BlockSpec((None, None, 4, ncp, CHUNK), lambda bi, hi: (bi, hi, 0, 0, 0)),
                  pl.BlockSpec((None, None, 4, CHUNK, LANES), lambda bi, hi: (bi, hi, 0, 0, 0))],
        out_specs=[o5(hd), o5(hd), o5(hd), o5(hd), o5(CHUNK),
                   pl.BlockSpec((2, None, None, ncp, LANES), lambda bi, hi: (0, bi, hi, 0, 0))],
        out_shape=[jax.ShapeDtypeStruct((2, b, h, t, hd), F32),
                   jax.ShapeDtypeStruct((2, b, h, t, hd), BF),
                   jax.ShapeDtypeStruct((2, b, h, t, hd), BF),
                   jax.ShapeDtypeStruct((2, b, h, t, hd), BF),
                   jax.ShapeDtypeStruct((2, b, h, t, CHUNK), BF),
                   jax.ShapeDtypeStruct((2, b, h, ncp, LANES), F32)],
        scratch_shapes=[pltpu.VMEM((t + 2 * CONV_PAD, LANES), F32),
                        pltpu.VMEM((t, hd), F32), pltpu.VMEM((t, hd), F32), pltpu.VMEM((t, hd), F32),
                        pltpu.VMEM((2, ncp, CHUNK), F32), pltpu.VMEM((2, CHUNK, LANES), F32)],
        compiler_params=_cparams(2),
        name="delta_prep",
    )(p1, p1, p1, conv_w9, conv_w9, conv_w9, g_rows, g_cols)


def _delta_scan_kernel(uf, wf, qf, kf, af, gf, ub, wb, qb, kb, ab, gb, of_ref, ob_ref, s_ref, *, hg):
    @pl.when(pl.program_id(2) == 0)
    def _():
        s_ref[...] = jnp.zeros_like(s_ref)

    for d, (u_r, w_r, q_r, k_r, a_r, g_r, o_r) in enumerate(
            ((uf, wf, qf, kf, af, gf, of_ref), (ub, wb, qb, kb, ab, gb, ob_ref))):
        for h in range(hg):
            s = s_ref[d, h]
            sb = s.astype(BF)
            vn = u_r[h] - jnp.dot(w_r[h], sb, preferred_element_type=F32)
            vnb = vn.astype(BF)
            o_r[h] = (jnp.dot(q_r[h], sb, preferred_element_type=F32)
                      + jnp.dot(a_r[h], vnb, preferred_element_type=F32))
            s_ref[d, h] = s * g_r[h] + lax.dot_general(k_r[h], vnb, (((0,), (0,)), ((), ())),
                                                       preferred_element_type=F32)


def _delta_scan(u, w, qd, kd, a, gl, ctx_len):
    _, b, h, t, hd = u.shape
    nc = t // CHUNK
    ncc = ctx_len // CHUNK
    hg = _pick(h, (8, 4, 2, 1))
    r6 = lambda arr: arr.reshape(2, b, h, nc, CHUNK, arr.shape[-1])
    u6, w6, q6, k6, a6 = r6(u), r6(w), r6(qd), r6(kd), r6(a)
    gl6 = gl[:, :, :, :nc].reshape(2, b, h, nc, 1, LANES)

    def cf(ci):
        return ci

    def cb(ci):
        return jnp.where(ci < ncc, ncc - 1 - ci, nc - 1 - (ci - ncc))

    def spec(d, cmap, rows, last):
        return pl.BlockSpec((None, None, hg, None, rows, last),
                            lambda bi, gi, ci: (d, bi, gi, cmap(ci), 0, 0))

    ins, specs = [], []
    for d, cmap in ((0, cf), (1, cb)):
        ins += [u6, w6, q6, k6, a6, gl6]
        specs += [spec(d, cmap, CHUNK, hd), spec(d, cmap, CHUNK, hd), spec(d, cmap, CHUNK, hd),
                  spec(d, cmap, CHUNK, hd), spec(d, cmap, CHUNK, CHUNK), spec(d, cmap, 1, LANES)]
    o_shape = jax.ShapeDtypeStruct((b, h, nc, CHUNK, hd), F32)
    ospec = lambda cmap: pl.BlockSpec((None, hg, None, CHUNK, hd), lambda bi, gi, ci: (bi, gi, cmap(ci), 0, 0))
    of, ob = pl.pallas_call(
        functools.partial(_delta_scan_kernel, hg=hg),
        grid=(b, h // hg, nc),
        in_specs=specs,
        out_specs=[ospec(cf), ospec(cb)],
        out_shape=[o_shape, o_shape],
        scratch_shapes=[pltpu.VMEM((2, hg, hd, hd), F32)],
        compiler_params=_cparams(3),
        name="delta_scan",
    )(*ins)
    return of.reshape(b, h, t, hd), ob.reshape(b, h, t, hd)


def _delta_out_kernel(of_ref, ob_ref, z_ref, g_ref, o_ref):
    o = of_ref[...] + ob_ref[...]
    o = o * lax.rsqrt(jnp.mean(o * o, axis=-1, keepdims=True) + NORM_EPS)
    z = z_ref[...].astype(F32)
    o_ref[...] = (o * g_ref[...] * (z * jax.nn.sigmoid(z))).astype(o_ref.dtype)


def _delta_out(of, ob, p1, o_gain, n_heads):
    b, h, t, hd = of.shape
    tt = _pick(t, (1024, 512, 256))
    return pl.pallas_call(
        _delta_out_kernel,
        grid=(b, h, t // tt),
        in_specs=[pl.BlockSpec((None, None, tt, hd), lambda bi, hi, ti: (bi, hi, ti, 0)),
                  pl.BlockSpec((None, None, tt, hd), lambda bi, hi, ti: (bi, hi, ti, 0)),
                  pl.BlockSpec((None, tt, hd), lambda bi, hi, ti: (bi, ti, 3 * n_heads + hi)),
                  pl.BlockSpec((1, hd), lambda bi, hi, ti: (0, 0))],
        out_specs=pl.BlockSpec((None, tt, hd), lambda bi, hi, ti: (bi, ti, hi)),
        out_shape=jax.ShapeDtypeStruct((b, t, h * hd), BF),
        compiler_params=_cparams(3),
        name="delta_out",
    )(of, ob, p1, o_gain.reshape(1, hd))


def _hy_conv_kernel(p_ref, w_ref, oc_ref, ox_ref, pad_ref, *, t_len, ctx_len):
    tt = TOKEN_TILE
    z8 = jnp.zeros((8, pad_ref.shape[1]), F32)
    pad_ref[pl.ds(0, 8), :] = z8
    pad_ref[pl.ds(8 + t_len, 8), :] = z8
    pad_ref[pl.ds(8, t_len), :] = p_ref[...].astype(F32)
    w0, w1, w2 = w_ref[0:1, :], w_ref[1:2, :], w_ref[2:3, :]
    for ti in range(t_len // tt):
        t0 = ti * tt
        r = lax.broadcasted_iota(jnp.int32, (tt, 1), 0) + t0
        left = pad_ref[pl.ds(8 + t0 - 1, tt), :]
        mid = pad_ref[pl.ds(8 + t0, tt), :]
        right = pad_ref[pl.ds(8 + t0 + 1, tt), :]
        ok_l = (r != 0) & (r != ctx_len)
        ok_r = (r != ctx_len - 1) & (r != t_len - 1)
        y = jnp.where(ok_l, left, 0.0) * w0 + mid * w1 + jnp.where(ok_r, right, 0.0) * w2
        if t0 < ctx_len:
            oc_ref[pl.ds(t0, tt), :] = y.astype(oc_ref.dtype)
        else:
            ox_ref[pl.ds(t0 - ctx_len, tt), :] = y.astype(ox_ref.dtype)


def _hy_conv(p_tail, hy_conv_w, ctx_len, width3):
    b, t, _ = p_tail.shape
    tc = _pick(width3, (512, 256, 128))
    return pl.pallas_call(
        functools.partial(_hy_conv_kernel, t_len=t, ctx_len=ctx_len),
        grid=(b, width3 // tc),
        in_specs=[pl.BlockSpec((None, t, tc), lambda bi, j: (bi, 0, j)),
                  pl.BlockSpec((3, tc), lambda bi, j: (0, j))],
        out_specs=[pl.BlockSpec((None, ctx_len, tc), lambda bi, j: (bi, 0, j)),
                   pl.BlockSpec((None, t - ctx_len, tc), lambda bi, j: (bi, 0, j))],
        out_shape=[jax.ShapeDtypeStruct((b, ctx_len, width3), BF),
                   jax.ShapeDtypeStruct((b, t - ctx_len, width3), BF)],
        scratch_shapes=[pltpu.VMEM((t + 16, tc), F32)],
        compiler_params=_cparams(2),
        name="hyena_short_conv",
    )(p_tail, hy_conv_w)


def _hy_filter_kernel(ft_ref, w1_ref, b1_ref, f1_ref, w2_ref, b2_ref, f2_ref, w3_ref, dec_ref, o_ref):
    part = pl.program_id(0)
    feats = ft_ref[...]
    hid = jnp.sin(f1_ref[...] * (_dot3(feats, w1_ref[...]) + b1_ref[...]))
    hid = jnp.sin(f2_ref[...] * (_dot3(hid, w2_ref[...]) + b2_ref[...]))
    filt = _dot3(hid, w3_ref[...])
    tpos = feats[:, 0:1]
    val = filt * jnp.exp(-tpos * jnp.abs(dec_ref[...]))
    row = lax.broadcasted_iota(jnp.int32, (val.shape[0], 1), 0) + pl.program_id(2) * val.shape[0]
    val = jnp.where((part == 1) & (row == 0), 0.0, val)
    o_ref[...] = val.astype(o_ref.dtype)


def _features(length, n_emb, k_pad):
    t = jnp.linspace(0.0, 1.0, length, dtype=F32)
    n_bands = (n_emb - 1) // 2
    omega = 2.0 * math.pi * jnp.arange(length, dtype=F32) / length
    bands = jnp.linspace(1e-4, n_bands - 1, n_bands, dtype=F32)
    ang = omega[:, None] * bands[None, :]
    feats = jnp.concatenate([t[:, None], jnp.cos(ang), -jnp.sin(ang)], axis=-1)
    feats = jnp.pad(feats, ((0, 0), (0, k_pad - n_emb)))
    shifted = jnp.concatenate([jnp.zeros((1, k_pad), F32), feats[:-1]], axis=0)
    return jnp.stack([feats, shifted])


def _hy_filter_stack(length, w1, b1, f1, w2, b2, f2, w3, decay, hyw):
    n_emb, fh = w1.shape
    k_pad = 64 if n_emb <= 64 else _pick(n_emb, (128,))
    feats = _features(length, n_emb, k_pad)
    w1p = jnp.pad(w1, ((0, k_pad - n_emb), (0, 0)))
    tr = _pick(length, (256, 128))
    tc = _pick(hyw, (512, 256, 128))
    nj = hyw // tc
    ni = length // tr
    return pl.pallas_call(
        _hy_filter_kernel,
        grid=(2, 2, ni, nj),
        in_specs=[pl.BlockSpec((None, tr, k_pad), lambda p, c, i, j: (p, i, 0)),
                  pl.BlockSpec((k_pad, fh), lambda p, c, i, j: (0, 0)),
                  pl.BlockSpec((1, fh), lambda p, c, i, j: (0, 0)),
                  pl.BlockSpec((1, fh), lambda p, c, i, j: (0, 0)),
                  pl.BlockSpec((fh, fh), lambda p, c, i, j: (0, 0)),
                  pl.BlockSpec((1, fh), lambda p, c, i, j: (0, 0)),
                  pl.BlockSpec((1, fh), lambda p, c, i, j: (0, 0)),
                  pl.BlockSpec((fh, tc), lambda p, c, i, j: (0, (2 * c + p) * nj + j)),
                  pl.BlockSpec((1, tc), lambda p, c, i, j: (0, (2 * c + p) * nj + j))],
        out_specs=pl.BlockSpec((tr, tc), lambda p, c, i, j: (p * ni + i, c * nj + j)),
        out_shape=jax.ShapeDtypeStruct((2 * length, 2 * hyw), BF),
        compiler_params=_cparams(4),
        name="hyena_filters",
    )(feats, w1p, b1.reshape(1, fh), f1.reshape(1, fh), w2, b2.reshape(1, fh), f2.reshape(1, fh),
      w3, decay.reshape(1, HY_N_FILT * hyw))


def _dft_tables(length, tile):
    n = 2 * length
    half = tile // 2
    rows = jnp.arange(n, dtype=jnp.int32)
    tile_i, r = rows // tile, rows % tile
    is_im = r >= half
    k = tile_i * half + jnp.where(is_im, r - half, r)
    t = jnp.arange(length, dtype=jnp.int32)
    ang = (2.0 * math.pi / n) * ((k[:, None] * t[None, :]) % n).astype(F32)
    nyq = jnp.where(t % 2 == 0, 1.0, -1.0).astype(F32)[None, :]
    is_nyq = (is_im & (k == 0))[:, None]
    fwd = jnp.where(is_im[:, None], -jnp.sin(ang), jnp.cos(ang))
    fwd = jnp.where(is_nyq, nyq, fwd)
    sgn = jnp.where(is_im[:, None] & ~is_nyq, -1.0, 1.0)
    fker = jnp.concatenate([fwd, fwd * sgn], axis=1)
    scale = jnp.where(k == 0, 1.0 / n, 2.0 / n)[None, :]
    inv = jnp.where(is_im[None, :], -jnp.sin(ang.T), jnp.cos(ang.T))
    inv = jnp.where(is_nyq.T, nyq.T, inv) * scale
    return fwd.astype(BF), fker.astype(BF), inv.astype(BF)


def _hy_long_convs(u_seq, kstack, hy_skip, hyw):
    b, ls, _ = u_seq.shape
    n = 2 * ls
    tmf = _pick(n, (512,))
    half = tmf // 2
    tn = _pick(hyw, (512, 256, 128))
    nj = hyw // tn
    fwd, fker, inv = _dft_tables(ls, tmf)

    (hspec,) = _matmul(
        fker, kstack,
        pl.BlockSpec((tmf, n), lambda g, j, i: (i, 0)),
        pl.BlockSpec((n, tn), lambda g, j, i: (0, j)),
        (1, 2 * nj, n // tmf),
        [(jax.ShapeDtypeStruct((n, 2 * hyw), F32), pl.BlockSpec((tmf, tn), lambda g, j, i: (i, j)))],
        _ep_store(F32), name="hyena_filter_dft")

    def spectrum_product(acc, extras, outs):
        hs = extras[0][...]
        re, im = acc[:half], acc[half:]
        hre, him = hs[:half], hs[half:]
        first = (lax.broadcasted_iota(jnp.int32, (half, 1), 0) == 0) & (pl.program_id(2) == 0)
        yre = jnp.where(first, re * hre, re * hre - im * him)
        yim = jnp.where(first, im * him, re * him + im * hre)
        outs[0][pl.ds(0, half), :] = yre.astype(BF)
        outs[0][pl.ds(half, half), :] = yim.astype(BF)

    tmi = _pick(ls, (TOKEN_TILE,))
    skip3 = hy_skip.reshape(2, 1, hyw)

    def conv(z_arr, z_off, conv_idx, mul_off):
        (spec,) = _matmul(
            fwd, z_arr,
            pl.BlockSpec((tmf, ls), lambda g, j, i: (i, 0)),
            pl.BlockSpec((None, ls, tn), lambda g, j, i: (g, 0, z_off + j)),
            (b, nj, n // tmf),
            [(jax.ShapeDtypeStruct((b, n, hyw), BF), pl.BlockSpec((None, tmf, tn), lambda g, j, i: (g, i, j)))],
            spectrum_product,
            extras=[(hspec, pl.BlockSpec((tmf, tn), lambda g, j, i: (i, conv_idx * nj + j)))],
            name="hyena_dft_fwd")

        def finish(acc, extras, outs):
            z = extras[0][...].astype(F32)
            xm = extras[2][...].astype(F32)
            outs[0][...] = (xm * (acc + z * extras[1][...])).astype(BF)

        (y,) = _matmul(
            inv, spec,
            pl.BlockSpec((tmi, n), lambda g, j, i: (i, 0)),
            pl.BlockSpec((None, n, tn), lambda g, j, i: (g, 0, j)),
            (b, nj, ls // tmi),
            [(jax.ShapeDtypeStruct((b, ls, hyw), BF), pl.BlockSpec((None, tmi, tn), lambda g, j, i: (g, i, j)))],
            finish,
            extras=[(z_arr, pl.BlockSpec((None, tmi, tn), lambda g, j, i: (g, i, z_off + j))),
                    (skip3, pl.BlockSpec((None, 1, tn), lambda g, j, i: (conv_idx, 0, j))),
                    (u_seq, pl.BlockSpec((None, tmi, tn), lambda g, j, i: (g, i, mul_off * nj + j)))],
            name="hyena_dft_inv")
        return y

    z2 = conv(u_seq, 0, 0, 1)
    return conv(z2, 0, 1, 2)


def _moe_up_kernel(x_ref, wg_ref, wu_ref, comb_ref, o_ref, wgb, wub):
    @pl.when(pl.program_id(1) == 0)
    def _():
        wgb[...] = wg_ref[...].astype(BF)
        wub[...] = wu_ref[...].astype(BF)
    x = x_ref[...]
    g = jnp.dot(x, wgb[...], preferred_element_type=F32)
    u = jnp.dot(x, wub[...], preferred_element_type=F32)
    comb = comb_ref[...]
    lane = lax.broadcasted_iota(jnp.int32, comb.shape, 1)
    scale = jnp.sum(jnp.where(lane == pl.program_id(0), comb, 0.0), axis=1, keepdims=True)
    o_ref[...] = (g * jax.nn.sigmoid(g) * u * scale).astype(o_ref.dtype)


def _moe_up(u, w_gate, w_up, comb, l):
    m, d = u.shape
    _, ne, _, f = w_gate.shape
    tm = _pick(m, (512, 256))
    return pl.pallas_call(
        _moe_up_kernel,
        grid=(ne, m // tm),
        in_specs=[pl.BlockSpec((tm, d), lambda e, i: (i, 0)),
                  pl.BlockSpec((None, None, d, f), lambda e, i: (l, e, 0, 0)),
                  pl.BlockSpec((None, None, d, f), lambda e, i: (l, e, 0, 0)),
                  pl.BlockSpec((tm, LANES), lambda e, i: (i, 0))],
        out_specs=pl.BlockSpec((None, tm, f), lambda e, i: (e, i, 0)),
        out_shape=jax.ShapeDtypeStruct((ne, m, f), BF),
        scratch_shapes=[pltpu.VMEM((d, f), BF), pltpu.VMEM((d, f), BF)],
        compiler_params=_cparams(2),
        name="moe_up",
    )(u, w_gate, w_up, comb)


def _row_gate(gt_ref, tm, t_len, ctx_len, n_batch):
    r = lax.broadcasted_iota(jnp.int32, (tm, 1), 0) + pl.program_id(2) * tm
    gate = jnp.zeros((tm, gt_ref.shape[1]), F32)
    for bi in range(n_batch):
        in_b = (r >= bi * t_len) & (r < (bi + 1) * t_len)
        is_c = r < bi * t_len + ctx_len
        gate = jnp.where(in_b & is_c, gt_ref[0:1, :], gate)
        gate = jnp.where(in_b & ~is_c, gt_ref[1 + bi:2 + bi, :], gate)
    return gate


def _moe_down_kernel(x_ref, w_ref, h_ref, gt_ref, o_ref, wbf, *, t_len, ctx_len, n_batch):
    ne = w_ref.shape[0]

    @pl.when(pl.program_id(2) == 0)
    def _():
        def it(e, c):
            wbf[e] = w_ref[e].astype(BF)
            return c
        lax.fori_loop(0, ne, it, 0)

    tm, tn = o_ref.shape

    def it(e, acc):
        return acc + jnp.dot(x_ref[e], wbf[e], preferred_element_type=F32)
    acc = lax.fori_loop(0, ne, it, jnp.zeros((tm, tn), F32))
    o_ref[...] = h_ref[...] + _row_gate(gt_ref, tm, t_len, ctx_len, n_batch) * acc


def _moe_down(hid, w_down, h, gate_tab, t_len, ctx_len, n_batch, l):
    ne, m, f = hid.shape
    d = w_down.shape[3]
    tm = _pick(m, (512, 256))
    tn = _pick(d, (256, 128))
    return pl.pallas_call(
        functools.partial(_moe_down_kernel, t_len=t_len, ctx_len=ctx_len, n_batch=n_batch),
        grid=(1, d // tn, m // tm),
        in_specs=[pl.BlockSpec((ne, tm, f), lambda g, j, i: (0, i, 0)),
                  pl.BlockSpec((None, ne, f, tn), lambda g, j, i: (l, 0, 0, j)),
                  pl.BlockSpec((tm, tn), lambda g, j, i: (i, j)),
                  pl.BlockSpec((8, tn), lambda g, j, i: (0, j))],
        out_specs=pl.BlockSpec((tm, tn), lambda g, j, i: (i, j)),
        out_shape=jax.ShapeDtypeStruct((m, d), F32),
        scratch_shapes=[pltpu.VMEM((ne, f, tn), BF)],
        compiler_params=_cparams(3),
        name="moe_down",
    )(hid, w_down, h, gate_tab)


def _final_norm_kernel(h_ref, g_ref, o_ref):
    xf = h_ref[...]
    o_ref[...] = xf * lax.rsqrt(jnp.mean(xf * xf, axis=-1, keepdims=True) + NORM_EPS) * g_ref[...]


def _final_norm(h, gain, n_batch, t_len, ctx_len):
    m, d = h.shape
    tt = TOKEN_TILE
    n_t, n_c = t_len // tt, ctx_len // tt
    n_x = n_t - n_c
    return pl.pallas_call(
        _final_norm_kernel,
        grid=(n_batch, n_x),
        in_specs=[pl.BlockSpec((tt, d), lambda bi, i: (bi * n_t + n_c + i, 0)),
                  pl.BlockSpec((1, d), lambda bi, i: (0, 0))],
        out_specs=pl.BlockSpec((None, tt, d), lambda bi, i: (bi, i, 0)),
        out_shape=jax.ShapeDtypeStruct((n_batch, t_len - ctx_len, d), F32),
        compiler_params=_cparams(2),
        name="final_norm",
    )(h, gain.reshape(1, d))


def kernel(x, c, ctx, c_ctx, ada_down, ada_up, ada_bias, norm_mix, norm_ffn, w_in, qkv_conv, a_log, dt_bias, o_norm, hy_conv, hy_w1, hy_b1, hy_f1, hy_w2, hy_b2, hy_f2, hy_w3, hy_decay, hy_skip, w_br_a, w_br_b, w_out, router_group, router_group_bias, router_expert, router_expert_bias, exp_gate, exp_up, exp_down, final_norm):
    n_batch, seq, d = x.shape
    ctx_len = ctx.shape[1]
    t_len = ctx_len + seq
    m = n_batch * t_len
    depth = w_in.shape[0]
    n_heads = a_log.shape[2]
    a_width = n_heads * LANES
    hyw = hy_skip.shape[2]
    n_exp = exp_gate.shape[1]
    dg_off = 4 * a_width
    hy_off = dg_off + 4 * n_heads
    tail_w = 3 * hyw + 2 * d
    n_t, n_c = t_len // TOKEN_TILE, ctx_len // TOKEN_TILE
    nc = t_len // CHUNK
    ncp = -(-nc // 8) * 8

    h = jnp.concatenate([ctx, x], axis=1).reshape(m, d)

    conds = jnp.zeros((8, d), F32).at[0].set(c_ctx).at[1:1 + n_batch].set(c)
    mod_all = _adaln_all(conds, ada_down, ada_up, ada_bias)
    mod_all = mod_all.reshape(depth, 8, 6, d)[:, :1 + n_batch]
    mod_all = jnp.pad(mod_all, ((0, 0), (0, 0), (0, 2), (0, 0)))

    tm = _pick(m, (512, 256))
    tn = 512

    def gate_table(mod_l, idx):
        return jnp.pad(mod_l[:, idx, :], ((0, 8 - (1 + n_batch)), (0, 0)))

    def residual_ep(acc, extras, outs):
        outs[0][...] = extras[0][...] + _row_gate(extras[1], acc.shape[0], t_len, ctx_len, n_batch) * acc

    for l in range(depth):
        mod_l = mod_all[l]
        u = _norm_mod(h, norm_mix[l], mod_l, 0, n_t, n_c)
        (p1,) = _matmul(
            u, w_in,
            pl.BlockSpec((tm, d), lambda g, j, i: (i, 0)),
            pl.BlockSpec((None, d, tn), lambda g, j, i, l=l: (l, 0, j)),
            (1, dg_off // tn, m // tm),
            [(jax.ShapeDtypeStruct((m, dg_off), BF), pl.BlockSpec((tm, tn), lambda g, j, i: (i, j)))],
            _ep_store(BF), name="proj_qkvz")
        w_tail = w_in[l, :, hy_off:]
        (p_tail,) = _matmul(
            u, w_tail,
            pl.BlockSpec((tm, d), lambda g, j, i: (i, 0)),
            pl.BlockSpec((d, tn), lambda g, j, i: (0, j)),
            (1, tail_w // tn, m // tm),
            [(jax.ShapeDtypeStruct((m, tail_w), BF), pl.BlockSpec((tm, tn), lambda g, j, i: (i, j)))],
            _ep_store(BF), name="proj_tail")
        w_dg_t = w_in[l, :, dg_off:hy_off].T
        a_exp = jnp.concatenate([jnp.exp(a_log[l].reshape(-1)), jnp.zeros((2 * n_heads,), F32)])[:, None]
        dt_b = jnp.concatenate([dt_bias[l].reshape(-1), jnp.zeros((2 * n_heads,), F32)])[:, None]
        gates = _gates(u, w_dg_t, a_exp, dt_b)

        g5 = gates.reshape(4, n_heads, n_batch, nc, CHUNK).transpose(2, 1, 0, 3, 4)
        g_rows = jnp.pad(g5, ((0, 0), (0, 0), (0, 0), (0, ncp - nc), (0, 0)))
        g_cols = jnp.pad(jnp.swapaxes(g5, 3, 4), ((0, 0), (0, 0), (0, 0), (0, 0), (0, LANES - nc)))

        p1_3 = p1.reshape(n_batch, t_len, dg_off)
        conv9 = qkv_conv[l].reshape(9, 3 * a_width)
        du, dw, dq, dk, da, dgl = _delta_prep(p1_3, conv9, g_rows, g_cols, n_heads, ctx_len)
        o_f, o_b = _delta_scan(du, dw, dq, dk, da, dgl, ctx_len)
        o_a = _delta_out(o_f, o_b, p1_3, o_norm[l], n_heads).reshape(m, a_width)

        p_tail3 = p_tail.reshape(n_batch, t_len, tail_w)
        u_c, u_x = _hy_conv(p_tail3, hy_conv[l], ctx_len, 3 * hyw)
        filt_args = (hy_w1[l], hy_b1[l], hy_f1[l], hy_w2[l], hy_b2[l], hy_f2[l], hy_w3[l], hy_decay[l])
        y_x = _hy_long_convs(u_x, _hy_filter_stack(seq, *filt_args, hyw), hy_skip[l], hyw)
        y_c = _hy_long_convs(u_c, _hy_filter_stack(ctx_len, *filt_args, hyw), hy_skip[l], hyw)
        y_b = jnp.concatenate([y_c, y_x], axis=1).reshape(m, hyw)

        ga_blk = (3 * hyw) // tn
        gb_blk = (3 * hyw + d) // tn

        def gated_ep(acc, extras, outs):
            outs[0][...] = jax.nn.sigmoid(extras[0][...].astype(F32)) * acc

        (br,) = _matmul(
            o_a, w_br_a,
            pl.BlockSpec((tm, a_width), lambda g, j, i: (i, 0)),
            pl.BlockSpec((None, a_width, tn), lambda g, j, i, l=l: (l, 0, j)),
            (1, d // tn, m // tm),
            [(jax.ShapeDtypeStruct((m, d), F32), pl.BlockSpec((tm, tn), lambda g, j, i: (i, j)))],
            gated_ep,
            extras=[(p_tail, pl.BlockSpec((tm, tn), lambda g, j, i: (i, ga_blk + j)))],
            name="branch_a")

        def merge_ep(acc, extras, outs):
            outs[0][...] = (extras[1][...] + jax.nn.sigmoid(extras[0][...].astype(F32)) * acc).astype(BF)

        (merged,) = _matmul(
            y_b, w_br_b,
            pl.BlockSpec((tm, hyw), lambda g, j, i: (i, 0)),
            pl.BlockSpec((None, hyw, tn), lambda g, j, i, l=l: (l, 0, j)),
            (1, d // tn, m // tm),
            [(jax.ShapeDtypeStruct((m, d), BF), pl.BlockSpec((tm, tn), lambda g, j, i: (i, j)))],
            merge_ep,
            extras=[(p_tail, pl.BlockSpec((tm, tn), lambda g, j, i: (i, gb_blk + j))),
                    (br, pl.BlockSpec((tm, tn), lambda g, j, i: (i, j)))],
            name="branch_b_merge")

        (h,) = _matmul(
            merged, w_out,
            pl.BlockSpec((tm, d), lambda g, j, i: (i, 0)),
            pl.BlockSpec((None, d, tn), lambda g, j, i, l=l: (l, 0, j)),
            (1, d // tn, m // tm),
            [(jax.ShapeDtypeStruct((m, d), F32), pl.BlockSpec((tm, tn), lambda g, j, i: (i, j)))],
            residual_ep,
            extras=[(h, pl.BlockSpec((tm, tn), lambda g, j, i: (i, j))),
                    (gate_table(mod_l, 2), pl.BlockSpec((8, tn), lambda g, j, i: (0, j)))],
            name="out_proj")

        w_router = jnp.pad(jnp.concatenate([router_expert[l], router_group[l]], axis=1),
                           ((0, 0), (0, LANES - n_exp - N_GROUPS)))
        b_router = jnp.pad(jnp.concatenate([router_expert_bias[l], router_group_bias[l]]),
                           (0, LANES - n_exp - N_GROUPS)).reshape(1, LANES)
        u2, comb = _norm_mod_router(h, norm_ffn[l], mod_l, 3, n_t, n_c, w_router, b_router, n_exp, N_GROUPS)
        hid = _moe_up(u2, exp_gate, exp_up, comb, l)
        h = _moe_down(hid, exp_down, h, gate_table(mod_l, 5), t_len, ctx_len, n_batch, l)

    return _final_norm(h, final_norm, n_batch, t_len, ctx_len)
```

```python
import functools
import math

import jax
import jax.numpy as jnp
from jax import lax
from jax.experimental import pallas as pl
from jax.experimental.pallas import tpu as pltpu

F32 = jnp.float32
BF = jnp.bfloat16

GRID_W = 64
CHUNK = 64
N_GROUPS = 4
EXPERTS_PER_GROUP = 8
HY_N_FILT = 4
NORM_EPS = 1e-6
TOKEN_TILE = 256
LANES = 128
CONV_PAD = 72
VMEM_LIMIT = 56 * 1024 * 1024


def _cparams(n_grid):
    return pltpu.CompilerParams(dimension_semantics=("arbitrary",) * n_grid,
                                vmem_limit_bytes=VMEM_LIMIT)


def _split_bf16(a):
    hi = a.astype(BF)
    lo = (a - hi.astype(F32)).astype(BF)
    return hi, lo


def _dot3(a, b):
    ah, al = _split_bf16(a)
    bh, bl = _split_bf16(b)
    d = functools.partial(jnp.dot, preferred_element_type=F32)
    return d(ah, bh) + (d(ah, bl) + d(al, bh))


def _dotx(a, b):
    return jnp.dot(a, b, precision=lax.Precision.HIGHEST, preferred_element_type=F32)


def _bdot(a, b):
    return jnp.dot(a.astype(BF), b.astype(BF), preferred_element_type=F32)


def _mm_kernel(*refs, n_extra, n_out, cast_w, prologue, epilogue):
    x_ref, w_ref = refs[0], refs[1]
    extras = refs[2:2 + n_extra]
    outs = refs[2 + n_extra:2 + n_extra + n_out]
    scratch = refs[2 + n_extra + n_out:]
    if cast_w:
        wbf = scratch[0]

        @pl.when(pl.program_id(2) == 0)
        def _():
            k = w_ref.shape[0]
            step = 512 if k % 512 == 0 else k
            if k == step:
                wbf[...] = w_ref[...].astype(BF)
            else:
                def it(r, c):
                    r0 = pl.multiple_of(r * step, step)
                    wbf[pl.ds(r0, step), :] = w_ref[pl.ds(r0, step), :].astype(BF)
                    return c
                lax.fori_loop(0, k // step, it, 0)
        w = wbf[...]
    else:
        w = w_ref[...]
    x = x_ref[...]
    if prologue is not None:
        x = prologue(x)
    if x.dtype != BF:
        x = x.astype(BF)
    acc = jnp.dot(x, w, preferred_element_type=F32)
    epilogue(acc, extras, outs)


def _matmul(x, w, x_spec, w_spec, grid, outs, epilogue, extras=(), prologue=None, name=None):
    cast_w = w.dtype != BF
    kw, tn = [d for d in w_spec.block_shape if d is not None]
    scratch = [pltpu.VMEM((kw, tn), BF)] if cast_w else []
    body = functools.partial(_mm_kernel, n_extra=len(extras), n_out=len(outs), cast_w=cast_w,
                             prologue=prologue, epilogue=epilogue)
    res = pl.pallas_call(
        body,
        grid=grid,
        in_specs=[x_spec, w_spec] + [s for _, s in extras],
        out_specs=[s for _, s in outs],
        out_shape=[o for o, _ in outs],
        scratch_shapes=scratch,
        compiler_params=_cparams(3),
        name=name,
    )(x, w, *[a for a, _ in extras])
    return res


def _ep_store(dtype):
    def ep(acc, extras, outs):
        outs[0][...] = acc.astype(dtype)
    return ep


def _pick(n, prefs):
    for p in prefs:
        if n % p == 0:
            return p
    return n


def _adaln_all(conds, ada_down, ada_up, ada_bias):
    nl, d, r = ada_down.shape
    n6 = ada_up.shape[2]
    (t,) = _matmul(
        conds, ada_down,
        pl.BlockSpec((8, d), lambda g, j, i: (0, 0)),
        pl.BlockSpec((None, d, r), lambda g, j, i: (g, 0, 0)),
        (nl, 1, 1),
        [(jax.ShapeDtypeStruct((nl, 8, r), F32), pl.BlockSpec((None, 8, r), lambda g, j, i: (g, 0, 0)))],
        _ep_store(F32), prologue=lambda v: v * jax.nn.sigmoid(v), name="adaln_down")
    tn = _pick(n6, (2048, 1024, 512, 256, 128))

    def ep(acc, extras, outs):
        outs[0][...] = acc + extras[0][...]

    (mod,) = _matmul(
        t, ada_up,
        pl.BlockSpec((None, 8, r), lambda g, j, i: (g, 0, 0)),
        pl.BlockSpec((None, r, tn), lambda g, j, i: (g, 0, j)),
        (nl, n6 // tn, 1),
        [(jax.ShapeDtypeStruct((nl, 8, n6), F32), pl.BlockSpec((None, 8, tn), lambda g, j, i: (g, 0, j)))],
        ep, extras=[(ada_bias.reshape(nl, 1, n6), pl.BlockSpec((None, 1, tn), lambda g, j, i: (g, 0, j)))],
        name="adaln_up")
    return mod


def _mod_row_map(n_t, n_c):
    def m(i):
        return (jnp.where(i % n_t < n_c, 0, 1 + i // n_t), 0, 0)
    return m


def _norm_mod_kernel(h_ref, g_ref, mod_ref, u_ref, *, off):
    xf = h_ref[...]
    y = xf * lax.rsqrt(jnp.mean(xf * xf, axis=-1, keepdims=True) + NORM_EPS) * g_ref[...]
    sh = mod_ref[off:off + 1, :]
    sc = mod_ref[off + 1:off + 2, :]
    u_ref[...] = (y * (1.0 + sc) + sh).astype(u_ref.dtype)


def _norm_mod(h, gain, mod_l, off, n_t, n_c):
    m, d = h.shape
    tt = TOKEN_TILE
    return pl.pallas_call(
        functools.partial(_norm_mod_kernel, off=off),
        grid=(m // tt,),
        in_specs=[pl.BlockSpec((tt, d), lambda i: (i, 0)),
                  pl.BlockSpec((1, d), lambda i: (0, 0)),
                  pl.BlockSpec((None, 8, d), _mod_row_map(n_t, n_c))],
        out_specs=pl.BlockSpec((tt, d), lambda i: (i, 0)),
        out_shape=jax.ShapeDtypeStruct((m, d), BF),
        compiler_params=_cparams(1),
        name="norm_mod",
    )(h, gain.reshape(1, d), mod_l)


def _router_kernel(h_ref, g_ref, mod_ref, wr_ref, br_ref, u_ref, comb_ref, *, off, n_exp, n_grp):
    xf = h_ref[...]
    y = xf * lax.rsqrt(jnp.mean(xf * xf, axis=-1, keepdims=True) + NORM_EPS) * g_ref[...]
    sh = mod_ref[off:off + 1, :]
    sc = mod_ref[off + 1:off + 2, :]
    u = y * (1.0 + sc) + sh
    u_ref[...] = u.astype(u_ref.dtype)
    logits = _dot3(u, wr_ref[...]) + br_ref[...]
    lane = lax.broadcasted_iota(jnp.int32, logits.shape, 1)
    neg = jnp.float32(-jnp.inf)
    big = jnp.int32(1 << 20)
    is_g = (lane >= n_exp) & (lane < n_exp + n_grp)
    lg = jnp.where(is_g, logits, neg)
    eg = jnp.exp(lg - jnp.max(lg, axis=1, keepdims=True))
    pg = eg / jnp.sum(eg, axis=1, keepdims=True)
    p_top = jnp.max(pg, axis=1, keepdims=True)
    g_idx = jnp.min(jnp.where(is_g & (pg == p_top), lane, big), axis=1, keepdims=True) - n_exp
    per = n_exp // n_grp
    in_grp = (lane >= g_idx * per) & (lane < (g_idx + 1) * per)
    le = jnp.where(in_grp, logits, neg)
    v1 = jnp.max(le, axis=1, keepdims=True)
    i1 = jnp.min(jnp.where(le == v1, lane, big), axis=1, keepdims=True)
    le2 = jnp.where(lane == i1, neg, le)
    v2 = jnp.max(le2, axis=1, keepdims=True)
    i2 = jnp.min(jnp.where(le2 == v2, lane, big), axis=1, keepdims=True)
    e2 = jnp.exp(v2 - v1)
    w1 = p_top / (1.0 + e2)
    w2 = p_top * e2 / (1.0 + e2)
    comb_ref[...] = jnp.where(lane == i1, w1, 0.0) + jnp.where(lane == i2, w2, 0.0)


def _norm_mod_router(h, gain, mod_l, off, n_t, n_c, w_router, b_router, n_exp, n_grp):
    m, d = h.shape
    tt = TOKEN_TILE
    return pl.pallas_call(
        functools.partial(_router_kernel, off=off, n_exp=n_exp, n_grp=n_grp),
        grid=(m // tt,),
        in_specs=[pl.BlockSpec((tt, d), lambda i: (i, 0)),
                  pl.BlockSpec((1, d), lambda i: (0, 0)),
                  pl.BlockSpec((None, 8, d), _mod_row_map(n_t, n_c)),
                  pl.BlockSpec((d, LANES), lambda i: (0, 0)),
                  pl.BlockSpec((1, LANES), lambda i: (0, 0))],
        out_specs=[pl.BlockSpec((tt, d), lambda i: (i, 0)),
                   pl.BlockSpec((tt, LANES), lambda i: (i, 0))],
        out_shape=[jax.ShapeDtypeStruct((m, d), BF), jax.ShapeDtypeStruct((m, LANES), F32)],
        compiler_params=_cparams(1),
        name="norm_mod_router",
    )(h, gain.reshape(1, d), mod_l, w_router, b_router)


def _gates_kernel(w_ref, u_ref, aexp_ref, dtb_ref, o_ref, *, n_decay):
    acc = lax.dot_general(w_ref[...].astype(BF), u_ref[...], (((1,), (1,)), ((), ())),
                          preferred_element_type=F32)
    row = lax.broadcasted_iota(jnp.int32, acc.shape, 0)
    z = acc + dtb_ref[...]
    softplus = jnp.maximum(z, 0.0) + jnp.log(1.0 + jnp.exp(-jnp.abs(z)))
    g = -aexp_ref[...] * softplus
    o_ref[...] = jnp.where(row < n_decay, g, jax.nn.sigmoid(acc))


def _gates(u, w_dg_t, a_exp, dt_b):
    m, d = u.shape
    r = w_dg_t.shape[0]
    tm = _pick(m, (512, 256))
    return pl.pallas_call(
        functools.partial(_gates_kernel, n_decay=r // 2),
        grid=(m // tm,),
        in_specs=[pl.BlockSpec((r, d), lambda i: (0, 0)),
                  pl.BlockSpec((tm, d), lambda i: (i, 0)),
                  pl.BlockSpec((r, 1), lambda i: (0, 0)),
                  pl.BlockSpec((r, 1), lambda i: (0, 0))],
        out_specs=pl.BlockSpec((r, tm), lambda i: (0, i)),
        out_shape=jax.ShapeDtypeStruct((r, m), F32),
        compiler_params=_cparams(1),
        name="delta_gates",
    )(w_dg_t, u, a_exp, dt_b)


def _conv_silu_tile(pad_ref, cw_ref, t0, n, seq_start, seq_len, width):
    one_row = width == seq_len
    r = lax.broadcasted_iota(jnp.int32, (n, 1), 0) + (t0 - seq_start)
    col = lax.rem(r, width)
    acc = jnp.zeros((n, LANES), F32)
    for dj in (-1, 0, 1):
        part = jnp.zeros((n, LANES), F32)
        for di in (-1, 0, 1):
            if one_row and di != 0:
                continue
            off = di * width + dj
            tap = pad_ref[pl.ds(CONV_PAD + t0 + off, n), :]
            wrow = cw_ref[(di + 1) * 3 + (dj + 1):(di + 1) * 3 + (dj + 1) + 1, :]
            ok = (r + off >= 0) & (r + off < seq_len)
            part = part + jnp.where(ok, tap, 0.0) * wrow
        ok_c = (col + dj >= 0) & (col + dj < width)
        acc = acc + jnp.where(ok_c, part, 0.0)
    return acc * jax.nn.sigmoid(acc)


def _bmm(a, b):
    return jnp.einsum('gik,gkj->gij', a.astype(BF), b.astype(BF), preferred_element_type=F32)


def _bmm_nt(a, b):
    return jnp.einsum('gik,gjk->gij', a.astype(BF), b.astype(BF), preferred_element_type=F32)


def _neumann_inverse(lmat, eye):
    x = -lmat
    p = eye + x
    n = lmat.shape[-1]
    k = 1
    while 2 * k < n:
        x = _bmm(x, x)
        p = p + _bmm(p, x)
        k *= 2
    return p


def _delta_prep_kernel(pq_ref, pk_ref, pv_ref, cq_ref, ck_ref, cv_ref, g_ref, gt_ref,
                       u_ref, w_ref, qd_ref, kd_ref, a_ref, gl_ref,
                       pad_ref, q_s, k_s, v_s, gcr_s, gct_s,
                       *, t_len, ctx_len, n_chunks, group):
    hd = q_s.shape[1]
    tt = TOKEN_TILE
    d = pl.program_id(2)

    @pl.when(d == 0)
    def _():
        zpad = jnp.zeros((CONV_PAD, LANES), F32)
        pad_ref[pl.ds(0, CONV_PAD), :] = zpad
        pad_ref[pl.ds(CONV_PAD + t_len, CONV_PAD), :] = zpad
        for src, cw, dst, kind in ((pq_ref, cq_ref, q_s, "q"), (pk_ref, ck_ref, k_s, "k"), (pv_ref, cv_ref, v_s, "v")):
            pad_ref[pl.ds(CONV_PAD, t_len), :] = src[...].astype(F32)
            for ti in range(t_len // tt):
                t0 = ti * tt
                if t0 < ctx_len:
                    a = _conv_silu_tile(pad_ref, cw, t0, tt, 0, ctx_len, ctx_len)
                else:
                    a = _conv_silu_tile(pad_ref, cw, t0, tt, ctx_len, t_len - ctx_len, GRID_W)
                if kind != "v":
                    a = a * lax.rsqrt(jnp.sum(a * a, axis=-1, keepdims=True) + NORM_EPS)
                if kind == "q":
                    a = a * (hd ** -0.5)
                dst[pl.ds(t0, tt), :] = a

    ii = lax.broadcasted_iota(jnp.int32, (CHUNK, CHUNK), 0)
    jj = lax.broadcasted_iota(jnp.int32, (CHUNK, CHUNK), 1)
    sdiff = (ii - jj) * (1 - 2 * d)
    eye = (ii == jj).astype(F32)
    incl = sdiff >= 0
    strict = sdiff > 0
    gcr_s[...] = _dotx(g_ref[d], (sdiff <= 0).astype(F32))
    gct_s[...] = _dotx(incl.astype(F32), gt_ref[d])
    gtot = jnp.where(d == 0, gcr_s[:, CHUNK - 1:CHUNK], gcr_s[:, 0:1])
    gl_ref[...] = jnp.broadcast_to(jnp.exp(gtot), gl_ref.shape)

    gc = group
    rows = gc * CHUNK
    lane_c = lax.broadcasted_iota(jnp.int32, (CHUNK, LANES), 1)

    def body(it, carry):
        c0 = it * gc
        t0 = pl.multiple_of(c0 * CHUNK, rows)
        q = q_s[pl.ds(t0, rows), :].reshape(gc, CHUNK, hd)
        k = k_s[pl.ds(t0, rows), :].reshape(gc, CHUNK, hd)
        v = v_s[pl.ds(t0, rows), :].reshape(gc, CHUNK, hd)
        gct = gct_s[...]
        bt = gt_ref[2 + d]
        gcols, bcols, grows, glasts = [], [], [], []
        for g in range(gc):
            sel = lane_c == c0 + g
            gcols.append(jnp.sum(jnp.where(sel, gct, 0.0), axis=1, keepdims=True))
            bcols.append(jnp.sum(jnp.where(sel, bt, 0.0), axis=1, keepdims=True))
            grow_g = gcr_s[pl.ds(c0 + g, 1), :]
            grows.append(grow_g)
            glasts.append(jnp.where(d == 0, grow_g[:, CHUNK - 1:CHUNK], grow_g[:, 0:1]))
        gcol = jnp.stack(gcols)
        bcol = jnp.stack(bcols)
        grow = jnp.stack(grows)
        glast = jnp.stack(glasts)
        decay = jnp.where(incl, jnp.exp(jnp.where(incl, gcol - grow, 0.0)), 0.0)
        kb = k * bcol
        lmat = jnp.where(strict, _bmm_nt(kb, k) * decay, 0.0)
        tinv = _neumann_inverse(lmat, eye[None])
        egc = jnp.exp(gcol)
        uw = _bmm(tinv, jnp.concatenate([v * bcol, kb * egc], axis=2))
        attn = (_bmm_nt(q, k) * decay).astype(BF)
        kdec = k * jnp.exp(glast - gcol)
        u_ref[pl.ds(t0, rows), :] = uw[:, :, :hd].reshape(rows, hd)
        w_ref[pl.ds(t0, rows), :] = uw[:, :, hd:].astype(BF).reshape(rows, hd)
        qd_ref[pl.ds(t0, rows), :] = (q * egc).astype(BF).reshape(rows, hd)
        a_ref[pl.ds(t0, rows), :] = attn.reshape(rows, CHUNK)
        for g in range(gc):
            kd_ref[c0 + g] = kdec[g].T.astype(BF)
        return carry

    lax.fori_loop(0, n_chunks // group, body, 0)


def _delta_prep(p1, conv_w9, g_rows, g_cols, n_heads, ctx_len):
    b, t, _ = p1.shape
    hd = LANES
    nc = t // CHUNK
    ncp = g_rows.shape[3]
    group = max(g for g in range(1, 18) if nc % g == 0)
    h = n_heads
    kern = functools.partial(_delta_prep_kernel, t_len=t, ctx_len=ctx_len, n_chunks=nc, group=group)
    tok = lambda off: pl.BlockSpec((None, t, hd), lambda bi, hi, di: (bi, 0, off + hi))
    cws = lambda off: pl.BlockSpec((9, hd), lambda bi, hi, di: (0, off + hi))
    o5 = lambda last: pl.BlockSpec((None, None, None, t, last), lambda bi, hi, di: (di, bi, hi, 0, 0))
    return pl.pallas_call(
        kern,
        grid=(b, h, 2),
        in_specs=[tok(0), tok(h), tok(2 * h), cws(0), cws(h), cws(2 * h),
                  pl.BlockSpec((None, None, 4, ncp, CHUNK), lambda bi, hi, di: (bi, hi, 0, 0, 0)),
                  pl.BlockSpec((None, None, 4, CHUNK, LANES), lambda bi, hi, di: (bi, hi, 0, 0, 0))],
        out_specs=[o5(hd), o5(hd), o5(hd),
                   pl.BlockSpec((None, None, None, nc, hd, CHUNK), lambda bi, hi, di: (di, bi, hi, 0, 0, 0)),
                   o5(CHUNK),
                   pl.BlockSpec((None, None, None, ncp, LANES), lambda bi, hi, di: (di, bi, hi, 0, 0))],
        out_shape=[jax.ShapeDtypeStruct((2, b, h, t, hd), F32),
                   jax.ShapeDtypeStruct((2, b, h, t, hd), BF),
                   jax.ShapeDtypeStruct((2, b, h, t, hd), BF),
                   jax.ShapeDtypeStruct((2, b, h, nc, hd, CHUNK), BF),
                   jax.ShapeDtypeStruct((2, b, h, t, CHUNK), BF),
                   jax.ShapeDtypeStruct((2, b, h, ncp, LANES), F32)],
        scratch_shapes=[pltpu.VMEM((t + 2 * CONV_PAD, LANES), F32),
                        pltpu.VMEM((t, hd), F32), pltpu.VMEM((t, hd), F32), pltpu.VMEM((t, hd), F32),
                        pltpu.VMEM((ncp, CHUNK), F32), pltpu.VMEM((CHUNK, LANES), F32)],
        compiler_params=_cparams(3),
        name="delta_prep",
    )(p1, p1, p1, conv_w9, conv_w9, conv_w9, g_rows, g_cols)


def _delta_scan_kernel(uf, wf, qf, kf, af, gf, ub, wb, qb, kb, ab, gb, of_ref, ob_ref, s_ref, *, hg):
    @pl.when(pl.program_id(2) == 0)
    def _():
        s_ref[...] = jnp.zeros_like(s_ref)

    cat = lambda a, b: jnp.concatenate([a[...], b[...]], axis=0)
    s = s_ref[...]
    sb = s.astype(BF)
    vn = cat(uf, ub) - _bmm(cat(wf, wb), sb)
    vnb = vn.astype(BF)
    o = _bmm(cat(qf, qb), sb) + _bmm(cat(af, ab), vnb)
    s_ref[...] = s * cat(gf, gb) + _bmm(cat(kf, kb), vnb)
    of_ref[...] = o[:hg]
    ob_ref[...] = o[hg:]


def _delta_scan(u, w, qd, kd, a, gl, ctx_len):
    _, b, h, t, hd = u.shape
    nc = t // CHUNK
    ncc = ctx_len // CHUNK
    hg = _pick(h, (8, 4, 2, 1))
    r6 = lambda arr: arr.reshape(2, b, h, nc, CHUNK, arr.shape[-1])
    u6, w6, q6, k6, a6 = r6(u), r6(w), r6(qd), kd, r6(a)
    gl6 = gl[:, :, :, :nc].reshape(2, b, h, nc, 1, LANES)

    def cf(ci):
        return ci

    def cb(ci):
        return jnp.where(ci < ncc, ncc - 1 - ci, nc - 1 - (ci - ncc))

    def spec(d, cmap, rows, last):
        return pl.BlockSpec((None, None, hg, None, rows, last),
                            lambda bi, gi, ci: (d, bi, gi, cmap(ci), 0, 0))

    ins, specs = [], []
    for d, cmap in ((0, cf), (1, cb)):
        ins += [u6, w6, q6, k6, a6, gl6]
        specs += [spec(d, cmap, CHUNK, hd), spec(d, cmap, CHUNK, hd), spec(d, cmap, CHUNK, hd),
                  spec(d, cmap, hd, CHUNK), spec(d, cmap, CHUNK, CHUNK), spec(d, cmap, 1, LANES)]
    o_shape = jax.ShapeDtypeStruct((b, h, nc, CHUNK, hd), F32)
    ospec = lambda cmap: pl.BlockSpec((None, hg, None, CHUNK, hd), lambda bi, gi, ci: (bi, gi, cmap(ci), 0, 0))
    of, ob = pl.pallas_call(
        functools.partial(_delta_scan_kernel, hg=hg),
        grid=(b, h // hg, nc),
        in_specs=specs,
        out_specs=[ospec(cf), ospec(cb)],
        out_shape=[o_shape, o_shape],
        scratch_shapes=[pltpu.VMEM((2 * hg, hd, hd), F32)],
        compiler_params=_cparams(3),
        name="delta_scan",
    )(*ins)
    return of.reshape(b, h, t, hd), ob.reshape(b, h, t, hd)


def _delta_out_kernel(of_ref, ob_ref, z_ref, g_ref, o_ref):
    o = of_ref[...] + ob_ref[...]
    o = o * lax.rsqrt(jnp.mean(o * o, axis=-1, keepdims=True) + NORM_EPS)
    z = z_ref[...].astype(F32)
    o_ref[...] = (o * g_ref[...] * (z * jax.nn.sigmoid(z))).astype(o_ref.dtype)


def _delta_out(of, ob, p1, o_gain, n_heads):
    b, h, t, hd = of.shape
    tt = _pick(t, (1024, 512, 256))
    return pl.pallas_call(
        _delta_out_kernel,
        grid=(b, h, t // tt),
        in_specs=[pl.BlockSpec((None, None, tt, hd), lambda bi, hi, ti: (bi, hi, ti, 0)),
                  pl.BlockSpec((None, None, tt, hd), lambda bi, hi, ti: (bi, hi, ti, 0)),
                  pl.BlockSpec((None, tt, hd), lambda bi, hi, ti: (bi, ti, 3 * n_heads + hi)),
                  pl.BlockSpec((1, hd), lambda bi, hi, ti: (0, 0))],
        out_specs=pl.BlockSpec((None, tt, hd), lambda bi, hi, ti: (bi, ti, hi)),
        out_shape=jax.ShapeDtypeStruct((b, t, h * hd), BF),
        compiler_params=_cparams(3),
        name="delta_out",
    )(of, ob, p1, o_gain.reshape(1, hd))


def _hy_conv_kernel(p_ref, w_ref, oc_ref, ox_ref, pad_ref, *, t_len, ctx_len):
    tt = TOKEN_TILE
    z8 = jnp.zeros((8, pad_ref.shape[1]), F32)
    pad_ref[pl.ds(0, 8), :] = z8
    pad_ref[pl.ds(8 + t_len, 8), :] = z8
    pad_ref[pl.ds(8, t_len), :] = p_ref[...].astype(F32)
    w0, w1, w2 = w_ref[0:1, :], w_ref[1:2, :], w_ref[2:3, :]
    for ti in range(t_len // tt):
        t0 = ti * tt
        r = lax.broadcasted_iota(jnp.int32, (tt, 1), 0) + t0
        left = pad_ref[pl.ds(8 + t0 - 1, tt), :]
        mid = pad_ref[pl.ds(8 + t0, tt), :]
        right = pad_ref[pl.ds(8 + t0 + 1, tt), :]
        ok_l = (r != 0) & (r != ctx_len)
        ok_r = (r != ctx_len - 1) & (r != t_len - 1)
        y = jnp.where(ok_l, left, 0.0) * w0 + mid * w1 + jnp.where(ok_r, right, 0.0) * w2
        if t0 < ctx_len:
            oc_ref[pl.ds(t0, tt), :] = y.astype(oc_ref.dtype)
        else:
            ox_ref[pl.ds(t0 - ctx_len, tt), :] = y.astype(ox_ref.dtype)


def _hy_conv(p_tail, hy_conv_w, ctx_len, width3):
    b, t, _ = p_tail.shape
    tc = _pick(width3, (512, 256, 128))
    return pl.pallas_call(
        functools.partial(_hy_conv_kernel, t_len=t, ctx_len=ctx_len),
        grid=(b, width3 // tc),
        in_specs=[pl.BlockSpec((None, t, tc), lambda bi, j: (bi, 0, j)),
                  pl.BlockSpec((3, tc), lambda bi, j: (0, j))],
        out_specs=[pl.BlockSpec((None, ctx_len, tc), lambda bi, j: (bi, 0, j)),
                   pl.BlockSpec((None, t - ctx_len, tc), lambda bi, j: (bi, 0, j))],
        out_shape=[jax.ShapeDtypeStruct((b, ctx_len, width3), BF),
                   jax.ShapeDtypeStruct((b, t - ctx_len, width3), BF)],
        scratch_shapes=[pltpu.VMEM((t + 16, tc), F32)],
        compiler_params=_cparams(2),
        name="hyena_short_conv",
    )(p_tail, hy_conv_w)


def _hy_filter_kernel(ft_ref, w1_ref, b1_ref, f1_ref, w2_ref, b2_ref, f2_ref, w3_ref, dec_ref, o_ref):
    part = pl.program_id(0)
    feats = ft_ref[...]
    hid = jnp.sin(f1_ref[...] * (_dot3(feats, w1_ref[...]) + b1_ref[...]))
    hid = jnp.sin(f2_ref[...] * (_dot3(hid, w2_ref[...]) + b2_ref[...]))
    filt = _dot3(hid, w3_ref[...])
    tpos = feats[:, 0:1]
    val = filt * jnp.exp(-tpos * jnp.abs(dec_ref[...]))
    row = lax.broadcasted_iota(jnp.int32, (val.shape[0], 1), 0) + pl.program_id(2) * val.shape[0]
    val = jnp.where((part == 1) & (row == 0), 0.0, val)
    o_ref[...] = val.astype(o_ref.dtype)


def _features(length, n_emb, k_pad):
    t = jnp.linspace(0.0, 1.0, length, dtype=F32)
    n_bands = (n_emb - 1) // 2
    omega = 2.0 * math.pi * jnp.arange(length, dtype=F32) / length
    bands = jnp.linspace(1e-4, n_bands - 1, n_bands, dtype=F32)
    ang = omega[:, None] * bands[None, :]
    feats = jnp.concatenate([t[:, None], jnp.cos(ang), -jnp.sin(ang)], axis=-1)
    feats = jnp.pad(feats, ((0, 0), (0, k_pad - n_emb)))
    shifted = jnp.concatenate([jnp.zeros((1, k_pad), F32), feats[:-1]], axis=0)
    return jnp.stack([feats, shifted])


def _hy_filter_stack(length, w1, b1, f1, w2, b2, f2, w3, decay, hyw):
    n_emb, fh = w1.shape
    k_pad = 64 if n_emb <= 64 else _pick(n_emb, (128,))
    feats = _features(length, n_emb, k_pad)
    w1p = jnp.pad(w1, ((0, k_pad - n_emb), (0, 0)))
    tr = _pick(length, (256, 128))
    tc = _pick(hyw, (512, 256, 128))
    nj = hyw // tc
    ni = length // tr
    return pl.pallas_call(
        _hy_filter_kernel,
        grid=(2, 2, ni, nj),
        in_specs=[pl.BlockSpec((None, tr, k_pad), lambda p, c, i, j: (p, i, 0)),
                  pl.BlockSpec((k_pad, fh), lambda p, c, i, j: (0, 0)),
                  pl.BlockSpec((1, fh), lambda p, c, i, j: (0, 0)),
                  pl.BlockSpec((1, fh), lambda p, c, i, j: (0, 0)),
                  pl.BlockSpec((fh, fh), lambda p, c, i, j: (0, 0)),
                  pl.BlockSpec((1, fh), lambda p, c, i, j: (0, 0)),
                  pl.BlockSpec((1, fh), lambda p, c, i, j: (0, 0)),
                  pl.BlockSpec((fh, tc), lambda p, c, i, j: (0, (2 * c + p) * nj + j)),
                  pl.BlockSpec((1, tc), lambda p, c, i, j: (0, (2 * c + p) * nj + j))],
        out_specs=pl.BlockSpec((tr, tc), lambda p, c, i, j: (p * ni + i, c * nj + j)),
        out_shape=jax.ShapeDtypeStruct((2 * length, 2 * hyw), BF),
        compiler_params=_cparams(4),
        name="hyena_filters",
    )(feats, w1p, b1.reshape(1, fh), f1.reshape(1, fh), w2, b2.reshape(1, fh), f2.reshape(1, fh),
      w3, decay.reshape(1, HY_N_FILT * hyw))


def _dft_tables(length, tile):
    n = 2 * length
    half = tile // 2
    rows = jnp.arange(n, dtype=jnp.int32)
    tile_i, r = rows // tile, rows % tile
    is_im = r >= half
    k = tile_i * half + jnp.where(is_im, r - half, r)
    t = jnp.arange(length, dtype=jnp.int32)
    ang = (2.0 * math.pi / n) * ((k[:, None] * t[None, :]) % n).astype(F32)
    nyq = jnp.where(t % 2 == 0, 1.0, -1.0).astype(F32)[None, :]
    is_nyq = (is_im & (k == 0))[:, None]
    fwd = jnp.where(is_im[:, None], -jnp.sin(ang), jnp.cos(ang))
    fwd = jnp.where(is_nyq, nyq, fwd)
    sgn = jnp.where(is_im[:, None] & ~is_nyq, -1.0, 1.0)
    fker = jnp.concatenate([fwd, fwd * sgn], axis=1)
    scale = jnp.where(k == 0, 1.0 / n, 2.0 / n)[None, :]
    inv = jnp.where(is_im[None, :], -jnp.sin(ang.T), jnp.cos(ang.T))
    inv = jnp.where(is_nyq.T, nyq.T, inv) * scale
    return fwd.astype(BF), fker.astype(BF), inv.astype(BF)


def _hy_long_convs(u_seq, kstack, hy_skip, hyw):
    b, ls, _ = u_seq.shape
    n = 2 * ls
    tmf = _pick(n, (512,))
    half = tmf // 2
    tn = _pick(hyw, (512, 256, 128))
    nj = hyw // tn
    fwd, fker, inv = _dft_tables(ls, tmf)

    (hspec,) = _matmul(
        fker, kstack,
        pl.BlockSpec((tmf, n), lambda g, j, i: (i, 0)),
        pl.BlockSpec((n, tn), lambda g, j, i: (0, j)),
        (1, 2 * nj, n // tmf),
        [(jax.ShapeDtypeStruct((n, 2 * hyw), F32), pl.BlockSpec((tmf, tn), lambda g, j, i: (i, j)))],
        _ep_store(F32), name="hyena_filter_dft")

    def spectrum_product(acc, extras, outs):
        hs = extras[0][...]
        re, im = acc[:half], acc[half:]
        hre, him = hs[:half], hs[half:]
        first = (lax.broadcasted_iota(jnp.int32, (half, 1), 0) == 0) & (pl.program_id(2) == 0)
        yre = jnp.where(first, re * hre, re * hre - im * him)
        yim = jnp.where(first, im * him, re * him + im * hre)
        outs[0][pl.ds(0, half), :] = yre.astype(BF)
        outs[0][pl.ds(half, half), :] = yim.astype(BF)

    tmi = _pick(ls, (TOKEN_TILE,))
    skip3 = hy_skip.reshape(2, 1, hyw)

    def conv(z_arr, z_off, conv_idx, mul_off):
        (spec,) = _matmul(
            fwd, z_arr,
            pl.BlockSpec((tmf, ls), lambda g, j, i: (i, 0)),
            pl.BlockSpec((None, ls, tn), lambda g, j, i: (g, 0, z_off + j)),
            (b, nj, n // tmf),
            [(jax.ShapeDtypeStruct((b, n, hyw), BF), pl.BlockSpec((None, tmf, tn), lambda g, j, i: (g, i, j)))],
            spectrum_product,
            extras=[(hspec, pl.BlockSpec((tmf, tn), lambda g, j, i: (i, conv_idx * nj + j)))],
            name="hyena_dft_fwd")

        def finish(acc, extras, outs):
            z = extras[0][...].astype(F32)
            xm = extras[2][...].astype(F32)
            outs[0][...] = (xm * (acc + z * extras[1][...])).astype(BF)

        (y,) = _matmul(
            inv, spec,
            pl.BlockSpec((tmi, n), lambda g, j, i: (i, 0)),
            pl.BlockSpec((None, n, tn), lambda g, j, i: (g, 0, j)),
            (b, nj, ls // tmi),
            [(jax.ShapeDtypeStruct((b, ls, hyw), BF), pl.BlockSpec((None, tmi, tn), lambda g, j, i: (g, i, j)))],
            finish,
            extras=[(z_arr, pl.BlockSpec((None, tmi, tn), lambda g, j, i: (g, i, z_off + j))),
                    (skip3, pl.BlockSpec((None, 1, tn), lambda g, j, i: (conv_idx, 0, j))),
                    (u_seq, pl.BlockSpec((None, tmi, tn), lambda g, j, i: (g, i, mul_off * nj + j)))],
            name="hyena_dft_inv")
        return y

    z2 = conv(u_seq, 0, 0, 1)
    return conv(z2, 0, 1, 2)


def _moe_up_kernel(x_ref, wg_ref, wu_ref, comb_ref, o_ref, wgb, wub):
    @pl.when(pl.program_id(1) == 0)
    def _():
        wgb[...] = wg_ref[...].astype(BF)
        wub[...] = wu_ref[...].astype(BF)
    x = x_ref[...]
    comb = comb_ref[...]
    lane = lax.broadcasted_iota(jnp.int32, comb.shape, 1)
    parts = []
    for s in range(wgb.shape[0]):
        g = jnp.dot(x, wgb[s], preferred_element_type=F32)
        u = jnp.dot(x, wub[s], preferred_element_type=F32)
        e = pl.program_id(0) * wgb.shape[0] + s
        scale = jnp.sum(jnp.where(lane == e, comb, 0.0), axis=1, keepdims=True)
        parts.append((g * jax.nn.sigmoid(g) * u * scale).astype(o_ref.dtype))
    o_ref[...] = jnp.concatenate(parts, axis=1)


MOE_EXPERTS_PER_STEP = 2


def _moe_up(u, w_gate, w_up, comb, l):
    m, d = u.shape
    _, ne, _, f = w_gate.shape
    tm = _pick(m, (512, 256))
    es = MOE_EXPERTS_PER_STEP
    wspec = pl.BlockSpec((None, es, d, f), lambda e, i: (l, e, 0, 0))
    return pl.pallas_call(
        _moe_up_kernel,
        grid=(ne // es, m // tm),
        in_specs=[pl.BlockSpec((tm, d), lambda e, i: (i, 0)), wspec, wspec,
                  pl.BlockSpec((tm, LANES), lambda e, i: (i, 0))],
        out_specs=pl.BlockSpec((tm, es * f), lambda e, i: (i, e)),
        out_shape=jax.ShapeDtypeStruct((m, ne * f), BF),
        scratch_shapes=[pltpu.VMEM((es, d, f), BF), pltpu.VMEM((es, d, f), BF)],
        compiler_params=_cparams(2),
        name="moe_up",
    )(u, w_gate, w_up, comb)


def _row_gate(gt_ref, tm, t_len, ctx_len, n_batch):
    r = lax.broadcasted_iota(jnp.int32, (tm, 1), 0) + pl.program_id(2) * tm
    gate = jnp.zeros((tm, gt_ref.shape[1]), F32)
    for bi in range(n_batch):
        in_b = (r >= bi * t_len) & (r < (bi + 1) * t_len)
        is_c = r < bi * t_len + ctx_len
        gate = jnp.where(in_b & is_c, gt_ref[0:1, :], gate)
        gate = jnp.where(in_b & ~is_c, gt_ref[1 + bi:2 + bi, :], gate)
    return gate


def _final_norm_kernel(h_ref, g_ref, o_ref):
    xf = h_ref[...]
    o_ref[...] = xf * lax.rsqrt(jnp.mean(xf * xf, axis=-1, keepdims=True) + NORM_EPS) * g_ref[...]


def _final_norm(h, gain, n_batch, t_len, ctx_len):
    m, d = h.shape
    tt = TOKEN_TILE
    n_t, n_c = t_len // tt, ctx_len // tt
    n_x = n_t - n_c
    return pl.pallas_call(
        _final_norm_kernel,
        grid=(n_batch, n_x),
        in_specs=[pl.BlockSpec((tt, d), lambda bi, i: (bi * n_t + n_c + i, 0)),
                  pl.BlockSpec((1, d), lambda bi, i: (0, 0))],
        out_specs=pl.BlockSpec((None, tt, d), lambda bi, i: (bi, i, 0)),
        out_shape=jax.ShapeDtypeStruct((n_batch, t_len - ctx_len, d), F32),
        compiler_params=_cparams(2),
        name="final_norm",
    )(h, gain.reshape(1, d))


def kernel(x, c, ctx, c_ctx, ada_down, ada_up, ada_bias, norm_mix, norm_ffn, w_in, qkv_conv, a_log, dt_bias, o_norm, hy_conv, hy_w1, hy_b1, hy_f1, hy_w2, hy_b2, hy_f2, hy_w3, hy_decay, hy_skip, w_br_a, w_br_b, w_out, router_group, router_group_bias, router_expert, router_expert_bias, exp_gate, exp_up, exp_down, final_norm):
    n_batch, seq, d = x.shape
    ctx_len = ctx.shape[1]
    t_len = ctx_len + seq
    m = n_batch * t_len
    depth = w_in.shape[0]
    n_heads = a_log.shape[2]
    a_width = n_heads * LANES
    hyw = hy_skip.shape[2]
    n_exp = exp_gate.shape[1]
    dg_off = 4 * a_width
    hy_off = dg_off + 4 * n_heads
    tail_w = 3 * hyw + 2 * d
    n_t, n_c = t_len // TOKEN_TILE, ctx_len // TOKEN_TILE
    nc = t_len // CHUNK
    ncp = -(-nc // 8) * 8

    h = jnp.concatenate([ctx, x], axis=1).reshape(m, d)

    conds = jnp.zeros((8, d), F32).at[0].set(c_ctx).at[1:1 + n_batch].set(c)
    mod_all = _adaln_all(conds, ada_down, ada_up, ada_bias)
    mod_all = mod_all.reshape(depth, 8, 6, d)[:, :1 + n_batch]
    mod_all = jnp.pad(mod_all, ((0, 0), (0, 0), (0, 2), (0, 0)))

    tm = _pick(m, (512, 256))
    tn = 512

    def gate_table(mod_l, idx):
        return jnp.pad(mod_l[:, idx, :], ((0, 8 - (1 + n_batch)), (0, 0)))

    def residual_ep(acc, extras, outs):
        outs[0][...] = extras[0][...] + _row_gate(extras[1], acc.shape[0], t_len, ctx_len, n_batch) * acc

    for l in range(depth):
        mod_l = mod_all[l]
        u = _norm_mod(h, norm_mix[l], mod_l, 0, n_t, n_c)
        (p1,) = _matmul(
            u, w_in,
            pl.BlockSpec((tm, d), lambda g, j, i: (i, 0)),
            pl.BlockSpec((None, d, tn), lambda g, j, i, l=l: (l, 0, j)),
            (1, dg_off // tn, m // tm),
            [(jax.ShapeDtypeStruct((m, dg_off), BF), pl.BlockSpec((tm, tn), lambda g, j, i: (i, j)))],
            _ep_store(BF), name="proj_qkvz")
        w_tail = w_in[l, :, hy_off:]
        (p_tail,) = _matmul(
            u, w_tail,
            pl.BlockSpec((tm, d), lambda g, j, i: (i, 0)),
            pl.BlockSpec((d, tn), lambda g, j, i: (0, j)),
            (1, tail_w // tn, m // tm),
            [(jax.ShapeDtypeStruct((m, tail_w), BF), pl.BlockSpec((tm, tn), lambda g, j, i: (i, j)))],
            _ep_store(BF), name="proj_tail")
        w_dg_t = w_in[l, :, dg_off:hy_off].T
        a_exp = jnp.concatenate([jnp.exp(a_log[l].reshape(-1)), jnp.zeros((2 * n_heads,), F32)])[:, None]
        dt_b = jnp.concatenate([dt_bias[l].reshape(-1), jnp.zeros((2 * n_heads,), F32)])[:, None]
        gates = _gates(u, w_dg_t, a_exp, dt_b)

        g5 = gates.reshape(4, n_heads, n_batch, nc, CHUNK).transpose(2, 1, 0, 3, 4)
        g_rows = jnp.pad(g5, ((0, 0), (0, 0), (0, 0), (0, ncp - nc), (0, 0)))
        g_cols = jnp.pad(jnp.swapaxes(g5, 3, 4), ((0, 0), (0, 0), (0, 0), (0, 0), (0, LANES - nc)))

        p1_3 = p1.reshape(n_batch, t_len, dg_off)
        conv9 = qkv_conv[l].reshape(9, 3 * a_width)
        du, dw, dq, dk, da, dgl = _delta_prep(p1_3, conv9, g_rows, g_cols, n_heads, ctx_len)
        o_f, o_b = _delta_scan(du, dw, dq, dk, da, dgl, ctx_len)
        o_a = _delta_out(o_f, o_b, p1_3, o_norm[l], n_heads).reshape(m, a_width)

        p_tail3 = p_tail.reshape(n_batch, t_len, tail_w)
        u_c, u_x = _hy_conv(p_tail3, hy_conv[l], ctx_len, 3 * hyw)
        filt_args = (hy_w1[l], hy_b1[l], hy_f1[l], hy_w2[l], hy_b2[l], hy_f2[l], hy_w3[l], hy_decay[l])
        y_x = _hy_long_convs(u_x, _hy_filter_stack(seq, *filt_args, hyw), hy_skip[l], hyw)
        y_c = _hy_long_convs(u_c, _hy_filter_stack(ctx_len, *filt_args, hyw), hy_skip[l], hyw)
        y_b = jnp.concatenate([y_c, y_x], axis=1).reshape(m, hyw)

        ga_blk = (3 * hyw) // tn
        gb_blk = (3 * hyw + d) // tn

        def gated_ep(acc, extras, outs):
            outs[0][...] = jax.nn.sigmoid(extras[0][...].astype(F32)) * acc

        (br,) = _matmul(
            o_a, w_br_a,
            pl.BlockSpec((tm, a_width), lambda g, j, i: (i, 0)),
            pl.BlockSpec((None, a_width, tn), lambda g, j, i, l=l: (l, 0, j)),
            (1, d // tn, m // tm),
            [(jax.ShapeDtypeStruct((m, d), F32), pl.BlockSpec((tm, tn), lambda g, j, i: (i, j)))],
            gated_ep,
            extras=[(p_tail, pl.BlockSpec((tm, tn), lambda g, j, i: (i, ga_blk + j)))],
            name="branch_a")

        def merge_ep(acc, extras, outs):
            outs[0][...] = (extras[1][...] + jax.nn.sigmoid(extras[0][...].astype(F32)) * acc).astype(BF)

        (merged,) = _matmul(
            y_b, w_br_b,
            pl.BlockSpec((tm, hyw), lambda g, j, i: (i, 0)),
            pl.BlockSpec((None, hyw, tn), lambda g, j, i, l=l: (l, 0, j)),
            (1, d // tn, m // tm),
            [(jax.ShapeDtypeStruct((m, d), BF), pl.BlockSpec((tm, tn), lambda g, j, i: (i, j)))],
            merge_ep,
            extras=[(p_tail, pl.BlockSpec((tm, tn), lambda g, j, i: (i, gb_blk + j))),
                    (br, pl.BlockSpec((tm, tn), lambda g, j, i: (i, j)))],
            name="branch_b_merge")

        (h,) = _matmul(
            merged, w_out,
            pl.BlockSpec((tm, d), lambda g, j, i: (i, 0)),
            pl.BlockSpec((None, d, tn), lambda g, j, i, l=l: (l, 0, j)),
            (1, d // tn, m // tm),
            [(jax.ShapeDtypeStruct((m, d), F32), pl.BlockSpec((tm, tn), lambda g, j, i: (i, j)))],
            residual_ep,
            extras=[(h, pl.BlockSpec((tm, tn), lambda g, j, i: (i, j))),
                    (gate_table(mod_l, 2), pl.BlockSpec((8, tn), lambda g, j, i: (0, j)))],
            name="out_proj")

        w_router = jnp.pad(jnp.concatenate([router_expert[l], router_group[l]], axis=1),
                           ((0, 0), (0, LANES - n_exp - N_GROUPS)))
        b_router = jnp.pad(jnp.concatenate([router_expert_bias[l], router_group_bias[l]]),
                           (0, LANES - n_exp - N_GROUPS)).reshape(1, LANES)
        u2, comb = _norm_mod_router(h, norm_ffn[l], mod_l, 3, n_t, n_c, w_router, b_router, n_exp, N_GROUPS)
        hid = _moe_up(u2, exp_gate, exp_up, comb, l)
        kf = hid.shape[1]
        tnd = _pick(d, (256, 128))
        (h,) = _matmul(
            hid, exp_down.reshape(depth, kf, d),
            pl.BlockSpec((tm, kf), lambda g, j, i: (i, 0)),
            pl.BlockSpec((None, kf, tnd), lambda g, j, i, l=l: (l, 0, j)),
            (1, d // tnd, m // tm),
            [(jax.ShapeDtypeStruct((m, d), F32), pl.BlockSpec((tm, tnd), lambda g, j, i: (i, j)))],
            residual_ep,
            extras=[(h, pl.BlockSpec((tm, tnd), lambda g, j, i: (i, j))),
                    (gate_table(mod_l, 5), pl.BlockSpec((8, tnd), lambda g, j, i: (0, j)))],
            name="moe_down")

    return _final_norm(h, final_norm, n_batch, t_len, ctx_len)
```

```python
import functools
import math

import jax
import jax.numpy as jnp
from jax import lax
from jax.experimental import pallas as pl
from jax.experimental.pallas import tpu as pltpu

F32 = jnp.float32
BF = jnp.bfloat16

GRID_W = 64
CHUNK = 64
N_GROUPS = 4
EXPERTS_PER_GROUP = 8
HY_N_FILT = 4
NORM_EPS = 1e-6
TOKEN_TILE = 256
LANES = 128
CONV_PAD = 72
VMEM_LIMIT = 56 * 1024 * 1024


def _cparams(n_grid):
    return pltpu.CompilerParams(dimension_semantics=("arbitrary",) * n_grid,
                                vmem_limit_bytes=VMEM_LIMIT)


def _split_bf16(a):
    hi = a.astype(BF)
    lo = (a - hi.astype(F32)).astype(BF)
    return hi, lo


def _dot3(a, b):
    ah, al = _split_bf16(a)
    bh, bl = _split_bf16(b)
    d = functools.partial(jnp.dot, preferred_element_type=F32)
    return d(ah, bh) + (d(ah, bl) + d(al, bh))


def _dotx(a, b):
    return jnp.dot(a, b, precision=lax.Precision.HIGHEST, preferred_element_type=F32)


def _bdot(a, b):
    return jnp.dot(a.astype(BF), b.astype(BF), preferred_element_type=F32)


def _mm_kernel(*refs, n_extra, n_out, cast_w, lane_shift, prologue, epilogue):
    x_ref, w_ref = refs[0], refs[1]
    n_w = 3 if lane_shift else 2
    extras = refs[n_w:n_w + n_extra]
    outs = refs[n_w + n_extra:n_w + n_extra + n_out]
    scratch = refs[n_w + n_extra + n_out:]
    if cast_w:
        wbf = scratch[0]

        @pl.when(pl.program_id(2) == 0)
        def _():
            k = w_ref.shape[0]
            step = 512 if k % 512 == 0 else k

            def load(r0):
                wa = w_ref[pl.ds(r0, step), :]
                if lane_shift:
                    wa = jnp.concatenate([wa[:, lane_shift:], refs[2][pl.ds(r0, step), :lane_shift]], axis=1)
                return wa.astype(BF)

            if k == step:
                wbf[...] = load(0)
            else:
                def it(r, c):
                    r0 = pl.multiple_of(r * step, step)
                    wbf[pl.ds(r0, step), :] = load(r0)
                    return c
                lax.fori_loop(0, k // step, it, 0)
        w = wbf[...]
    else:
        w = w_ref[...]
    x = x_ref[...]
    if prologue is not None:
        x = prologue(x)
    if x.dtype != BF:
        x = x.astype(BF)
    acc = jnp.dot(x, w, preferred_element_type=F32)
    epilogue(acc, extras, outs)


def _matmul(x, w, x_spec, w_spec, grid, outs, epilogue, extras=(), prologue=None, name=None,
            w_next_spec=None, lane_shift=0):
    cast_w = w.dtype != BF
    assert cast_w or not lane_shift
    kw, tn = [d for d in w_spec.block_shape if d is not None]
    scratch = [pltpu.VMEM((kw, tn), BF)] if cast_w else []
    body = functools.partial(_mm_kernel, n_extra=len(extras), n_out=len(outs), cast_w=cast_w,
                             lane_shift=lane_shift, prologue=prologue, epilogue=epilogue)
    w_ops, w_specs = ([w, w], [w_spec, w_next_spec]) if lane_shift else ([w], [w_spec])
    res = pl.pallas_call(
        body,
        grid=grid,
        in_specs=[x_spec] + w_specs + [s for _, s in extras],
        out_specs=[s for _, s in outs],
        out_shape=[o for o, _ in outs],
        scratch_shapes=scratch,
        compiler_params=_cparams(3),
        name=name,
    )(x, *w_ops, *[a for a, _ in extras])
    return res


def _ep_store(dtype):
    def ep(acc, extras, outs):
        outs[0][...] = acc.astype(dtype)
    return ep


def _pick(n, prefs):
    for p in prefs:
        if n % p == 0:
            return p
    return n


def _adaln_all(conds, ada_down, ada_up, ada_bias):
    nl, d, r = ada_down.shape
    n6 = ada_up.shape[2]
    (t,) = _matmul(
        conds, ada_down,
        pl.BlockSpec((8, d), lambda g, j, i: (0, 0)),
        pl.BlockSpec((None, d, r), lambda g, j, i: (g, 0, 0)),
        (nl, 1, 1),
        [(jax.ShapeDtypeStruct((nl, 8, r), F32), pl.BlockSpec((None, 8, r), lambda g, j, i: (g, 0, 0)))],
        _ep_store(F32), prologue=lambda v: v * jax.nn.sigmoid(v), name="adaln_down")
    tn = _pick(n6, (2048, 1024, 512, 256, 128))

    def ep(acc, extras, outs):
        outs[0][...] = acc + extras[0][...]

    (mod,) = _matmul(
        t, ada_up,
        pl.BlockSpec((None, 8, r), lambda g, j, i: (g, 0, 0)),
        pl.BlockSpec((None, r, tn), lambda g, j, i: (g, 0, j)),
        (nl, n6 // tn, 1),
        [(jax.ShapeDtypeStruct((nl, 8, n6), F32), pl.BlockSpec((None, 8, tn), lambda g, j, i: (g, 0, j)))],
        ep, extras=[(ada_bias.reshape(nl, 1, n6), pl.BlockSpec((None, 1, tn), lambda g, j, i: (g, 0, j)))],
        name="adaln_up")
    return mod


def _mod_row_map(n_t, n_c):
    def m(i):
        return (jnp.where(i % n_t < n_c, 0, 1 + i // n_t), 0, 0)
    return m


def _norm_mod_kernel(h_ref, g_ref, mod_ref, u_ref, *, off):
    xf = h_ref[...]
    y = xf * lax.rsqrt(jnp.mean(xf * xf, axis=-1, keepdims=True) + NORM_EPS) * g_ref[...]
    sh = mod_ref[off:off + 1, :]
    sc = mod_ref[off + 1:off + 2, :]
    u_ref[...] = (y * (1.0 + sc) + sh).astype(u_ref.dtype)


def _norm_mod(h, gain, mod_l, off, n_t, n_c):
    m, d = h.shape
    tt = TOKEN_TILE
    return pl.pallas_call(
        functools.partial(_norm_mod_kernel, off=off),
        grid=(m // tt,),
        in_specs=[pl.BlockSpec((tt, d), lambda i: (i, 0)),
                  pl.BlockSpec((1, d), lambda i: (0, 0)),
                  pl.BlockSpec((None, 8, d), _mod_row_map(n_t, n_c))],
        out_specs=pl.BlockSpec((tt, d), lambda i: (i, 0)),
        out_shape=jax.ShapeDtypeStruct((m, d), BF),
        compiler_params=_cparams(1),
        name="norm_mod",
    )(h, gain.reshape(1, d), mod_l)


def _router_kernel(h_ref, g_ref, mod_ref, wr_ref, br_ref, u_ref, comb_ref, *, off, n_exp, n_grp):
    xf = h_ref[...]
    y = xf * lax.rsqrt(jnp.mean(xf * xf, axis=-1, keepdims=True) + NORM_EPS) * g_ref[...]
    sh = mod_ref[off:off + 1, :]
    sc = mod_ref[off + 1:off + 2, :]
    u = y * (1.0 + sc) + sh
    u_ref[...] = u.astype(u_ref.dtype)
    logits = _dot3(u, wr_ref[...]) + br_ref[...]
    lane = lax.broadcasted_iota(jnp.int32, logits.shape, 1)
    neg = jnp.float32(-jnp.inf)
    big = jnp.int32(1 << 20)
    is_g = (lane >= n_exp) & (lane < n_exp + n_grp)
    lg = jnp.where(is_g, logits, neg)
    eg = jnp.exp(lg - jnp.max(lg, axis=1, keepdims=True))
    pg = eg / jnp.sum(eg, axis=1, keepdims=True)
    p_top = jnp.max(pg, axis=1, keepdims=True)
    g_idx = jnp.min(jnp.where(is_g & (pg == p_top), lane, big), axis=1, keepdims=True) - n_exp
    per = n_exp // n_grp
    in_grp = (lane >= g_idx * per) & (lane < (g_idx + 1) * per)
    le = jnp.where(in_grp, logits, neg)
    v1 = jnp.max(le, axis=1, keepdims=True)
    i1 = jnp.min(jnp.where(le == v1, lane, big), axis=1, keepdims=True)
    le2 = jnp.where(lane == i1, neg, le)
    v2 = jnp.max(le2, axis=1, keepdims=True)
    i2 = jnp.min(jnp.where(le2 == v2, lane, big), axis=1, keepdims=True)
    e2 = jnp.exp(v2 - v1)
    w1 = p_top / (1.0 + e2)
    w2 = p_top * e2 / (1.0 + e2)
    comb_ref[...] = jnp.where(lane == i1, w1, 0.0) + jnp.where(lane == i2, w2, 0.0)


def _norm_mod_router(h, gain, mod_l, off, n_t, n_c, w_router, b_router, n_exp, n_grp):
    m, d = h.shape
    tt = TOKEN_TILE
    return pl.pallas_call(
        functools.partial(_router_kernel, off=off, n_exp=n_exp, n_grp=n_grp),
        grid=(m // tt,),
        in_specs=[pl.BlockSpec((tt, d), lambda i: (i, 0)),
                  pl.BlockSpec((1, d), lambda i: (0, 0)),
                  pl.BlockSpec((None, 8, d), _mod_row_map(n_t, n_c)),
                  pl.BlockSpec((d, LANES), lambda i: (0, 0)),
                  pl.BlockSpec((1, LANES), lambda i: (0, 0))],
        out_specs=[pl.BlockSpec((tt, d), lambda i: (i, 0)),
                   pl.BlockSpec((tt, LANES), lambda i: (i, 0))],
        out_shape=[jax.ShapeDtypeStruct((m, d), BF), jax.ShapeDtypeStruct((m, LANES), F32)],
        compiler_params=_cparams(1),
        name="norm_mod_router",
    )(h, gain.reshape(1, d), mod_l, w_router, b_router)


def _gates_kernel(w_ref, u_ref, aexp_ref, dtb_ref, o_ref, *, n_decay):
    acc = lax.dot_general(w_ref[...].astype(BF), u_ref[...], (((1,), (1,)), ((), ())),
                          preferred_element_type=F32)
    row = lax.broadcasted_iota(jnp.int32, acc.shape, 0)
    z = acc + dtb_ref[...]
    softplus = jnp.maximum(z, 0.0) + jnp.log(1.0 + jnp.exp(-jnp.abs(z)))
    g = -aexp_ref[...] * softplus
    o_ref[...] = jnp.where(row < n_decay, g, jax.nn.sigmoid(acc))


def _gates(u, w_dg_t, a_exp, dt_b):
    m, d = u.shape
    r = w_dg_t.shape[0]
    tm = _pick(m, (512, 256))
    return pl.pallas_call(
        functools.partial(_gates_kernel, n_decay=r // 2),
        grid=(m // tm,),
        in_specs=[pl.BlockSpec((r, d), lambda i: (0, 0)),
                  pl.BlockSpec((tm, d), lambda i: (i, 0)),
                  pl.BlockSpec((r, 1), lambda i: (0, 0)),
                  pl.BlockSpec((r, 1), lambda i: (0, 0))],
        out_specs=pl.BlockSpec((r, tm), lambda i: (0, i)),
        out_shape=jax.ShapeDtypeStruct((r, m), F32),
        compiler_params=_cparams(1),
        name="delta_gates",
    )(w_dg_t, u, a_exp, dt_b)


def _conv_silu_tile(pad_ref, cw_ref, t0, n, seq_start, seq_len, width):
    one_row = width == seq_len
    r = lax.broadcasted_iota(jnp.int32, (n, 1), 0) + (t0 - seq_start)
    col = lax.rem(r, width)
    acc = jnp.zeros((n, LANES), F32)
    for dj in (-1, 0, 1):
        part = jnp.zeros((n, LANES), F32)
        for di in (-1, 0, 1):
            if one_row and di != 0:
                continue
            off = di * width + dj
            tap = pad_ref[pl.ds(CONV_PAD + t0 + off, n), :]
            wrow = cw_ref[(di + 1) * 3 + (dj + 1):(di + 1) * 3 + (dj + 1) + 1, :]
            ok = (r + off >= 0) & (r + off < seq_len)
            part = part + jnp.where(ok, tap, 0.0) * wrow
        ok_c = (col + dj >= 0) & (col + dj < width)
        acc = acc + jnp.where(ok_c, part, 0.0)
    return acc * jax.nn.sigmoid(acc)


def _bmm(a, b):
    return jnp.einsum('gik,gkj->gij', a.astype(BF), b.astype(BF), preferred_element_type=F32)


def _bmm_nt(a, b):
    return jnp.einsum('gik,gjk->gij', a.astype(BF), b.astype(BF), preferred_element_type=F32)


def _neumann_inverse(lmat, eye):
    x = -lmat
    p = eye + x
    n = lmat.shape[-1]
    k = 1
    while 2 * k < n:
        x = _bmm(x, x)
        p = p + _bmm(p, x)
        k *= 2
    return p


def _delta_prep_kernel(pq_ref, pk_ref, pv_ref, cq_ref, ck_ref, cv_ref, g_ref, gt_ref,
                       u_ref, w_ref, qd_ref, kd_ref, a_ref, gl_ref,
                       pad_ref, q_s, k_s, v_s, gcr_s, gct_s,
                       *, t_len, ctx_len, n_chunks, group):
    hd = q_s.shape[1]
    tt = TOKEN_TILE
    d = pl.program_id(2)

    @pl.when(d == 0)
    def _():
        zpad = jnp.zeros((CONV_PAD, LANES), F32)
        pad_ref[pl.ds(0, CONV_PAD), :] = zpad
        pad_ref[pl.ds(CONV_PAD + t_len, CONV_PAD), :] = zpad
        for src, cw, dst, kind in ((pq_ref, cq_ref, q_s, "q"), (pk_ref, ck_ref, k_s, "k"), (pv_ref, cv_ref, v_s, "v")):
            pad_ref[pl.ds(CONV_PAD, t_len), :] = src[...].astype(F32)
            for ti in range(t_len // tt):
                t0 = ti * tt
                if t0 < ctx_len:
                    a = _conv_silu_tile(pad_ref, cw, t0, tt, 0, ctx_len, ctx_len)
                else:
                    a = _conv_silu_tile(pad_ref, cw, t0, tt, ctx_len, t_len - ctx_len, GRID_W)
                if kind != "v":
                    a = a * lax.rsqrt(jnp.sum(a * a, axis=-1, keepdims=True) + NORM_EPS)
                if kind == "q":
                    a = a * (hd ** -0.5)
                dst[pl.ds(t0, tt), :] = a

    ii = lax.broadcasted_iota(jnp.int32, (CHUNK, CHUNK), 0)
    jj = lax.broadcasted_iota(jnp.int32, (CHUNK, CHUNK), 1)
    sdiff = (ii - jj) * (1 - 2 * d)
    eye = (ii == jj).astype(F32)
    incl = sdiff >= 0
    strict = sdiff > 0
    gcr_s[...] = _dotx(g_ref[d], (sdiff <= 0).astype(F32))
    gct_s[...] = _dotx(incl.astype(F32), gt_ref[d])
    gtot = jnp.where(d == 0, gcr_s[:, CHUNK - 1:CHUNK], gcr_s[:, 0:1])
    gl_ref[...] = jnp.broadcast_to(jnp.exp(gtot), gl_ref.shape)

    gc = group
    rows = gc * CHUNK
    lane_c = lax.broadcasted_iota(jnp.int32, (CHUNK, LANES), 1)

    def body(it, carry):
        c0 = it * gc
        t0 = pl.multiple_of(c0 * CHUNK, rows)
        q = q_s[pl.ds(t0, rows), :].reshape(gc, CHUNK, hd)
        k = k_s[pl.ds(t0, rows), :].reshape(gc, CHUNK, hd)
        v = v_s[pl.ds(t0, rows), :].reshape(gc, CHUNK, hd)
        gct = gct_s[...]
        bt = gt_ref[2 + d]
        gcols, bcols, grows, glasts = [], [], [], []
        for g in range(gc):
            sel = lane_c == c0 + g
            gcols.append(jnp.sum(jnp.where(sel, gct, 0.0), axis=1, keepdims=True))
            bcols.append(jnp.sum(jnp.where(sel, bt, 0.0), axis=1, keepdims=True))
            grow_g = gcr_s[pl.ds(c0 + g, 1), :]
            grows.append(grow_g)
            glasts.append(jnp.where(d == 0, grow_g[:, CHUNK - 1:CHUNK], grow_g[:, 0:1]))
        gcol = jnp.stack(gcols)
        bcol = jnp.stack(bcols)
        grow = jnp.stack(grows)
        glast = jnp.stack(glasts)
        decay = jnp.where(incl, jnp.exp(jnp.where(incl, gcol - grow, 0.0)), 0.0)
        kb = k * bcol
        lmat = jnp.where(strict, _bmm_nt(kb, k) * decay, 0.0)
        tinv = _neumann_inverse(lmat, eye[None])
        egc = jnp.exp(gcol)
        uw = _bmm(tinv, jnp.concatenate([v * bcol, kb * egc], axis=2))
        attn = (_bmm_nt(q, k) * decay).astype(BF)
        kdec = k * jnp.exp(glast - gcol)
        u_ref[pl.ds(t0, rows), :] = uw[:, :, :hd].reshape(rows, hd)
        w_ref[pl.ds(t0, rows), :] = uw[:, :, hd:].astype(BF).reshape(rows, hd)
        qd_ref[pl.ds(t0, rows), :] = (q * egc).astype(BF).reshape(rows, hd)
        a_ref[pl.ds(t0, rows), :] = attn.reshape(rows, CHUNK)
        for g in range(gc):
            kd_ref[c0 + g] = kdec[g].T.astype(BF)
        return carry

    lax.fori_loop(0, n_chunks // group, body, 0)


def _delta_prep(p1, conv_w9, g_rows, g_cols, n_heads, ctx_len):
    b, t, _ = p1.shape
    hd = LANES
    nc = t // CHUNK
    ncp = g_rows.shape[3]
    group = max(g for g in range(1, 18) if nc % g == 0)
    h = n_heads
    kern = functools.partial(_delta_prep_kernel, t_len=t, ctx_len=ctx_len, n_chunks=nc, group=group)
    tok = lambda off: pl.BlockSpec((None, t, hd), lambda bi, hi, di: (bi, 0, off + hi))
    cws = lambda off: pl.BlockSpec((9, hd), lambda bi, hi, di: (0, off + hi))
    o5 = lambda last: pl.BlockSpec((None, None, None, t, last), lambda bi, hi, di: (di, bi, hi, 0, 0))
    return pl.pallas_call(
        kern,
        grid=(b, h, 2),
        in_specs=[tok(0), tok(h), tok(2 * h), cws(0), cws(h), cws(2 * h),
                  pl.BlockSpec((None, None, 4, ncp, CHUNK), lambda bi, hi, di: (bi, hi, 0, 0, 0)),
                  pl.BlockSpec((None, None, 4, CHUNK, LANES), lambda bi, hi, di: (bi, hi, 0, 0, 0))],
        out_specs=[o5(hd), o5(hd), o5(hd),
                   pl.BlockSpec((None, None, None, nc, hd, CHUNK), lambda bi, hi, di: (di, bi, hi, 0, 0, 0)),
                   o5(CHUNK),
                   pl.BlockSpec((None, None, None, ncp, LANES), lambda bi, hi, di: (di, bi, hi, 0, 0))],
        out_shape=[jax.ShapeDtypeStruct((2, b, h, t, hd), F32),
                   jax.ShapeDtypeStruct((2, b, h, t, hd), BF),
                   jax.ShapeDtypeStruct((2, b, h, t, hd), BF),
                   jax.ShapeDtypeStruct((2, b, h, nc, hd, CHUNK), BF),
                   jax.ShapeDtypeStruct((2, b, h, t, CHUNK), BF),
                   jax.ShapeDtypeStruct((2, b, h, ncp, LANES), F32)],
        scratch_shapes=[pltpu.VMEM((t + 2 * CONV_PAD, LANES), F32),
                        pltpu.VMEM((t, hd), F32), pltpu.VMEM((t, hd), F32), pltpu.VMEM((t, hd), F32),
                        pltpu.VMEM((ncp, CHUNK), F32), pltpu.VMEM((CHUNK, LANES), F32)],
        compiler_params=_cparams(3),
        name="delta_prep",
    )(p1, p1, p1, conv_w9, conv_w9, conv_w9, g_rows, g_cols)


def _delta_scan_kernel(uf, wf, qf, kf, af, gf, ub, wb, qb, kb, ab, gb, of_ref, ob_ref, s_ref, *, hg):
    @pl.when(pl.program_id(2) == 0)
    def _():
        s_ref[...] = jnp.zeros_like(s_ref)

    cat = lambda a, b: jnp.concatenate([a[...], b[...]], axis=0)
    s = s_ref[...]
    sb = s.astype(BF)
    vn = cat(uf, ub) - _bmm(cat(wf, wb), sb)
    vnb = vn.astype(BF)
    o = _bmm(cat(qf, qb), sb) + _bmm(cat(af, ab), vnb)
    s_ref[...] = s * cat(gf, gb) + _bmm(cat(kf, kb), vnb)
    of_ref[...] = o[:hg]
    ob_ref[...] = o[hg:]


def _delta_scan(u, w, qd, kd, a, gl, ctx_len):
    _, b, h, t, hd = u.shape
    nc = t // CHUNK
    ncc = ctx_len // CHUNK
    hg = _pick(h, (8, 4, 2, 1))
    r6 = lambda arr: arr.reshape(2, b, h, nc, CHUNK, arr.shape[-1])
    u6, w6, q6, k6, a6 = r6(u), r6(w), r6(qd), kd, r6(a)
    gl6 = gl[:, :, :, :nc].reshape(2, b, h, nc, 1, LANES)

    def cf(ci):
        return ci

    def cb(ci):
        return jnp.where(ci < ncc, ncc - 1 - ci, nc - 1 - (ci - ncc))

    def spec(d, cmap, rows, last):
        return pl.BlockSpec((None, None, hg, None, rows, last),
                            lambda bi, gi, ci: (d, bi, gi, cmap(ci), 0, 0))

    ins, specs = [], []
    for d, cmap in ((0, cf), (1, cb)):
        ins += [u6, w6, q6, k6, a6, gl6]
        specs += [spec(d, cmap, CHUNK, hd), spec(d, cmap, CHUNK, hd), spec(d, cmap, CHUNK, hd),
                  spec(d, cmap, hd, CHUNK), spec(d, cmap, CHUNK, CHUNK), spec(d, cmap, 1, LANES)]
    o_shape = jax.ShapeDtypeStruct((b, h, nc, CHUNK, hd), F32)
    ospec = lambda cmap: pl.BlockSpec((None, hg, None, CHUNK, hd), lambda bi, gi, ci: (bi, gi, cmap(ci), 0, 0))
    of, ob = pl.pallas_call(
        functools.partial(_delta_scan_kernel, hg=hg),
        grid=(b, h // hg, nc),
        in_specs=specs,
        out_specs=[ospec(cf), ospec(cb)],
        out_shape=[o_shape, o_shape],
        scratch_shapes=[pltpu.VMEM((2 * hg, hd, hd), F32)],
        compiler_params=_cparams(3),
        name="delta_scan",
    )(*ins)
    return of.reshape(b, h, t, hd), ob.reshape(b, h, t, hd)


def _delta_out_kernel(of_ref, ob_ref, z_ref, g_ref, o_ref):
    o = of_ref[...] + ob_ref[...]
    o = o * lax.rsqrt(jnp.mean(o * o, axis=-1, keepdims=True) + NORM_EPS)
    z = z_ref[...].astype(F32)
    o_ref[...] = (o * g_ref[...] * (z * jax.nn.sigmoid(z))).astype(o_ref.dtype)


def _delta_out(of, ob, p1, o_gain, n_heads):
    b, h, t, hd = of.shape
    tt = _pick(t, (1024, 512, 256))
    return pl.pallas_call(
        _delta_out_kernel,
        grid=(b, h, t // tt),
        in_specs=[pl.BlockSpec((None, None, tt, hd), lambda bi, hi, ti: (bi, hi, ti, 0)),
                  pl.BlockSpec((None, None, tt, hd), lambda bi, hi, ti: (bi, hi, ti, 0)),
                  pl.BlockSpec((None, tt, hd), lambda bi, hi, ti: (bi, ti, 3 * n_heads + hi)),
                  pl.BlockSpec((1, hd), lambda bi, hi, ti: (0, 0))],
        out_specs=pl.BlockSpec((None, tt, hd), lambda bi, hi, ti: (bi, ti, hi)),
        out_shape=jax.ShapeDtypeStruct((b, t, h * hd), BF),
        compiler_params=_cparams(3),
        name="delta_out",
    )(of, ob, p1, o_gain.reshape(1, hd))


def _hy_conv_kernel(p_ref, w_ref, oc_ref, ox_ref, pad_ref, *, t_len, ctx_len):
    tt = TOKEN_TILE
    z8 = jnp.zeros((8, pad_ref.shape[1]), F32)
    pad_ref[pl.ds(0, 8), :] = z8
    pad_ref[pl.ds(8 + t_len, 8), :] = z8
    pad_ref[pl.ds(8, t_len), :] = p_ref[...].astype(F32)
    w0, w1, w2 = w_ref[0:1, :], w_ref[1:2, :], w_ref[2:3, :]
    for ti in range(t_len // tt):
        t0 = ti * tt
        r = lax.broadcasted_iota(jnp.int32, (tt, 1), 0) + t0
        left = pad_ref[pl.ds(8 + t0 - 1, tt), :]
        mid = pad_ref[pl.ds(8 + t0, tt), :]
        right = pad_ref[pl.ds(8 + t0 + 1, tt), :]
        ok_l = (r != 0) & (r != ctx_len)
        ok_r = (r != ctx_len - 1) & (r != t_len - 1)
        y = jnp.where(ok_l, left, 0.0) * w0 + mid * w1 + jnp.where(ok_r, right, 0.0) * w2
        if t0 < ctx_len:
            oc_ref[pl.ds(t0, tt), :] = y.astype(oc_ref.dtype)
        else:
            ox_ref[pl.ds(t0 - ctx_len, tt), :] = y.astype(ox_ref.dtype)


def _hy_conv(p_tail, hy_conv_w, ctx_len, width3):
    b, t, _ = p_tail.shape
    tc = _pick(width3, (512, 256, 128))
    return pl.pallas_call(
        functools.partial(_hy_conv_kernel, t_len=t, ctx_len=ctx_len),
        grid=(b, width3 // tc),
        in_specs=[pl.BlockSpec((None, t, tc), lambda bi, j: (bi, 0, j)),
                  pl.BlockSpec((3, tc), lambda bi, j: (0, j))],
        out_specs=[pl.BlockSpec((None, ctx_len, tc), lambda bi, j: (bi, 0, j)),
                   pl.BlockSpec((None, t - ctx_len, tc), lambda bi, j: (bi, 0, j))],
        out_shape=[jax.ShapeDtypeStruct((b, ctx_len, width3), BF),
                   jax.ShapeDtypeStruct((b, t - ctx_len, width3), BF)],
        scratch_shapes=[pltpu.VMEM((t + 16, tc), F32)],
        compiler_params=_cparams(2),
        name="hyena_short_conv",
    )(p_tail, hy_conv_w)


def _hy_filter_kernel(ft_ref, w1_ref, b1_ref, f1_ref, w2_ref, b2_ref, f2_ref, w3_ref, dec_ref, o_ref, hid_s,
                      *, mask_lag0):
    part = pl.program_id(0)
    feats = ft_ref[...]

    @pl.when((pl.program_id(2) == 0) & (pl.program_id(3) == 0))
    def _():
        hid = jnp.sin(f1_ref[...] * (_dot3(feats, w1_ref[...]) + b1_ref[...]))
        hid_s[...] = jnp.sin(f2_ref[...] * (_dot3(hid, w2_ref[...]) + b2_ref[...]))

    filt = _dot3(hid_s[...], w3_ref[...])
    tpos = feats[:, 0:1]
    val = filt * jnp.exp(-tpos * jnp.abs(dec_ref[...]))
    if mask_lag0:
        row = lax.broadcasted_iota(jnp.int32, (val.shape[0], 1), 0) + pl.program_id(1) * val.shape[0]
        val = jnp.where((part == 1) & (row == 0), 0.0, val)
    o_ref[...] = val.astype(o_ref.dtype)


def _features(length, n_emb, k_pad, reverse_tail):
    t = jnp.linspace(0.0, 1.0, length, dtype=F32)
    n_bands = (n_emb - 1) // 2
    omega = 2.0 * math.pi * jnp.arange(length, dtype=F32) / length
    bands = jnp.linspace(1e-4, n_bands - 1, n_bands, dtype=F32)
    ang = omega[:, None] * bands[None, :]
    feats = jnp.concatenate([t[:, None], jnp.cos(ang), -jnp.sin(ang)], axis=-1)
    feats = jnp.pad(feats, ((0, 0), (0, k_pad - n_emb)))
    if reverse_tail:
        tail = feats[::-1]
    else:
        tail = jnp.concatenate([jnp.zeros((1, k_pad), F32), feats[:-1]], axis=0)
    return jnp.stack([feats, tail])


def _hy_filter_stack(length, w1, b1, f1, w2, b2, f2, w3, decay, hyw, reverse_tail):
    n_emb, fh = w1.shape
    k_pad = 64 if n_emb <= 64 else _pick(n_emb, (128,))
    feats = _features(length, n_emb, k_pad, reverse_tail)
    w1p = jnp.pad(w1, ((0, k_pad - n_emb), (0, 0)))
    tr = _pick(length, (256, 128))
    tc = _pick(hyw, (512, 256, 128))
    nj = hyw // tc
    ni = length // tr
    return pl.pallas_call(
        functools.partial(_hy_filter_kernel, mask_lag0=not reverse_tail),
        grid=(2, ni, 2, nj),
        in_specs=[pl.BlockSpec((None, tr, k_pad), lambda p, i, c, j: (p, i, 0)),
                  pl.BlockSpec((k_pad, fh), lambda p, i, c, j: (0, 0)),
                  pl.BlockSpec((1, fh), lambda p, i, c, j: (0, 0)),
                  pl.BlockSpec((1, fh), lambda p, i, c, j: (0, 0)),
                  pl.BlockSpec((fh, fh), lambda p, i, c, j: (0, 0)),
                  pl.BlockSpec((1, fh), lambda p, i, c, j: (0, 0)),
                  pl.BlockSpec((1, fh), lambda p, i, c, j: (0, 0)),
                  pl.BlockSpec((fh, tc), lambda p, i, c, j: (0, (2 * c + p) * nj + j)),
                  pl.BlockSpec((1, tc), lambda p, i, c, j: (0, (2 * c + p) * nj + j))],
        out_specs=pl.BlockSpec((tr, tc), lambda p, i, c, j: (p * ni + i, c * nj + j)),
        out_shape=jax.ShapeDtypeStruct((2 * length, 2 * hyw), BF),
        scratch_shapes=[pltpu.VMEM((tr, fh), F32)],
        compiler_params=_cparams(4),
        name="hyena_filters",
    )(feats, w1p, b1.reshape(1, fh), f1.reshape(1, fh), w2, b2.reshape(1, fh), f2.reshape(1, fh),
      w3, decay.reshape(1, HY_N_FILT * hyw))


def _dft_tables(length, tile):
    n = 2 * length
    half = tile // 2
    rows = jnp.arange(n, dtype=jnp.int32)
    tile_i, r = rows // tile, rows % tile
    is_im = r >= half
    k = tile_i * half + jnp.where(is_im, r - half, r)
    t = jnp.arange(length, dtype=jnp.int32)
    ang = (2.0 * math.pi / n) * ((k[:, None] * t[None, :]) % n).astype(F32)
    nyq = jnp.where(t % 2 == 0, 1.0, -1.0).astype(F32)[None, :]
    is_nyq = (is_im & (k == 0))[:, None]
    fwd = jnp.where(is_im[:, None], -jnp.sin(ang), jnp.cos(ang))
    fwd = jnp.where(is_nyq, nyq, fwd)
    sgn = jnp.where(is_im[:, None] & ~is_nyq, -1.0, 1.0)
    fker = jnp.concatenate([fwd, fwd * sgn], axis=1)
    scale = jnp.where(k == 0, 1.0 / n, 2.0 / n)[None, :]
    inv = jnp.where(is_im[None, :], -jnp.sin(ang.T), jnp.cos(ang.T))
    inv = jnp.where(is_nyq.T, nyq.T, inv) * scale
    return fwd.astype(BF), fker.astype(BF), inv.astype(BF)


def _hy_long_convs(u_seq, kstack, hy_skip, hyw):
    b, ls, _ = u_seq.shape
    n = 2 * ls
    tmf = _pick(n, (512,))
    half = tmf // 2
    tn = _pick(hyw, (512, 256, 128))
    nj = hyw // tn
    fwd, fker, inv = _dft_tables(ls, tmf)

    (hspec,) = _matmul(
        fker, kstack,
        pl.BlockSpec((tmf, n), lambda g, j, i: (i, 0)),
        pl.BlockSpec((n, tn), lambda g, j, i: (0, j)),
        (1, 2 * nj, n // tmf),
        [(jax.ShapeDtypeStruct((n, 2 * hyw), F32), pl.BlockSpec((tmf, tn), lambda g, j, i: (i, j)))],
        _ep_store(F32), name="hyena_filter_dft")

    def spectrum_product(acc, extras, outs):
        hs = extras[0][...]
        re, im = acc[:half], acc[half:]
        hre, him = hs[:half], hs[half:]
        first = (lax.broadcasted_iota(jnp.int32, (half, 1), 0) == 0) & (pl.program_id(2) == 0)
        yre = jnp.where(first, re * hre, re * hre - im * him)
        yim = jnp.where(first, im * him, re * him + im * hre)
        outs[0][pl.ds(0, half), :] = yre.astype(BF)
        outs[0][pl.ds(half, half), :] = yim.astype(BF)

    tmi = _pick(ls, (TOKEN_TILE,))
    skip3 = hy_skip.reshape(2, 1, hyw)

    def conv(z_arr, z_off, conv_idx, mul_off):
        (spec,) = _matmul(
            fwd, z_arr,
            pl.BlockSpec((tmf, ls), lambda g, j, i: (i, 0)),
            pl.BlockSpec((None, ls, tn), lambda g, j, i: (g, 0, z_off + j)),
            (b, nj, n // tmf),
            [(jax.ShapeDtypeStruct((b, n, hyw), BF), pl.BlockSpec((None, tmf, tn), lambda g, j, i: (g, i, j)))],
            spectrum_product,
            extras=[(hspec, pl.BlockSpec((tmf, tn), lambda g, j, i: (i, conv_idx * nj + j)))],
            name="hyena_dft_fwd")

        def finish(acc, extras, outs):
            z = extras[0][...].astype(F32)
            xm = extras[2][...].astype(F32)
            outs[0][...] = (xm * (acc + z * extras[1][...])).astype(BF)

        (y,) = _matmul(
            inv, spec,
            pl.BlockSpec((tmi, n), lambda g, j, i: (i, 0)),
            pl.BlockSpec((None, n, tn), lambda g, j, i: (g, 0, j)),
            (b, nj, ls // tmi),
            [(jax.ShapeDtypeStruct((b, ls, hyw), BF), pl.BlockSpec((None, tmi, tn), lambda g, j, i: (g, i, j)))],
            finish,
            extras=[(z_arr, pl.BlockSpec((None, tmi, tn), lambda g, j, i: (g, i, z_off + j))),
                    (skip3, pl.BlockSpec((None, 1, tn), lambda g, j, i: (conv_idx, 0, j))),
                    (u_seq, pl.BlockSpec((None, tmi, tn), lambda g, j, i: (g, i, mul_off * nj + j)))],
            name="hyena_dft_inv")
        return y

    z2 = conv(u_seq, 0, 0, 1)
    return conv(z2, 0, 1, 2)


FFT_FAST = 64
FFT_K1_GROUP = 4


def _fft_tables(length):
    bf = FFT_FAST
    a_n = length // bf
    na = 2 * a_n
    n = 2 * length
    b = jnp.arange(bf, dtype=jnp.int32)
    k1 = jnp.arange(na, dtype=jnp.int32)
    a = jnp.arange(na, dtype=jnp.int32)
    nn = bf * a[None, None, :] + b[:, None, None]
    ang = (2.0 * math.pi / n) * ((k1[None, :, None] * nn) % n).astype(F32)
    m1k = jnp.concatenate([jnp.cos(ang), -jnp.sin(ang)], axis=1)
    m1 = m1k[:, :, :a_n]
    ang_t = jnp.swapaxes(ang, 1, 2)[:, :a_n, :]
    g2 = jnp.concatenate([jnp.cos(ang_t), -jnp.sin(ang_t)], axis=2) / n
    k2 = jnp.arange(bf, dtype=jnp.int32)
    ang2 = (2.0 * math.pi / bf) * ((k2[:, None] * b[None, :]) % bf).astype(F32)
    c2, s2 = jnp.cos(ang2), jnp.sin(ang2)
    f2 = jnp.concatenate([jnp.concatenate([c2, s2], axis=1), jnp.concatenate([-s2, c2], axis=1)], axis=0)
    i1 = jnp.concatenate([jnp.concatenate([c2.T, -s2.T], axis=1), jnp.concatenate([s2.T, c2.T], axis=1)], axis=0)
    return m1.astype(BF), m1k.astype(BF), f2.astype(BF), i1.astype(BF), g2.astype(BF)


def _fft_mid_kernel(p_ref, h_ref, f2_ref, i1_ref, o_ref):
    bf = f2_ref.shape[0] // 2
    for g in range(p_ref.shape[0]):
        x = jnp.dot(f2_ref[...], p_ref[g], preferred_element_type=F32)
        hs = h_ref[g].astype(F32)
        xr, xi = x[:bf], x[bf:]
        hr, hi = hs[:bf], hs[bf:]
        y = jnp.concatenate([xr * hr - xi * hi, xr * hi + xi * hr], axis=0).astype(BF)
        o_ref[g] = jnp.dot(i1_ref[...], y, preferred_element_type=F32).astype(o_ref.dtype)


def _fft_spec_kernel(p_ref, f2_ref, o_ref):
    for g in range(p_ref.shape[0]):
        o_ref[g] = jnp.dot(f2_ref[...], p_ref[g], preferred_element_type=F32).astype(o_ref.dtype)


def _fft_regroup(p, bf, na):
    lead = p.shape[:-3]
    c = p.shape[-1]
    nl = len(lead)
    p = p.reshape(*lead, bf, 2, na, c)
    p = jnp.transpose(p, tuple(range(nl)) + (nl + 2, nl + 1, nl + 0, nl + 3))
    return p.reshape(*lead, na, 2 * bf, c)


def _fft_ungroup(q, bf, na):
    lead = q.shape[:-3]
    c = q.shape[-1]
    nl = len(lead)
    q = q.reshape(*lead, na, 2, bf, c)
    q = jnp.transpose(q, tuple(range(nl)) + (nl + 2, nl + 1, nl + 0, nl + 3))
    return q.reshape(*lead, bf, 2 * na, c)


def _hy_long_convs_fft(u_seq, kern, hy_skip, hyw):
    nb, ls, _ = u_seq.shape
    bf = FFT_FAST
    a_n = ls // bf
    na = 2 * a_n
    tc = _pick(hyw, (2048, 1024, 512, 256, 128))
    nct = hyw // tc
    kg = _pick(na, (FFT_K1_GROUP, 2, 1))
    m1, m1k, f2, i1, g2 = _fft_tables(ls)
    u2 = u_seq.reshape(nb, a_n, bf * 3 * hyw)
    k2 = kern.reshape(na, bf * 2 * hyw)

    (ph,) = _matmul(
        m1k, k2,
        pl.BlockSpec((None, 2 * na, na), lambda g, j, i: (j // (2 * nct), 0, 0)),
        pl.BlockSpec((na, tc), lambda g, j, i: (0, j)),
        (1, bf * 2 * nct, 1),
        [(jax.ShapeDtypeStruct((bf, 2 * na, 2 * hyw), BF),
          pl.BlockSpec((None, 2 * na, tc), lambda g, j, i: (j // (2 * nct), 0, j % (2 * nct))))],
        _ep_store(BF), name="hyena_fft_kern_s1")
    hspec = pl.pallas_call(
        _fft_spec_kernel,
        grid=(na // kg, 2 * nct),
        in_specs=[pl.BlockSpec((kg, 2 * bf, tc), lambda k, j: (k, 0, j)),
                  pl.BlockSpec((2 * bf, 2 * bf), lambda k, j: (0, 0))],
        out_specs=pl.BlockSpec((kg, 2 * bf, tc), lambda k, j: (k, 0, j)),
        out_shape=jax.ShapeDtypeStruct((na, 2 * bf, 2 * hyw), BF),
        compiler_params=_cparams(2),
        name="hyena_fft_kern_s2",
    )(_fft_regroup(ph, bf, na), f2)

    skip3 = hy_skip.reshape(2, 1, hyw)

    def conv(z2, z_sections, z_sec, conv_idx, mul_sec):
        zblk = lambda j: (j // nct) * (z_sections * nct) + z_sec * nct + j % nct
        ublk = lambda j: (j // nct) * (3 * nct) + mul_sec * nct + j % nct
        (p,) = _matmul(
            m1, z2,
            pl.BlockSpec((None, 2 * na, a_n), lambda g, j, i: (j // nct, 0, 0)),
            pl.BlockSpec((None, a_n, tc), lambda g, j, i: (g, 0, zblk(j))),
            (nb, bf * nct, 1),
            [(jax.ShapeDtypeStruct((nb, bf, 2 * na, hyw), BF),
              pl.BlockSpec((None, None, 2 * na, tc), lambda g, j, i: (g, j // nct, 0, j % nct)))],
            _ep_store(BF), name="hyena_fft_s1")
        q = pl.pallas_call(
            _fft_mid_kernel,
            grid=(na // kg, nct, nb),
            in_specs=[pl.BlockSpec((None, kg, 2 * bf, tc), lambda k, j, g: (g, k, 0, j)),
                      pl.BlockSpec((kg, 2 * bf, tc), lambda k, j, g: (k, 0, conv_idx * nct + j)),
                      pl.BlockSpec((2 * bf, 2 * bf), lambda k, j, g: (0, 0)),
                      pl.BlockSpec((2 * bf, 2 * bf), lambda k, j, g: (0, 0))],
            out_specs=pl.BlockSpec((None, kg, 2 * bf, tc), lambda k, j, g: (g, k, 0, j)),
            out_shape=jax.ShapeDtypeStruct((nb, na, 2 * bf, hyw), BF),
            compiler_params=_cparams(3),
            name="hyena_fft_mid",
        )(_fft_regroup(p, bf, na), hspec, f2, i1)

        def finish(acc, extras, outs):
            z = extras[0][...].astype(F32)
            xm = extras[2][...].astype(F32)
            outs[0][...] = (xm * (acc + z * extras[1][...])).astype(BF)

        (y,) = _matmul(
            g2, _fft_ungroup(q, bf, na),
            pl.BlockSpec((None, a_n, 2 * na), lambda g, j, i: (j // nct, 0, 0)),
            pl.BlockSpec((None, None, 2 * na, tc), lambda g, j, i: (g, j // nct, 0, j % nct)),
            (nb, bf * nct, 1),
            [(jax.ShapeDtypeStruct((nb, a_n, bf * hyw), BF),
              pl.BlockSpec((None, a_n, tc), lambda g, j, i: (g, 0, j)))],
            finish,
            extras=[(z2, pl.BlockSpec((None, a_n, tc), lambda g, j, i: (g, 0, zblk(j)))),
                    (skip3, pl.BlockSpec((None, 1, tc), lambda g, j, i: (conv_idx, 0, j % nct))),
                    (u2, pl.BlockSpec((None, a_n, tc), lambda g, j, i: (g, 0, ublk(j))))],
            name="hyena_fft_s3")
        return y

    z_mid = conv(u2, 3, 0, 0, 1)
    y = conv(z_mid, 1, 0, 1, 2)
    return y.reshape(nb, ls, hyw)


def _moe_up_kernel(x_ref, wg_ref, wu_ref, comb_ref, o_ref, wgb, wub):
    @pl.when(pl.program_id(1) == 0)
    def _():
        wgb[...] = wg_ref[...].astype(BF)
        wub[...] = wu_ref[...].astype(BF)
    x = x_ref[...]
    comb = comb_ref[...]
    lane = lax.broadcasted_iota(jnp.int32, comb.shape, 1)
    parts = []
    for s in range(wgb.shape[0]):
        g = jnp.dot(x, wgb[s], preferred_element_type=F32)
        u = jnp.dot(x, wub[s], preferred_element_type=F32)
        e = pl.program_id(0) * wgb.shape[0] + s
        scale = jnp.sum(jnp.where(lane == e, comb, 0.0), axis=1, keepdims=True)
        parts.append((g * jax.nn.sigmoid(g) * u * scale).astype(o_ref.dtype))
    o_ref[...] = jnp.concatenate(parts, axis=1)


MOE_EXPERTS_PER_STEP = 2


def _moe_up(u, w_gate, w_up, comb, l):
    m, d = u.shape
    _, ne, _, f = w_gate.shape
    tm = _pick(m, (512, 256))
    es = MOE_EXPERTS_PER_STEP
    wspec = pl.BlockSpec((None, es, d, f), lambda e, i: (l, e, 0, 0))
    return pl.pallas_call(
        _moe_up_kernel,
        grid=(ne // es, m // tm),
        in_specs=[pl.BlockSpec((tm, d), lambda e, i: (i, 0)), wspec, wspec,
                  pl.BlockSpec((tm, LANES), lambda e, i: (i, 0))],
        out_specs=pl.BlockSpec((tm, es * f), lambda e, i: (i, e)),
        out_shape=jax.ShapeDtypeStruct((m, ne * f), BF),
        scratch_shapes=[pltpu.VMEM((es, d, f), BF), pltpu.VMEM((es, d, f), BF)],
        compiler_params=_cparams(2),
        name="moe_up",
    )(u, w_gate, w_up, comb)


def _row_gate(gt_ref, tm, t_len, ctx_len, n_batch):
    r = lax.broadcasted_iota(jnp.int32, (tm, 1), 0) + pl.program_id(2) * tm
    gate = jnp.zeros((tm, gt_ref.shape[1]), F32)
    for bi in range(n_batch):
        in_b = (r >= bi * t_len) & (r < (bi + 1) * t_len)
        is_c = r < bi * t_len + ctx_len
        gate = jnp.where(in_b & is_c, gt_ref[0:1, :], gate)
        gate = jnp.where(in_b & ~is_c, gt_ref[1 + bi:2 + bi, :], gate)
    return gate


def _final_norm_kernel(h_ref, g_ref, o_ref):
    xf = h_ref[...]
    o_ref[...] = xf * lax.rsqrt(jnp.mean(xf * xf, axis=-1, keepdims=True) + NORM_EPS) * g_ref[...]


def _final_norm(h, gain, n_batch, t_len, ctx_len):
    m, d = h.shape
    tt = TOKEN_TILE
    n_t, n_c = t_len // tt, ctx_len // tt
    n_x = n_t - n_c
    return pl.pallas_call(
        _final_norm_kernel,
        grid=(n_batch, n_x),
        in_specs=[pl.BlockSpec((tt, d), lambda bi, i: (bi * n_t + n_c + i, 0)),
                  pl.BlockSpec((1, d), lambda bi, i: (0, 0))],
        out_specs=pl.BlockSpec((None, tt, d), lambda bi, i: (bi, i, 0)),
        out_shape=jax.ShapeDtypeStruct((n_batch, t_len - ctx_len, d), F32),
        compiler_params=_cparams(2),
        name="final_norm",
    )(h, gain.reshape(1, d))


def kernel(x, c, ctx, c_ctx, ada_down, ada_up, ada_bias, norm_mix, norm_ffn, w_in, qkv_conv, a_log, dt_bias, o_norm, hy_conv, hy_w1, hy_b1, hy_f1, hy_w2, hy_b2, hy_f2, hy_w3, hy_decay, hy_skip, w_br_a, w_br_b, w_out, router_group, router_group_bias, router_expert, router_expert_bias, exp_gate, exp_up, exp_down, final_norm):
    n_batch, seq, d = x.shape
    ctx_len = ctx.shape[1]
    t_len = ctx_len + seq
    m = n_batch * t_len
    depth = w_in.shape[0]
    n_heads = a_log.shape[2]
    a_width = n_heads * LANES
    hyw = hy_skip.shape[2]
    n_exp = exp_gate.shape[1]
    dg_off = 4 * a_width
    hy_off = dg_off + 4 * n_heads
    tail_w = 3 * hyw + 2 * d
    n_t, n_c = t_len // TOKEN_TILE, ctx_len // TOKEN_TILE
    nc = t_len // CHUNK
    ncp = -(-nc // 8) * 8

    h = jnp.concatenate([ctx, x], axis=1).reshape(m, d)

    conds = jnp.zeros((8, d), F32).at[0].set(c_ctx).at[1:1 + n_batch].set(c)
    mod_all = _adaln_all(conds, ada_down, ada_up, ada_bias)
    mod_all = mod_all.reshape(depth, 8, 6, d)[:, :1 + n_batch]
    mod_all = jnp.pad(mod_all, ((0, 0), (0, 0), (0, 2), (0, 0)))

    tm = _pick(m, (512, 256))
    tn = 512

    def gate_table(mod_l, idx):
        return jnp.pad(mod_l[:, idx, :], ((0, 8 - (1 + n_batch)), (0, 0)))

    def residual_ep(acc, extras, outs):
        outs[0][...] = extras[0][...] + _row_gate(extras[1], acc.shape[0], t_len, ctx_len, n_batch) * acc

    for l in range(depth):
        mod_l = mod_all[l]
        u = _norm_mod(h, norm_mix[l], mod_l, 0, n_t, n_c)
        (p1,) = _matmul(
            u, w_in,
            pl.BlockSpec((tm, d), lambda g, j, i: (i, 0)),
            pl.BlockSpec((None, d, tn), lambda g, j, i, l=l: (l, 0, j)),
            (1, dg_off // tn, m // tm),
            [(jax.ShapeDtypeStruct((m, dg_off), BF), pl.BlockSpec((tm, tn), lambda g, j, i: (i, j)))],
            _ep_store(BF), name="proj_qkvz")
        tail_base = (hy_off // LANES) * LANES
        (p_tail,) = _matmul(
            u, w_in,
            pl.BlockSpec((tm, d), lambda g, j, i: (i, 0)),
            pl.BlockSpec((None, d, tn), lambda g, j, i, l=l: (l, 0, tail_base // tn + j)),
            (1, tail_w // tn, m // tm),
            [(jax.ShapeDtypeStruct((m, tail_w), BF), pl.BlockSpec((tm, tn), lambda g, j, i: (i, j)))],
            _ep_store(BF), name="proj_tail",
            w_next_spec=pl.BlockSpec((None, d, LANES),
                                     lambda g, j, i, l=l: (l, 0, (tail_base + (j + 1) * tn) // LANES)),
            lane_shift=hy_off - tail_base)
        w_dg_t = w_in[l, :, dg_off:hy_off].T
        a_exp = jnp.concatenate([jnp.exp(a_log[l].reshape(-1)), jnp.zeros((2 * n_heads,), F32)])[:, None]
        dt_b = jnp.concatenate([dt_bias[l].reshape(-1), jnp.zeros((2 * n_heads,), F32)])[:, None]
        gates = _gates(u, w_dg_t, a_exp, dt_b)

        g5 = gates.reshape(4, n_heads, n_batch, nc, CHUNK).transpose(2, 1, 0, 3, 4)
        g_rows = jnp.pad(g5, ((0, 0), (0, 0), (0, 0), (0, ncp - nc), (0, 0)))
        g_cols = jnp.pad(jnp.swapaxes(g5, 3, 4), ((0, 0), (0, 0), (0, 0), (0, 0), (0, LANES - nc)))

        p1_3 = p1.reshape(n_batch, t_len, dg_off)
        conv9 = qkv_conv[l].reshape(9, 3 * a_width)
        du, dw, dq, dk, da, dgl = _delta_prep(p1_3, conv9, g_rows, g_cols, n_heads, ctx_len)
        o_f, o_b = _delta_scan(du, dw, dq, dk, da, dgl, ctx_len)
        o_a = _delta_out(o_f, o_b, p1_3, o_norm[l], n_heads).reshape(m, a_width)

        p_tail3 = p_tail.reshape(n_batch, t_len, tail_w)
        u_c, u_x = _hy_conv(p_tail3, hy_conv[l], ctx_len, 3 * hyw)
        filt_args = (hy_w1[l], hy_b1[l], hy_f1[l], hy_w2[l], hy_b2[l], hy_f2[l], hy_w3[l], hy_decay[l])
        y_x = _hy_long_convs_fft(u_x, _hy_filter_stack(seq, *filt_args, hyw, True), hy_skip[l], hyw)
        y_c = _hy_long_convs(u_c, _hy_filter_stack(ctx_len, *filt_args, hyw, False), hy_skip[l], hyw)
        y_b = jnp.concatenate([y_c, y_x], axis=1).reshape(m, hyw)

        ga_blk = (3 * hyw) // tn
        gb_blk = (3 * hyw + d) // tn

        def gated_ep(acc, extras, outs):
            outs[0][...] = jax.nn.sigmoid(extras[0][...].astype(F32)) * acc

        (br,) = _matmul(
            o_a, w_br_a,
            pl.BlockSpec((tm, a_width), lambda g, j, i: (i, 0)),
            pl.BlockSpec((None, a_width, tn), lambda g, j, i, l=l: (l, 0, j)),
            (1, d // tn, m // tm),
            [(jax.ShapeDtypeStruct((m, d), F32), pl.BlockSpec((tm, tn), lambda g, j, i: (i, j)))],
            gated_ep,
            extras=[(p_tail, pl.BlockSpec((tm, tn), lambda g, j, i: (i, ga_blk + j)))],
            name="branch_a")

        def merge_ep(acc, extras, outs):
            outs[0][...] = (extras[1][...] + jax.nn.sigmoid(extras[0][...].astype(F32)) * acc).astype(BF)

        (merged,) = _matmul(
            y_b, w_br_b,
            pl.BlockSpec((tm, hyw), lambda g, j, i: (i, 0)),
            pl.BlockSpec((None, hyw, tn), lambda g, j, i, l=l: (l, 0, j)),
            (1, d // tn, m // tm),
            [(jax.ShapeDtypeStruct((m, d), BF), pl.BlockSpec((tm, tn), lambda g, j, i: (i, j)))],
            merge_ep,
            extras=[(p_tail, pl.BlockSpec((tm, tn), lambda g, j, i: (i, gb_blk + j))),
                    (br, pl.BlockSpec((tm, tn), lambda g, j, i: (i, j)))],
            name="branch_b_merge")

        (h,) = _matmul(
            merged, w_out,
            pl.BlockSpec((tm, d), lambda g, j, i: (i, 0)),
            pl.BlockSpec((None, d, tn), lambda g, j, i, l=l: (l, 0, j)),
            (1, d // tn, m // tm),
            [(jax.ShapeDtypeStruct((m, d), F32), pl.BlockSpec((tm, tn), lambda g, j, i: (i, j)))],
            residual_ep,
            extras=[(h, pl.BlockSpec((tm, tn), lambda g, j, i: (i, j))),
                    (gate_table(mod_l, 2), pl.BlockSpec((8, tn), lambda g, j, i: (0, j)))],
            name="out_proj")

        w_router = jnp.pad(jnp.concatenate([router_expert[l], router_group[l]], axis=1),
                           ((0, 0), (0, LANES - n_exp - N_GROUPS)))
        b_router = jnp.pad(jnp.concatenate([router_expert_bias[l], router_group_bias[l]]),
                           (0, LANES - n_exp - N_GROUPS)).reshape(1, LANES)
        u2, comb = _norm_mod_router(h, norm_ffn[l], mod_l, 3, n_t, n_c, w_router, b_router, n_exp, N_GROUPS)
        hid = _moe_up(u2, exp_gate, exp_up, comb, l)
        kf = hid.shape[1]
        tnd = _pick(d, (256, 128))
        (h,) = _matmul(
            hid, exp_down.reshape(depth, kf, d),
            pl.BlockSpec((tm, kf), lambda g, j, i: (i, 0)),
            pl.BlockSpec((None, kf, tnd), lambda g, j, i, l=l: (l, 0, j)),
            (1, d // tnd, m // tm),
            [(jax.ShapeDtypeStruct((m, d), F32), pl.BlockSpec((tm, tnd), lambda g, j, i: (i, j)))],
            residual_ep,
            extras=[(h, pl.BlockSpec((tm, tnd), lambda g, j, i: (i, j))),
                    (gate_table(mod_l, 5), pl.BlockSpec((8, tnd), lambda g, j, i: (0, j)))],
            name="moe_down")

    return _final_norm(h, final_norm, n_batch, t_len, ctx_len)
```

```python
import functools
import math

import jax
import jax.numpy as jnp
from jax import lax
from jax.experimental import pallas as pl
from jax.experimental.pallas import tpu as pltpu

F32 = jnp.float32
BF = jnp.bfloat16

GRID_W = 64
CHUNK = 64
N_GROUPS = 4
EXPERTS_PER_GROUP = 8
HY_N_FILT = 4
NORM_EPS = 1e-6
TOKEN_TILE = 256
LANES = 128
CONV_PAD = 72
VMEM_LIMIT = 56 * 1024 * 1024


def _cparams(n_grid):
    return pltpu.CompilerParams(dimension_semantics=("arbitrary",) * n_grid,
                                vmem_limit_bytes=VMEM_LIMIT)


def _split_bf16(a):
    hi = a.astype(BF)
    lo = (a - hi.astype(F32)).astype(BF)
    return hi, lo


def _dot3(a, b):
    ah, al = _split_bf16(a)
    bh, bl = _split_bf16(b)
    d = functools.partial(jnp.dot, preferred_element_type=F32)
    return d(ah, bh) + (d(ah, bl) + d(al, bh))


def _dotx(a, b):
    return jnp.dot(a, b, precision=lax.Precision.HIGHEST, preferred_element_type=F32)


def _bdot(a, b):
    return jnp.dot(a.astype(BF), b.astype(BF), preferred_element_type=F32)


def _mm_kernel(*refs, n_extra, n_out, cast_w, prologue, epilogue):
    x_ref, w_ref = refs[0], refs[1]
    extras = refs[2:2 + n_extra]
    outs = refs[2 + n_extra:2 + n_extra + n_out]
    scratch = refs[2 + n_extra + n_out:]
    if cast_w:
        wbf = scratch[0]

        @pl.when(pl.program_id(2) == 0)
        def _():
            k = w_ref.shape[0]
            step = 512 if k % 512 == 0 else k
            if k == step:
                wbf[...] = w_ref[...].astype(BF)
            else:
                def it(r, c):
                    r0 = pl.multiple_of(r * step, step)
                    wbf[pl.ds(r0, step), :] = w_ref[pl.ds(r0, step), :].astype(BF)
                    return c
                lax.fori_loop(0, k // step, it, 0)
        w = wbf[...]
    else:
        w = w_ref[...]
    x = x_ref[...]
    if prologue is not None:
        x = prologue(x)
    if x.dtype != BF:
        x = x.astype(BF)
    acc = jnp.dot(x, w, preferred_element_type=F32)
    epilogue(acc, extras, outs)


def _matmul(x, w, x_spec, w_spec, grid, outs, epilogue, extras=(), prologue=None, name=None):
    cast_w = w.dtype != BF
    kw, tn = [d for d in w_spec.block_shape if d is not None]
    scratch = [pltpu.VMEM((kw, tn), BF)] if cast_w else []
    body = functools.partial(_mm_kernel, n_extra=len(extras), n_out=len(outs), cast_w=cast_w,
                             prologue=prologue, epilogue=epilogue)
    res = pl.pallas_call(
        body,
        grid=grid,
        in_specs=[x_spec, w_spec] + [s for _, s in extras],
        out_specs=[s for _, s in outs],
        out_shape=[o for o, _ in outs],
        scratch_shapes=scratch,
        compiler_params=_cparams(3),
        name=name,
    )(x, w, *[a for a, _ in extras])
    return res


def _ep_store(dtype):
    def ep(acc, extras, outs):
        outs[0][...] = acc.astype(dtype)
    return ep


def _pick(n, prefs):
    for p in prefs:
        if n % p == 0:
            return p
    return n


def _adaln_all(conds, ada_down, ada_up, ada_bias):
    nl, d, r = ada_down.shape
    n6 = ada_up.shape[2]
    (t,) = _matmul(
        conds, ada_down,
        pl.BlockSpec((8, d), lambda g, j, i: (0, 0)),
        pl.BlockSpec((None, d, r), lambda g, j, i: (g, 0, 0)),
        (nl, 1, 1),
        [(jax.ShapeDtypeStruct((nl, 8, r), F32), pl.BlockSpec((None, 8, r), lambda g, j, i: (g, 0, 0)))],
        _ep_store(F32), prologue=lambda v: v * jax.nn.sigmoid(v), name="adaln_down")
    tn = _pick(n6, (2048, 1024, 512, 256, 128))

    def ep(acc, extras, outs):
        outs[0][...] = acc + extras[0][...]

    (mod,) = _matmul(
        t, ada_up,
        pl.BlockSpec((None, 8, r), lambda g, j, i: (g, 0, 0)),
        pl.BlockSpec((None, r, tn), lambda g, j, i: (g, 0, j)),
        (nl, n6 // tn, 1),
        [(jax.ShapeDtypeStruct((nl, 8, n6), F32), pl.BlockSpec((None, 8, tn), lambda g, j, i: (g, 0, j)))],
        ep, extras=[(ada_bias.reshape(nl, 1, n6), pl.BlockSpec((None, 1, tn), lambda g, j, i: (g, 0, j)))],
        name="adaln_up")
    return mod


def _mod_row_map(n_t, n_c):
    def m(i):
        return (jnp.where(i % n_t < n_c, 0, 1 + i // n_t), 0, 0)
    return m


def _norm_mod_kernel(h_ref, g_ref, mod_ref, u_ref, *, off):
    xf = h_ref[...]
    y = xf * lax.rsqrt(jnp.mean(xf * xf, axis=-1, keepdims=True) + NORM_EPS) * g_ref[...]
    sh = mod_ref[off:off + 1, :]
    sc = mod_ref[off + 1:off + 2, :]
    u_ref[...] = (y * (1.0 + sc) + sh).astype(u_ref.dtype)


def _norm_mod(h, gain, mod_l, off, n_t, n_c):
    m, d = h.shape
    tt = TOKEN_TILE
    return pl.pallas_call(
        functools.partial(_norm_mod_kernel, off=off),
        grid=(m // tt,),
        in_specs=[pl.BlockSpec((tt, d), lambda i: (i, 0)),
                  pl.BlockSpec((1, d), lambda i: (0, 0)),
                  pl.BlockSpec((None, 8, d), _mod_row_map(n_t, n_c))],
        out_specs=pl.BlockSpec((tt, d), lambda i: (i, 0)),
        out_shape=jax.ShapeDtypeStruct((m, d), BF),
        compiler_params=_cparams(1),
        name="norm_mod",
    )(h, gain.reshape(1, d), mod_l)


def _router_kernel(h_ref, g_ref, mod_ref, wr_ref, br_ref, u_ref, comb_ref, *, off, n_exp, n_grp):
    xf = h_ref[...]
    y = xf * lax.rsqrt(jnp.mean(xf * xf, axis=-1, keepdims=True) + NORM_EPS) * g_ref[...]
    sh = mod_ref[off:off + 1, :]
    sc = mod_ref[off + 1:off + 2, :]
    u = y * (1.0 + sc) + sh
    u_ref[...] = u.astype(u_ref.dtype)
    logits = _dot3(u, wr_ref[...]) + br_ref[...]
    lane = lax.broadcasted_iota(jnp.int32, logits.shape, 1)
    neg = jnp.float32(-jnp.inf)
    big = jnp.int32(1 << 20)
    is_g = (lane >= n_exp) & (lane < n_exp + n_grp)
    lg = jnp.where(is_g, logits, neg)
    eg = jnp.exp(lg - jnp.max(lg, axis=1, keepdims=True))
    pg = eg / jnp.sum(eg, axis=1, keepdims=True)
    p_top = jnp.max(pg, axis=1, keepdims=True)
    g_idx = jnp.min(jnp.where(is_g & (pg == p_top), lane, big), axis=1, keepdims=True) - n_exp
    per = n_exp // n_grp
    in_grp = (lane >= g_idx * per) & (lane < (g_idx + 1) * per)
    le = jnp.where(in_grp, logits, neg)
    v1 = jnp.max(le, axis=1, keepdims=True)
    i1 = jnp.min(jnp.where(le == v1, lane, big), axis=1, keepdims=True)
    le2 = jnp.where(lane == i1, neg, le)
    v2 = jnp.max(le2, axis=1, keepdims=True)
    i2 = jnp.min(jnp.where(le2 == v2, lane, big), axis=1, keepdims=True)
    e2 = jnp.exp(v2 - v1)
    w1 = p_top / (1.0 + e2)
    w2 = p_top * e2 / (1.0 + e2)
    comb_ref[...] = jnp.where(lane == i1, w1, 0.0) + jnp.where(lane == i2, w2, 0.0)


def _norm_mod_router(h, gain, mod_l, off, n_t, n_c, w_router, b_router, n_exp, n_grp):
    m, d = h.shape
    tt = TOKEN_TILE
    return pl.pallas_call(
        functools.partial(_router_kernel, off=off, n_exp=n_exp, n_grp=n_grp),
        grid=(m // tt,),
        in_specs=[pl.BlockSpec((tt, d), lambda i: (i, 0)),
                  pl.BlockSpec((1, d), lambda i: (0, 0)),
                  pl.BlockSpec((None, 8, d), _mod_row_map(n_t, n_c)),
                  pl.BlockSpec((d, LANES), lambda i: (0, 0)),
                  pl.BlockSpec((1, LANES), lambda i: (0, 0))],
        out_specs=[pl.BlockSpec((tt, d), lambda i: (i, 0)),
                   pl.BlockSpec((tt, LANES), lambda i: (i, 0))],
        out_shape=[jax.ShapeDtypeStruct((m, d), BF), jax.ShapeDtypeStruct((m, LANES), F32)],
        compiler_params=_cparams(1),
        name="norm_mod_router",
    )(h, gain.reshape(1, d), mod_l, w_router, b_router)


def _gates_kernel(w_ref, u_ref, aexp_ref, dtb_ref, o_ref, *, n_decay):
    acc = lax.dot_general(w_ref[...].astype(BF), u_ref[...], (((1,), (1,)), ((), ())),
                          preferred_element_type=F32)
    row = lax.broadcasted_iota(jnp.int32, acc.shape, 0)
    z = acc + dtb_ref[...]
    softplus = jnp.maximum(z, 0.0) + jnp.log(1.0 + jnp.exp(-jnp.abs(z)))
    g = -aexp_ref[...] * softplus
    o_ref[...] = jnp.where(row < n_decay, g, jax.nn.sigmoid(acc))


def _gates(u, w_dg_t, a_exp, dt_b):
    m, d = u.shape
    r = w_dg_t.shape[0]
    tm = _pick(m, (512, 256))
    return pl.pallas_call(
        functools.partial(_gates_kernel, n_decay=r // 2),
        grid=(m // tm,),
        in_specs=[pl.BlockSpec((r, d), lambda i: (0, 0)),
                  pl.BlockSpec((tm, d), lambda i: (i, 0)),
                  pl.BlockSpec((r, 1), lambda i: (0, 0)),
                  pl.BlockSpec((r, 1), lambda i: (0, 0))],
        out_specs=pl.BlockSpec((r, tm), lambda i: (0, i)),
        out_shape=jax.ShapeDtypeStruct((r, m), F32),
        compiler_params=_cparams(1),
        name="delta_gates",
    )(w_dg_t, u, a_exp, dt_b)


def _conv_silu_tile(pad_ref, cw_ref, t0, n, seq_start, seq_len, width):
    one_row = width == seq_len
    r = lax.broadcasted_iota(jnp.int32, (n, 1), 0) + (t0 - seq_start)
    col = lax.rem(r, width)
    acc = jnp.zeros((n, LANES), F32)
    for dj in (-1, 0, 1):
        part = jnp.zeros((n, LANES), F32)
        for di in (-1, 0, 1):
            if one_row and di != 0:
                continue
            off = di * width + dj
            tap = pad_ref[pl.ds(CONV_PAD + t0 + off, n), :]
            wrow = cw_ref[(di + 1) * 3 + (dj + 1):(di + 1) * 3 + (dj + 1) + 1, :]
            ok = (r + off >= 0) & (r + off < seq_len)
            part = part + jnp.where(ok, tap, 0.0) * wrow
        ok_c = (col + dj >= 0) & (col + dj < width)
        acc = acc + jnp.where(ok_c, part, 0.0)
    return acc * jax.nn.sigmoid(acc)


def _bmm(a, b):
    return jnp.einsum('gik,gkj->gij', a.astype(BF), b.astype(BF), preferred_element_type=F32)


def _bmm_nt(a, b):
    return jnp.einsum('gik,gjk->gij', a.astype(BF), b.astype(BF), preferred_element_type=F32)


def _neumann_inverse(lmat, eye):
    x = -lmat
    p = eye + x
    n = lmat.shape[-1]
    k = 1
    while 2 * k < n:
        x = _bmm(x, x)
        p = p + _bmm(p, x)
        k *= 2
    return p


def _delta_prep_kernel(pq_ref, pk_ref, pv_ref, cq_ref, ck_ref, cv_ref, g_ref, gt_ref,
                       u_ref, w_ref, qd_ref, kd_ref, a_ref, gl_ref,
                       pad_ref, q_s, k_s, v_s, gcr_s, gct_s,
                       *, t_len, ctx_len, n_chunks, group):
    hd = q_s.shape[1]
    tt = TOKEN_TILE
    d = pl.program_id(2)

    @pl.when(d == 0)
    def _():
        zpad = jnp.zeros((CONV_PAD, LANES), F32)
        pad_ref[pl.ds(0, CONV_PAD), :] = zpad
        pad_ref[pl.ds(CONV_PAD + t_len, CONV_PAD), :] = zpad
        for src, cw, dst, kind in ((pq_ref, cq_ref, q_s, "q"), (pk_ref, ck_ref, k_s, "k"), (pv_ref, cv_ref, v_s, "v")):
            pad_ref[pl.ds(CONV_PAD, t_len), :] = src[...].astype(F32)
            for ti in range(t_len // tt):
                t0 = ti * tt
                if t0 < ctx_len:
                    a = _conv_silu_tile(pad_ref, cw, t0, tt, 0, ctx_len, ctx_len)
                else:
                    a = _conv_silu_tile(pad_ref, cw, t0, tt, ctx_len, t_len - ctx_len, GRID_W)
                if kind != "v":
                    a = a * lax.rsqrt(jnp.sum(a * a, axis=-1, keepdims=True) + NORM_EPS)
                if kind == "q":
                    a = a * (hd ** -0.5)
                dst[pl.ds(t0, tt), :] = a

    ii = lax.broadcasted_iota(jnp.int32, (CHUNK, CHUNK), 0)
    jj = lax.broadcasted_iota(jnp.int32, (CHUNK, CHUNK), 1)
    sdiff = (ii - jj) * (1 - 2 * d)
    eye = (ii == jj).astype(F32)
    incl = sdiff >= 0
    strict = sdiff > 0
    gcr_s[...] = _dotx(g_ref[d], (sdiff <= 0).astype(F32))
    gct_s[...] = _dotx(incl.astype(F32), gt_ref[d])
    gtot = jnp.where(d == 0, gcr_s[:, CHUNK - 1:CHUNK], gcr_s[:, 0:1])
    gl_ref[...] = jnp.broadcast_to(jnp.exp(gtot), gl_ref.shape)

    gc = group
    rows = gc * CHUNK
    lane_c = lax.broadcasted_iota(jnp.int32, (CHUNK, LANES), 1)
    eye_hd = (lax.broadcasted_iota(jnp.int32, (hd, hd), 0)
              == lax.broadcasted_iota(jnp.int32, (hd, hd), 1)).astype(BF)[None]

    def body(it, carry):
        c0 = it * gc
        t0 = pl.multiple_of(c0 * CHUNK, rows)
        q = q_s[pl.ds(t0, rows), :].reshape(gc, CHUNK, hd)
        k = k_s[pl.ds(t0, rows), :].reshape(gc, CHUNK, hd)
        v = v_s[pl.ds(t0, rows), :].reshape(gc, CHUNK, hd)
        gct = gct_s[...]
        bt = gt_ref[2 + d]
        gcols, bcols, grows, glasts = [], [], [], []
        for g in range(gc):
            sel = lane_c == c0 + g
            gcols.append(jnp.sum(jnp.where(sel, gct, 0.0), axis=1, keepdims=True))
            bcols.append(jnp.sum(jnp.where(sel, bt, 0.0), axis=1, keepdims=True))
            grow_g = gcr_s[pl.ds(c0 + g, 1), :]
            grows.append(grow_g)
            glasts.append(jnp.where(d == 0, grow_g[:, CHUNK - 1:CHUNK], grow_g[:, 0:1]))
        gcol = jnp.stack(gcols)
        bcol = jnp.stack(bcols)
        grow = jnp.stack(grows)
        glast = jnp.stack(glasts)
        decay = jnp.where(incl, jnp.exp(jnp.where(incl, gcol - grow, 0.0)), 0.0)
        kb = k * bcol
        lmat = jnp.where(strict, _bmm_nt(kb, k) * decay, 0.0)
        tinv = _neumann_inverse(lmat, eye[None])
        egc = jnp.exp(gcol)
        uw = _bmm(tinv, jnp.concatenate([v * bcol, kb * egc], axis=2))
        attn = (_bmm_nt(q, k) * decay).astype(BF)
        kdec = k * jnp.exp(glast - gcol)
        kdec_t = _bmm_nt(jnp.broadcast_to(eye_hd, (gc, hd, hd)), kdec).astype(BF)
        u_ref[pl.ds(t0, rows), :] = uw[:, :, :hd].reshape(rows, hd)
        w_ref[pl.ds(t0, rows), :] = uw[:, :, hd:].astype(BF).reshape(rows, hd)
        qd_ref[pl.ds(t0, rows), :] = (q * egc).astype(BF).reshape(rows, hd)
        a_ref[pl.ds(t0, rows), :] = attn.reshape(rows, CHUNK)
        kd_ref[pl.ds(c0, gc)] = kdec_t
        return carry

    lax.fori_loop(0, n_chunks // group, body, 0)


def _delta_prep(p1, conv_w9, g_rows, g_cols, n_heads, ctx_len):
    b, t, _ = p1.shape
    hd = LANES
    nc = t // CHUNK
    ncp = g_rows.shape[3]
    group = max(g for g in range(1, 18) if nc % g == 0)
    h = n_heads
    kern = functools.partial(_delta_prep_kernel, t_len=t, ctx_len=ctx_len, n_chunks=nc, group=group)
    tok = lambda off: pl.BlockSpec((None, t, hd), lambda bi, hi, di: (bi, 0, off + hi))
    cws = lambda off: pl.BlockSpec((9, hd), lambda bi, hi, di: (0, off + hi))
    o5 = lambda last: pl.BlockSpec((None, None, None, t, last), lambda bi, hi, di: (di, bi, hi, 0, 0))
    return pl.pallas_call(
        kern,
        grid=(b, h, 2),
        in_specs=[tok(0), tok(h), tok(2 * h), cws(0), cws(h), cws(2 * h),
                  pl.BlockSpec((None, None, 4, ncp, CHUNK), lambda bi, hi, di: (bi, hi, 0, 0, 0)),
                  pl.BlockSpec((None, None, 4, CHUNK, LANES), lambda bi, hi, di: (bi, hi, 0, 0, 0))],
        out_specs=[o5(hd), o5(hd), o5(hd),
                   pl.BlockSpec((None, None, None, nc, hd, CHUNK), lambda bi, hi, di: (di, bi, hi, 0, 0, 0)),
                   o5(CHUNK),
                   pl.BlockSpec((None, None, None, ncp, LANES), lambda bi, hi, di: (di, bi, hi, 0, 0))],
        out_shape=[jax.ShapeDtypeStruct((2, b, h, t, hd), F32),
                   jax.ShapeDtypeStruct((2, b, h, t, hd), BF),
                   jax.ShapeDtypeStruct((2, b, h, t, hd), BF),
                   jax.ShapeDtypeStruct((2, b, h, nc, hd, CHUNK), BF),
                   jax.ShapeDtypeStruct((2, b, h, t, CHUNK), BF),
                   jax.ShapeDtypeStruct((2, b, h, ncp, LANES), F32)],
        scratch_shapes=[pltpu.VMEM((t + 2 * CONV_PAD, LANES), F32),
                        pltpu.VMEM((t, hd), F32), pltpu.VMEM((t, hd), F32), pltpu.VMEM((t, hd), F32),
                        pltpu.VMEM((ncp, CHUNK), F32), pltpu.VMEM((CHUNK, LANES), F32)],
        compiler_params=_cparams(3),
        name="delta_prep",
    )(p1, p1, p1, conv_w9, conv_w9, conv_w9, g_rows, g_cols)


def _delta_scan_kernel(uf, wf, qf, kf, af, gf, ub, wb, qb, kb, ab, gb, of_ref, ob_ref, s_ref, *, hg):
    @pl.when(pl.program_id(2) == 0)
    def _():
        s_ref[...] = jnp.zeros_like(s_ref)

    cat = lambda a, b: jnp.concatenate([a[...], b[...]], axis=0)
    s = s_ref[...]
    sb = s.astype(BF)
    vn = cat(uf, ub) - _bmm(cat(wf, wb), sb)
    vnb = vn.astype(BF)
    o = _bmm(cat(qf, qb), sb) + _bmm(cat(af, ab), vnb)
    s_ref[...] = s * cat(gf, gb) + _bmm(cat(kf, kb), vnb)
    of_ref[...] = o[:hg]
    ob_ref[...] = o[hg:]


def _delta_scan(u, w, qd, kd, a, gl, ctx_len):
    _, b, h, t, hd = u.shape
    nc = t // CHUNK
    ncc = ctx_len // CHUNK
    hg = _pick(h, (8, 4, 2, 1))
    r6 = lambda arr: arr.reshape(2, b, h, nc, CHUNK, arr.shape[-1])
    u6, w6, q6, k6, a6 = r6(u), r6(w), r6(qd), kd, r6(a)
    gl6 = gl[:, :, :, :nc].reshape(2, b, h, nc, 1, LANES)

    def cf(ci):
        return ci

    def cb(ci):
        return jnp.where(ci < ncc, ncc - 1 - ci, nc - 1 - (ci - ncc))

    def spec(d, cmap, rows, last):
        return pl.BlockSpec((None, None, hg, None, rows, last),
                            lambda bi, gi, ci: (d, bi, gi, cmap(ci), 0, 0))

    ins, specs = [], []
    for d, cmap in ((0, cf), (1, cb)):
        ins += [u6, w6, q6, k6, a6, gl6]
        specs += [spec(d, cmap, CHUNK, hd), spec(d, cmap, CHUNK, hd), spec(d, cmap, CHUNK, hd),
                  spec(d, cmap, hd, CHUNK), spec(d, cmap, CHUNK, CHUNK), spec(d, cmap, 1, LANES)]
    o_shape = jax.ShapeDtypeStruct((b, h, nc, CHUNK, hd), F32)
    ospec = lambda cmap: pl.BlockSpec((None, hg, None, CHUNK, hd), lambda bi, gi, ci: (bi, gi, cmap(ci), 0, 0))
    of, ob = pl.pallas_call(
        functools.partial(_delta_scan_kernel, hg=hg),
        grid=(b, h // hg, nc),
        in_specs=specs,
        out_specs=[ospec(cf), ospec(cb)],
        out_shape=[o_shape, o_shape],
        scratch_shapes=[pltpu.VMEM((2 * hg, hd, hd), F32)],
        compiler_params=_cparams(3),
        name="delta_scan",
    )(*ins)
    return of.reshape(b, h, t, hd), ob.reshape(b, h, t, hd)


def _delta_out_kernel(of_ref, ob_ref, z_ref, g_ref, o_ref):
    hd = of_ref.shape[2]
    for h in range(of_ref.shape[0]):
        o = of_ref[h] + ob_ref[h]
        o = o * lax.rsqrt(jnp.mean(o * o, axis=-1, keepdims=True) + NORM_EPS)
        z = z_ref[:, h * hd:(h + 1) * hd].astype(F32)
        o_ref[:, h * hd:(h + 1) * hd] = (o * g_ref[...] * (z * jax.nn.sigmoid(z))).astype(o_ref.dtype)


def _delta_out(of, ob, p1, o_gain, n_heads):
    b, h, t, hd = of.shape
    tt = TOKEN_TILE
    return pl.pallas_call(
        _delta_out_kernel,
        grid=(b, t // tt),
        in_specs=[pl.BlockSpec((None, h, tt, hd), lambda bi, ti: (bi, 0, ti, 0)),
                  pl.BlockSpec((None, h, tt, hd), lambda bi, ti: (bi, 0, ti, 0)),
                  pl.BlockSpec((None, tt, h * hd), lambda bi, ti: (bi, ti, 3)),
                  pl.BlockSpec((1, hd), lambda bi, ti: (0, 0))],
        out_specs=pl.BlockSpec((None, tt, h * hd), lambda bi, ti: (bi, ti, 0)),
        out_shape=jax.ShapeDtypeStruct((b, t, h * hd), BF),
        compiler_params=_cparams(2),
        name="delta_out",
    )(of, ob, p1, o_gain.reshape(1, hd))


def _hy_conv_kernel(p_ref, w_ref, oc_ref, ox_ref, pad_ref, *, t_len, ctx_len):
    tt = TOKEN_TILE
    z8 = jnp.zeros((8, pad_ref.shape[1]), F32)
    pad_ref[pl.ds(0, 8), :] = z8
    pad_ref[pl.ds(8 + t_len, 8), :] = z8
    pad_ref[pl.ds(8, t_len), :] = p_ref[...].astype(F32)
    w0, w1, w2 = w_ref[0:1, :], w_ref[1:2, :], w_ref[2:3, :]
    for ti in range(t_len // tt):
        t0 = ti * tt
        r = lax.broadcasted_iota(jnp.int32, (tt, 1), 0) + t0
        left = pad_ref[pl.ds(8 + t0 - 1, tt), :]
        mid = pad_ref[pl.ds(8 + t0, tt), :]
        right = pad_ref[pl.ds(8 + t0 + 1, tt), :]
        ok_l = (r != 0) & (r != ctx_len)
        ok_r = (r != ctx_len - 1) & (r != t_len - 1)
        y = jnp.where(ok_l, left, 0.0) * w0 + mid * w1 + jnp.where(ok_r, right, 0.0) * w2
        if t0 < ctx_len:
            oc_ref[pl.ds(t0, tt), :] = y.astype(oc_ref.dtype)
        else:
            ox_ref[pl.ds(t0 - ctx_len, tt), :] = y.astype(ox_ref.dtype)


def _hy_conv(p_tail, hy_conv_w, ctx_len, width3):
    b, t, _ = p_tail.shape
    tc = _pick(width3, (512, 256, 128))
    return pl.pallas_call(
        functools.partial(_hy_conv_kernel, t_len=t, ctx_len=ctx_len),
        grid=(b, width3 // tc),
        in_specs=[pl.BlockSpec((None, t, tc), lambda bi, j: (bi, 0, j)),
                  pl.BlockSpec((3, tc), lambda bi, j: (0, j))],
        out_specs=[pl.BlockSpec((None, ctx_len, tc), lambda bi, j: (bi, 0, j)),
                   pl.BlockSpec((None, t - ctx_len, tc), lambda bi, j: (bi, 0, j))],
        out_shape=[jax.ShapeDtypeStruct((b, ctx_len, width3), BF),
                   jax.ShapeDtypeStruct((b, t - ctx_len, width3), BF)],
        scratch_shapes=[pltpu.VMEM((t + 16, tc), F32)],
        compiler_params=_cparams(2),
        name="hyena_short_conv",
    )(p_tail, hy_conv_w)


def _hy_filter_kernel(ft_ref, w1_ref, b1_ref, f1_ref, w2_ref, b2_ref, f2_ref, w3_ref, dec_ref, o_ref, hid_s,
                      *, mask_lag0):
    part = pl.program_id(0)
    feats = ft_ref[...]

    @pl.when((pl.program_id(2) == 0) & (pl.program_id(3) == 0))
    def _():
        hid = jnp.sin(f1_ref[...] * (_dot3(feats, w1_ref[...]) + b1_ref[...]))
        hid_s[...] = jnp.sin(f2_ref[...] * (_dot3(hid, w2_ref[...]) + b2_ref[...]))

    filt = _dot3(hid_s[...], w3_ref[...])
    tpos = feats[:, 0:1]
    val = filt * jnp.exp(-tpos * jnp.abs(dec_ref[...]))
    if mask_lag0:
        row = lax.broadcasted_iota(jnp.int32, (val.shape[0], 1), 0) + pl.program_id(1) * val.shape[0]
        val = jnp.where((part == 1) & (row == 0), 0.0, val)
    o_ref[...] = val.astype(o_ref.dtype)


def _features(length, n_emb, k_pad, reverse_tail):
    t = jnp.linspace(0.0, 1.0, length, dtype=F32)
    n_bands = (n_emb - 1) // 2
    omega = 2.0 * math.pi * jnp.arange(length, dtype=F32) / length
    bands = jnp.linspace(1e-4, n_bands - 1, n_bands, dtype=F32)
    ang = omega[:, None] * bands[None, :]
    feats = jnp.concatenate([t[:, None], jnp.cos(ang), -jnp.sin(ang)], axis=-1)
    feats = jnp.pad(feats, ((0, 0), (0, k_pad - n_emb)))
    if reverse_tail:
        tail = feats[::-1]
    else:
        tail = jnp.concatenate([jnp.zeros((1, k_pad), F32), feats[:-1]], axis=0)
    return jnp.stack([feats, tail])


def _hy_filter_stack(length, w1, b1, f1, w2, b2, f2, w3, decay, hyw, reverse_tail):
    n_emb, fh = w1.shape
    k_pad = 64 if n_emb <= 64 else _pick(n_emb, (128,))
    feats = _features(length, n_emb, k_pad, reverse_tail)
    w1p = jnp.pad(w1, ((0, k_pad - n_emb), (0, 0)))
    tr = _pick(length, (256, 128))
    tc = _pick(hyw, (512, 256, 128))
    nj = hyw // tc
    ni = length // tr
    return pl.pallas_call(
        functools.partial(_hy_filter_kernel, mask_lag0=not reverse_tail),
        grid=(2, ni, 2, nj),
        in_specs=[pl.BlockSpec((None, tr, k_pad), lambda p, i, c, j: (p, i, 0)),
                  pl.BlockSpec((k_pad, fh), lambda p, i, c, j: (0, 0)),
                  pl.BlockSpec((1, fh), lambda p, i, c, j: (0, 0)),
                  pl.BlockSpec((1, fh), lambda p, i, c, j: (0, 0)),
                  pl.BlockSpec((fh, fh), lambda p, i, c, j: (0, 0)),
                  pl.BlockSpec((1, fh), lambda p, i, c, j: (0, 0)),
                  pl.BlockSpec((1, fh), lambda p, i, c, j: (0, 0)),
                  pl.BlockSpec((fh, tc), lambda p, i, c, j: (0, (2 * c + p) * nj + j)),
                  pl.BlockSpec((1, tc), lambda p, i, c, j: (0, (2 * c + p) * nj + j))],
        out_specs=pl.BlockSpec((tr, tc), lambda p, i, c, j: (p * ni + i, c * nj + j)),
        out_shape=jax.ShapeDtypeStruct((2 * length, 2 * hyw), BF),
        scratch_shapes=[pltpu.VMEM((tr, fh), F32)],
        compiler_params=_cparams(4),
        name="hyena_filters",
    )(feats, w1p, b1.reshape(1, fh), f1.reshape(1, fh), w2, b2.reshape(1, fh), f2.reshape(1, fh),
      w3, decay.reshape(1, HY_N_FILT * hyw))


def _dft_tables(length, tile):
    n = 2 * length
    half = tile // 2
    rows = jnp.arange(n, dtype=jnp.int32)
    tile_i, r = rows // tile, rows % tile
    is_im = r >= half
    k = tile_i * half + jnp.where(is_im, r - half, r)
    t = jnp.arange(length, dtype=jnp.int32)
    ang = (2.0 * math.pi / n) * ((k[:, None] * t[None, :]) % n).astype(F32)
    nyq = jnp.where(t % 2 == 0, 1.0, -1.0).astype(F32)[None, :]
    is_nyq = (is_im & (k == 0))[:, None]
    fwd = jnp.where(is_im[:, None], -jnp.sin(ang), jnp.cos(ang))
    fwd = jnp.where(is_nyq, nyq, fwd)
    sgn = jnp.where(is_im[:, None] & ~is_nyq, -1.0, 1.0)
    fker = jnp.concatenate([fwd, fwd * sgn], axis=1)
    scale = jnp.where(k == 0, 1.0 / n, 2.0 / n)[None, :]
    inv = jnp.where(is_im[None, :], -jnp.sin(ang.T), jnp.cos(ang.T))
    inv = jnp.where(is_nyq.T, nyq.T, inv) * scale
    return fwd.astype(BF), fker.astype(BF), inv.astype(BF)


def _hy_long_convs(u_seq, kstack, hy_skip, hyw):
    b, ls, _ = u_seq.shape
    n = 2 * ls
    tmf = _pick(n, (512,))
    half = tmf // 2
    tn = _pick(hyw, (512, 256, 128))
    nj = hyw // tn
    fwd, fker, inv = _dft_tables(ls, tmf)

    (hspec,) = _matmul(
        fker, kstack,
        pl.BlockSpec((tmf, n), lambda g, j, i: (i, 0)),
        pl.BlockSpec((n, tn), lambda g, j, i: (0, j)),
        (1, 2 * nj, n // tmf),
        [(jax.ShapeDtypeStruct((n, 2 * hyw), F32), pl.BlockSpec((tmf, tn), lambda g, j, i: (i, j)))],
        _ep_store(F32), name="hyena_filter_dft")

    def spectrum_product(acc, extras, outs):
        hs = extras[0][...]
        re, im = acc[:half], acc[half:]
        hre, him = hs[:half], hs[half:]
        first = (lax.broadcasted_iota(jnp.int32, (half, 1), 0) == 0) & (pl.program_id(2) == 0)
        yre = jnp.where(first, re * hre, re * hre - im * him)
        yim = jnp.where(first, im * him, re * him + im * hre)
        outs[0][pl.ds(0, half), :] = yre.astype(BF)
        outs[0][pl.ds(half, half), :] = yim.astype(BF)

    tmi = _pick(ls, (TOKEN_TILE,))
    skip3 = hy_skip.reshape(2, 1, hyw)

    def conv(z_arr, z_off, conv_idx, mul_off):
        (spec,) = _matmul(
            fwd, z_arr,
            pl.BlockSpec((tmf, ls), lambda g, j, i: (i, 0)),
            pl.BlockSpec((None, ls, tn), lambda g, j, i: (g, 0, z_off + j)),
            (b, nj, n // tmf),
            [(jax.ShapeDtypeStruct((b, n, hyw), BF), pl.BlockSpec((None, tmf, tn), lambda g, j, i: (g, i, j)))],
            spectrum_product,
            extras=[(hspec, pl.BlockSpec((tmf, tn), lambda g, j, i: (i, conv_idx * nj + j)))],
            name="hyena_dft_fwd")

        def finish(acc, extras, outs):
            z = extras[0][...].astype(F32)
            xm = extras[2][...].astype(F32)
            outs[0][...] = (xm * (acc + z * extras[1][...])).astype(BF)

        (y,) = _matmul(
            inv, spec,
            pl.BlockSpec((tmi, n), lambda g, j, i: (i, 0)),
            pl.BlockSpec((None, n, tn), lambda g, j, i: (g, 0, j)),
            (b, nj, ls // tmi),
            [(jax.ShapeDtypeStruct((b, ls, hyw), BF), pl.BlockSpec((None, tmi, tn), lambda g, j, i: (g, i, j)))],
            finish,
            extras=[(z_arr, pl.BlockSpec((None, tmi, tn), lambda g, j, i: (g, i, z_off + j))),
                    (skip3, pl.BlockSpec((None, 1, tn), lambda g, j, i: (conv_idx, 0, j))),
                    (u_seq, pl.BlockSpec((None, tmi, tn), lambda g, j, i: (g, i, mul_off * nj + j)))],
            name="hyena_dft_inv")
        return y

    z2 = conv(u_seq, 0, 0, 1)
    return conv(z2, 0, 1, 2)


FFT_FAST = 64
FFT_K1_GROUP = 4


def _fft_tables(length):
    bf = FFT_FAST
    a_n = length // bf
    na = 2 * a_n
    n = 2 * length
    b = jnp.arange(bf, dtype=jnp.int32)
    k1 = jnp.arange(na, dtype=jnp.int32)
    a = jnp.arange(na, dtype=jnp.int32)
    nn = bf * a[None, None, :] + b[:, None, None]
    ang = (2.0 * math.pi / n) * ((k1[None, :, None] * nn) % n).astype(F32)
    cs, sn = jnp.cos(ang), jnp.sin(ang)
    m1k = jnp.concatenate([cs, -sn], axis=1)
    cd, sd = cs[:, :, :a_n], sn[:, :, :a_n]
    m1 = jnp.concatenate([jnp.concatenate([cd, sd], axis=2),
                          jnp.concatenate([-sd, cd], axis=2)], axis=1)
    ct, st = jnp.swapaxes(cd, 1, 2) / n, jnp.swapaxes(sd, 1, 2) / n
    g2 = jnp.concatenate([jnp.concatenate([ct, -st], axis=2),
                          jnp.concatenate([st, ct], axis=2)], axis=1)
    k2 = jnp.arange(bf, dtype=jnp.int32)
    ang2 = (2.0 * math.pi / bf) * ((k2[:, None] * b[None, :]) % bf).astype(F32)
    c2, s2 = jnp.cos(ang2), jnp.sin(ang2)
    f2 = jnp.concatenate([jnp.concatenate([c2, s2], axis=1), jnp.concatenate([-s2, c2], axis=1)], axis=0)
    i1 = jnp.concatenate([jnp.concatenate([c2.T, -s2.T], axis=1), jnp.concatenate([s2.T, c2.T], axis=1)], axis=0)
    return m1.astype(BF), m1k.astype(BF), f2.astype(BF), i1.astype(BF), g2.astype(BF)


def _fft_mid_kernel(p_ref, h_ref, f2_ref, i1_ref, o_ref):
    bf = f2_ref.shape[0] // 2
    for g in range(p_ref.shape[0]):
        x = jnp.dot(f2_ref[...], p_ref[g], preferred_element_type=F32)
        hs = h_ref[g].astype(F32)
        xr, xi = x[:bf], x[bf:]
        hr, hi = hs[:bf], hs[bf:]
        y = jnp.concatenate([xr * hr - xi * hi, xr * hi + xi * hr], axis=0).astype(BF)
        o_ref[g] = jnp.dot(i1_ref[...], y, preferred_element_type=F32).astype(o_ref.dtype)


def _fft_spec_kernel(p_ref, f2_ref, o_ref):
    for g in range(p_ref.shape[0]):
        o_ref[g] = jnp.dot(f2_ref[...], p_ref[g], preferred_element_type=F32).astype(o_ref.dtype)


def _fft_regroup(p, bf, na):
    lead = p.shape[:-3]
    c = p.shape[-1]
    nl = len(lead)
    p = p.reshape(*lead, bf, 2, na, c)
    p = jnp.transpose(p, tuple(range(nl)) + (nl + 2, nl + 1, nl + 0, nl + 3))
    return p.reshape(*lead, na, 2 * bf, c)


def _fft_ungroup(q, bf, na):
    lead = q.shape[:-3]
    c = q.shape[-1]
    nl = len(lead)
    q = q.reshape(*lead, na, 2, bf, c)
    q = jnp.transpose(q, tuple(range(nl)) + (nl + 2, nl + 1, nl + 0, nl + 3))
    return q.reshape(*lead, bf, 2 * na, c)


def _hy_long_convs_fft(u_seq, kern, hy_skip, hyw):
    nb_in, ls, _ = u_seq.shape
    if nb_in % 2:
        u_seq = jnp.concatenate([u_seq, jnp.zeros_like(u_seq[:1])], axis=0)
    nb = u_seq.shape[0] // 2
    bf = FFT_FAST
    a_n = ls // bf
    na = 2 * a_n
    ra = 2 * a_n
    tc = _pick(hyw, (2048, 1024, 512, 256, 128))
    nct = hyw // tc
    kg = _pick(na, (FFT_K1_GROUP, 2, 1))
    m1, m1k, f2, i1, g2 = _fft_tables(ls)
    u2 = u_seq.reshape(nb, ra, bf * 3 * hyw)
    k2 = kern.reshape(na, bf * 2 * hyw)

    (ph,) = _matmul(
        m1k, k2,
        pl.BlockSpec((None, 2 * na, na), lambda g, j, i: (j // (2 * nct), 0, 0)),
        pl.BlockSpec((na, tc), lambda g, j, i: (0, j)),
        (1, bf * 2 * nct, 1),
        [(jax.ShapeDtypeStruct((bf, 2 * na, 2 * hyw), BF),
          pl.BlockSpec((None, 2 * na, tc), lambda g, j, i: (j // (2 * nct), 0, j % (2 * nct))))],
        _ep_store(BF), name="hyena_fft_kern_s1")
    hspec = pl.pallas_call(
        _fft_spec_kernel,
        grid=(na // kg, 2 * nct),
        in_specs=[pl.BlockSpec((kg, 2 * bf, tc), lambda k, j: (k, 0, j)),
                  pl.BlockSpec((2 * bf, 2 * bf), lambda k, j: (0, 0))],
        out_specs=pl.BlockSpec((kg, 2 * bf, tc), lambda k, j: (k, 0, j)),
        out_shape=jax.ShapeDtypeStruct((na, 2 * bf, 2 * hyw), BF),
        compiler_params=_cparams(2),
        name="hyena_fft_kern_s2",
    )(_fft_regroup(ph, bf, na), f2)

    skip3 = hy_skip.reshape(2, 1, hyw)

    def conv(z2, z_sections, z_sec, conv_idx, mul_sec):
        zblk = lambda j: (j // nct) * (z_sections * nct) + z_sec * nct + j % nct
        ublk = lambda j: (j // nct) * (3 * nct) + mul_sec * nct + j % nct
        (p,) = _matmul(
            m1, z2,
            pl.BlockSpec((None, 2 * na, ra), lambda g, j, i: (j // nct, 0, 0)),
            pl.BlockSpec((None, ra, tc), lambda g, j, i: (g, 0, zblk(j))),
            (nb, bf * nct, 1),
            [(jax.ShapeDtypeStruct((nb, bf, 2 * na, hyw), BF),
              pl.BlockSpec((None, None, 2 * na, tc), lambda g, j, i: (g, j // nct, 0, j % nct)))],
            _ep_store(BF), name="hyena_fft_s1")
        q = pl.pallas_call(
            _fft_mid_kernel,
            grid=(na // kg, nct, nb),
            in_specs=[pl.BlockSpec((None, kg, 2 * bf, tc), lambda k, j, g: (g, k, 0, j)),
                      pl.BlockSpec((kg, 2 * bf, tc), lambda k, j, g: (k, 0, conv_idx * nct + j)),
                      pl.BlockSpec((2 * bf, 2 * bf), lambda k, j, g: (0, 0)),
                      pl.BlockSpec((2 * bf, 2 * bf), lambda k, j, g: (0, 0))],
            out_specs=pl.BlockSpec((None, kg, 2 * bf, tc), lambda k, j, g: (g, k, 0, j)),
            out_shape=jax.ShapeDtypeStruct((nb, na, 2 * bf, hyw), BF),
            compiler_params=_cparams(3),
            name="hyena_fft_mid",
        )(_fft_regroup(p, bf, na), hspec, f2, i1)

        def finish(acc, extras, outs):
            z = extras[0][...].astype(F32)
            xm = extras[2][...].astype(F32)
            outs[0][...] = (xm * (acc + z * extras[1][...])).astype(BF)

        (y,) = _matmul(
            g2, _fft_ungroup(q, bf, na),
            pl.BlockSpec((None, ra, 2 * na), lambda g, j, i: (j // nct, 0, 0)),
            pl.BlockSpec((None, None, 2 * na, tc), lambda g, j, i: (g, j // nct, 0, j % nct)),
            (nb, bf * nct, 1),
            [(jax.ShapeDtypeStruct((nb, ra, bf * hyw), BF),
              pl.BlockSpec((None, ra, tc), lambda g, j, i: (g, 0, j)))],
            finish,
            extras=[(z2, pl.BlockSpec((None, ra, tc), lambda g, j, i: (g, 0, zblk(j)))),
                    (skip3, pl.BlockSpec((None, 1, tc), lambda g, j, i: (conv_idx, 0, j % nct))),
                    (u2, pl.BlockSpec((None, ra, tc), lambda g, j, i: (g, 0, ublk(j))))],
            name="hyena_fft_s3")
        return y

    z_mid = conv(u2, 3, 0, 0, 1)
    y = conv(z_mid, 1, 0, 1, 2)
    return y.reshape(2 * nb, ls, hyw)[:nb_in]


def _moe_up_kernel(x_ref, wg_ref, wu_ref, comb_ref, o_ref):
    x = x_ref[...]
    comb = comb_ref[...]
    lane = lax.broadcasted_iota(jnp.int32, comb.shape, 1)
    parts = []
    for s in range(wg_ref.shape[0]):
        g = jnp.dot(x, wg_ref[s], preferred_element_type=F32)
        u = jnp.dot(x, wu_ref[s], preferred_element_type=F32)
        e = pl.program_id(0) * wg_ref.shape[0] + s
        scale = jnp.sum(jnp.where(lane == e, comb, 0.0), axis=1, keepdims=True)
        parts.append((g * jax.nn.sigmoid(g) * u * scale).astype(o_ref.dtype))
    o_ref[...] = jnp.concatenate(parts, axis=1)


MOE_EXPERTS_PER_STEP = 2


def _moe_up(u, w_gate, w_up, comb, l):
    m, d = u.shape
    _, ne, _, f = w_gate.shape
    tm = _pick(m, (1088, 512, 256))
    es = MOE_EXPERTS_PER_STEP
    wspec = pl.BlockSpec((None, es, d, f), lambda e, i: (l, e, 0, 0))
    return pl.pallas_call(
        _moe_up_kernel,
        grid=(ne // es, m // tm),
        in_specs=[pl.BlockSpec((tm, d), lambda e, i: (i, 0)), wspec, wspec,
                  pl.BlockSpec((tm, LANES), lambda e, i: (i, 0))],
        out_specs=pl.BlockSpec((tm, es * f), lambda e, i: (i, e)),
        out_shape=jax.ShapeDtypeStruct((m, ne * f), BF),
        compiler_params=_cparams(2),
        name="moe_up",
    )(u, w_gate, w_up, comb)


def _row_gate(gt_ref, tm, t_len, ctx_len, n_batch):
    r = lax.broadcasted_iota(jnp.int32, (tm, 1), 0) + pl.program_id(2) * tm
    gate = jnp.zeros((tm, gt_ref.shape[1]), F32)
    for bi in range(n_batch):
        in_b = (r >= bi * t_len) & (r < (bi + 1) * t_len)
        is_c = r < bi * t_len + ctx_len
        gate = jnp.where(in_b & is_c, gt_ref[0:1, :], gate)
        gate = jnp.where(in_b & ~is_c, gt_ref[1 + bi:2 + bi, :], gate)
    return gate


def _final_norm_kernel(h_ref, g_ref, o_ref):
    xf = h_ref[...]
    o_ref[...] = xf * lax.rsqrt(jnp.mean(xf * xf, axis=-1, keepdims=True) + NORM_EPS) * g_ref[...]


def _final_norm(h, gain, n_batch, t_len, ctx_len):
    m, d = h.shape
    tt = TOKEN_TILE
    n_t, n_c = t_len // tt, ctx_len // tt
    n_x = n_t - n_c
    return pl.pallas_call(
        _final_norm_kernel,
        grid=(n_batch, n_x),
        in_specs=[pl.BlockSpec((tt, d), lambda bi, i: (bi * n_t + n_c + i, 0)),
                  pl.BlockSpec((1, d), lambda bi, i: (0, 0))],
        out_specs=pl.BlockSpec((None, tt, d), lambda bi, i: (bi, i, 0)),
        out_shape=jax.ShapeDtypeStruct((n_batch, t_len - ctx_len, d), F32),
        compiler_params=_cparams(2),
        name="final_norm",
    )(h, gain.reshape(1, d))


def kernel(x, c, ctx, c_ctx, ada_down, ada_up, ada_bias, norm_mix, norm_ffn, w_in, qkv_conv, a_log, dt_bias, o_norm, hy_conv, hy_w1, hy_b1, hy_f1, hy_w2, hy_b2, hy_f2, hy_w3, hy_decay, hy_skip, w_br_a, w_br_b, w_out, router_group, router_group_bias, router_expert, router_expert_bias, exp_gate, exp_up, exp_down, final_norm):
    n_batch, seq, d = x.shape
    ctx_len = ctx.shape[1]
    t_len = ctx_len + seq
    m = n_batch * t_len
    depth = w_in.shape[0]
    n_heads = a_log.shape[2]
    a_width = n_heads * LANES
    hyw = hy_skip.shape[2]
    n_exp = exp_gate.shape[1]
    dg_off = 4 * a_width
    hy_off = dg_off + 4 * n_heads
    tail_w = 3 * hyw + 2 * d
    n_t, n_c = t_len // TOKEN_TILE, ctx_len // TOKEN_TILE
    nc = t_len // CHUNK
    ncp = -(-nc // 8) * 8

    h = jnp.concatenate([ctx, x], axis=1).reshape(m, d)

    conds = jnp.zeros((8, d), F32).at[0].set(c_ctx).at[1:1 + n_batch].set(c)
    mod_all = _adaln_all(conds, ada_down, ada_up, ada_bias)
    mod_all = mod_all.reshape(depth, 8, 6, d)[:, :1 + n_batch]
    mod_all = jnp.pad(mod_all, ((0, 0), (0, 0), (0, 2), (0, 0)))

    tm = _pick(m, (1088, 512, 256))
    tmp = tm
    tn = 512

    w_qkvz = w_in[:, :, :dg_off].astype(BF)
    w_tail = w_in[:, :, hy_off:].astype(BF)
    exp_gate_b = exp_gate.astype(BF)
    exp_up_b = exp_up.astype(BF)

    def gate_table(mod_l, idx):
        return jnp.pad(mod_l[:, idx, :], ((0, 8 - (1 + n_batch)), (0, 0)))

    def residual_ep(acc, extras, outs):
        outs[0][...] = extras[0][...] + _row_gate(extras[1], acc.shape[0], t_len, ctx_len, n_batch) * acc

    for l in range(depth):
        mod_l = mod_all[l]
        u = _norm_mod(h, norm_mix[l], mod_l, 0, n_t, n_c)
        (p1,) = _matmul(
            u, w_qkvz,
            pl.BlockSpec((tmp, d), lambda g, j, i: (i, 0)),
            pl.BlockSpec((None, d, tn), lambda g, j, i, l=l: (l, 0, j)),
            (1, dg_off // tn, m // tmp),
            [(jax.ShapeDtypeStruct((m, dg_off), BF), pl.BlockSpec((tmp, tn), lambda g, j, i: (i, j)))],
            _ep_store(BF), name="proj_qkvz")
        (p_tail,) = _matmul(
            u, w_tail,
            pl.BlockSpec((tmp, d), lambda g, j, i: (i, 0)),
            pl.BlockSpec((None, d, tn), lambda g, j, i, l=l: (l, 0, j)),
            (1, tail_w // tn, m // tmp),
            [(jax.ShapeDtypeStruct((m, tail_w), BF), pl.BlockSpec((tmp, tn), lambda g, j, i: (i, j)))],
            _ep_store(BF), name="proj_tail")
        w_dg_t = w_in[l, :, dg_off:hy_off].T
        a_exp = jnp.concatenate([jnp.exp(a_log[l].reshape(-1)), jnp.zeros((2 * n_heads,), F32)])[:, None]
        dt_b = jnp.concatenate([dt_bias[l].reshape(-1), jnp.zeros((2 * n_heads,), F32)])[:, None]
        gates = _gates(u, w_dg_t, a_exp, dt_b)

        g5 = gates.reshape(4, n_heads, n_batch, nc, CHUNK).transpose(2, 1, 0, 3, 4)
        g_rows = jnp.pad(g5, ((0, 0), (0, 0), (0, 0), (0, ncp - nc), (0, 0)))
        g_cols = jnp.pad(jnp.swapaxes(g5, 3, 4), ((0, 0), (0, 0), (0, 0), (0, 0), (0, LANES - nc)))

        p1_3 = p1.reshape(n_batch, t_len, dg_off)
        conv9 = qkv_conv[l].reshape(9, 3 * a_width)
        du, dw, dq, dk, da, dgl = _delta_prep(p1_3, conv9, g_rows, g_cols, n_heads, ctx_len)
        o_f, o_b = _delta_scan(du, dw, dq, dk, da, dgl, ctx_len)
        o_a = _delta_out(o_f, o_b, p1_3, o_norm[l], n_heads).reshape(m, a_width)

        p_tail3 = p_tail.reshape(n_batch, t_len, tail_w)
        u_c, u_x = _hy_conv(p_tail3, hy_conv[l], ctx_len, 3 * hyw)
        filt_args = (hy_w1[l], hy_b1[l], hy_f1[l], hy_w2[l], hy_b2[l], hy_f2[l], hy_w3[l], hy_decay[l])
        y_x = _hy_long_convs_fft(u_x, _hy_filter_stack(seq, *filt_args, hyw, True), hy_skip[l], hyw)
        y_c = _hy_long_convs(u_c, _hy_filter_stack(ctx_len, *filt_args, hyw, False), hy_skip[l], hyw)
        y_b = jnp.concatenate([y_c, y_x], axis=1).reshape(m, hyw)

        ga_blk = (3 * hyw) // tn
        gb_blk = (3 * hyw + d) // tn

        def gated_ep(acc, extras, outs):
            outs[0][...] = jax.nn.sigmoid(extras[0][...].astype(F32)) * acc

        (br,) = _matmul(
            o_a, w_br_a,
            pl.BlockSpec((tm, a_width), lambda g, j, i: (i, 0)),
            pl.BlockSpec((None, a_width, tn), lambda g, j, i, l=l: (l, 0, j)),
            (1, d // tn, m // tm),
            [(jax.ShapeDtypeStruct((m, d), F32), pl.BlockSpec((tm, tn), lambda g, j, i: (i, j)))],
            gated_ep,
            extras=[(p_tail, pl.BlockSpec((tm, tn), lambda g, j, i: (i, ga_blk + j)))],
            name="branch_a")

        def merge_ep(acc, extras, outs):
            outs[0][...] = (extras[1][...] + jax.nn.sigmoid(extras[0][...].astype(F32)) * acc).astype(BF)

        (merged,) = _matmul(
            y_b, w_br_b,
            pl.BlockSpec((tm, hyw), lambda g, j, i: (i, 0)),
            pl.BlockSpec((None, hyw, tn), lambda g, j, i, l=l: (l, 0, j)),
            (1, d // tn, m // tm),
            [(jax.ShapeDtypeStruct((m, d), BF), pl.BlockSpec((tm, tn), lambda g, j, i: (i, j)))],
            merge_ep,
            extras=[(p_tail, pl.BlockSpec((tm, tn), lambda g, j, i: (i, gb_blk + j))),
                    (br, pl.BlockSpec((tm, tn), lambda g, j, i: (i, j)))],
            name="branch_b_merge")

        (h,) = _matmul(
            merged, w_out,
            pl.BlockSpec((tm, d), lambda g, j, i: (i, 0)),
            pl.BlockSpec((None, d, tn), lambda g, j, i, l=l: (l, 0, j)),
            (1, d // tn, m // tm),
            [(jax.ShapeDtypeStruct((m, d), F32), pl.BlockSpec((tm, tn), lambda g, j, i: (i, j)))],
            residual_ep,
            extras=[(h, pl.BlockSpec((tm, tn), lambda g, j, i: (i, j))),
                    (gate_table(mod_l, 2), pl.BlockSpec((8, tn), lambda g, j, i: (0, j)))],
            name="out_proj")

        w_router = jnp.pad(jnp.concatenate([router_expert[l], router_group[l]], axis=1),
                           ((0, 0), (0, LANES - n_exp - N_GROUPS)))
        b_router = jnp.pad(jnp.concatenate([router_expert_bias[l], router_group_bias[l]]),
                           (0, LANES - n_exp - N_GROUPS)).reshape(1, LANES)
        u2, comb = _norm_mod_router(h, norm_ffn[l], mod_l, 3, n_t, n_c, w_router, b_router, n_exp, N_GROUPS)
        hid = _moe_up(u2, exp_gate_b, exp_up_b, comb, l)
        kf = hid.shape[1]
        tnd = _pick(d, (256, 128))
        (h,) = _matmul(
            hid, exp_down.reshape(depth, kf, d),
            pl.BlockSpec((tm, kf), lambda g, j, i: (i, 0)),
            pl.BlockSpec((None, kf, tnd), lambda g, j, i, l=l: (l, 0, j)),
            (1, d // tnd, m // tm),
            [(jax.ShapeDtypeStruct((m, d), F32), pl.BlockSpec((tm, tnd), lambda g, j, i: (i, j)))],
            residual_ep,
            extras=[(h, pl.BlockSpec((tm, tnd), lambda g, j, i: (i, j))),
                    (gate_table(mod_l, 5), pl.BlockSpec((8, tnd), lambda g, j, i: (0, j)))],
            name="moe_down")

    return _final_norm(h, final_norm, n_batch, t_len, ctx_len)
```

```python
import functools
import math

import jax
import jax.numpy as jnp
from jax import lax
from jax.experimental import pallas as pl
from jax.experimental.pallas import tpu as pltpu

F32 = jnp.float32
BF = jnp.bfloat16

GRID_W = 64
CHUNK = 64
N_GROUPS = 4
EXPERTS_PER_GROUP = 8
HY_N_FILT = 4
NORM_EPS = 1e-6
TOKEN_TILE = 256
LANES = 128
CONV_PAD = 72
VMEM_LIMIT = 56 * 1024 * 1024


def _cparams(n_grid):
    return pltpu.CompilerParams(dimension_semantics=("arbitrary",) * n_grid,
                                vmem_limit_bytes=VMEM_LIMIT)


def _split_bf16(a):
    hi = a.astype(BF)
    lo = (a - hi.astype(F32)).astype(BF)
    return hi, lo


def _dot3(a, b):
    ah, al = _split_bf16(a)
    bh, bl = _split_bf16(b)
    d = functools.partial(jnp.dot, preferred_element_type=F32)
    return d(ah, bh) + (d(ah, bl) + d(al, bh))


def _dotx(a, b):
    return jnp.dot(a, b, precision=lax.Precision.HIGHEST, preferred_element_type=F32)


def _bdot(a, b):
    return jnp.dot(a.astype(BF), b.astype(BF), preferred_element_type=F32)


def _mm_kernel(*refs, n_extra, n_out, cast_w, prologue, epilogue):
    x_ref, w_ref = refs[0], refs[1]
    extras = refs[2:2 + n_extra]
    outs = refs[2 + n_extra:2 + n_extra + n_out]
    scratch = refs[2 + n_extra + n_out:]
    if cast_w:
        wbf = scratch[0]

        @pl.when(pl.program_id(2) == 0)
        def _():
            k = w_ref.shape[0]
            step = 512 if k % 512 == 0 else k
            if k == step:
                wbf[...] = w_ref[...].astype(BF)
            else:
                def it(r, c):
                    r0 = pl.multiple_of(r * step, step)
                    wbf[pl.ds(r0, step), :] = w_ref[pl.ds(r0, step), :].astype(BF)
                    return c
                lax.fori_loop(0, k // step, it, 0)
        w = wbf[...]
    else:
        w = w_ref[...]
    x = x_ref[...]
    if prologue is not None:
        x = prologue(x)
    if x.dtype != BF:
        x = x.astype(BF)
    acc = jnp.dot(x, w, preferred_element_type=F32)
    epilogue(acc, extras, outs)


def _matmul(x, w, x_spec, w_spec, grid, outs, epilogue, extras=(), prologue=None, name=None):
    cast_w = w.dtype != BF
    kw, tn = [d for d in w_spec.block_shape if d is not None]
    scratch = [pltpu.VMEM((kw, tn), BF)] if cast_w else []
    body = functools.partial(_mm_kernel, n_extra=len(extras), n_out=len(outs), cast_w=cast_w,
                             prologue=prologue, epilogue=epilogue)
    res = pl.pallas_call(
        body,
        grid=grid,
        in_specs=[x_spec, w_spec] + [s for _, s in extras],
        out_specs=[s for _, s in outs],
        out_shape=[o for o, _ in outs],
        scratch_shapes=scratch,
        compiler_params=_cparams(3),
        name=name,
    )(x, w, *[a for a, _ in extras])
    return res


def _ep_store(dtype):
    def ep(acc, extras, outs):
        outs[0][...] = acc.astype(dtype)
    return ep


def _pick(n, prefs):
    for p in prefs:
        if n % p == 0:
            return p
    return n


def _adaln_all(conds, ada_down, ada_up, ada_bias):
    nl, d, r = ada_down.shape
    n6 = ada_up.shape[2]
    (t,) = _matmul(
        conds, ada_down,
        pl.BlockSpec((8, d), lambda g, j, i: (0, 0)),
        pl.BlockSpec((None, d, r), lambda g, j, i: (g, 0, 0)),
        (nl, 1, 1),
        [(jax.ShapeDtypeStruct((nl, 8, r), F32), pl.BlockSpec((None, 8, r), lambda g, j, i: (g, 0, 0)))],
        _ep_store(F32), prologue=lambda v: v * jax.nn.sigmoid(v), name="adaln_down")
    tn = _pick(n6, (2048, 1024, 512, 256, 128))

    def ep(acc, extras, outs):
        outs[0][...] = acc + extras[0][...]

    (mod,) = _matmul(
        t, ada_up,
        pl.BlockSpec((None, 8, r), lambda g, j, i: (g, 0, 0)),
        pl.BlockSpec((None, r, tn), lambda g, j, i: (g, 0, j)),
        (nl, n6 // tn, 1),
        [(jax.ShapeDtypeStruct((nl, 8, n6), F32), pl.BlockSpec((None, 8, tn), lambda g, j, i: (g, 0, j)))],
        ep, extras=[(ada_bias.reshape(nl, 1, n6), pl.BlockSpec((None, 1, tn), lambda g, j, i: (g, 0, j)))],
        name="adaln_up")
    return mod


def _mod_row_map(n_t, n_c):
    def m(i):
        return (jnp.where(i % n_t < n_c, 0, 1 + i // n_t), 0, 0)
    return m


def _norm_mod_kernel(h_ref, g_ref, mod_ref, u_ref, *, off):
    xf = h_ref[...]
    y = xf * lax.rsqrt(jnp.mean(xf * xf, axis=-1, keepdims=True) + NORM_EPS) * g_ref[...]
    sh = mod_ref[off:off + 1, :]
    sc = mod_ref[off + 1:off + 2, :]
    u_ref[...] = (y * (1.0 + sc) + sh).astype(u_ref.dtype)


def _norm_mod(h, gain, mod_l, off, n_t, n_c):
    m, d = h.shape
    tt = TOKEN_TILE
    return pl.pallas_call(
        functools.partial(_norm_mod_kernel, off=off),
        grid=(m // tt,),
        in_specs=[pl.BlockSpec((tt, d), lambda i: (i, 0)),
                  pl.BlockSpec((1, d), lambda i: (0, 0)),
                  pl.BlockSpec((None, 8, d), _mod_row_map(n_t, n_c))],
        out_specs=pl.BlockSpec((tt, d), lambda i: (i, 0)),
        out_shape=jax.ShapeDtypeStruct((m, d), BF),
        compiler_params=_cparams(1),
        name="norm_mod",
    )(h, gain.reshape(1, d), mod_l)


def _router_kernel(h_ref, g_ref, mod_ref, wr_ref, br_ref, u_ref, comb_ref, *, off, n_exp, n_grp):
    xf = h_ref[...]
    y = xf * lax.rsqrt(jnp.mean(xf * xf, axis=-1, keepdims=True) + NORM_EPS) * g_ref[...]
    sh = mod_ref[off:off + 1, :]
    sc = mod_ref[off + 1:off + 2, :]
    u = y * (1.0 + sc) + sh
    u_ref[...] = u.astype(u_ref.dtype)
    logits = _dot3(u, wr_ref[...]) + br_ref[...]
    lane = lax.broadcasted_iota(jnp.int32, logits.shape, 1)
    neg = jnp.float32(-jnp.inf)
    big = jnp.int32(1 << 20)
    is_g = (lane >= n_exp) & (lane < n_exp + n_grp)
    lg = jnp.where(is_g, logits, neg)
    eg = jnp.exp(lg - jnp.max(lg, axis=1, keepdims=True))
    pg = eg / jnp.sum(eg, axis=1, keepdims=True)
    p_top = jnp.max(pg, axis=1, keepdims=True)
    g_idx = jnp.min(jnp.where(is_g & (pg == p_top), lane, big), axis=1, keepdims=True) - n_exp
    per = n_exp // n_grp
    in_grp = (lane >= g_idx * per) & (lane < (g_idx + 1) * per)
    le = jnp.where(in_grp, logits, neg)
    v1 = jnp.max(le, axis=1, keepdims=True)
    i1 = jnp.min(jnp.where(le == v1, lane, big), axis=1, keepdims=True)
    le2 = jnp.where(lane == i1, neg, le)
    v2 = jnp.max(le2, axis=1, keepdims=True)
    i2 = jnp.min(jnp.where(le2 == v2, lane, big), axis=1, keepdims=True)
    e2 = jnp.exp(v2 - v1)
    w1 = p_top / (1.0 + e2)
    w2 = p_top * e2 / (1.0 + e2)
    comb_ref[...] = jnp.where(lane == i1, w1, 0.0) + jnp.where(lane == i2, w2, 0.0)


def _norm_mod_router(h, gain, mod_l, off, n_t, n_c, w_router, b_router, n_exp, n_grp):
    m, d = h.shape
    tt = TOKEN_TILE
    return pl.pallas_call(
        functools.partial(_router_kernel, off=off, n_exp=n_exp, n_grp=n_grp),
        grid=(m // tt,),
        in_specs=[pl.BlockSpec((tt, d), lambda i: (i, 0)),
                  pl.BlockSpec((1, d), lambda i: (0, 0)),
                  pl.BlockSpec((None, 8, d), _mod_row_map(n_t, n_c)),
                  pl.BlockSpec((d, LANES), lambda i: (0, 0)),
                  pl.BlockSpec((1, LANES), lambda i: (0, 0))],
        out_specs=[pl.BlockSpec((tt, d), lambda i: (i, 0)),
                   pl.BlockSpec((tt, LANES), lambda i: (i, 0))],
        out_shape=[jax.ShapeDtypeStruct((m, d), BF), jax.ShapeDtypeStruct((m, LANES), F32)],
        compiler_params=_cparams(1),
        name="norm_mod_router",
    )(h, gain.reshape(1, d), mod_l, w_router, b_router)


def _gates_kernel(w_ref, u_ref, aexp_ref, dtb_ref, o_ref, *, n_decay):
    acc = lax.dot_general(w_ref[...].astype(BF), u_ref[...], (((1,), (1,)), ((), ())),
                          preferred_element_type=F32)
    row = lax.broadcasted_iota(jnp.int32, acc.shape, 0)
    z = acc + dtb_ref[...]
    softplus = jnp.maximum(z, 0.0) + jnp.log(1.0 + jnp.exp(-jnp.abs(z)))
    g = -aexp_ref[...] * softplus
    o_ref[...] = jnp.where(row < n_decay, g, jax.nn.sigmoid(acc))


def _gates(u, w_dg_t, a_exp, dt_b):
    m, d = u.shape
    r = w_dg_t.shape[0]
    tm = _pick(m, (512, 256))
    return pl.pallas_call(
        functools.partial(_gates_kernel, n_decay=r // 2),
        grid=(m // tm,),
        in_specs=[pl.BlockSpec((r, d), lambda i: (0, 0)),
                  pl.BlockSpec((tm, d), lambda i: (i, 0)),
                  pl.BlockSpec((r, 1), lambda i: (0, 0)),
                  pl.BlockSpec((r, 1), lambda i: (0, 0))],
        out_specs=pl.BlockSpec((r, tm), lambda i: (0, i)),
        out_shape=jax.ShapeDtypeStruct((r, m), F32),
        compiler_params=_cparams(1),
        name="delta_gates",
    )(w_dg_t, u, a_exp, dt_b)


def _conv_silu_tile(pad_ref, cw_ref, t0, n, seq_start, seq_len, width):
    one_row = width == seq_len
    r = lax.broadcasted_iota(jnp.int32, (n, 1), 0) + (t0 - seq_start)
    col = lax.rem(r, width)
    acc = jnp.zeros((n, LANES), F32)
    for dj in (-1, 0, 1):
        part = jnp.zeros((n, LANES), F32)
        for di in (-1, 0, 1):
            if one_row and di != 0:
                continue
            off = di * width + dj
            tap = pad_ref[pl.ds(CONV_PAD + t0 + off, n), :]
            wrow = cw_ref[(di + 1) * 3 + (dj + 1):(di + 1) * 3 + (dj + 1) + 1, :]
            ok = (r + off >= 0) & (r + off < seq_len)
            part = part + jnp.where(ok, tap, 0.0) * wrow
        ok_c = (col + dj >= 0) & (col + dj < width)
        acc = acc + jnp.where(ok_c, part, 0.0)
    return acc * jax.nn.sigmoid(acc)


def _bmm(a, b):
    return jnp.einsum('gik,gkj->gij', a.astype(BF), b.astype(BF), preferred_element_type=F32)


def _bmm_nt(a, b):
    return jnp.einsum('gik,gjk->gij', a.astype(BF), b.astype(BF), preferred_element_type=F32)


def _neumann_inverse(lmat, eye):
    x = -lmat
    p = eye + x
    n = lmat.shape[-1]
    k = 1
    while 2 * k < n:
        x = _bmm(x, x)
        p = p + _bmm(p, x)
        k *= 2
    return p


def _delta_prep_kernel(pq_ref, pk_ref, pv_ref, cq_ref, ck_ref, cv_ref, g_ref, gt_ref,
                       u_ref, w_ref, qd_ref, kd_ref, a_ref, gl_ref,
                       pad_ref, q_s, k_s, v_s, gcr_s, gct_s,
                       *, t_len, ctx_len, n_chunks, group):
    hd = q_s.shape[1]
    tt = TOKEN_TILE
    d = pl.program_id(2)

    @pl.when(d == 0)
    def _():
        zpad = jnp.zeros((CONV_PAD, LANES), F32)
        pad_ref[pl.ds(0, CONV_PAD), :] = zpad
        pad_ref[pl.ds(CONV_PAD + t_len, CONV_PAD), :] = zpad
        for src, cw, dst, kind in ((pq_ref, cq_ref, q_s, "q"), (pk_ref, ck_ref, k_s, "k"), (pv_ref, cv_ref, v_s, "v")):
            pad_ref[pl.ds(CONV_PAD, t_len), :] = src[...].astype(F32)
            for ti in range(t_len // tt):
                t0 = ti * tt
                if t0 < ctx_len:
                    a = _conv_silu_tile(pad_ref, cw, t0, tt, 0, ctx_len, ctx_len)
                else:
                    a = _conv_silu_tile(pad_ref, cw, t0, tt, ctx_len, t_len - ctx_len, GRID_W)
                if kind != "v":
                    a = a * lax.rsqrt(jnp.sum(a * a, axis=-1, keepdims=True) + NORM_EPS)
                if kind == "q":
                    a = a * (hd ** -0.5)
                dst[pl.ds(t0, tt), :] = a

    ii = lax.broadcasted_iota(jnp.int32, (CHUNK, CHUNK), 0)
    jj = lax.broadcasted_iota(jnp.int32, (CHUNK, CHUNK), 1)
    sdiff = (ii - jj) * (1 - 2 * d)
    eye = (ii == jj).astype(F32)
    incl = sdiff >= 0
    strict = sdiff > 0
    gcr_s[...] = _dotx(g_ref[d], (sdiff <= 0).astype(F32))
    gct_s[...] = _dotx(incl.astype(F32), gt_ref[d])
    gtot = jnp.where(d == 0, gcr_s[:, CHUNK - 1:CHUNK], gcr_s[:, 0:1])
    gl_ref[...] = jnp.broadcast_to(jnp.exp(gtot), gl_ref.shape)

    gc = group
    rows = gc * CHUNK
    lane_c = lax.broadcasted_iota(jnp.int32, (CHUNK, LANES), 1)
    eye_hd = (lax.broadcasted_iota(jnp.int32, (hd, hd), 0)
              == lax.broadcasted_iota(jnp.int32, (hd, hd), 1)).astype(BF)[None]

    def body(it, carry):
        c0 = it * gc
        t0 = pl.multiple_of(c0 * CHUNK, rows)
        q = q_s[pl.ds(t0, rows), :].reshape(gc, CHUNK, hd)
        k = k_s[pl.ds(t0, rows), :].reshape(gc, CHUNK, hd)
        v = v_s[pl.ds(t0, rows), :].reshape(gc, CHUNK, hd)
        gct = gct_s[...]
        bt = gt_ref[2 + d]
        gcols, bcols, grows, glasts = [], [], [], []
        for g in range(gc):
            sel = lane_c == c0 + g
            gcols.append(jnp.sum(jnp.where(sel, gct, 0.0), axis=1, keepdims=True))
            bcols.append(jnp.sum(jnp.where(sel, bt, 0.0), axis=1, keepdims=True))
            grow_g = gcr_s[pl.ds(c0 + g, 1), :]
            grows.append(grow_g)
            glasts.append(jnp.where(d == 0, grow_g[:, CHUNK - 1:CHUNK], grow_g[:, 0:1]))
        gcol = jnp.stack(gcols)
        bcol = jnp.stack(bcols)
        grow = jnp.stack(grows)
        glast = jnp.stack(glasts)
        decay = jnp.where(incl, jnp.exp(jnp.where(incl, gcol - grow, 0.0)), 0.0)
        kb = k * bcol
        lmat = jnp.where(strict, _bmm_nt(kb, k) * decay, 0.0)
        tinv = _neumann_inverse(lmat, eye[None])
        egc = jnp.exp(gcol)
        uw = _bmm(tinv, jnp.concatenate([v * bcol, kb * egc], axis=2))
        attn = (_bmm_nt(q, k) * decay).astype(BF)
        kdec = k * jnp.exp(glast - gcol)
        kdec_t = _bmm_nt(jnp.broadcast_to(eye_hd, (gc, hd, hd)), kdec).astype(BF)
        u_ref[pl.ds(t0, rows), :] = uw[:, :, :hd].reshape(rows, hd)
        w_ref[pl.ds(t0, rows), :] = uw[:, :, hd:].astype(BF).reshape(rows, hd)
        qd_ref[pl.ds(t0, rows), :] = (q * egc).astype(BF).reshape(rows, hd)
        a_ref[pl.ds(t0, rows), :] = attn.reshape(rows, CHUNK)
        kd_ref[pl.ds(c0, gc)] = kdec_t
        return carry

    lax.fori_loop(0, n_chunks // group, body, 0)


def _delta_prep(p1, conv_w9, g_rows, g_cols, n_heads, ctx_len):
    b, t, _ = p1.shape
    hd = LANES
    nc = t // CHUNK
    ncp = g_rows.shape[3]
    group = max(g for g in range(1, 18) if nc % g == 0)
    h = n_heads
    kern = functools.partial(_delta_prep_kernel, t_len=t, ctx_len=ctx_len, n_chunks=nc, group=group)
    tok = lambda off: pl.BlockSpec((None, t, hd), lambda bi, hi, di: (bi, 0, off + hi))
    cws = lambda off: pl.BlockSpec((9, hd), lambda bi, hi, di: (0, off + hi))
    o5 = lambda last: pl.BlockSpec((None, None, None, t, last), lambda bi, hi, di: (di, bi, hi, 0, 0))
    return pl.pallas_call(
        kern,
        grid=(b, h, 2),
        in_specs=[tok(0), tok(h), tok(2 * h), cws(0), cws(h), cws(2 * h),
                  pl.BlockSpec((None, None, 4, ncp, CHUNK), lambda bi, hi, di: (bi, hi, 0, 0, 0)),
                  pl.BlockSpec((None, None, 4, CHUNK, LANES), lambda bi, hi, di: (bi, hi, 0, 0, 0))],
        out_specs=[o5(hd), o5(hd), o5(hd),
                   pl.BlockSpec((None, None, None, nc, hd, CHUNK), lambda bi, hi, di: (di, bi, hi, 0, 0, 0)),
                   o5(CHUNK),
                   pl.BlockSpec((None, None, None, ncp, LANES), lambda bi, hi, di: (di, bi, hi, 0, 0))],
        out_shape=[jax.ShapeDtypeStruct((2, b, h, t, hd), F32),
                   jax.ShapeDtypeStruct((2, b, h, t, hd), BF),
                   jax.ShapeDtypeStruct((2, b, h, t, hd), BF),
                   jax.ShapeDtypeStruct((2, b, h, nc, hd, CHUNK), BF),
                   jax.ShapeDtypeStruct((2, b, h, t, CHUNK), BF),
                   jax.ShapeDtypeStruct((2, b, h, ncp, LANES), F32)],
        scratch_shapes=[pltpu.VMEM((t + 2 * CONV_PAD, LANES), F32),
                        pltpu.VMEM((t, hd), F32), pltpu.VMEM((t, hd), F32), pltpu.VMEM((t, hd), F32),
                        pltpu.VMEM((ncp, CHUNK), F32), pltpu.VMEM((CHUNK, LANES), F32)],
        compiler_params=_cparams(3),
        name="delta_prep",
    )(p1, p1, p1, conv_w9, conv_w9, conv_w9, g_rows, g_cols)


def _delta_scan_kernel(uf, wf, qf, kf, af, gf, ub, wb, qb, kb, ab, gb, of_ref, ob_ref, s_ref, *, hg):
    @pl.when(pl.program_id(2) == 0)
    def _():
        s_ref[...] = jnp.zeros_like(s_ref)

    cs = uf.shape[1]
    s = s_ref[...]
    for t in range(cs):
        tb = cs - 1 - t
        cat = lambda a, b: jnp.concatenate([a[:, t], b[:, tb]], axis=0)
        sb = s.astype(BF)
        vn = cat(uf, ub) - _bmm(cat(wf, wb), sb)
        vnb = vn.astype(BF)
        o = _bmm(cat(qf, qb), sb) + _bmm(cat(af, ab), vnb)
        s = s * cat(gf, gb) + _bmm(cat(kf, kb), vnb)
        of_ref[:, t] = o[:hg]
        ob_ref[:, tb] = o[hg:]
    s_ref[...] = s


def _delta_scan(u, w, qd, kd, a, gl, ctx_len):
    _, b, h, t, hd = u.shape
    nc = t // CHUNK
    ncc = ctx_len // CHUNK
    hg = _pick(h, (8, 4, 2, 1))
    cs = max(c for c in (4, 2, 1) if ncc % c == 0 and nc % c == 0)
    nblk, nblk_c = nc // cs, ncc // cs
    r6 = lambda arr: arr.reshape(2, b, h, nc, CHUNK, arr.shape[-1])
    u6, w6, q6, k6, a6 = r6(u), r6(w), r6(qd), kd, r6(a)
    gl6 = gl[:, :, :, :nc].reshape(2, b, h, nc, 1, LANES)

    def cf(si):
        return si

    def cb(si):
        return jnp.where(si < nblk_c, nblk_c - 1 - si, nblk - 1 - (si - nblk_c))

    def spec(d, cmap, rows, last):
        return pl.BlockSpec((None, None, hg, cs, rows, last),
                            lambda bi, gi, si: (d, bi, gi, cmap(si), 0, 0))

    ins, specs = [], []
    for d, cmap in ((0, cf), (1, cb)):
        ins += [u6, w6, q6, k6, a6, gl6]
        specs += [spec(d, cmap, CHUNK, hd), spec(d, cmap, CHUNK, hd), spec(d, cmap, CHUNK, hd),
                  spec(d, cmap, hd, CHUNK), spec(d, cmap, CHUNK, CHUNK), spec(d, cmap, 1, LANES)]
    o_shape = jax.ShapeDtypeStruct((b, h, nc, CHUNK, hd), F32)
    ospec = lambda cmap: pl.BlockSpec((None, hg, cs, CHUNK, hd), lambda bi, gi, si: (bi, gi, cmap(si), 0, 0))
    of, ob = pl.pallas_call(
        functools.partial(_delta_scan_kernel, hg=hg),
        grid=(b, h // hg, nblk),
        in_specs=specs,
        out_specs=[ospec(cf), ospec(cb)],
        out_shape=[o_shape, o_shape],
        scratch_shapes=[pltpu.VMEM((2 * hg, hd, hd), F32)],
        compiler_params=_cparams(3),
        name="delta_scan",
    )(*ins)
    return of.reshape(b, h, t, hd), ob.reshape(b, h, t, hd)


def _delta_out_kernel(of_ref, ob_ref, z_ref, g_ref, o_ref):
    hd = of_ref.shape[2]
    for h in range(of_ref.shape[0]):
        o = of_ref[h] + ob_ref[h]
        o = o * lax.rsqrt(jnp.mean(o * o, axis=-1, keepdims=True) + NORM_EPS)
        z = z_ref[:, h * hd:(h + 1) * hd].astype(F32)
        o_ref[:, h * hd:(h + 1) * hd] = (o * g_ref[...] * (z * jax.nn.sigmoid(z))).astype(o_ref.dtype)


def _delta_out(of, ob, p1, o_gain, n_heads):
    b, h, t, hd = of.shape
    tt = TOKEN_TILE
    return pl.pallas_call(
        _delta_out_kernel,
        grid=(b, t // tt),
        in_specs=[pl.BlockSpec((None, h, tt, hd), lambda bi, ti: (bi, 0, ti, 0)),
                  pl.BlockSpec((None, h, tt, hd), lambda bi, ti: (bi, 0, ti, 0)),
                  pl.BlockSpec((None, tt, h * hd), lambda bi, ti: (bi, ti, 3)),
                  pl.BlockSpec((1, hd), lambda bi, ti: (0, 0))],
        out_specs=pl.BlockSpec((None, tt, h * hd), lambda bi, ti: (bi, ti, 0)),
        out_shape=jax.ShapeDtypeStruct((b, t, h * hd), BF),
        compiler_params=_cparams(2),
        name="delta_out",
    )(of, ob, p1, o_gain.reshape(1, hd))


def _hy_conv_kernel(p_ref, w_ref, oc_ref, ox_ref, pad_ref, *, t_len, ctx_len):
    tt = TOKEN_TILE
    z8 = jnp.zeros((8, pad_ref.shape[1]), F32)
    pad_ref[pl.ds(0, 8), :] = z8
    pad_ref[pl.ds(8 + t_len, 8), :] = z8
    pad_ref[pl.ds(8, t_len), :] = p_ref[...].astype(F32)
    w0, w1, w2 = w_ref[0:1, :], w_ref[1:2, :], w_ref[2:3, :]
    for ti in range(t_len // tt):
        t0 = ti * tt
        r = lax.broadcasted_iota(jnp.int32, (tt, 1), 0) + t0
        left = pad_ref[pl.ds(8 + t0 - 1, tt), :]
        mid = pad_ref[pl.ds(8 + t0, tt), :]
        right = pad_ref[pl.ds(8 + t0 + 1, tt), :]
        ok_l = (r != 0) & (r != ctx_len)
        ok_r = (r != ctx_len - 1) & (r != t_len - 1)
        y = jnp.where(ok_l, left, 0.0) * w0 + mid * w1 + jnp.where(ok_r, right, 0.0) * w2
        if t0 < ctx_len:
            oc_ref[pl.ds(t0, tt), :] = y.astype(oc_ref.dtype)
        else:
            ox_ref[pl.ds(t0 - ctx_len, tt), :] = y.astype(ox_ref.dtype)


def _hy_conv(p_tail, hy_conv_w, ctx_len, width3):
    b, t, _ = p_tail.shape
    tc = _pick(width3, (512, 256, 128))
    return pl.pallas_call(
        functools.partial(_hy_conv_kernel, t_len=t, ctx_len=ctx_len),
        grid=(b, width3 // tc),
        in_specs=[pl.BlockSpec((None, t, tc), lambda bi, j: (bi, 0, j)),
                  pl.BlockSpec((3, tc), lambda bi, j: (0, j))],
        out_specs=[pl.BlockSpec((None, ctx_len, tc), lambda bi, j: (bi, 0, j)),
                   pl.BlockSpec((None, t - ctx_len, tc), lambda bi, j: (bi, 0, j))],
        out_shape=[jax.ShapeDtypeStruct((b, ctx_len, width3), BF),
                   jax.ShapeDtypeStruct((b, t - ctx_len, width3), BF)],
        scratch_shapes=[pltpu.VMEM((t + 16, tc), F32)],
        compiler_params=_cparams(2),
        name="hyena_short_conv",
    )(p_tail, hy_conv_w)


def _hy_filter_kernel(ft_ref, w1_ref, b1_ref, f1_ref, w2_ref, b2_ref, f2_ref, w3_ref, dec_ref, o_ref, hid_s,
                      *, mask_lag0):
    part = pl.program_id(0)
    feats = ft_ref[...]

    @pl.when((pl.program_id(2) == 0) & (pl.program_id(3) == 0))
    def _():
        hid = jnp.sin(f1_ref[...] * (_dot3(feats, w1_ref[...]) + b1_ref[...]))
        hid_s[...] = jnp.sin(f2_ref[...] * (_dot3(hid, w2_ref[...]) + b2_ref[...]))

    filt = _bdot(hid_s[...], w3_ref[...])
    tpos = feats[:, 0:1]
    val = filt * jnp.exp(-tpos * jnp.abs(dec_ref[...]))
    if mask_lag0:
        row = lax.broadcasted_iota(jnp.int32, (val.shape[0], 1), 0) + pl.program_id(1) * val.shape[0]
        val = jnp.where((part == 1) & (row == 0), 0.0, val)
    o_ref[...] = val.astype(o_ref.dtype)


def _features(length, n_emb, k_pad, reverse_tail):
    t = jnp.linspace(0.0, 1.0, length, dtype=F32)
    n_bands = (n_emb - 1) // 2
    omega = 2.0 * math.pi * jnp.arange(length, dtype=F32) / length
    bands = jnp.linspace(1e-4, n_bands - 1, n_bands, dtype=F32)
    ang = omega[:, None] * bands[None, :]
    feats = jnp.concatenate([t[:, None], jnp.cos(ang), -jnp.sin(ang)], axis=-1)
    feats = jnp.pad(feats, ((0, 0), (0, k_pad - n_emb)))
    if reverse_tail:
        tail = feats[::-1]
    else:
        tail = jnp.concatenate([jnp.zeros((1, k_pad), F32), feats[:-1]], axis=0)
    return jnp.stack([feats, tail])


def _hy_filter_stack(length, w1, b1, f1, w2, b2, f2, w3, decay, hyw, reverse_tail):
    n_emb, fh = w1.shape
    k_pad = 64 if n_emb <= 64 else _pick(n_emb, (128,))
    feats = _features(length, n_emb, k_pad, reverse_tail)
    w1p = jnp.pad(w1, ((0, k_pad - n_emb), (0, 0)))
    tr = _pick(length, (512, 256, 128))
    tc = _pick(hyw, (2048, 1024, 512, 256, 128))
    nj = hyw // tc
    ni = length // tr
    return pl.pallas_call(
        functools.partial(_hy_filter_kernel, mask_lag0=not reverse_tail),
        grid=(2, ni, 2, nj),
        in_specs=[pl.BlockSpec((None, tr, k_pad), lambda p, i, c, j: (p, i, 0)),
                  pl.BlockSpec((k_pad, fh), lambda p, i, c, j: (0, 0)),
                  pl.BlockSpec((1, fh), lambda p, i, c, j: (0, 0)),
                  pl.BlockSpec((1, fh), lambda p, i, c, j: (0, 0)),
                  pl.BlockSpec((fh, fh), lambda p, i, c, j: (0, 0)),
                  pl.BlockSpec((1, fh), lambda p, i, c, j: (0, 0)),
                  pl.BlockSpec((1, fh), lambda p, i, c, j: (0, 0)),
                  pl.BlockSpec((fh, tc), lambda p, i, c, j: (0, (2 * c + p) * nj + j)),
                  pl.BlockSpec((1, tc), lambda p, i, c, j: (0, (2 * c + p) * nj + j))],
        out_specs=pl.BlockSpec((tr, tc), lambda p, i, c, j: (p * ni + i, c * nj + j)),
        out_shape=jax.ShapeDtypeStruct((2 * length, 2 * hyw), BF),
        scratch_shapes=[pltpu.VMEM((tr, fh), F32)],
        compiler_params=_cparams(4),
        name="hyena_filters",
    )(feats, w1p, b1.reshape(1, fh), f1.reshape(1, fh), w2, b2.reshape(1, fh), f2.reshape(1, fh),
      w3, decay.reshape(1, HY_N_FILT * hyw))


def _dft_tables(length, tile):
    n = 2 * length
    half = tile // 2
    rows = jnp.arange(n, dtype=jnp.int32)
    tile_i, r = rows // tile, rows % tile
    is_im = r >= half
    k = tile_i * half + jnp.where(is_im, r - half, r)
    t = jnp.arange(length, dtype=jnp.int32)
    ang = (2.0 * math.pi / n) * ((k[:, None] * t[None, :]) % n).astype(F32)
    nyq = jnp.where(t % 2 == 0, 1.0, -1.0).astype(F32)[None, :]
    is_nyq = (is_im & (k == 0))[:, None]
    fwd = jnp.where(is_im[:, None], -jnp.sin(ang), jnp.cos(ang))
    fwd = jnp.where(is_nyq, nyq, fwd)
    sgn = jnp.where(is_im[:, None] & ~is_nyq, -1.0, 1.0)
    fker = jnp.concatenate([fwd, fwd * sgn], axis=1)
    scale = jnp.where(k == 0, 1.0 / n, 2.0 / n)[None, :]
    inv = jnp.where(is_im[None, :], -jnp.sin(ang.T), jnp.cos(ang.T))
    inv = jnp.where(is_nyq.T, nyq.T, inv) * scale
    return fwd.astype(BF), fker.astype(BF), inv.astype(BF)


def _hy_long_convs(u_seq, kstack, hy_skip, hyw):
    b, ls, _ = u_seq.shape
    n = 2 * ls
    tmf = _pick(n, (512,))
    half = tmf // 2
    tn = _pick(hyw, (512, 256, 128))
    nj = hyw // tn
    fwd, fker, inv = _dft_tables(ls, tmf)

    (hspec,) = _matmul(
        fker, kstack,
        pl.BlockSpec((tmf, n), lambda g, j, i: (i, 0)),
        pl.BlockSpec((n, tn), lambda g, j, i: (0, j)),
        (1, 2 * nj, n // tmf),
        [(jax.ShapeDtypeStruct((n, 2 * hyw), F32), pl.BlockSpec((tmf, tn), lambda g, j, i: (i, j)))],
        _ep_store(F32), name="hyena_filter_dft")

    def spectrum_product(acc, extras, outs):
        hs = extras[0][...]
        re, im = acc[:half], acc[half:]
        hre, him = hs[:half], hs[half:]
        first = (lax.broadcasted_iota(jnp.int32, (half, 1), 0) == 0) & (pl.program_id(2) == 0)
        yre = jnp.where(first, re * hre, re * hre - im * him)
        yim = jnp.where(first, im * him, re * him + im * hre)
        outs[0][pl.ds(0, half), :] = yre.astype(BF)
        outs[0][pl.ds(half, half), :] = yim.astype(BF)

    tmi = _pick(ls, (TOKEN_TILE,))
    skip3 = hy_skip.reshape(2, 1, hyw)

    def conv(z_arr, z_off, conv_idx, mul_off):
        (spec,) = _matmul(
            fwd, z_arr,
            pl.BlockSpec((tmf, ls), lambda g, j, i: (i, 0)),
            pl.BlockSpec((None, ls, tn), lambda g, j, i: (g, 0, z_off + j)),
            (b, nj, n // tmf),
            [(jax.ShapeDtypeStruct((b, n, hyw), BF), pl.BlockSpec((None, tmf, tn), lambda g, j, i: (g, i, j)))],
            spectrum_product,
            extras=[(hspec, pl.BlockSpec((tmf, tn), lambda g, j, i: (i, conv_idx * nj + j)))],
            name="hyena_dft_fwd")

        def finish(acc, extras, outs):
            z = extras[0][...].astype(F32)
            xm = extras[2][...].astype(F32)
            outs[0][...] = (xm * (acc + z * extras[1][...])).astype(BF)

        (y,) = _matmul(
            inv, spec,
            pl.BlockSpec((tmi, n), lambda g, j, i: (i, 0)),
            pl.BlockSpec((None, n, tn), lambda g, j, i: (g, 0, j)),
            (b, nj, ls // tmi),
            [(jax.ShapeDtypeStruct((b, ls, hyw), BF), pl.BlockSpec((None, tmi, tn), lambda g, j, i: (g, i, j)))],
            finish,
            extras=[(z_arr, pl.BlockSpec((None, tmi, tn), lambda g, j, i: (g, i, z_off + j))),
                    (skip3, pl.BlockSpec((None, 1, tn), lambda g, j, i: (conv_idx, 0, j))),
                    (u_seq, pl.BlockSpec((None, tmi, tn), lambda g, j, i: (g, i, mul_off * nj + j)))],
            name="hyena_dft_inv")
        return y

    z2 = conv(u_seq, 0, 0, 1)
    return conv(z2, 0, 1, 2)


FFT_FAST = 64
FFT_K1_GROUP = 4


def _fft_tables(length):
    bf = FFT_FAST
    a_n = length // bf
    na = 2 * a_n
    n = 2 * length
    b = jnp.arange(bf, dtype=jnp.int32)
    k1 = jnp.arange(na, dtype=jnp.int32)
    a = jnp.arange(na, dtype=jnp.int32)
    nn = bf * a[None, None, :] + b[:, None, None]
    ang = (2.0 * math.pi / n) * ((k1[None, :, None] * nn) % n).astype(F32)
    cs, sn = jnp.cos(ang), jnp.sin(ang)
    m1k = jnp.concatenate([cs, -sn], axis=1)
    cd, sd = cs[:, :, :a_n], sn[:, :, :a_n]
    m1 = jnp.concatenate([jnp.concatenate([cd, sd], axis=2),
                          jnp.concatenate([-sd, cd], axis=2)], axis=1)
    ct, st = jnp.swapaxes(cd, 1, 2) / n, jnp.swapaxes(sd, 1, 2) / n
    g2 = jnp.concatenate([jnp.concatenate([ct, -st], axis=2),
                          jnp.concatenate([st, ct], axis=2)], axis=1)
    k2 = jnp.arange(bf, dtype=jnp.int32)
    ang2 = (2.0 * math.pi / bf) * ((k2[:, None] * b[None, :]) % bf).astype(F32)
    c2, s2 = jnp.cos(ang2), jnp.sin(ang2)
    f2 = jnp.concatenate([jnp.concatenate([c2, s2], axis=1), jnp.concatenate([-s2, c2], axis=1)], axis=0)
    i1 = jnp.concatenate([jnp.concatenate([c2.T, -s2.T], axis=1), jnp.concatenate([s2.T, c2.T], axis=1)], axis=0)
    return m1.astype(BF), m1k.astype(BF), f2.astype(BF), i1.astype(BF), g2.astype(BF)


def _fft_mid_kernel(p_ref, h_ref, f2_ref, i1_ref, o_ref):
    bf = f2_ref.shape[0] // 2
    for g in range(p_ref.shape[0]):
        x = jnp.dot(f2_ref[...], p_ref[g], preferred_element_type=F32)
        hs = h_ref[g].astype(F32)
        xr, xi = x[:bf], x[bf:]
        hr, hi = hs[:bf], hs[bf:]
        y = jnp.concatenate([xr * hr - xi * hi, xr * hi + xi * hr], axis=0).astype(BF)
        o_ref[g] = jnp.dot(i1_ref[...], y, preferred_element_type=F32).astype(o_ref.dtype)


def _fft_spec_kernel(p_ref, f2_ref, o_ref):
    for g in range(p_ref.shape[0]):
        o_ref[g] = jnp.dot(f2_ref[...], p_ref[g], preferred_element_type=F32).astype(o_ref.dtype)


def _fft_regroup(p, bf, na):
    lead = p.shape[:-3]
    c = p.shape[-1]
    nl = len(lead)
    p = p.reshape(*lead, bf, 2, na, c)
    p = jnp.transpose(p, tuple(range(nl)) + (nl + 2, nl + 1, nl + 0, nl + 3))
    return p.reshape(*lead, na, 2 * bf, c)


def _fft_ungroup(q, bf, na):
    lead = q.shape[:-3]
    c = q.shape[-1]
    nl = len(lead)
    q = q.reshape(*lead, na, 2, bf, c)
    q = jnp.transpose(q, tuple(range(nl)) + (nl + 2, nl + 1, nl + 0, nl + 3))
    return q.reshape(*lead, bf, 2 * na, c)


def _hy_long_convs_fft(u_seq, kern, hy_skip, hyw):
    nb_in, ls, _ = u_seq.shape
    if nb_in % 2:
        u_seq = jnp.concatenate([u_seq, jnp.zeros_like(u_seq[:1])], axis=0)
    nb = u_seq.shape[0] // 2
    bf = FFT_FAST
    a_n = ls // bf
    na = 2 * a_n
    ra = 2 * a_n
    tc = _pick(hyw, (2048, 1024, 512, 256, 128))
    nct = hyw // tc
    kg = _pick(na, (FFT_K1_GROUP, 2, 1))
    m1, m1k, f2, i1, g2 = _fft_tables(ls)
    u2 = u_seq.reshape(nb, ra, bf * 3 * hyw)
    k2 = kern.reshape(na, bf * 2 * hyw)

    (ph,) = _matmul(
        m1k, k2,
        pl.BlockSpec((None, 2 * na, na), lambda g, j, i: (j // (2 * nct), 0, 0)),
        pl.BlockSpec((na, tc), lambda g, j, i: (0, j)),
        (1, bf * 2 * nct, 1),
        [(jax.ShapeDtypeStruct((bf, 2 * na, 2 * hyw), BF),
          pl.BlockSpec((None, 2 * na, tc), lambda g, j, i: (j // (2 * nct), 0, j % (2 * nct))))],
        _ep_store(BF), name="hyena_fft_kern_s1")
    hspec = pl.pallas_call(
        _fft_spec_kernel,
        grid=(na // kg, 2 * nct),
        in_specs=[pl.BlockSpec((kg, 2 * bf, tc), lambda k, j: (k, 0, j)),
                  pl.BlockSpec((2 * bf, 2 * bf), lambda k, j: (0, 0))],
        out_specs=pl.BlockSpec((kg, 2 * bf, tc), lambda k, j: (k, 0, j)),
        out_shape=jax.ShapeDtypeStruct((na, 2 * bf, 2 * hyw), BF),
        compiler_params=_cparams(2),
        name="hyena_fft_kern_s2",
    )(_fft_regroup(ph, bf, na), f2)

    skip3 = hy_skip.reshape(2, 1, hyw)

    def conv(z2, z_sections, z_sec, conv_idx, mul_sec):
        zblk = lambda j: (j // nct) * (z_sections * nct) + z_sec * nct + j % nct
        ublk = lambda j: (j // nct) * (3 * nct) + mul_sec * nct + j % nct
        (p,) = _matmul(
            m1, z2,
            pl.BlockSpec((None, 2 * na, ra), lambda g, j, i: (j // nct, 0, 0)),
            pl.BlockSpec((None, ra, tc), lambda g, j, i: (g, 0, zblk(j))),
            (nb, bf * nct, 1),
            [(jax.ShapeDtypeStruct((nb, bf, 2 * na, hyw), BF),
              pl.BlockSpec((None, None, 2 * na, tc), lambda g, j, i: (g, j // nct, 0, j % nct)))],
            _ep_store(BF), name="hyena_fft_s1")
        q = pl.pallas_call(
            _fft_mid_kernel,
            grid=(na // kg, nct, nb),
            in_specs=[pl.BlockSpec((None, kg, 2 * bf, tc), lambda k, j, g: (g, k, 0, j)),
                      pl.BlockSpec((kg, 2 * bf, tc), lambda k, j, g: (k, 0, conv_idx * nct + j)),
                      pl.BlockSpec((2 * bf, 2 * bf), lambda k, j, g: (0, 0)),
                      pl.BlockSpec((2 * bf, 2 * bf), lambda k, j, g: (0, 0))],
            out_specs=pl.BlockSpec((None, kg, 2 * bf, tc), lambda k, j, g: (g, k, 0, j)),
            out_shape=jax.ShapeDtypeStruct((nb, na, 2 * bf, hyw), BF),
            compiler_params=_cparams(3),
            name="hyena_fft_mid",
        )(_fft_regroup(p, bf, na), hspec, f2, i1)

        def finish(acc, extras, outs):
            z = extras[0][...].astype(F32)
            xm = extras[2][...].astype(F32)
            outs[0][...] = (xm * (acc + z * extras[1][...])).astype(BF)

        (y,) = _matmul(
            g2, _fft_ungroup(q, bf, na),
            pl.BlockSpec((None, ra, 2 * na), lambda g, j, i: (j // nct, 0, 0)),
            pl.BlockSpec((None, None, 2 * na, tc), lambda g, j, i: (g, j // nct, 0, j % nct)),
            (nb, bf * nct, 1),
            [(jax.ShapeDtypeStruct((nb, ra, bf * hyw), BF),
              pl.BlockSpec((None, ra, tc), lambda g, j, i: (g, 0, j)))],
            finish,
            extras=[(z2, pl.BlockSpec((None, ra, tc), lambda g, j, i: (g, 0, zblk(j)))),
                    (skip3, pl.BlockSpec((None, 1, tc), lambda g, j, i: (conv_idx, 0, j % nct))),
                    (u2, pl.BlockSpec((None, ra, tc), lambda g, j, i: (g, 0, ublk(j))))],
            name="hyena_fft_s3")
        return y

    z_mid = conv(u2, 3, 0, 0, 1)
    y = conv(z_mid, 1, 0, 1, 2)
    return y.reshape(2 * nb, ls, hyw)[:nb_in]


def _moe_up_kernel(x_ref, wg_ref, wu_ref, comb_ref, o_ref, *, es):
    x = x_ref[...]
    g = jnp.dot(x, wg_ref[...], preferred_element_type=F32)
    u = jnp.dot(x, wu_ref[...], preferred_element_type=F32)
    comb = comb_ref[...]
    lane = lax.broadcasted_iota(jnp.int32, comb.shape, 1)
    f = g.shape[1] // es
    col = lax.broadcasted_iota(jnp.int32, (1, g.shape[1]), 1)
    scale = jnp.zeros_like(g)
    for s in range(es):
        e = pl.program_id(0) * es + s
        sc = jnp.sum(jnp.where(lane == e, comb, 0.0), axis=1, keepdims=True)
        scale = jnp.where((col >= s * f) & (col < (s + 1) * f), sc, scale)
    o_ref[...] = (g * jax.nn.sigmoid(g) * u * scale).astype(o_ref.dtype)


MOE_EXPERTS_PER_STEP = 4


def _moe_group_weights(w):
    nl, ne, d, f = w.shape
    es = MOE_EXPERTS_PER_STEP
    w = w.reshape(nl, ne // es, es, d, f).transpose(0, 1, 3, 2, 4)
    return w.reshape(nl, ne // es, d, es * f).astype(BF)


def _moe_up(u, w_gate, w_up, comb, l):
    m, d = u.shape
    _, ng, _, nf = w_gate.shape
    tm = _pick(m, (544, 512, 256))
    wspec = pl.BlockSpec((None, None, d, nf), lambda e, i: (l, e, 0, 0))
    return pl.pallas_call(
        functools.partial(_moe_up_kernel, es=MOE_EXPERTS_PER_STEP),
        grid=(ng, m // tm),
        in_specs=[pl.BlockSpec((tm, d), lambda e, i: (i, 0)), wspec, wspec,
                  pl.BlockSpec((tm, LANES), lambda e, i: (i, 0))],
        out_specs=pl.BlockSpec((tm, nf), lambda e, i: (i, e)),
        out_shape=jax.ShapeDtypeStruct((m, ng * nf), BF),
        compiler_params=_cparams(2),
        name="moe_up",
    )(u, w_gate, w_up, comb)


def _row_gate(gt_ref, tm, t_len, ctx_len, n_batch):
    r = lax.broadcasted_iota(jnp.int32, (tm, 1), 0) + pl.program_id(2) * tm
    gate = jnp.zeros((tm, gt_ref.shape[1]), F32)
    for bi in range(n_batch):
        in_b = (r >= bi * t_len) & (r < (bi + 1) * t_len)
        is_c = r < bi * t_len + ctx_len
        gate = jnp.where(in_b & is_c, gt_ref[0:1, :], gate)
        gate = jnp.where(in_b & ~is_c, gt_ref[1 + bi:2 + bi, :], gate)
    return gate


def _final_norm_kernel(h_ref, g_ref, o_ref):
    xf = h_ref[...]
    o_ref[...] = xf * lax.rsqrt(jnp.mean(xf * xf, axis=-1, keepdims=True) + NORM_EPS) * g_ref[...]


def _final_norm(h, gain, n_batch, t_len, ctx_len):
    m, d = h.shape
    tt = TOKEN_TILE
    n_t, n_c = t_len // tt, ctx_len // tt
    n_x = n_t - n_c
    return pl.pallas_call(
        _final_norm_kernel,
        grid=(n_batch, n_x),
        in_specs=[pl.BlockSpec((tt, d), lambda bi, i: (bi * n_t + n_c + i, 0)),
                  pl.BlockSpec((1, d), lambda bi, i: (0, 0))],
        out_specs=pl.BlockSpec((None, tt, d), lambda bi, i: (bi, i, 0)),
        out_shape=jax.ShapeDtypeStruct((n_batch, t_len - ctx_len, d), F32),
        compiler_params=_cparams(2),
        name="final_norm",
    )(h, gain.reshape(1, d))


def kernel(x, c, ctx, c_ctx, ada_down, ada_up, ada_bias, norm_mix, norm_ffn, w_in, qkv_conv, a_log, dt_bias, o_norm, hy_conv, hy_w1, hy_b1, hy_f1, hy_w2, hy_b2, hy_f2, hy_w3, hy_decay, hy_skip, w_br_a, w_br_b, w_out, router_group, router_group_bias, router_expert, router_expert_bias, exp_gate, exp_up, exp_down, final_norm):
    n_batch, seq, d = x.shape
    ctx_len = ctx.shape[1]
    t_len = ctx_len + seq
    m = n_batch * t_len
    depth = w_in.shape[0]
    n_heads = a_log.shape[2]
    a_width = n_heads * LANES
    hyw = hy_skip.shape[2]
    n_exp = exp_gate.shape[1]
    dg_off = 4 * a_width
    hy_off = dg_off + 4 * n_heads
    tail_w = 3 * hyw + 2 * d
    n_t, n_c = t_len // TOKEN_TILE, ctx_len // TOKEN_TILE
    nc = t_len // CHUNK
    ncp = -(-nc // 8) * 8

    h = jnp.concatenate([ctx, x], axis=1).reshape(m, d)

    conds = jnp.zeros((8, d), F32).at[0].set(c_ctx).at[1:1 + n_batch].set(c)
    mod_all = _adaln_all(conds, ada_down, ada_up, ada_bias)
    mod_all = mod_all.reshape(depth, 8, 6, d)[:, :1 + n_batch]
    mod_all = jnp.pad(mod_all, ((0, 0), (0, 0), (0, 2), (0, 0)))

    tm = _pick(m, (1088, 512, 256))
    tmp = tm
    tn = 512

    w_qkvz = w_in[:, :, :dg_off].astype(BF)
    w_tail = w_in[:, :, hy_off:].astype(BF)
    exp_gate_b = _moe_group_weights(exp_gate)
    exp_up_b = _moe_group_weights(exp_up)

    def gate_table(mod_l, idx):
        return jnp.pad(mod_l[:, idx, :], ((0, 8 - (1 + n_batch)), (0, 0)))

    def residual_ep(acc, extras, outs):
        outs[0][...] = extras[0][...] + _row_gate(extras[1], acc.shape[0], t_len, ctx_len, n_batch) * acc

    for l in range(depth):
        mod_l = mod_all[l]
        u = _norm_mod(h, norm_mix[l], mod_l, 0, n_t, n_c)
        (p1,) = _matmul(
            u, w_qkvz,
            pl.BlockSpec((tmp, d), lambda g, j, i: (i, 0)),
            pl.BlockSpec((None, d, tn), lambda g, j, i, l=l: (l, 0, j)),
            (1, dg_off // tn, m // tmp),
            [(jax.ShapeDtypeStruct((m, dg_off), BF), pl.BlockSpec((tmp, tn), lambda g, j, i: (i, j)))],
            _ep_store(BF), name="proj_qkvz")
        (p_tail,) = _matmul(
            u, w_tail,
            pl.BlockSpec((tmp, d), lambda g, j, i: (i, 0)),
            pl.BlockSpec((None, d, tn), lambda g, j, i, l=l: (l, 0, j)),
            (1, tail_w // tn, m // tmp),
            [(jax.ShapeDtypeStruct((m, tail_w), BF), pl.BlockSpec((tmp, tn), lambda g, j, i: (i, j)))],
            _ep_store(BF), name="proj_tail")
        w_dg_t = w_in[l, :, dg_off:hy_off].T
        a_exp = jnp.concatenate([jnp.exp(a_log[l].reshape(-1)), jnp.zeros((2 * n_heads,), F32)])[:, None]
        dt_b = jnp.concatenate([dt_bias[l].reshape(-1), jnp.zeros((2 * n_heads,), F32)])[:, None]
        gates = _gates(u, w_dg_t, a_exp, dt_b)

        g5 = gates.reshape(4, n_heads, n_batch, nc, CHUNK).transpose(2, 1, 0, 3, 4)
        g_rows = jnp.pad(g5, ((0, 0), (0, 0), (0, 0), (0, ncp - nc), (0, 0)))
        g_cols = jnp.pad(jnp.swapaxes(g5, 3, 4), ((0, 0), (0, 0), (0, 0), (0, 0), (0, LANES - nc)))

        p1_3 = p1.reshape(n_batch, t_len, dg_off)
        conv9 = qkv_conv[l].reshape(9, 3 * a_width)
        du, dw, dq, dk, da, dgl = _delta_prep(p1_3, conv9, g_rows, g_cols, n_heads, ctx_len)
        o_f, o_b = _delta_scan(du, dw, dq, dk, da, dgl, ctx_len)
        o_a = _delta_out(o_f, o_b, p1_3, o_norm[l], n_heads).reshape(m, a_width)

        p_tail3 = p_tail.reshape(n_batch, t_len, tail_w)
        u_c, u_x = _hy_conv(p_tail3, hy_conv[l], ctx_len, 3 * hyw)
        filt_args = (hy_w1[l], hy_b1[l], hy_f1[l], hy_w2[l], hy_b2[l], hy_f2[l], hy_w3[l], hy_decay[l])
        y_x = _hy_long_convs_fft(u_x, _hy_filter_stack(seq, *filt_args, hyw, True), hy_skip[l], hyw)
        y_c = _hy_long_convs(u_c, _hy_filter_stack(ctx_len, *filt_args, hyw, False), hy_skip[l], hyw)
        y_b = jnp.concatenate([y_c, y_x], axis=1).reshape(m, hyw)

        ga_blk = (3 * hyw) // tn
        gb_blk = (3 * hyw + d) // tn

        def gated_ep(acc, extras, outs):
            outs[0][...] = jax.nn.sigmoid(extras[0][...].astype(F32)) * acc

        (br,) = _matmul(
            o_a, w_br_a,
            pl.BlockSpec((tm, a_width), lambda g, j, i: (i, 0)),
            pl.BlockSpec((None, a_width, tn), lambda g, j, i, l=l: (l, 0, j)),
            (1, d // tn, m // tm),
            [(jax.ShapeDtypeStruct((m, d), F32), pl.BlockSpec((tm, tn), lambda g, j, i: (i, j)))],
            gated_ep,
            extras=[(p_tail, pl.BlockSpec((tm, tn), lambda g, j, i: (i, ga_blk + j)))],
            name="branch_a")

        def merge_ep(acc, extras, outs):
            outs[0][...] = (extras[1][...] + jax.nn.sigmoid(extras[0][...].astype(F32)) * acc).astype(BF)

        (merged,) = _matmul(
            y_b, w_br_b,
            pl.BlockSpec((tm, hyw), lambda g, j, i: (i, 0)),
            pl.BlockSpec((None, hyw, tn), lambda g, j, i, l=l: (l, 0, j)),
            (1, d // tn, m // tm),
            [(jax.ShapeDtypeStruct((m, d), BF), pl.BlockSpec((tm, tn), lambda g, j, i: (i, j)))],
            merge_ep,
            extras=[(p_tail, pl.BlockSpec((tm, tn), lambda g, j, i: (i, gb_blk + j))),
                    (br, pl.BlockSpec((tm, tn), lambda g, j, i: (i, j)))],
            name="branch_b_merge")

        (h,) = _matmul(
            merged, w_out,
            pl.BlockSpec((tm, d), lambda g, j, i: (i, 0)),
            pl.BlockSpec((None, d, tn), lambda g, j, i, l=l: (l, 0, j)),
            (1, d // tn, m // tm),
            [(jax.ShapeDtypeStruct((m, d), F32), pl.BlockSpec((tm, tn), lambda g, j, i: (i, j)))],
            residual_ep,
            extras=[(h, pl.BlockSpec((tm, tn), lambda g, j, i: (i, j))),
                    (gate_table(mod_l, 2), pl.BlockSpec((8, tn), lambda g, j, i: (0, j)))],
            name="out_proj")

        w_router = jnp.pad(jnp.concatenate([router_expert[l], router_group[l]], axis=1),
                           ((0, 0), (0, LANES - n_exp - N_GROUPS)))
        b_router = jnp.pad(jnp.concatenate([router_expert_bias[l], router_group_bias[l]]),
                           (0, LANES - n_exp - N_GROUPS)).reshape(1, LANES)
        u2, comb = _norm_mod_router(h, norm_ffn[l], mod_l, 3, n_t, n_c, w_router, b_router, n_exp, N_GROUPS)
        hid = _moe_up(u2, exp_gate_b, exp_up_b, comb, l)
        kf = hid.shape[1]
        tnd = _pick(d, (256, 128))
        (h,) = _matmul(
            hid, exp_down.reshape(depth, kf, d),
            pl.BlockSpec((tm, kf), lambda g, j, i: (i, 0)),
            pl.BlockSpec((None, kf, tnd), lambda g, j, i, l=l: (l, 0, j)),
            (1, d // tnd, m // tm),
            [(jax.ShapeDtypeStruct((m, d), F32), pl.BlockSpec((tm, tnd), lambda g, j, i: (i, j)))],
            residual_ep,
            extras=[(h, pl.BlockSpec((tm, tnd), lambda g, j, i: (i, j))),
                    (gate_table(mod_l, 5), pl.BlockSpec((8, tnd), lambda g, j, i: (0, j)))],
            name="moe_down")

    return _final_norm(h, final_norm, n_batch, t_len, ctx_len)
```

```python
import functools
import math

import jax
import jax.numpy as jnp
from jax import lax
from jax.experimental import pallas as pl
from jax.experimental.pallas import tpu as pltpu

F32 = jnp.float32
BF = jnp.bfloat16

GRID_W = 64
CHUNK = 64
N_GROUPS = 4
EXPERTS_PER_GROUP = 8
HY_N_FILT = 4
NORM_EPS = 1e-6
TOKEN_TILE = 256
LANES = 128
CONV_PAD = 72
VMEM_LIMIT = 56 * 1024 * 1024


def _cparams(n_grid):
    return pltpu.CompilerParams(dimension_semantics=("arbitrary",) * n_grid,
                                vmem_limit_bytes=VMEM_LIMIT)


def _split_bf16(a):
    hi = a.astype(BF)
    lo = (a - hi.astype(F32)).astype(BF)
    return hi, lo


def _dot3(a, b):
    ah, al = _split_bf16(a)
    bh, bl = _split_bf16(b)
    d = functools.partial(jnp.dot, preferred_element_type=F32)
    return d(ah, bh) + (d(ah, bl) + d(al, bh))


def _dotx(a, b):
    return jnp.dot(a, b, precision=lax.Precision.HIGHEST, preferred_element_type=F32)


def _bdot(a, b):
    return jnp.dot(a.astype(BF), b.astype(BF), preferred_element_type=F32)


def _mm_kernel(*refs, n_extra, n_out, stage_w, lane_shift, prologue, epilogue):
    x_ref, w_ref = refs[0], refs[1]
    n_w = 3 if lane_shift else 2
    extras = refs[n_w:n_w + n_extra]
    outs = refs[n_w + n_extra:n_w + n_extra + n_out]
    scratch = refs[n_w + n_extra + n_out:]
    if stage_w:
        wbf = scratch[0]

        @pl.when(pl.program_id(2) == 0)
        def _():
            k = w_ref.shape[0]
            step = 512 if k % 512 == 0 else k

            def load(r0):
                wa = w_ref[pl.ds(r0, step), :]
                if lane_shift:
                    wa = jnp.concatenate([wa[:, lane_shift:], refs[2][pl.ds(r0, step), :lane_shift]], axis=1)
                return wa.astype(BF)

            if k == step:
                wbf[...] = load(0)
            else:
                def it(r, c):
                    r0 = pl.multiple_of(r * step, step)
                    wbf[pl.ds(r0, step), :] = load(r0)
                    return c
                lax.fori_loop(0, k // step, it, 0)
        w = wbf[...]
    else:
        w = w_ref[...]
    x = x_ref[...]
    if prologue is not None:
        x = prologue(x)
    if x.dtype != BF:
        x = x.astype(BF)
    acc = jnp.dot(x, w, preferred_element_type=F32)
    epilogue(acc, extras, outs)


def _matmul(x, w, x_spec, w_spec, grid, outs, epilogue, extras=(), prologue=None, name=None,
            w_next_spec=None, lane_shift=0):
    stage_w = w.dtype != BF or lane_shift > 0
    kw, tn = [d for d in w_spec.block_shape if d is not None]
    scratch = [pltpu.VMEM((kw, tn), BF)] if stage_w else []
    body = functools.partial(_mm_kernel, n_extra=len(extras), n_out=len(outs), stage_w=stage_w,
                             lane_shift=lane_shift, prologue=prologue, epilogue=epilogue)
    w_ops, w_specs = ([w, w], [w_spec, w_next_spec]) if lane_shift else ([w], [w_spec])
    res = pl.pallas_call(
        body,
        grid=grid,
        in_specs=[x_spec] + w_specs + [s for _, s in extras],
        out_specs=[s for _, s in outs],
        out_shape=[o for o, _ in outs],
        scratch_shapes=scratch,
        compiler_params=_cparams(3),
        name=name,
    )(x, *w_ops, *[a for a, _ in extras])
    return res


def _merge_kernel(xa_ref, xb_ref, wa_ref, wb_ref, ga_ref, gb_ref, o_ref, wa_bf, wb_bf):
    @pl.when(pl.program_id(1) == 0)
    def _():
        wa_bf[...] = wa_ref[...].astype(BF)
        wb_bf[...] = wb_ref[...].astype(BF)
    a = jnp.dot(xa_ref[...], wa_bf[...], preferred_element_type=F32)
    b = jnp.dot(xb_ref[...], wb_bf[...], preferred_element_type=F32)
    o_ref[...] = (jax.nn.sigmoid(ga_ref[...].astype(F32)) * a
                  + jax.nn.sigmoid(gb_ref[...].astype(F32)) * b).astype(o_ref.dtype)


def _ep_store(dtype):
    def ep(acc, extras, outs):
        outs[0][...] = acc.astype(dtype)
    return ep


def _pick(n, prefs):
    for p in prefs:
        if n % p == 0:
            return p
    return n


def _adaln_all(conds, ada_down, ada_up, ada_bias):
    nl, d, r = ada_down.shape
    n6 = ada_up.shape[2]
    (t,) = _matmul(
        conds, ada_down,
        pl.BlockSpec((8, d), lambda g, j, i: (0, 0)),
        pl.BlockSpec((None, d, r), lambda g, j, i: (g, 0, 0)),
        (nl, 1, 1),
        [(jax.ShapeDtypeStruct((nl, 8, r), F32), pl.BlockSpec((None, 8, r), lambda g, j, i: (g, 0, 0)))],
        _ep_store(F32), prologue=lambda v: v * jax.nn.sigmoid(v), name="adaln_down")
    tn = _pick(n6, (2048, 1024, 512, 256, 128))

    def ep(acc, extras, outs):
        outs[0][...] = acc + extras[0][...]

    (mod,) = _matmul(
        t, ada_up,
        pl.BlockSpec((None, 8, r), lambda g, j, i: (g, 0, 0)),
        pl.BlockSpec((None, r, tn), lambda g, j, i: (g, 0, j)),
        (nl, n6 // tn, 1),
        [(jax.ShapeDtypeStruct((nl, 8, n6), F32), pl.BlockSpec((None, 8, tn), lambda g, j, i: (g, 0, j)))],
        ep, extras=[(ada_bias.reshape(nl, 1, n6), pl.BlockSpec((None, 1, tn), lambda g, j, i: (g, 0, j)))],
        name="adaln_up")
    return mod


def _mod_row_map(n_t, n_c):
    def m(i):
        return (jnp.where(i % n_t < n_c, 0, 1 + i // n_t), 0, 0)
    return m


def _norm_mod_kernel(h_ref, g_ref, mod_ref, u_ref, *, off):
    xf = h_ref[...]
    y = xf * lax.rsqrt(jnp.mean(xf * xf, axis=-1, keepdims=True) + NORM_EPS) * g_ref[...]
    sh = mod_ref[off:off + 1, :]
    sc = mod_ref[off + 1:off + 2, :]
    u_ref[...] = (y * (1.0 + sc) + sh).astype(u_ref.dtype)


def _norm_mod(h, gain, mod_l, off, n_t, n_c):
    m, d = h.shape
    tt = TOKEN_TILE
    return pl.pallas_call(
        functools.partial(_norm_mod_kernel, off=off),
        grid=(m // tt,),
        in_specs=[pl.BlockSpec((tt, d), lambda i: (i, 0)),
                  pl.BlockSpec((1, d), lambda i: (0, 0)),
                  pl.BlockSpec((None, 8, d), _mod_row_map(n_t, n_c))],
        out_specs=pl.BlockSpec((tt, d), lambda i: (i, 0)),
        out_shape=jax.ShapeDtypeStruct((m, d), BF),
        compiler_params=_cparams(1),
        name="norm_mod",
    )(h, gain.reshape(1, d), mod_l)


def _router_kernel(h_ref, g_ref, mod_ref, wr_ref, br_ref, u_ref, comb_ref, *, off, n_exp, n_grp):
    xf = h_ref[...]
    y = xf * lax.rsqrt(jnp.mean(xf * xf, axis=-1, keepdims=True) + NORM_EPS) * g_ref[...]
    sh = mod_ref[off:off + 1, :]
    sc = mod_ref[off + 1:off + 2, :]
    u = y * (1.0 + sc) + sh
    u_ref[...] = u.astype(u_ref.dtype)
    logits = _dot3(u, wr_ref[...]) + br_ref[...]
    lane = lax.broadcasted_iota(jnp.int32, logits.shape, 1)
    neg = jnp.float32(-jnp.inf)
    big = jnp.int32(1 << 20)
    is_g = (lane >= n_exp) & (lane < n_exp + n_grp)
    lg = jnp.where(is_g, logits, neg)
    eg = jnp.exp(lg - jnp.max(lg, axis=1, keepdims=True))
    pg = eg / jnp.sum(eg, axis=1, keepdims=True)
    p_top = jnp.max(pg, axis=1, keepdims=True)
    g_idx = jnp.min(jnp.where(is_g & (pg == p_top), lane, big), axis=1, keepdims=True) - n_exp
    per = n_exp // n_grp
    in_grp = (lane >= g_idx * per) & (lane < (g_idx + 1) * per)
    le = jnp.where(in_grp, logits, neg)
    v1 = jnp.max(le, axis=1, keepdims=True)
    i1 = jnp.min(jnp.where(le == v1, lane, big), axis=1, keepdims=True)
    le2 = jnp.where(lane == i1, neg, le)
    v2 = jnp.max(le2, axis=1, keepdims=True)
    i2 = jnp.min(jnp.where(le2 == v2, lane, big), axis=1, keepdims=True)
    e2 = jnp.exp(v2 - v1)
    w1 = p_top / (1.0 + e2)
    w2 = p_top * e2 / (1.0 + e2)
    comb_ref[...] = jnp.where(lane == i1, w1, 0.0) + jnp.where(lane == i2, w2, 0.0)


def _norm_mod_router(h, gain, mod_l, off, n_t, n_c, w_router, b_router, n_exp, n_grp):
    m, d = h.shape
    tt = TOKEN_TILE
    return pl.pallas_call(
        functools.partial(_router_kernel, off=off, n_exp=n_exp, n_grp=n_grp),
        grid=(m // tt,),
        in_specs=[pl.BlockSpec((tt, d), lambda i: (i, 0)),
                  pl.BlockSpec((1, d), lambda i: (0, 0)),
                  pl.BlockSpec((None, 8, d), _mod_row_map(n_t, n_c)),
                  pl.BlockSpec((d, LANES), lambda i: (0, 0)),
                  pl.BlockSpec((1, LANES), lambda i: (0, 0))],
        out_specs=[pl.BlockSpec((tt, d), lambda i: (i, 0)),
                   pl.BlockSpec((tt, LANES), lambda i: (i, 0))],
        out_shape=[jax.ShapeDtypeStruct((m, d), BF), jax.ShapeDtypeStruct((m, LANES), F32)],
        compiler_params=_cparams(1),
        name="norm_mod_router",
    )(h, gain.reshape(1, d), mod_l, w_router, b_router)


def _gates_kernel(w_ref, u_ref, aexp_ref, dtb_ref, o_ref, *, n_decay):
    acc = lax.dot_general(w_ref[...].astype(BF), u_ref[...], (((1,), (1,)), ((), ())),
                          preferred_element_type=F32)
    row = lax.broadcasted_iota(jnp.int32, acc.shape, 0)
    z = acc + dtb_ref[...]
    softplus = jnp.maximum(z, 0.0) + jnp.log(1.0 + jnp.exp(-jnp.abs(z)))
    g = -aexp_ref[...] * softplus
    o_ref[...] = jnp.where(row < n_decay, g, jax.nn.sigmoid(acc))


def _gates(u, w_dg_t, a_exp, dt_b):
    m, d = u.shape
    r = w_dg_t.shape[0]
    tm = _pick(m, (512, 256))
    return pl.pallas_call(
        functools.partial(_gates_kernel, n_decay=r // 2),
        grid=(m // tm,),
        in_specs=[pl.BlockSpec((r, d), lambda i: (0, 0)),
                  pl.BlockSpec((tm, d), lambda i: (i, 0)),
                  pl.BlockSpec((r, 1), lambda i: (0, 0)),
                  pl.BlockSpec((r, 1), lambda i: (0, 0))],
        out_specs=pl.BlockSpec((r, tm), lambda i: (0, i)),
        out_shape=jax.ShapeDtypeStruct((r, m), F32),
        compiler_params=_cparams(1),
        name="delta_gates",
    )(w_dg_t, u, a_exp, dt_b)


def _conv_silu_tile(pad_ref, cw_ref, t0, n, seq_start, seq_len, width):
    one_row = width == seq_len
    r = lax.broadcasted_iota(jnp.int32, (n, 1), 0) + (t0 - seq_start)
    col = lax.rem(r, width)
    acc = jnp.zeros((n, LANES), F32)
    for dj in (-1, 0, 1):
        part = jnp.zeros((n, LANES), F32)
        for di in (-1, 0, 1):
            if one_row and di != 0:
                continue
            off = di * width + dj
            tap = pad_ref[pl.ds(CONV_PAD + t0 + off, n), :]
            wrow = cw_ref[(di + 1) * 3 + (dj + 1):(di + 1) * 3 + (dj + 1) + 1, :]
            lo, hi = t0 - seq_start + off, t0 - seq_start + n - 1 + off
            if lo < 0 or hi >= seq_len:
                tap = jnp.where((r + off >= 0) & (r + off < seq_len), tap, 0.0)
            part = part + tap * wrow
        ok_c = (col + dj >= 0) & (col + dj < width)
        acc = acc + jnp.where(ok_c, part, 0.0)
    return acc * jax.nn.sigmoid(acc)


def _bmm(a, b):
    return jnp.einsum('gik,gkj->gij', a.astype(BF), b.astype(BF), preferred_element_type=F32)


def _bmm_nt(a, b):
    return jnp.einsum('gik,gjk->gij', a.astype(BF), b.astype(BF), preferred_element_type=F32)


def _neumann_inverse(lmat, eye):
    x = -lmat
    p = eye + x
    n = lmat.shape[-1]
    k = 1
    while 2 * k < n:
        x = _bmm(x, x)
        p = p + _bmm(p, x)
        k *= 2
    return p


def _delta_prep_kernel(pq_ref, pk_ref, pv_ref, cq_ref, ck_ref, cv_ref, g_ref, gt_ref,
                       u_ref, w_ref, qd_ref, kd_ref, a_ref, gl_ref,
                       pad_ref, q_s, k_s, v_s, gcr_s, gct_s,
                       *, t_len, ctx_len, n_chunks, group):
    hd = q_s.shape[1]
    tt = TOKEN_TILE
    d = pl.program_id(2)

    @pl.when(d == 0)
    def _():
        zpad = jnp.zeros((CONV_PAD, LANES), F32)
        pad_ref[pl.ds(0, CONV_PAD), :] = zpad
        pad_ref[pl.ds(CONV_PAD + t_len, CONV_PAD), :] = zpad
        for src, cw, dst, kind in ((pq_ref, cq_ref, q_s, "q"), (pk_ref, ck_ref, k_s, "k"), (pv_ref, cv_ref, v_s, "v")):
            pad_ref[pl.ds(CONV_PAD, t_len), :] = src[...].astype(F32)
            for ti in range(t_len // tt):
                t0 = ti * tt
                if t0 < ctx_len:
                    a = _conv_silu_tile(pad_ref, cw, t0, tt, 0, ctx_len, ctx_len)
                else:
                    a = _conv_silu_tile(pad_ref, cw, t0, tt, ctx_len, t_len - ctx_len, GRID_W)
                if kind != "v":
                    a = a * lax.rsqrt(jnp.sum(a * a, axis=-1, keepdims=True) + NORM_EPS)
                if kind == "q":
                    a = a * (hd ** -0.5)
                dst[pl.ds(t0, tt), :] = a

    ii = lax.broadcasted_iota(jnp.int32, (CHUNK, CHUNK), 0)
    jj = lax.broadcasted_iota(jnp.int32, (CHUNK, CHUNK), 1)
    sdiff = (ii - jj) * (1 - 2 * d)
    eye = (ii == jj).astype(F32)
    incl = sdiff >= 0
    strict = sdiff > 0
    gcr_s[...] = _dotx(g_ref[d], (sdiff <= 0).astype(F32))
    gct_s[...] = _dotx(incl.astype(F32), gt_ref[d])
    gtot = jnp.where(d == 0, gcr_s[:, CHUNK - 1:CHUNK], gcr_s[:, 0:1])
    gl_ref[...] = jnp.broadcast_to(jnp.exp(gtot), gl_ref.shape)

    gc = group
    rows = gc * CHUNK
    lane_c = lax.broadcasted_iota(jnp.int32, (CHUNK, LANES), 1)
    eye_hd = (lax.broadcasted_iota(jnp.int32, (hd, hd), 0)
              == lax.broadcasted_iota(jnp.int32, (hd, hd), 1)).astype(BF)[None]

    def body(it, carry):
        c0 = it * gc
        t0 = pl.multiple_of(c0 * CHUNK, rows)
        q = q_s[pl.ds(t0, rows), :].reshape(gc, CHUNK, hd)
        k = k_s[pl.ds(t0, rows), :].reshape(gc, CHUNK, hd)
        v = v_s[pl.ds(t0, rows), :].reshape(gc, CHUNK, hd)
        gct = gct_s[...]
        bt = gt_ref[2 + d]
        gcols, bcols, grows, glasts = [], [], [], []
        for g in range(gc):
            sel = lane_c == c0 + g
            gcols.append(jnp.sum(jnp.where(sel, gct, 0.0), axis=1, keepdims=True))
            bcols.append(jnp.sum(jnp.where(sel, bt, 0.0), axis=1, keepdims=True))
            grow_g = gcr_s[pl.ds(c0 + g, 1), :]
            grows.append(grow_g)
            glasts.append(jnp.where(d == 0, grow_g[:, CHUNK - 1:CHUNK], grow_g[:, 0:1]))
        gcol = jnp.broadcast_to(jnp.stack(gcols), (gc, CHUNK, hd))
        bcol = jnp.broadcast_to(jnp.stack(bcols), (gc, CHUNK, hd))
        grow = jnp.stack(grows)
        glast = jnp.stack(glasts)
        decay = jnp.where(incl, jnp.exp(jnp.where(incl, gcol[:, :, :CHUNK] - grow, 0.0)), 0.0)
        kb = k * bcol
        lmat = jnp.where(strict, _bmm_nt(kb, k) * decay, 0.0)
        tinv = _neumann_inverse(lmat, eye[None])
        egc = jnp.exp(gcol)
        uw = _bmm(tinv, jnp.concatenate([v * bcol, kb * egc], axis=2))
        attn = (_bmm_nt(q, k) * decay).astype(BF)
        kdec = k * jnp.exp(glast - gcol)
        kdec_t = _bmm_nt(jnp.broadcast_to(eye_hd, (gc, hd, hd)), kdec).astype(BF)
        u_ref[pl.ds(t0, rows), :] = uw[:, :, :hd].reshape(rows, hd)
        w_ref[pl.ds(t0, rows), :] = uw[:, :, hd:].astype(BF).reshape(rows, hd)
        qd_ref[pl.ds(t0, rows), :] = (q * egc).astype(BF).reshape(rows, hd)
        a_ref[pl.ds(t0, rows), :] = attn.reshape(rows, CHUNK)
        kd_ref[pl.ds(c0, gc)] = kdec_t
        return carry

    lax.fori_loop(0, n_chunks // group, body, 0)


def _delta_prep(p1, conv_w9, g_rows, g_cols, n_heads, ctx_len):
    b, t, _ = p1.shape
    hd = LANES
    nc = t // CHUNK
    ncp = g_rows.shape[3]
    group = max(g for g in range(1, 18) if nc % g == 0)
    h = n_heads
    kern = functools.partial(_delta_prep_kernel, t_len=t, ctx_len=ctx_len, n_chunks=nc, group=group)
    tok = lambda off: pl.BlockSpec((None, t, hd), lambda bi, hi, di: (bi, 0, off + hi))
    cws = lambda off: pl.BlockSpec((9, hd), lambda bi, hi, di: (0, off + hi))
    o5 = lambda last: pl.BlockSpec((None, None, None, t, last), lambda bi, hi, di: (di, bi, hi, 0, 0))
    return pl.pallas_call(
        kern,
        grid=(b, h, 2),
        in_specs=[tok(0), tok(h), tok(2 * h), cws(0), cws(h), cws(2 * h),
                  pl.BlockSpec((None, None, 4, ncp, CHUNK), lambda bi, hi, di: (bi, hi, 0, 0, 0)),
                  pl.BlockSpec((None, None, 4, CHUNK, LANES), lambda bi, hi, di: (bi, hi, 0, 0, 0))],
        out_specs=[o5(hd), o5(hd), o5(hd),
                   pl.BlockSpec((None, None, None, nc, hd, CHUNK), lambda bi, hi, di: (di, bi, hi, 0, 0, 0)),
                   o5(CHUNK),
                   pl.BlockSpec((None, None, None, ncp, LANES), lambda bi, hi, di: (di, bi, hi, 0, 0))],
        out_shape=[jax.ShapeDtypeStruct((2, b, h, t, hd), F32),
                   jax.ShapeDtypeStruct((2, b, h, t, hd), BF),
                   jax.ShapeDtypeStruct((2, b, h, t, hd), BF),
                   jax.ShapeDtypeStruct((2, b, h, nc, hd, CHUNK), BF),
                   jax.ShapeDtypeStruct((2, b, h, t, CHUNK), BF),
                   jax.ShapeDtypeStruct((2, b, h, ncp, LANES), F32)],
        scratch_shapes=[pltpu.VMEM((t + 2 * CONV_PAD, LANES), F32),
                        pltpu.VMEM((t, hd), F32), pltpu.VMEM((t, hd), F32), pltpu.VMEM((t, hd), F32),
                        pltpu.VMEM((ncp, CHUNK), F32), pltpu.VMEM((CHUNK, LANES), F32)],
        compiler_params=_cparams(3),
        name="delta_prep",
    )(p1, p1, p1, conv_w9, conv_w9, conv_w9, g_rows, g_cols)


def _delta_scan_kernel(uf, wf, qf, kf, af, gf, ub, wb, qb, kb, ab, gb, of_ref, ob_ref, s_ref, *, hg):
    @pl.when(pl.program_id(2) == 0)
    def _():
        s_ref[...] = jnp.zeros_like(s_ref)

    cs = uf.shape[1]
    s = s_ref[...]
    for t in range(cs):
        tb = cs - 1 - t
        cat = lambda a, b: jnp.concatenate([a[:, t], b[:, tb]], axis=0)
        sb = s.astype(BF)
        vn = cat(uf, ub) - _bmm(cat(wf, wb), sb)
        vnb = vn.astype(BF)
        o = _bmm(cat(qf, qb), sb) + _bmm(cat(af, ab), vnb)
        s = s * cat(gf, gb) + _bmm(cat(kf, kb), vnb)
        of_ref[:, t] = o[:hg]
        ob_ref[:, tb] = o[hg:]
    s_ref[...] = s


def _delta_scan(u, w, qd, kd, a, gl, ctx_len):
    _, b, h, t, hd = u.shape
    nc = t // CHUNK
    ncc = ctx_len // CHUNK
    hg = _pick(h, (8, 4, 2, 1))
    cs = max(c for c in (4, 2, 1) if ncc % c == 0 and nc % c == 0)
    nblk, nblk_c = nc // cs, ncc // cs
    r6 = lambda arr: arr.reshape(2, b, h, nc, CHUNK, arr.shape[-1])
    u6, w6, q6, k6, a6 = r6(u), r6(w), r6(qd), kd, r6(a)
    gl6 = gl[:, :, :, :nc].reshape(2, b, h, nc, 1, LANES)

    def cf(si):
        return si

    def cb(si):
        return jnp.where(si < nblk_c, nblk_c - 1 - si, nblk - 1 - (si - nblk_c))

    def spec(d, cmap, rows, last):
        return pl.BlockSpec((None, None, hg, cs, rows, last),
                            lambda bi, gi, si: (d, bi, gi, cmap(si), 0, 0))

    ins, specs = [], []
    for d, cmap in ((0, cf), (1, cb)):
        ins += [u6, w6, q6, k6, a6, gl6]
        specs += [spec(d, cmap, CHUNK, hd), spec(d, cmap, CHUNK, hd), spec(d, cmap, CHUNK, hd),
                  spec(d, cmap, hd, CHUNK), spec(d, cmap, CHUNK, CHUNK), spec(d, cmap, 1, LANES)]
    o_shape = jax.ShapeDtypeStruct((b, h, nc, CHUNK, hd), F32)
    ospec = lambda cmap: pl.BlockSpec((None, hg, cs, CHUNK, hd), lambda bi, gi, si: (bi, gi, cmap(si), 0, 0))
    of, ob = pl.pallas_call(
        functools.partial(_delta_scan_kernel, hg=hg),
        grid=(b, h // hg, nblk),
        in_specs=specs,
        out_specs=[ospec(cf), ospec(cb)],
        out_shape=[o_shape, o_shape],
        scratch_shapes=[pltpu.VMEM((2 * hg, hd, hd), F32)],
        compiler_params=_cparams(3),
        name="delta_scan",
    )(*ins)
    return of.reshape(b, h, t, hd), ob.reshape(b, h, t, hd)


def _delta_out_kernel(of_ref, ob_ref, z_ref, g_ref, o_ref):
    hd = of_ref.shape[2]
    for h in range(of_ref.shape[0]):
        o = of_ref[h] + ob_ref[h]
        o = o * lax.rsqrt(jnp.mean(o * o, axis=-1, keepdims=True) + NORM_EPS)
        z = z_ref[:, h * hd:(h + 1) * hd].astype(F32)
        o_ref[:, h * hd:(h + 1) * hd] = (o * g_ref[...] * (z * jax.nn.sigmoid(z))).astype(o_ref.dtype)


def _delta_out(of, ob, p1, o_gain, n_heads):
    b, h, t, hd = of.shape
    tt = TOKEN_TILE
    return pl.pallas_call(
        _delta_out_kernel,
        grid=(b, t // tt),
        in_specs=[pl.BlockSpec((None, h, tt, hd), lambda bi, ti: (bi, 0, ti, 0)),
                  pl.BlockSpec((None, h, tt, hd), lambda bi, ti: (bi, 0, ti, 0)),
                  pl.BlockSpec((None, tt, h * hd), lambda bi, ti: (bi, ti, 3)),
                  pl.BlockSpec((1, hd), lambda bi, ti: (0, 0))],
        out_specs=pl.BlockSpec((None, tt, h * hd), lambda bi, ti: (bi, ti, 0)),
        out_shape=jax.ShapeDtypeStruct((b, t, h * hd), BF),
        compiler_params=_cparams(2),
        name="delta_out",
    )(of, ob, p1, o_gain.reshape(1, hd))


def _hy_conv_kernel(p_ref, w_ref, oc_ref, ox_ref, pad_ref, *, t_len, ctx_len):
    tt = TOKEN_TILE
    z8 = jnp.zeros((8, pad_ref.shape[1]), F32)
    pad_ref[pl.ds(0, 8), :] = z8
    pad_ref[pl.ds(8 + t_len, 8), :] = z8
    pad_ref[pl.ds(8, t_len), :] = p_ref[...].astype(F32)
    w0, w1, w2 = w_ref[0:1, :], w_ref[1:2, :], w_ref[2:3, :]
    for ti in range(t_len // tt):
        t0 = ti * tt
        r = lax.broadcasted_iota(jnp.int32, (tt, 1), 0) + t0
        left = pad_ref[pl.ds(8 + t0 - 1, tt), :]
        mid = pad_ref[pl.ds(8 + t0, tt), :]
        right = pad_ref[pl.ds(8 + t0 + 1, tt), :]
        ok_l = (r != 0) & (r != ctx_len)
        ok_r = (r != ctx_len - 1) & (r != t_len - 1)
        y = jnp.where(ok_l, left, 0.0) * w0 + mid * w1 + jnp.where(ok_r, right, 0.0) * w2
        if t0 < ctx_len:
            oc_ref[pl.ds(t0, tt), :] = y.astype(oc_ref.dtype)
        else:
            ox_ref[pl.ds(t0 - ctx_len, tt), :] = y.astype(ox_ref.dtype)


def _hy_conv(p_tail, hy_conv_w, ctx_len, width3):
    b, t, _ = p_tail.shape
    tc = _pick(width3, (512, 256, 128))
    return pl.pallas_call(
        functools.partial(_hy_conv_kernel, t_len=t, ctx_len=ctx_len),
        grid=(b, width3 // tc),
        in_specs=[pl.BlockSpec((None, t, tc), lambda bi, j: (bi, 0, j)),
                  pl.BlockSpec((3, tc), lambda bi, j: (0, j))],
        out_specs=[pl.BlockSpec((None, ctx_len, tc), lambda bi, j: (bi, 0, j)),
                   pl.BlockSpec((None, t - ctx_len, tc), lambda bi, j: (bi, 0, j))],
        out_shape=[jax.ShapeDtypeStruct((b, ctx_len, width3), BF),
                   jax.ShapeDtypeStruct((b, t - ctx_len, width3), BF)],
        scratch_shapes=[pltpu.VMEM((t + 16, tc), F32)],
        compiler_params=_cparams(2),
        name="hyena_short_conv",
    )(p_tail, hy_conv_w)


def _hy_filter_kernel(ft_ref, w1_ref, b1_ref, f1_ref, w2_ref, b2_ref, f2_ref, w3_ref, dec_ref, o_ref, hid_s,
                      *, mask_lag0):
    part = pl.program_id(0)
    feats = ft_ref[...]

    @pl.when((pl.program_id(2) == 0) & (pl.program_id(3) == 0))
    def _():
        hid = jnp.sin(f1_ref[...] * (_dot3(feats, w1_ref[...]) + b1_ref[...]))
        hid_s[...] = jnp.sin(f2_ref[...] * (_dot3(hid, w2_ref[...]) + b2_ref[...]))

    filt = _bdot(hid_s[...], w3_ref[...])
    tpos = feats[:, 0:1]
    val = filt * jnp.exp(-tpos * jnp.abs(dec_ref[...]))
    if mask_lag0:
        row = lax.broadcasted_iota(jnp.int32, (val.shape[0], 1), 0) + pl.program_id(1) * val.shape[0]
        val = jnp.where((part == 1) & (row == 0), 0.0, val)
    o_ref[...] = val.astype(o_ref.dtype)


def _features(length, n_emb, k_pad, reverse_tail):
    t = jnp.linspace(0.0, 1.0, length, dtype=F32)
    n_bands = (n_emb - 1) // 2
    omega = 2.0 * math.pi * jnp.arange(length, dtype=F32) / length
    bands = jnp.linspace(1e-4, n_bands - 1, n_bands, dtype=F32)
    ang = omega[:, None] * bands[None, :]
    feats = jnp.concatenate([t[:, None], jnp.cos(ang), -jnp.sin(ang)], axis=-1)
    feats = jnp.pad(feats, ((0, 0), (0, k_pad - n_emb)))
    if reverse_tail:
        tail = feats[::-1]
    else:
        tail = jnp.concatenate([jnp.zeros((1, k_pad), F32), feats[:-1]], axis=0)
    return jnp.stack([feats, tail])


def _hy_filter_stack(length, w1, b1, f1, w2, b2, f2, w3, decay, hyw, reverse_tail):
    n_emb, fh = w1.shape
    k_pad = 64 if n_emb <= 64 else _pick(n_emb, (128,))
    feats = _features(length, n_emb, k_pad, reverse_tail)
    w1p = jnp.pad(w1, ((0, k_pad - n_emb), (0, 0)))
    tr = _pick(length, (512, 256, 128))
    tc = _pick(hyw, (2048, 1024, 512, 256, 128))
    nj = hyw // tc
    ni = length // tr
    return pl.pallas_call(
        functools.partial(_hy_filter_kernel, mask_lag0=not reverse_tail),
        grid=(2, ni, 2, nj),
        in_specs=[pl.BlockSpec((None, tr, k_pad), lambda p, i, c, j: (p, i, 0)),
                  pl.BlockSpec((k_pad, fh), lambda p, i, c, j: (0, 0)),
                  pl.BlockSpec((1, fh), lambda p, i, c, j: (0, 0)),
                  pl.BlockSpec((1, fh), lambda p, i, c, j: (0, 0)),
                  pl.BlockSpec((fh, fh), lambda p, i, c, j: (0, 0)),
                  pl.BlockSpec((1, fh), lambda p, i, c, j: (0, 0)),
                  pl.BlockSpec((1, fh), lambda p, i, c, j: (0, 0)),
                  pl.BlockSpec((fh, tc), lambda p, i, c, j: (0, (2 * c + p) * nj + j)),
                  pl.BlockSpec((1, tc), lambda p, i, c, j: (0, (2 * c + p) * nj + j))],
        out_specs=pl.BlockSpec((tr, tc), lambda p, i, c, j: (p * ni + i, c * nj + j)),
        out_shape=jax.ShapeDtypeStruct((2 * length, 2 * hyw), BF),
        scratch_shapes=[pltpu.VMEM((tr, fh), F32)],
        compiler_params=_cparams(4),
        name="hyena_filters",
    )(feats, w1p, b1.reshape(1, fh), f1.reshape(1, fh), w2, b2.reshape(1, fh), f2.reshape(1, fh),
      w3, decay.reshape(1, HY_N_FILT * hyw))


def _dft_tables(length, tile):
    n = 2 * length
    half = tile // 2
    rows = jnp.arange(n, dtype=jnp.int32)
    tile_i, r = rows // tile, rows % tile
    is_im = r >= half
    k = tile_i * half + jnp.where(is_im, r - half, r)
    t = jnp.arange(length, dtype=jnp.int32)
    ang = (2.0 * math.pi / n) * ((k[:, None] * t[None, :]) % n).astype(F32)
    nyq = jnp.where(t % 2 == 0, 1.0, -1.0).astype(F32)[None, :]
    is_nyq = (is_im & (k == 0))[:, None]
    fwd = jnp.where(is_im[:, None], -jnp.sin(ang), jnp.cos(ang))
    fwd = jnp.where(is_nyq, nyq, fwd)
    sgn = jnp.where(is_im[:, None] & ~is_nyq, -1.0, 1.0)
    fker = jnp.concatenate([fwd, fwd * sgn], axis=1)
    scale = jnp.where(k == 0, 1.0 / n, 2.0 / n)[None, :]
    inv = jnp.where(is_im[None, :], -jnp.sin(ang.T), jnp.cos(ang.T))
    inv = jnp.where(is_nyq.T, nyq.T, inv) * scale
    return fwd.astype(BF), fker.astype(BF), inv.astype(BF)


def _hy_long_convs(u_seq, kstack, hy_skip, hyw):
    b, ls, _ = u_seq.shape
    n = 2 * ls
    tmf = _pick(n, (512,))
    half = tmf // 2
    tn = _pick(hyw, (512, 256, 128))
    nj = hyw // tn
    fwd, fker, inv = _dft_tables(ls, tmf)

    (hspec,) = _matmul(
        fker, kstack,
        pl.BlockSpec((tmf, n), lambda g, j, i: (i, 0)),
        pl.BlockSpec((n, tn), lambda g, j, i: (0, j)),
        (1, 2 * nj, n // tmf),
        [(jax.ShapeDtypeStruct((n, 2 * hyw), F32), pl.BlockSpec((tmf, tn), lambda g, j, i: (i, j)))],
        _ep_store(F32), name="hyena_filter_dft")

    def spectrum_product(acc, extras, outs):
        hs = extras[0][...]
        re, im = acc[:half], acc[half:]
        hre, him = hs[:half], hs[half:]
        first = (lax.broadcasted_iota(jnp.int32, (half, 1), 0) == 0) & (pl.program_id(2) == 0)
        yre = jnp.where(first, re * hre, re * hre - im * him)
        yim = jnp.where(first, im * him, re * him + im * hre)
        outs[0][pl.ds(0, half), :] = yre.astype(BF)
        outs[0][pl.ds(half, half), :] = yim.astype(BF)

    tmi = _pick(ls, (TOKEN_TILE,))
    skip3 = hy_skip.reshape(2, 1, hyw)

    def conv(z_arr, z_off, conv_idx, mul_off):
        (spec,) = _matmul(
            fwd, z_arr,
            pl.BlockSpec((tmf, ls), lambda g, j, i: (i, 0)),
            pl.BlockSpec((None, ls, tn), lambda g, j, i: (g, 0, z_off + j)),
            (b, nj, n // tmf),
            [(jax.ShapeDtypeStruct((b, n, hyw), BF), pl.BlockSpec((None, tmf, tn), lambda g, j, i: (g, i, j)))],
            spectrum_product,
            extras=[(hspec, pl.BlockSpec((tmf, tn), lambda g, j, i: (i, conv_idx * nj + j)))],
            name="hyena_dft_fwd")

        def finish(acc, extras, outs):
            z = extras[0][...].astype(F32)
            xm = extras[2][...].astype(F32)
            outs[0][...] = (xm * (acc + z * extras[1][...])).astype(BF)

        (y,) = _matmul(
            inv, spec,
            pl.BlockSpec((tmi, n), lambda g, j, i: (i, 0)),
            pl.BlockSpec((None, n, tn), lambda g, j, i: (g, 0, j)),
            (b, nj, ls // tmi),
            [(jax.ShapeDtypeStruct((b, ls, hyw), BF), pl.BlockSpec((None, tmi, tn), lambda g, j, i: (g, i, j)))],
            finish,
            extras=[(z_arr, pl.BlockSpec((None, tmi, tn), lambda g, j, i: (g, i, z_off + j))),
                    (skip3, pl.BlockSpec((None, 1, tn), lambda g, j, i: (conv_idx, 0, j))),
                    (u_seq, pl.BlockSpec((None, tmi, tn), lambda g, j, i: (g, i, mul_off * nj + j)))],
            name="hyena_dft_inv")
        return y

    z2 = conv(u_seq, 0, 0, 1)
    return conv(z2, 0, 1, 2)


FFT_FAST = 64
FFT_K1_GROUP = 4


def _fft_tables(length):
    bf = FFT_FAST
    a_n = length // bf
    na = 2 * a_n
    n = 2 * length
    b = jnp.arange(bf, dtype=jnp.int32)
    k1 = jnp.arange(na, dtype=jnp.int32)
    a = jnp.arange(na, dtype=jnp.int32)
    nn = bf * a[None, None, :] + b[:, None, None]
    ang = (2.0 * math.pi / n) * ((k1[None, :, None] * nn) % n).astype(F32)
    cs, sn = jnp.cos(ang), jnp.sin(ang)
    m1k = jnp.concatenate([cs, -sn], axis=1)
    cd, sd = cs[:, :, :a_n], sn[:, :, :a_n]
    m1 = jnp.concatenate([jnp.concatenate([cd, sd], axis=2),
                          jnp.concatenate([-sd, cd], axis=2)], axis=1)
    ct, st = jnp.swapaxes(cd, 1, 2) / n, jnp.swapaxes(sd, 1, 2) / n
    g2 = jnp.concatenate([jnp.concatenate([ct, -st], axis=2),
                          jnp.concatenate([st, ct], axis=2)], axis=1)
    k2 = jnp.arange(bf, dtype=jnp.int32)
    ang2 = (2.0 * math.pi / bf) * ((k2[:, None] * b[None, :]) % bf).astype(F32)
    c2, s2 = jnp.cos(ang2), jnp.sin(ang2)
    f2 = jnp.concatenate([jnp.concatenate([c2, s2], axis=1), jnp.concatenate([-s2, c2], axis=1)], axis=0)
    i1 = jnp.concatenate([jnp.concatenate([c2.T, -s2.T], axis=1), jnp.concatenate([s2.T, c2.T], axis=1)], axis=0)
    return m1.astype(BF), m1k.astype(BF), f2.astype(BF), i1.astype(BF), g2.astype(BF)


def _fft_mid_kernel(p_ref, h_ref, f2_ref, i1_ref, o_ref):
    bf = f2_ref.shape[0] // 2
    for g in range(p_ref.shape[0]):
        x = jnp.dot(f2_ref[...], p_ref[g], preferred_element_type=F32)
        hs = h_ref[g].astype(F32)
        xr, xi = x[:bf], x[bf:]
        hr, hi = hs[:bf], hs[bf:]
        y = jnp.concatenate([xr * hr - xi * hi, xr * hi + xi * hr], axis=0).astype(BF)
        o_ref[g] = jnp.dot(i1_ref[...], y, preferred_element_type=F32).astype(o_ref.dtype)


def _fft_spec_kernel(p_ref, f2_ref, o_ref):
    for g in range(p_ref.shape[0]):
        o_ref[g] = jnp.dot(f2_ref[...], p_ref[g], preferred_element_type=F32).astype(o_ref.dtype)


def _fft_regroup(p, bf, na):
    lead = p.shape[:-3]
    c = p.shape[-1]
    nl = len(lead)
    p = p.reshape(*lead, bf, 2, na, c)
    p = jnp.transpose(p, tuple(range(nl)) + (nl + 2, nl + 1, nl + 0, nl + 3))
    return p.reshape(*lead, na, 2 * bf, c)


def _fft_ungroup(q, bf, na):
    lead = q.shape[:-3]
    c = q.shape[-1]
    nl = len(lead)
    q = q.reshape(*lead, na, 2, bf, c)
    q = jnp.transpose(q, tuple(range(nl)) + (nl + 2, nl + 1, nl + 0, nl + 3))
    return q.reshape(*lead, bf, 2 * na, c)


def _hy_long_convs_fft(u_seq, kern, hy_skip, hyw):
    nb_in, ls, _ = u_seq.shape
    if nb_in % 2:
        u_seq = jnp.concatenate([u_seq, jnp.zeros_like(u_seq[:1])], axis=0)
    nb = u_seq.shape[0] // 2
    bf = FFT_FAST
    a_n = ls // bf
    na = 2 * a_n
    ra = 2 * a_n
    tc = _pick(hyw, (2048, 1024, 512, 256, 128))
    nct = hyw // tc
    kg = _pick(na, (FFT_K1_GROUP, 2, 1))
    m1, m1k, f2, i1, g2 = _fft_tables(ls)
    u2 = u_seq.reshape(nb, ra, bf * 3 * hyw)
    k2 = kern.reshape(na, bf * 2 * hyw)

    (ph,) = _matmul(
        m1k, k2,
        pl.BlockSpec((None, 2 * na, na), lambda g, j, i: (j // (2 * nct), 0, 0)),
        pl.BlockSpec((na, tc), lambda g, j, i: (0, j)),
        (1, bf * 2 * nct, 1),
        [(jax.ShapeDtypeStruct((bf, 2 * na, 2 * hyw), BF),
          pl.BlockSpec((None, 2 * na, tc), lambda g, j, i: (j // (2 * nct), 0, j % (2 * nct))))],
        _ep_store(BF), name="hyena_fft_kern_s1")
    hspec = pl.pallas_call(
        _fft_spec_kernel,
        grid=(na // kg, 2 * nct),
        in_specs=[pl.BlockSpec((kg, 2 * bf, tc), lambda k, j: (k, 0, j)),
                  pl.BlockSpec((2 * bf, 2 * bf), lambda k, j: (0, 0))],
        out_specs=pl.BlockSpec((kg, 2 * bf, tc), lambda k, j: (k, 0, j)),
        out_shape=jax.ShapeDtypeStruct((na, 2 * bf, 2 * hyw), BF),
        compiler_params=_cparams(2),
        name="hyena_fft_kern_s2",
    )(_fft_regroup(ph, bf, na), f2)

    skip3 = hy_skip.reshape(2, 1, hyw)

    def conv(z2, z_sections, z_sec, conv_idx, mul_sec):
        zblk = lambda j: (j // nct) * (z_sections * nct) + z_sec * nct + j % nct
        ublk = lambda j: (j // nct) * (3 * nct) + mul_sec * nct + j % nct
        (p,) = _matmul(
            m1, z2,
            pl.BlockSpec((None, 2 * na, ra), lambda g, j, i: (j // nct, 0, 0)),
            pl.BlockSpec((None, ra, tc), lambda g, j, i: (g, 0, zblk(j))),
            (nb, bf * nct, 1),
            [(jax.ShapeDtypeStruct((nb, bf, 2 * na, hyw), BF),
              pl.BlockSpec((None, None, 2 * na, tc), lambda g, j, i: (g, j // nct, 0, j % nct)))],
            _ep_store(BF), name="hyena_fft_s1")
        q = pl.pallas_call(
            _fft_mid_kernel,
            grid=(na // kg, nct, nb),
            in_specs=[pl.BlockSpec((None, kg, 2 * bf, tc), lambda k, j, g: (g, k, 0, j)),
                      pl.BlockSpec((kg, 2 * bf, tc), lambda k, j, g: (k, 0, conv_idx * nct + j)),
                      pl.BlockSpec((2 * bf, 2 * bf), lambda k, j, g: (0, 0)),
                      pl.BlockSpec((2 * bf, 2 * bf), lambda k, j, g: (0, 0))],
            out_specs=pl.BlockSpec((None, kg, 2 * bf, tc), lambda k, j, g: (g, k, 0, j)),
            out_shape=jax.ShapeDtypeStruct((nb, na, 2 * bf, hyw), BF),
            compiler_params=_cparams(3),
            name="hyena_fft_mid",
        )(_fft_regroup(p, bf, na), hspec, f2, i1)

        def finish(acc, extras, outs):
            z = extras[0][...].astype(F32)
            xm = extras[2][...].astype(F32)
            outs[0][...] = (xm * (acc + z * extras[1][...])).astype(BF)

        (y,) = _matmul(
            g2, _fft_ungroup(q, bf, na),
            pl.BlockSpec((None, ra, 2 * na), lambda g, j, i: (j // nct, 0, 0)),
            pl.BlockSpec((None, None, 2 * na, tc), lambda g, j, i: (g, j // nct, 0, j % nct)),
            (nb, bf * nct, 1),
            [(jax.ShapeDtypeStruct((nb, ra, bf * hyw), BF),
              pl.BlockSpec((None, ra, tc), lambda g, j, i: (g, 0, j)))],
            finish,
            extras=[(z2, pl.BlockSpec((None, ra, tc), lambda g, j, i: (g, 0, zblk(j)))),
                    (skip3, pl.BlockSpec((None, 1, tc), lambda g, j, i: (conv_idx, 0, j % nct))),
                    (u2, pl.BlockSpec((None, ra, tc), lambda g, j, i: (g, 0, ublk(j))))],
            name="hyena_fft_s3")
        return y

    z_mid = conv(u2, 3, 0, 0, 1)
    y = conv(z_mid, 1, 0, 1, 2)
    return y.reshape(2 * nb, ls, hyw)[:nb_in]


def _moe_up_kernel(x_ref, wg_ref, wu_ref, comb_ref, o_ref, *, es):
    x = x_ref[...]
    g = jnp.dot(x, wg_ref[...], preferred_element_type=F32)
    u = jnp.dot(x, wu_ref[...], preferred_element_type=F32)
    comb = comb_ref[...]
    lane = lax.broadcasted_iota(jnp.int32, comb.shape, 1)
    f = g.shape[1] // es
    col = lax.broadcasted_iota(jnp.int32, (1, g.shape[1]), 1)
    scale = jnp.zeros_like(g)
    for s in range(es):
        e = pl.program_id(0) * es + s
        sc = jnp.sum(jnp.where(lane == e, comb, 0.0), axis=1, keepdims=True)
        scale = jnp.where((col >= s * f) & (col < (s + 1) * f), sc, scale)
    o_ref[...] = (g * jax.nn.sigmoid(g) * u * scale).astype(o_ref.dtype)


MOE_EXPERTS_PER_STEP = 4


def _moe_group_weights(w):
    nl, ne, d, f = w.shape
    es = MOE_EXPERTS_PER_STEP
    w = w.reshape(nl, ne // es, es, d, f).transpose(0, 1, 3, 2, 4)
    return w.reshape(nl, ne // es, d, es * f).astype(BF)


def _moe_up(u, w_gate, w_up, comb, l):
    m, d = u.shape
    _, ng, _, nf = w_gate.shape
    tm = _pick(m, (544, 512, 256))
    wspec = pl.BlockSpec((None, None, d, nf), lambda e, i: (l, e, 0, 0))
    return pl.pallas_call(
        functools.partial(_moe_up_kernel, es=MOE_EXPERTS_PER_STEP),
        grid=(ng, m // tm),
        in_specs=[pl.BlockSpec((tm, d), lambda e, i: (i, 0)), wspec, wspec,
                  pl.BlockSpec((tm, LANES), lambda e, i: (i, 0))],
        out_specs=pl.BlockSpec((tm, nf), lambda e, i: (i, e)),
        out_shape=jax.ShapeDtypeStruct((m, ng * nf), BF),
        compiler_params=_cparams(2),
        name="moe_up",
    )(u, w_gate, w_up, comb)


def _row_gate(gt_ref, tm, t_len, ctx_len, n_batch):
    r = lax.broadcasted_iota(jnp.int32, (tm, 1), 0) + pl.program_id(2) * tm
    gate = jnp.zeros((tm, gt_ref.shape[1]), F32)
    for bi in range(n_batch):
        in_b = (r >= bi * t_len) & (r < (bi + 1) * t_len)
        is_c = r < bi * t_len + ctx_len
        gate = jnp.where(in_b & is_c, gt_ref[0:1, :], gate)
        gate = jnp.where(in_b & ~is_c, gt_ref[1 + bi:2 + bi, :], gate)
    return gate


def _final_norm_kernel(h_ref, g_ref, o_ref):
    xf = h_ref[...]
    o_ref[...] = xf * lax.rsqrt(jnp.mean(xf * xf, axis=-1, keepdims=True) + NORM_EPS) * g_ref[...]


def _final_norm(h, gain, n_batch, t_len, ctx_len):
    m, d = h.shape
    tt = TOKEN_TILE
    n_t, n_c = t_len // tt, ctx_len // tt
    n_x = n_t - n_c
    return pl.pallas_call(
        _final_norm_kernel,
        grid=(n_batch, n_x),
        in_specs=[pl.BlockSpec((tt, d), lambda bi, i: (bi * n_t + n_c + i, 0)),
                  pl.BlockSpec((1, d), lambda bi, i: (0, 0))],
        out_specs=pl.BlockSpec((None, tt, d), lambda bi, i: (bi, i, 0)),
        out_shape=jax.ShapeDtypeStruct((n_batch, t_len - ctx_len, d), F32),
        compiler_params=_cparams(2),
        name="final_norm",
    )(h, gain.reshape(1, d))


def kernel(x, c, ctx, c_ctx, ada_down, ada_up, ada_bias, norm_mix, norm_ffn, w_in, qkv_conv, a_log, dt_bias, o_norm, hy_conv, hy_w1, hy_b1, hy_f1, hy_w2, hy_b2, hy_f2, hy_w3, hy_decay, hy_skip, w_br_a, w_br_b, w_out, router_group, router_group_bias, router_expert, router_expert_bias, exp_gate, exp_up, exp_down, final_norm):
    n_batch, seq, d = x.shape
    ctx_len = ctx.shape[1]
    t_len = ctx_len + seq
    m = n_batch * t_len
    depth = w_in.shape[0]
    n_heads = a_log.shape[2]
    a_width = n_heads * LANES
    hyw = hy_skip.shape[2]
    n_exp = exp_gate.shape[1]
    dg_off = 4 * a_width
    hy_off = dg_off + 4 * n_heads
    tail_w = 3 * hyw + 2 * d
    n_t, n_c = t_len // TOKEN_TILE, ctx_len // TOKEN_TILE
    nc = t_len // CHUNK
    ncp = -(-nc // 8) * 8

    h = jnp.concatenate([ctx, x], axis=1).reshape(m, d)

    conds = jnp.zeros((8, d), F32).at[0].set(c_ctx).at[1:1 + n_batch].set(c)
    mod_all = _adaln_all(conds, ada_down, ada_up, ada_bias)
    mod_all = mod_all.reshape(depth, 8, 6, d)[:, :1 + n_batch]
    mod_all = jnp.pad(mod_all, ((0, 0), (0, 0), (0, 2), (0, 0)))

    tm = _pick(m, (1088, 512, 256))
    tmp = tm
    tn = 512

    w_in_b = w_in.astype(BF)
    tail_base = (hy_off // LANES) * LANES
    exp_gate_b = _moe_group_weights(exp_gate)
    exp_up_b = _moe_group_weights(exp_up)

    def gate_table(mod_l, idx):
        return jnp.pad(mod_l[:, idx, :], ((0, 8 - (1 + n_batch)), (0, 0)))

    def residual_ep(acc, extras, outs):
        outs[0][...] = extras[0][...] + _row_gate(extras[1], acc.shape[0], t_len, ctx_len, n_batch) * acc

    for l in range(depth):
        mod_l = mod_all[l]
        u = _norm_mod(h, norm_mix[l], mod_l, 0, n_t, n_c)
        (p1,) = _matmul(
            u, w_in_b,
            pl.BlockSpec((tmp, d), lambda g, j, i: (i, 0)),
            pl.BlockSpec((None, d, tn), lambda g, j, i, l=l: (l, 0, j)),
            (1, dg_off // tn, m // tmp),
            [(jax.ShapeDtypeStruct((m, dg_off), BF), pl.BlockSpec((tmp, tn), lambda g, j, i: (i, j)))],
            _ep_store(BF), name="proj_qkvz")
        (p_tail,) = _matmul(
            u, w_in_b,
            pl.BlockSpec((tmp, d), lambda g, j, i: (i, 0)),
            pl.BlockSpec((None, d, tn), lambda g, j, i, l=l: (l, 0, tail_base // tn + j)),
            (1, tail_w // tn, m // tmp),
            [(jax.ShapeDtypeStruct((m, tail_w), BF), pl.BlockSpec((tmp, tn), lambda g, j, i: (i, j)))],
            _ep_store(BF), name="proj_tail",
            w_next_spec=pl.BlockSpec((None, d, LANES),
                                     lambda g, j, i, l=l: (l, 0, (tail_base + (j + 1) * tn) // LANES)),
            lane_shift=hy_off - tail_base)
        w_dg_t = w_in[l, :, dg_off:hy_off].T
        a_exp = jnp.concatenate([jnp.exp(a_log[l].reshape(-1)), jnp.zeros((2 * n_heads,), F32)])[:, None]
        dt_b = jnp.concatenate([dt_bias[l].reshape(-1), jnp.zeros((2 * n_heads,), F32)])[:, None]
        gates = _gates(u, w_dg_t, a_exp, dt_b)

        g5 = gates.reshape(4, n_heads, n_batch, nc, CHUNK).transpose(2, 1, 0, 3, 4)
        g_rows = jnp.pad(g5, ((0, 0), (0, 0), (0, 0), (0, ncp - nc), (0, 0)))
        g_cols = jnp.pad(jnp.swapaxes(g5, 3, 4), ((0, 0), (0, 0), (0, 0), (0, 0), (0, LANES - nc)))

        p1_3 = p1.reshape(n_batch, t_len, dg_off)
        conv9 = qkv_conv[l].reshape(9, 3 * a_width)
        du, dw, dq, dk, da, dgl = _delta_prep(p1_3, conv9, g_rows, g_cols, n_heads, ctx_len)
        o_f, o_b = _delta_scan(du, dw, dq, dk, da, dgl, ctx_len)
        o_a = _delta_out(o_f, o_b, p1_3, o_norm[l], n_heads).reshape(m, a_width)

        p_tail3 = p_tail.reshape(n_batch, t_len, tail_w)
        u_c, u_x = _hy_conv(p_tail3, hy_conv[l], ctx_len, 3 * hyw)
        filt_args = (hy_w1[l], hy_b1[l], hy_f1[l], hy_w2[l], hy_b2[l], hy_f2[l], hy_w3[l], hy_decay[l])
        y_x = _hy_long_convs_fft(u_x, _hy_filter_stack(seq, *filt_args, hyw, True), hy_skip[l], hyw)
        y_c = _hy_long_convs(u_c, _hy_filter_stack(ctx_len, *filt_args, hyw, False), hy_skip[l], hyw)
        y_b = jnp.concatenate([y_c, y_x], axis=1).reshape(m, hyw)

        ga_blk = (3 * hyw) // tn
        gb_blk = (3 * hyw + d) // tn

        merged = pl.pallas_call(
            _merge_kernel,
            grid=(d // tn, m // tm),
            in_specs=[pl.BlockSpec((tm, a_width), lambda j, i: (i, 0)),
                      pl.BlockSpec((tm, hyw), lambda j, i: (i, 0)),
                      pl.BlockSpec((None, a_width, tn), lambda j, i, l=l: (l, 0, j)),
                      pl.BlockSpec((None, hyw, tn), lambda j, i, l=l: (l, 0, j)),
                      pl.BlockSpec((tm, tn), lambda j, i: (i, ga_blk + j)),
                      pl.BlockSpec((tm, tn), lambda j, i: (i, gb_blk + j))],
            out_specs=pl.BlockSpec((tm, tn), lambda j, i: (i, j)),
            out_shape=jax.ShapeDtypeStruct((m, d), BF),
            scratch_shapes=[pltpu.VMEM((a_width, tn), BF), pltpu.VMEM((hyw, tn), BF)],
            compiler_params=_cparams(2),
            name="branch_merge",
        )(o_a, y_b, w_br_a, w_br_b, p_tail, p_tail)

        (h,) = _matmul(
            merged, w_out,
            pl.BlockSpec((tm, d), lambda g, j, i: (i, 0)),
            pl.BlockSpec((None, d, tn), lambda g, j, i, l=l: (l, 0, j)),
            (1, d // tn, m // tm),
            [(jax.ShapeDtypeStruct((m, d), F32), pl.BlockSpec((tm, tn), lambda g, j, i: (i, j)))],
            residual_ep,
            extras=[(h, pl.BlockSpec((tm, tn), lambda g, j, i: (i, j))),
                    (gate_table(mod_l, 2), pl.BlockSpec((8, tn), lambda g, j, i: (0, j)))],
            name="out_proj")

        w_router = jnp.pad(jnp.concatenate([router_expert[l], router_group[l]], axis=1),
                           ((0, 0), (0, LANES - n_exp - N_GROUPS)))
        b_router = jnp.pad(jnp.concatenate([router_expert_bias[l], router_group_bias[l]]),
                           (0, LANES - n_exp - N_GROUPS)).reshape(1, LANES)
        u2, comb = _norm_mod_router(h, norm_ffn[l], mod_l, 3, n_t, n_c, w_router, b_router, n_exp, N_GROUPS)
        hid = _moe_up(u2, exp_gate_b, exp_up_b, comb, l)
        kf = hid.shape[1]
        tnd = _pick(d, (512, 256, 128))
        tmd = _pick(m, (544, 512, 256))
        (h,) = _matmul(
            hid, exp_down.reshape(depth, kf, d),
            pl.BlockSpec((tmd, kf), lambda g, j, i: (i, 0)),
            pl.BlockSpec((None, kf, tnd), lambda g, j, i, l=l: (l, 0, j)),
            (1, d // tnd, m // tmd),
            [(jax.ShapeDtypeStruct((m, d), F32), pl.BlockSpec((tmd, tnd), lambda g, j, i: (i, j)))],
            residual_ep,
            extras=[(h, pl.BlockSpec((tmd, tnd), lambda g, j, i: (i, j))),
                    (gate_table(mod_l, 5), pl.BlockSpec((8, tnd), lambda g, j, i: (0, j)))],
            name="moe_down")

    return _final_norm(h, final_norm, n_batch, t_len, ctx_len)
```

```python
import functools
import math

import jax
import jax.numpy as jnp
from jax import lax
from jax.experimental import pallas as pl
from jax.experimental.pallas import tpu as pltpu

F32 = jnp.float32
BF = jnp.bfloat16

GRID_W = 64
CHUNK = 64
N_GROUPS = 4
EXPERTS_PER_GROUP = 8
HY_N_FILT = 4
NORM_EPS = 1e-6
TOKEN_TILE = 256
LANES = 128
CONV_PAD = 72
VMEM_LIMIT = 56 * 1024 * 1024


def _cparams(n_grid):
    return pltpu.CompilerParams(dimension_semantics=("arbitrary",) * n_grid,
                                vmem_limit_bytes=VMEM_LIMIT)


def _split_bf16(a):
    hi = a.astype(BF)
    lo = (a - hi.astype(F32)).astype(BF)
    return hi, lo


def _dot3(a, b):
    ah, al = _split_bf16(a)
    bh, bl = _split_bf16(b)
    d = functools.partial(jnp.dot, preferred_element_type=F32)
    return d(ah, bh) + (d(ah, bl) + d(al, bh))


def _dotx(a, b):
    return jnp.dot(a, b, precision=lax.Precision.HIGHEST, preferred_element_type=F32)


def _bdot(a, b):
    return jnp.dot(a.astype(BF), b.astype(BF), preferred_element_type=F32)


def _mm_kernel(*refs, n_extra, n_out, stage_w, lane_shift, prologue, epilogue):
    x_ref, w_ref = refs[0], refs[1]
    n_w = 3 if lane_shift else 2
    extras = refs[n_w:n_w + n_extra]
    outs = refs[n_w + n_extra:n_w + n_extra + n_out]
    scratch = refs[n_w + n_extra + n_out:]
    if stage_w:
        wbf = scratch[0]

        @pl.when(pl.program_id(2) == 0)
        def _():
            k = w_ref.shape[0]
            step = 512 if k % 512 == 0 else k

            def load(r0):
                wa = w_ref[pl.ds(r0, step), :]
                if lane_shift:
                    wa = jnp.concatenate([wa[:, lane_shift:], refs[2][pl.ds(r0, step), :lane_shift]], axis=1)
                return wa.astype(BF)

            if k == step:
                wbf[...] = load(0)
            else:
                def it(r, c):
                    r0 = pl.multiple_of(r * step, step)
                    wbf[pl.ds(r0, step), :] = load(r0)
                    return c
                lax.fori_loop(0, k // step, it, 0)
        w = wbf[...]
    else:
        w = w_ref[...]
    x = x_ref[...]
    if prologue is not None:
        x = prologue(x)
    if x.dtype != BF:
        x = x.astype(BF)
    acc = jnp.dot(x, w, preferred_element_type=F32)
    epilogue(acc, extras, outs)


def _matmul(x, w, x_spec, w_spec, grid, outs, epilogue, extras=(), prologue=None, name=None,
            w_next_spec=None, lane_shift=0):
    stage_w = w.dtype != BF or lane_shift > 0
    kw, tn = [d for d in w_spec.block_shape if d is not None]
    scratch = [pltpu.VMEM((kw, tn), BF)] if stage_w else []
    body = functools.partial(_mm_kernel, n_extra=len(extras), n_out=len(outs), stage_w=stage_w,
                             lane_shift=lane_shift, prologue=prologue, epilogue=epilogue)
    w_ops, w_specs = ([w, w], [w_spec, w_next_spec]) if lane_shift else ([w], [w_spec])
    res = pl.pallas_call(
        body,
        grid=grid,
        in_specs=[x_spec] + w_specs + [s for _, s in extras],
        out_specs=[s for _, s in outs],
        out_shape=[o for o, _ in outs],
        scratch_shapes=scratch,
        compiler_params=_cparams(3),
        name=name,
    )(x, *w_ops, *[a for a, _ in extras])
    return res


def _merge_kernel(xa_ref, xb_ref, wa_ref, wb_ref, ga_ref, gb_ref, o_ref, wa_bf, wb_bf):
    @pl.when(pl.program_id(1) == 0)
    def _():
        wa_bf[...] = wa_ref[...].astype(BF)
        wb_bf[...] = wb_ref[...].astype(BF)
    a = jnp.dot(xa_ref[...], wa_bf[...], preferred_element_type=F32)
    b = jnp.dot(xb_ref[...], wb_bf[...], preferred_element_type=F32)
    o_ref[...] = (jax.nn.sigmoid(ga_ref[...].astype(F32)) * a
                  + jax.nn.sigmoid(gb_ref[...].astype(F32)) * b).astype(o_ref.dtype)


def _ep_store(dtype):
    def ep(acc, extras, outs):
        outs[0][...] = acc.astype(dtype)
    return ep


def _pick(n, prefs):
    for p in prefs:
        if n % p == 0:
            return p
    return n


def _adaln_all(conds, ada_down, ada_up, ada_bias):
    nl, d, r = ada_down.shape
    n6 = ada_up.shape[2]
    (t,) = _matmul(
        conds, ada_down,
        pl.BlockSpec((8, d), lambda g, j, i: (0, 0)),
        pl.BlockSpec((None, d, r), lambda g, j, i: (g, 0, 0)),
        (nl, 1, 1),
        [(jax.ShapeDtypeStruct((nl, 8, r), F32), pl.BlockSpec((None, 8, r), lambda g, j, i: (g, 0, 0)))],
        _ep_store(F32), prologue=lambda v: v * jax.nn.sigmoid(v), name="adaln_down")
    tn = _pick(n6, (2048, 1024, 512, 256, 128))

    def ep(acc, extras, outs):
        outs[0][...] = acc + extras[0][...]

    (mod,) = _matmul(
        t, ada_up,
        pl.BlockSpec((None, 8, r), lambda g, j, i: (g, 0, 0)),
        pl.BlockSpec((None, r, tn), lambda g, j, i: (g, 0, j)),
        (nl, n6 // tn, 1),
        [(jax.ShapeDtypeStruct((nl, 8, n6), F32), pl.BlockSpec((None, 8, tn), lambda g, j, i: (g, 0, j)))],
        ep, extras=[(ada_bias.reshape(nl, 1, n6), pl.BlockSpec((None, 1, tn), lambda g, j, i: (g, 0, j)))],
        name="adaln_up")
    return mod


def _mod_row_map(n_t, n_c):
    def m(i):
        return (jnp.where(i % n_t < n_c, 0, 1 + i // n_t), 0, 0)
    return m


def _norm_mod_kernel(h_ref, g_ref, mod_ref, u_ref, *, off):
    xf = h_ref[...]
    y = xf * lax.rsqrt(jnp.mean(xf * xf, axis=-1, keepdims=True) + NORM_EPS) * g_ref[...]
    sh = mod_ref[off:off + 1, :]
    sc = mod_ref[off + 1:off + 2, :]
    u_ref[...] = (y * (1.0 + sc) + sh).astype(u_ref.dtype)


def _norm_mod(h, gain, mod_l, off, n_t, n_c):
    m, d = h.shape
    tt = TOKEN_TILE
    return pl.pallas_call(
        functools.partial(_norm_mod_kernel, off=off),
        grid=(m // tt,),
        in_specs=[pl.BlockSpec((tt, d), lambda i: (i, 0)),
                  pl.BlockSpec((1, d), lambda i: (0, 0)),
                  pl.BlockSpec((None, 8, d), _mod_row_map(n_t, n_c))],
        out_specs=pl.BlockSpec((tt, d), lambda i: (i, 0)),
        out_shape=jax.ShapeDtypeStruct((m, d), BF),
        compiler_params=_cparams(1),
        name="norm_mod",
    )(h, gain.reshape(1, d), mod_l)


def _router_kernel(h_ref, g_ref, mod_ref, wr_ref, br_ref, u_ref, comb_ref, *, off, n_exp, n_grp):
    xf = h_ref[...]
    y = xf * lax.rsqrt(jnp.mean(xf * xf, axis=-1, keepdims=True) + NORM_EPS) * g_ref[...]
    sh = mod_ref[off:off + 1, :]
    sc = mod_ref[off + 1:off + 2, :]
    u = y * (1.0 + sc) + sh
    u_ref[...] = u.astype(u_ref.dtype)
    logits = _dot3(u, wr_ref[...]) + br_ref[...]
    lane = lax.broadcasted_iota(jnp.int32, logits.shape, 1)
    neg = jnp.float32(-jnp.inf)
    big = jnp.int32(1 << 20)
    is_g = (lane >= n_exp) & (lane < n_exp + n_grp)
    lg = jnp.where(is_g, logits, neg)
    eg = jnp.exp(lg - jnp.max(lg, axis=1, keepdims=True))
    pg = eg / jnp.sum(eg, axis=1, keepdims=True)
    p_top = jnp.max(pg, axis=1, keepdims=True)
    g_idx = jnp.min(jnp.where(is_g & (pg == p_top), lane, big), axis=1, keepdims=True) - n_exp
    per = n_exp // n_grp
    in_grp = (lane >= g_idx * per) & (lane < (g_idx + 1) * per)
    le = jnp.where(in_grp, logits, neg)
    v1 = jnp.max(le, axis=1, keepdims=True)
    i1 = jnp.min(jnp.where(le == v1, lane, big), axis=1, keepdims=True)
    le2 = jnp.where(lane == i1, neg, le)
    v2 = jnp.max(le2, axis=1, keepdims=True)
    i2 = jnp.min(jnp.where(le2 == v2, lane, big), axis=1, keepdims=True)
    e2 = jnp.exp(v2 - v1)
    w1 = p_top / (1.0 + e2)
    w2 = p_top * e2 / (1.0 + e2)
    comb_ref[...] = jnp.where(lane == i1, w1, 0.0) + jnp.where(lane == i2, w2, 0.0)


def _norm_mod_router(h, gain, mod_l, off, n_t, n_c, w_router, b_router, n_exp, n_grp):
    m, d = h.shape
    tt = TOKEN_TILE
    return pl.pallas_call(
        functools.partial(_router_kernel, off=off, n_exp=n_exp, n_grp=n_grp),
        grid=(m // tt,),
        in_specs=[pl.BlockSpec((tt, d), lambda i: (i, 0)),
                  pl.BlockSpec((1, d), lambda i: (0, 0)),
                  pl.BlockSpec((None, 8, d), _mod_row_map(n_t, n_c)),
                  pl.BlockSpec((d, LANES), lambda i: (0, 0)),
                  pl.BlockSpec((1, LANES), lambda i: (0, 0))],
        out_specs=[pl.BlockSpec((tt, d), lambda i: (i, 0)),
                   pl.BlockSpec((tt, LANES), lambda i: (i, 0))],
        out_shape=[jax.ShapeDtypeStruct((m, d), BF), jax.ShapeDtypeStruct((m, LANES), F32)],
        compiler_params=_cparams(1),
        name="norm_mod_router",
    )(h, gain.reshape(1, d), mod_l, w_router, b_router)


def _gates_kernel(w_ref, u_ref, aexp_ref, dtb_ref, o_ref, *, n_decay):
    acc = lax.dot_general(w_ref[...].astype(BF), u_ref[...], (((1,), (1,)), ((), ())),
                          preferred_element_type=F32)
    row = lax.broadcasted_iota(jnp.int32, acc.shape, 0)
    z = acc + dtb_ref[...]
    softplus = jnp.maximum(z, 0.0) + jnp.log(1.0 + jnp.exp(-jnp.abs(z)))
    g = -aexp_ref[...] * softplus
    o_ref[...] = jnp.where(row < n_decay, g, jax.nn.sigmoid(acc))


def _gates(u, w_dg_t, a_exp, dt_b):
    m, d = u.shape
    r = w_dg_t.shape[0]
    tm = _pick(m, (512, 256))
    return pl.pallas_call(
        functools.partial(_gates_kernel, n_decay=r // 2),
        grid=(m // tm,),
        in_specs=[pl.BlockSpec((r, d), lambda i: (0, 0)),
                  pl.BlockSpec((tm, d), lambda i: (i, 0)),
                  pl.BlockSpec((r, 1), lambda i: (0, 0)),
                  pl.BlockSpec((r, 1), lambda i: (0, 0))],
        out_specs=pl.BlockSpec((r, tm), lambda i: (0, i)),
        out_shape=jax.ShapeDtypeStruct((r, m), F32),
        compiler_params=_cparams(1),
        name="delta_gates",
    )(w_dg_t, u, a_exp, dt_b)


def _conv_silu_tile(pad_ref, cw_ref, t0, n, seq_start, seq_len, width):
    one_row = width == seq_len
    r = lax.broadcasted_iota(jnp.int32, (n, 1), 0) + (t0 - seq_start)
    col = lax.rem(r, width)
    acc = jnp.zeros((n, LANES), F32)
    for dj in (-1, 0, 1):
        part = jnp.zeros((n, LANES), F32)
        for di in (-1, 0, 1):
            if one_row and di != 0:
                continue
            off = di * width + dj
            tap = pad_ref[pl.ds(CONV_PAD + t0 + off, n), :]
            wrow = cw_ref[(di + 1) * 3 + (dj + 1):(di + 1) * 3 + (dj + 1) + 1, :]
            lo, hi = t0 - seq_start + off, t0 - seq_start + n - 1 + off
            if lo < 0 or hi >= seq_len:
                tap = jnp.where((r + off >= 0) & (r + off < seq_len), tap, 0.0)
            part = part + tap * wrow
        ok_c = (col + dj >= 0) & (col + dj < width)
        acc = acc + jnp.where(ok_c, part, 0.0)
    return acc * jax.nn.sigmoid(acc)


def _bmm(a, b):
    return jnp.einsum('gik,gkj->gij', a.astype(BF), b.astype(BF), preferred_element_type=F32)


def _bmm_nt(a, b):
    return jnp.einsum('gik,gjk->gij', a.astype(BF), b.astype(BF), preferred_element_type=F32)


def _neumann_inverse(lmat, eye):
    x = -lmat
    p = eye + x
    n = lmat.shape[-1]
    k = 1
    while 2 * k < n:
        x = _bmm(x, x)
        p = p + _bmm(p, x)
        k *= 2
    return p


def _delta_prep_kernel(pq_ref, pk_ref, pv_ref, cq_ref, ck_ref, cv_ref, g_ref, gt_ref,
                       u_ref, w_ref, qd_ref, kd_ref, a_ref, gl_ref,
                       pad_ref, q_s, k_s, v_s, gcr_s, gct_s,
                       *, t_len, ctx_len, n_chunks, group):
    hd = q_s.shape[1]
    tt = TOKEN_TILE
    d = pl.program_id(2)

    @pl.when(d == 0)
    def _():
        zpad = jnp.zeros((CONV_PAD, LANES), F32)
        pad_ref[pl.ds(0, CONV_PAD), :] = zpad
        pad_ref[pl.ds(CONV_PAD + t_len, CONV_PAD), :] = zpad
        for src, cw, dst, kind in ((pq_ref, cq_ref, q_s, "q"), (pk_ref, ck_ref, k_s, "k"), (pv_ref, cv_ref, v_s, "v")):
            pad_ref[pl.ds(CONV_PAD, t_len), :] = src[...].astype(F32)
            for ti in range(t_len // tt):
                t0 = ti * tt
                if t0 < ctx_len:
                    a = _conv_silu_tile(pad_ref, cw, t0, tt, 0, ctx_len, ctx_len)
                else:
                    a = _conv_silu_tile(pad_ref, cw, t0, tt, ctx_len, t_len - ctx_len, GRID_W)
                if kind != "v":
                    a = a * lax.rsqrt(jnp.sum(a * a, axis=-1, keepdims=True) + NORM_EPS)
                if kind == "q":
                    a = a * (hd ** -0.5)
                dst[pl.ds(t0, tt), :] = a

    ii = lax.broadcasted_iota(jnp.int32, (CHUNK, CHUNK), 0)
    jj = lax.broadcasted_iota(jnp.int32, (CHUNK, CHUNK), 1)
    sdiff = (ii - jj) * (1 - 2 * d)
    eye = (ii == jj).astype(F32)
    incl = sdiff >= 0
    strict = sdiff > 0
    gcr_s[...] = _dotx(g_ref[d], (sdiff <= 0).astype(F32))
    gct_s[...] = _dotx(incl.astype(F32), gt_ref[d])
    gtot = jnp.where(d == 0, gcr_s[:, CHUNK - 1:CHUNK], gcr_s[:, 0:1])
    gl_ref[...] = jnp.broadcast_to(jnp.exp(gtot), gl_ref.shape)

    gc = group
    rows = gc * CHUNK
    lane_c = lax.broadcasted_iota(jnp.int32, (CHUNK, LANES), 1)
    eye_hd = (lax.broadcasted_iota(jnp.int32, (hd, hd), 0)
              == lax.broadcasted_iota(jnp.int32, (hd, hd), 1)).astype(BF)[None]

    def body(it, carry):
        c0 = it * gc
        t0 = pl.multiple_of(c0 * CHUNK, rows)
        q = q_s[pl.ds(t0, rows), :].reshape(gc, CHUNK, hd)
        k = k_s[pl.ds(t0, rows), :].reshape(gc, CHUNK, hd)
        v = v_s[pl.ds(t0, rows), :].reshape(gc, CHUNK, hd)
        gct = gct_s[...]
        bt = gt_ref[2 + d]
        gcols, bcols, grows, glasts = [], [], [], []
        for g in range(gc):
            sel = lane_c == c0 + g
            gcols.append(jnp.sum(jnp.where(sel, gct, 0.0), axis=1, keepdims=True))
            bcols.append(jnp.sum(jnp.where(sel, bt, 0.0), axis=1, keepdims=True))
            grow_g = gcr_s[pl.ds(c0 + g, 1), :]
            grows.append(grow_g)
            glasts.append(jnp.where(d == 0, grow_g[:, CHUNK - 1:CHUNK], grow_g[:, 0:1]))
        gcol = jnp.broadcast_to(jnp.stack(gcols), (gc, CHUNK, hd))
        bcol = jnp.broadcast_to(jnp.stack(bcols), (gc, CHUNK, hd))
        grow = jnp.stack(grows)
        glast = jnp.stack(glasts)
        decay = jnp.where(incl, jnp.exp(jnp.where(incl, gcol[:, :, :CHUNK] - grow, 0.0)), 0.0)
        kb = k * bcol
        lmat = jnp.where(strict, _bmm_nt(kb, k) * decay, 0.0)
        tinv = _neumann_inverse(lmat, eye[None])
        egc = jnp.exp(gcol)
        uw = _bmm(tinv, jnp.concatenate([v * bcol, kb * egc], axis=2))
        attn = (_bmm_nt(q, k) * decay).astype(BF)
        kdec = k * jnp.exp(glast - gcol)
        kdec_t = _bmm_nt(jnp.broadcast_to(eye_hd, (gc, hd, hd)), kdec).astype(BF)
        u_ref[pl.ds(t0, rows), :] = uw[:, :, :hd].reshape(rows, hd)
        w_ref[pl.ds(t0, rows), :] = uw[:, :, hd:].astype(BF).reshape(rows, hd)
        qd_ref[pl.ds(t0, rows), :] = (q * egc).astype(BF).reshape(rows, hd)
        a_ref[pl.ds(t0, rows), :] = attn.reshape(rows, CHUNK)
        kd_ref[pl.ds(c0, gc)] = kdec_t
        return carry

    lax.fori_loop(0, n_chunks // group, body, 0)


def _delta_prep(p1, conv_w9, g_rows, g_cols, n_heads, ctx_len):
    b, t, _ = p1.shape
    hd = LANES
    nc = t // CHUNK
    ncp = g_rows.shape[3]
    group = max(g for g in range(1, 18) if nc % g == 0)
    h = n_heads
    kern = functools.partial(_delta_prep_kernel, t_len=t, ctx_len=ctx_len, n_chunks=nc, group=group)
    tok = lambda off: pl.BlockSpec((None, t, hd), lambda bi, hi, di: (bi, 0, off + hi))
    cws = lambda off: pl.BlockSpec((9, hd), lambda bi, hi, di: (0, off + hi))
    o5 = lambda last: pl.BlockSpec((None, None, None, t, last), lambda bi, hi, di: (di, bi, hi, 0, 0))
    return pl.pallas_call(
        kern,
        grid=(b, h, 2),
        in_specs=[tok(0), tok(h), tok(2 * h), cws(0), cws(h), cws(2 * h),
                  pl.BlockSpec((None, None, 4, ncp, CHUNK), lambda bi, hi, di: (bi, hi, 0, 0, 0)),
                  pl.BlockSpec((None, None, 4, CHUNK, LANES), lambda bi, hi, di: (bi, hi, 0, 0, 0))],
        out_specs=[o5(hd), o5(hd), o5(hd),
                   pl.BlockSpec((None, None, None, nc, hd, CHUNK), lambda bi, hi, di: (di, bi, hi, 0, 0, 0)),
                   o5(CHUNK),
                   pl.BlockSpec((None, None, None, ncp, LANES), lambda bi, hi, di: (di, bi, hi, 0, 0))],
        out_shape=[jax.ShapeDtypeStruct((2, b, h, t, hd), F32),
                   jax.ShapeDtypeStruct((2, b, h, t, hd), BF),
                   jax.ShapeDtypeStruct((2, b, h, t, hd), BF),
                   jax.ShapeDtypeStruct((2, b, h, nc, hd, CHUNK), BF),
                   jax.ShapeDtypeStruct((2, b, h, t, CHUNK), BF),
                   jax.ShapeDtypeStruct((2, b, h, ncp, LANES), F32)],
        scratch_shapes=[pltpu.VMEM((t + 2 * CONV_PAD, LANES), F32),
                        pltpu.VMEM((t, hd), F32), pltpu.VMEM((t, hd), F32), pltpu.VMEM((t, hd), F32),
                        pltpu.VMEM((ncp, CHUNK), F32), pltpu.VMEM((CHUNK, LANES), F32)],
        compiler_params=_cparams(3),
        name="delta_prep",
    )(p1, p1, p1, conv_w9, conv_w9, conv_w9, g_rows, g_cols)


def _delta_scan_kernel(uf, wf, qf, kf, af, gf, ub, wb, qb, kb, ab, gb, of_ref, ob_ref, s_ref, *, hg):
    @pl.when(pl.program_id(2) == 0)
    def _():
        s_ref[...] = jnp.zeros_like(s_ref)

    cs = uf.shape[1]
    s = s_ref[...]
    for t in range(cs):
        tb = cs - 1 - t
        cat = lambda a, b: jnp.concatenate([a[:, t], b[:, tb]], axis=0)
        sb = s.astype(BF)
        vn = cat(uf, ub) - _bmm(cat(wf, wb), sb)
        vnb = vn.astype(BF)
        o = _bmm(cat(qf, qb), sb) + _bmm(cat(af, ab), vnb)
        s = s * cat(gf, gb) + _bmm(cat(kf, kb), vnb)
        of_ref[:, t] = o[:hg]
        ob_ref[:, tb] = o[hg:]
    s_ref[...] = s


def _delta_scan(u, w, qd, kd, a, gl, ctx_len):
    _, b, h, t, hd = u.shape
    nc = t // CHUNK
    ncc = ctx_len // CHUNK
    hg = _pick(h, (8, 4, 2, 1))
    cs = max(c for c in (4, 2, 1) if ncc % c == 0 and nc % c == 0)
    nblk, nblk_c = nc // cs, ncc // cs
    r6 = lambda arr: arr.reshape(2, b, h, nc, CHUNK, arr.shape[-1])
    u6, w6, q6, k6, a6 = r6(u), r6(w), r6(qd), kd, r6(a)
    gl6 = gl[:, :, :, :nc].reshape(2, b, h, nc, 1, LANES)

    def cf(si):
        return si

    def cb(si):
        return jnp.where(si < nblk_c, nblk_c - 1 - si, nblk - 1 - (si - nblk_c))

    def spec(d, cmap, rows, last):
        return pl.BlockSpec((None, None, hg, cs, rows, last),
                            lambda bi, gi, si: (d, bi, gi, cmap(si), 0, 0))

    ins, specs = [], []
    for d, cmap in ((0, cf), (1, cb)):
        ins += [u6, w6, q6, k6, a6, gl6]
        specs += [spec(d, cmap, CHUNK, hd), spec(d, cmap, CHUNK, hd), spec(d, cmap, CHUNK, hd),
                  spec(d, cmap, hd, CHUNK), spec(d, cmap, CHUNK, CHUNK), spec(d, cmap, 1, LANES)]
    o_shape = jax.ShapeDtypeStruct((b, h, nc, CHUNK, hd), F32)
    ospec = lambda cmap: pl.BlockSpec((None, hg, cs, CHUNK, hd), lambda bi, gi, si: (bi, gi, cmap(si), 0, 0))
    of, ob = pl.pallas_call(
        functools.partial(_delta_scan_kernel, hg=hg),
        grid=(b, h // hg, nblk),
        in_specs=specs,
        out_specs=[ospec(cf), ospec(cb)],
        out_shape=[o_shape, o_shape],
        scratch_shapes=[pltpu.VMEM((2 * hg, hd, hd), F32)],
        compiler_params=_cparams(3),
        name="delta_scan",
    )(*ins)
    return of.reshape(b, h, t, hd), ob.reshape(b, h, t, hd)


def _delta_out_kernel(of_ref, ob_ref, z_ref, g_ref, o_ref):
    hd = of_ref.shape[2]
    for h in range(of_ref.shape[0]):
        o = of_ref[h] + ob_ref[h]
        o = o * lax.rsqrt(jnp.mean(o * o, axis=-1, keepdims=True) + NORM_EPS)
        z = z_ref[:, h * hd:(h + 1) * hd].astype(F32)
        o_ref[:, h * hd:(h + 1) * hd] = (o * g_ref[...] * (z * jax.nn.sigmoid(z))).astype(o_ref.dtype)


def _delta_out(of, ob, p1, o_gain, n_heads):
    b, h, t, hd = of.shape
    tt = TOKEN_TILE
    return pl.pallas_call(
        _delta_out_kernel,
        grid=(b, t // tt),
        in_specs=[pl.BlockSpec((None, h, tt, hd), lambda bi, ti: (bi, 0, ti, 0)),
                  pl.BlockSpec((None, h, tt, hd), lambda bi, ti: (bi, 0, ti, 0)),
                  pl.BlockSpec((None, tt, h * hd), lambda bi, ti: (bi, ti, 3)),
                  pl.BlockSpec((1, hd), lambda bi, ti: (0, 0))],
        out_specs=pl.BlockSpec((None, tt, h * hd), lambda bi, ti: (bi, ti, 0)),
        out_shape=jax.ShapeDtypeStruct((b, t, h * hd), BF),
        compiler_params=_cparams(2),
        name="delta_out",
    )(of, ob, p1, o_gain.reshape(1, hd))


def _hy_conv_kernel(p_ref, w_ref, oc_ref, ox_ref, pad_ref, *, t_len, ctx_len):
    tt = TOKEN_TILE
    z8 = jnp.zeros((8, pad_ref.shape[1]), F32)
    pad_ref[pl.ds(0, 8), :] = z8
    pad_ref[pl.ds(8 + t_len, 8), :] = z8
    pad_ref[pl.ds(8, t_len), :] = p_ref[...].astype(F32)
    w0, w1, w2 = w_ref[0:1, :], w_ref[1:2, :], w_ref[2:3, :]
    for ti in range(t_len // tt):
        t0 = ti * tt
        r = lax.broadcasted_iota(jnp.int32, (tt, 1), 0) + t0
        left = pad_ref[pl.ds(8 + t0 - 1, tt), :]
        mid = pad_ref[pl.ds(8 + t0, tt), :]
        right = pad_ref[pl.ds(8 + t0 + 1, tt), :]
        ok_l = (r != 0) & (r != ctx_len)
        ok_r = (r != ctx_len - 1) & (r != t_len - 1)
        y = jnp.where(ok_l, left, 0.0) * w0 + mid * w1 + jnp.where(ok_r, right, 0.0) * w2
        if t0 < ctx_len:
            oc_ref[pl.ds(t0, tt), :] = y.astype(oc_ref.dtype)
        else:
            ox_ref[pl.ds(t0 - ctx_len, tt), :] = y.astype(ox_ref.dtype)


def _hy_conv(p_tail, hy_conv_w, ctx_len, width3):
    b, t, _ = p_tail.shape
    tc = _pick(width3, (512, 256, 128))
    return pl.pallas_call(
        functools.partial(_hy_conv_kernel, t_len=t, ctx_len=ctx_len),
        grid=(b, width3 // tc),
        in_specs=[pl.BlockSpec((None, t, tc), lambda bi, j: (bi, 0, j)),
                  pl.BlockSpec((3, tc), lambda bi, j: (0, j))],
        out_specs=[pl.BlockSpec((None, ctx_len, tc), lambda bi, j: (bi, 0, j)),
                   pl.BlockSpec((None, t - ctx_len, tc), lambda bi, j: (bi, 0, j))],
        out_shape=[jax.ShapeDtypeStruct((b, ctx_len, width3), BF),
                   jax.ShapeDtypeStruct((b, t - ctx_len, width3), F32)],
        scratch_shapes=[pltpu.VMEM((t + 16, tc), F32)],
        compiler_params=_cparams(2),
        name="hyena_short_conv",
    )(p_tail, hy_conv_w)


def _hy_filter_kernel(ft_ref, w1_ref, b1_ref, f1_ref, w2_ref, b2_ref, f2_ref, w3_ref, dec_ref, o_ref, hid_s,
                      *, mask_lag0):
    part = pl.program_id(0)
    feats = ft_ref[...]

    @pl.when((pl.program_id(2) == 0) & (pl.program_id(3) == 0))
    def _():
        hid = jnp.sin(f1_ref[...] * (_dot3(feats, w1_ref[...]) + b1_ref[...]))
        hid_s[...] = jnp.sin(f2_ref[...] * (_dot3(hid, w2_ref[...]) + b2_ref[...]))

    filt = _bdot(hid_s[...], w3_ref[...])
    tpos = feats[:, 0:1]
    val = filt * jnp.exp(-tpos * jnp.abs(dec_ref[...]))
    if mask_lag0:
        row = lax.broadcasted_iota(jnp.int32, (val.shape[0], 1), 0) + pl.program_id(1) * val.shape[0]
        val = jnp.where((part == 1) & (row == 0), 0.0, val)
    o_ref[...] = val.astype(o_ref.dtype)


def _features(length, n_emb, k_pad, reverse_tail):
    t = jnp.linspace(0.0, 1.0, length, dtype=F32)
    n_bands = (n_emb - 1) // 2
    omega = 2.0 * math.pi * jnp.arange(length, dtype=F32) / length
    bands = jnp.linspace(1e-4, n_bands - 1, n_bands, dtype=F32)
    ang = omega[:, None] * bands[None, :]
    feats = jnp.concatenate([t[:, None], jnp.cos(ang), -jnp.sin(ang)], axis=-1)
    feats = jnp.pad(feats, ((0, 0), (0, k_pad - n_emb)))
    if reverse_tail:
        tail = feats[::-1]
    else:
        tail = jnp.concatenate([jnp.zeros((1, k_pad), F32), feats[:-1]], axis=0)
    return jnp.stack([feats, tail])


def _hy_filter_stack(length, w1, b1, f1, w2, b2, f2, w3, decay, hyw, reverse_tail):
    n_emb, fh = w1.shape
    k_pad = 64 if n_emb <= 64 else _pick(n_emb, (128,))
    feats = _features(length, n_emb, k_pad, reverse_tail)
    w1p = jnp.pad(w1, ((0, k_pad - n_emb), (0, 0)))
    tr = _pick(length, (512, 256, 128))
    tc = _pick(hyw, (2048, 1024, 512, 256, 128))
    nj = hyw // tc
    ni = length // tr
    return pl.pallas_call(
        functools.partial(_hy_filter_kernel, mask_lag0=not reverse_tail),
        grid=(2, ni, 2, nj),
        in_specs=[pl.BlockSpec((None, tr, k_pad), lambda p, i, c, j: (p, i, 0)),
                  pl.BlockSpec((k_pad, fh), lambda p, i, c, j: (0, 0)),
                  pl.BlockSpec((1, fh), lambda p, i, c, j: (0, 0)),
                  pl.BlockSpec((1, fh), lambda p, i, c, j: (0, 0)),
                  pl.BlockSpec((fh, fh), lambda p, i, c, j: (0, 0)),
                  pl.BlockSpec((1, fh), lambda p, i, c, j: (0, 0)),
                  pl.BlockSpec((1, fh), lambda p, i, c, j: (0, 0)),
                  pl.BlockSpec((fh, tc), lambda p, i, c, j: (0, (2 * c + p) * nj + j)),
                  pl.BlockSpec((1, tc), lambda p, i, c, j: (0, (2 * c + p) * nj + j))],
        out_specs=pl.BlockSpec((tr, tc), lambda p, i, c, j: (p * ni + i, c * nj + j)),
        out_shape=jax.ShapeDtypeStruct((2 * length, 2 * hyw), BF),
        scratch_shapes=[pltpu.VMEM((tr, fh), F32)],
        compiler_params=_cparams(4),
        name="hyena_filters",
    )(feats, w1p, b1.reshape(1, fh), f1.reshape(1, fh), w2, b2.reshape(1, fh), f2.reshape(1, fh),
      w3, decay.reshape(1, HY_N_FILT * hyw))


def _dft_tables(length, tile):
    n = 2 * length
    half = tile // 2
    rows = jnp.arange(n, dtype=jnp.int32)
    tile_i, r = rows // tile, rows % tile
    is_im = r >= half
    k = tile_i * half + jnp.where(is_im, r - half, r)
    t = jnp.arange(length, dtype=jnp.int32)
    ang = (2.0 * math.pi / n) * ((k[:, None] * t[None, :]) % n).astype(F32)
    nyq = jnp.where(t % 2 == 0, 1.0, -1.0).astype(F32)[None, :]
    is_nyq = (is_im & (k == 0))[:, None]
    fwd = jnp.where(is_im[:, None], -jnp.sin(ang), jnp.cos(ang))
    fwd = jnp.where(is_nyq, nyq, fwd)
    sgn = jnp.where(is_im[:, None] & ~is_nyq, -1.0, 1.0)
    fker = jnp.concatenate([fwd, fwd * sgn], axis=1)
    scale = jnp.where(k == 0, 1.0 / n, 2.0 / n)[None, :]
    inv = jnp.where(is_im[None, :], -jnp.sin(ang.T), jnp.cos(ang.T))
    inv = jnp.where(is_nyq.T, nyq.T, inv) * scale
    return fwd.astype(BF), fker.astype(BF), inv.astype(BF)


def _hy_long_convs(u_seq, kstack, hy_skip, hyw):
    b, ls, _ = u_seq.shape
    n = 2 * ls
    tmf = _pick(n, (512,))
    half = tmf // 2
    tn = _pick(hyw, (512, 256, 128))
    nj = hyw // tn
    fwd, fker, inv = _dft_tables(ls, tmf)

    (hspec,) = _matmul(
        fker, kstack,
        pl.BlockSpec((tmf, n), lambda g, j, i: (i, 0)),
        pl.BlockSpec((n, tn), lambda g, j, i: (0, j)),
        (1, 2 * nj, n // tmf),
        [(jax.ShapeDtypeStruct((n, 2 * hyw), F32), pl.BlockSpec((tmf, tn), lambda g, j, i: (i, j)))],
        _ep_store(F32), name="hyena_filter_dft")

    def spectrum_product(acc, extras, outs):
        hs = extras[0][...]
        re, im = acc[:half], acc[half:]
        hre, him = hs[:half], hs[half:]
        first = (lax.broadcasted_iota(jnp.int32, (half, 1), 0) == 0) & (pl.program_id(2) == 0)
        yre = jnp.where(first, re * hre, re * hre - im * him)
        yim = jnp.where(first, im * him, re * him + im * hre)
        outs[0][pl.ds(0, half), :] = yre.astype(BF)
        outs[0][pl.ds(half, half), :] = yim.astype(BF)

    tmi = _pick(ls, (TOKEN_TILE,))
    skip3 = hy_skip.reshape(2, 1, hyw)

    def conv(z_arr, z_off, conv_idx, mul_off):
        (spec,) = _matmul(
            fwd, z_arr,
            pl.BlockSpec((tmf, ls), lambda g, j, i: (i, 0)),
            pl.BlockSpec((None, ls, tn), lambda g, j, i: (g, 0, z_off + j)),
            (b, nj, n // tmf),
            [(jax.ShapeDtypeStruct((b, n, hyw), BF), pl.BlockSpec((None, tmf, tn), lambda g, j, i: (g, i, j)))],
            spectrum_product,
            extras=[(hspec, pl.BlockSpec((tmf, tn), lambda g, j, i: (i, conv_idx * nj + j)))],
            name="hyena_dft_fwd")

        def finish(acc, extras, outs):
            z = extras[0][...].astype(F32)
            xm = extras[2][...].astype(F32)
            outs[0][...] = (xm * (acc + z * extras[1][...])).astype(BF)

        (y,) = _matmul(
            inv, spec,
            pl.BlockSpec((tmi, n), lambda g, j, i: (i, 0)),
            pl.BlockSpec((None, n, tn), lambda g, j, i: (g, 0, j)),
            (b, nj, ls // tmi),
            [(jax.ShapeDtypeStruct((b, ls, hyw), BF), pl.BlockSpec((None, tmi, tn), lambda g, j, i: (g, i, j)))],
            finish,
            extras=[(z_arr, pl.BlockSpec((None, tmi, tn), lambda g, j, i: (g, i, z_off + j))),
                    (skip3, pl.BlockSpec((None, 1, tn), lambda g, j, i: (conv_idx, 0, j))),
                    (u_seq, pl.BlockSpec((None, tmi, tn), lambda g, j, i: (g, i, mul_off * nj + j)))],
            name="hyena_dft_inv")
        return y

    z2 = conv(u_seq, 0, 0, 1)
    return conv(z2, 0, 1, 2)


FFT_FAST = 64
FFT_K1_GROUP = 4


def _fft_tables(length):
    bf = FFT_FAST
    a_n = length // bf
    na = 2 * a_n
    n = 2 * length
    b = jnp.arange(bf, dtype=jnp.int32)
    k1 = jnp.arange(na, dtype=jnp.int32)
    a = jnp.arange(na, dtype=jnp.int32)
    nn = bf * a[None, None, :] + b[:, None, None]
    ang = (2.0 * math.pi / n) * ((k1[None, :, None] * nn) % n).astype(F32)
    cs, sn = jnp.cos(ang), jnp.sin(ang)
    m1k = jnp.concatenate([cs, -sn], axis=1)
    cd, sd = cs[:, :, :a_n], sn[:, :, :a_n]
    m1 = jnp.concatenate([jnp.concatenate([cd, sd], axis=2),
                          jnp.concatenate([-sd, cd], axis=2)], axis=1)
    ct, st = jnp.swapaxes(cd, 1, 2) / n, jnp.swapaxes(sd, 1, 2) / n
    g2 = jnp.concatenate([jnp.concatenate([ct, -st], axis=2),
                          jnp.concatenate([st, ct], axis=2)], axis=1)
    k2 = jnp.arange(bf, dtype=jnp.int32)
    ang2 = (2.0 * math.pi / bf) * ((k2[:, None] * b[None, :]) % bf).astype(F32)
    c2, s2 = jnp.cos(ang2), jnp.sin(ang2)
    f2 = jnp.concatenate([jnp.concatenate([c2, s2], axis=1), jnp.concatenate([-s2, c2], axis=1)], axis=0)
    i1 = jnp.concatenate([jnp.concatenate([c2.T, -s2.T], axis=1), jnp.concatenate([s2.T, c2.T], axis=1)], axis=0)
    return m1.astype(BF), m1k.astype(BF), f2.astype(BF), i1.astype(BF), g2.astype(BF)


def _fft_s1_kernel(z_ref, m_ref, o_ref):
    for bl in range(m_ref.shape[0]):
        z = jnp.concatenate([z_ref[0, :, bl, :], z_ref[1, :, bl, :]], axis=0).astype(BF)
        o_ref[bl] = jnp.dot(m_ref[bl], z, preferred_element_type=F32)


def _fft_mid_kernel(pr_ref, pi_ref, h_ref, f2_ref, i1_ref, o_ref):
    bf = f2_ref.shape[0] // 2
    for kl in range(h_ref.shape[0]):
        p = jnp.concatenate([pr_ref[:, kl, :], pi_ref[:, kl, :]], axis=0).astype(BF)
        x = jnp.dot(f2_ref[...], p, preferred_element_type=F32)
        hs = h_ref[kl].astype(F32)
        xr, xi = x[:bf], x[bf:]
        hr, hi = hs[:bf], hs[bf:]
        y = jnp.concatenate([xr * hr - xi * hi, xr * hi + xi * hr], axis=0).astype(BF)
        o_ref[kl] = jnp.dot(i1_ref[...], y, preferred_element_type=F32)


def _fft_s3_kernel(qr_ref, qi_ref, g_ref, z_ref, skip_ref, xm_ref, o_ref):
    a_n = z_ref.shape[1]
    for bl in range(g_ref.shape[0]):
        q = jnp.concatenate([qr_ref[:, bl, :], qi_ref[:, bl, :]], axis=0).astype(BF)
        y = jnp.dot(g_ref[bl], q, preferred_element_type=F32)
        for s in range(2):
            ys = y[s * a_n:(s + 1) * a_n]
            o_ref[s, :, bl, :] = xm_ref[s, :, bl, :] * (ys + z_ref[s, :, bl, :] * skip_ref[...])


def _fft_spec_kernel(p_ref, f2_ref, o_ref):
    for g in range(p_ref.shape[0]):
        o_ref[g] = jnp.dot(f2_ref[...], p_ref[g], preferred_element_type=F32).astype(o_ref.dtype)


def _fft_regroup(p, bf, na):
    lead = p.shape[:-3]
    c = p.shape[-1]
    nl = len(lead)
    p = p.reshape(*lead, bf, 2, na, c)
    p = jnp.transpose(p, tuple(range(nl)) + (nl + 2, nl + 1, nl + 0, nl + 3))
    return p.reshape(*lead, na, 2 * bf, c)


def _hy_long_convs_fft(u_seq, kern, hy_skip, hyw):
    nb_in, ls, _ = u_seq.shape
    if nb_in % 2:
        u_seq = jnp.concatenate([u_seq, jnp.zeros_like(u_seq[:1])], axis=0)
    nb = u_seq.shape[0] // 2
    bf = FFT_FAST
    a_n = ls // bf
    na = 2 * a_n
    ra = 2 * a_n
    tc = _pick(hyw, (2048, 1024, 512, 256, 128))
    nct = hyw // tc
    kg = _pick(na, (FFT_K1_GROUP, 2, 1))
    m1, m1k, f2, i1, g2 = _fft_tables(ls)
    k2 = kern.reshape(na, bf * 2 * hyw)

    (ph,) = _matmul(
        m1k, k2,
        pl.BlockSpec((None, 2 * na, na), lambda g, j, i: (j // (2 * nct), 0, 0)),
        pl.BlockSpec((na, tc), lambda g, j, i: (0, j)),
        (1, bf * 2 * nct, 1),
        [(jax.ShapeDtypeStruct((bf, 2 * na, 2 * hyw), BF),
          pl.BlockSpec((None, 2 * na, tc), lambda g, j, i: (j // (2 * nct), 0, j % (2 * nct))))],
        _ep_store(BF), name="hyena_fft_kern_s1")
    hspec = pl.pallas_call(
        _fft_spec_kernel,
        grid=(na // kg, 2 * nct),
        in_specs=[pl.BlockSpec((kg, 2 * bf, tc), lambda k, j: (k, 0, j)),
                  pl.BlockSpec((2 * bf, 2 * bf), lambda k, j: (0, 0))],
        out_specs=pl.BlockSpec((kg, 2 * bf, tc), lambda k, j: (k, 0, j)),
        out_shape=jax.ShapeDtypeStruct((na, 2 * bf, 2 * hyw), BF),
        compiler_params=_cparams(2),
        name="hyena_fft_kern_s2",
    )(_fft_regroup(ph, bf, na), f2)

    skip3 = hy_skip.reshape(2, 1, hyw)

    bg = 8
    td = _pick(hyw, (1024, 512, 256, 128))
    ndt = hyw // td
    u5 = u_seq.reshape(nb, 2, a_n, bf, 3 * hyw)

    def conv(z5, z_sec, conv_idx, mul_sec):
        zspec = pl.BlockSpec((None, 2, a_n, bg, td), lambda g, b, j: (g, 0, 0, b, z_sec * ndt + j))
        p = pl.pallas_call(
            _fft_s1_kernel,
            grid=(nb, bf // bg, ndt),
            in_specs=[zspec, pl.BlockSpec((bg, 2 * na, ra), lambda g, b, j: (b, 0, 0))],
            out_specs=pl.BlockSpec((None, bg, 2 * na, td), lambda g, b, j: (g, b, 0, j)),
            out_shape=jax.ShapeDtypeStruct((nb, bf, 2 * na, hyw), F32),
            compiler_params=_cparams(3),
            name="hyena_fft_s1",
        )(z5, m1)
        q = pl.pallas_call(
            _fft_mid_kernel,
            grid=(na // bg, ndt, nb),
            in_specs=[pl.BlockSpec((None, bf, bg, td), lambda k, j, g: (g, 0, k, j)),
                      pl.BlockSpec((None, bf, bg, td), lambda k, j, g: (g, 0, na // bg + k, j)),
                      pl.BlockSpec((bg, 2 * bf, td), lambda k, j, g: (k, 0, conv_idx * ndt + j)),
                      pl.BlockSpec((2 * bf, 2 * bf), lambda k, j, g: (0, 0)),
                      pl.BlockSpec((2 * bf, 2 * bf), lambda k, j, g: (0, 0))],
            out_specs=pl.BlockSpec((None, bg, 2 * bf, td), lambda k, j, g: (g, k, 0, j)),
            out_shape=jax.ShapeDtypeStruct((nb, na, 2 * bf, hyw), F32),
            compiler_params=_cparams(3),
            name="hyena_fft_mid",
        )(p, p, hspec, f2, i1)
        return pl.pallas_call(
            _fft_s3_kernel,
            grid=(nb, bf // bg, ndt),
            in_specs=[pl.BlockSpec((None, na, bg, td), lambda g, b, j: (g, 0, b, j)),
                      pl.BlockSpec((None, na, bg, td), lambda g, b, j: (g, 0, bf // bg + b, j)),
                      pl.BlockSpec((bg, ra, 2 * na), lambda g, b, j: (b, 0, 0)),
                      zspec,
                      pl.BlockSpec((None, 1, td), lambda g, b, j: (conv_idx, 0, j)),
                      pl.BlockSpec((None, 2, a_n, bg, td), lambda g, b, j: (g, 0, 0, b, mul_sec * ndt + j))],
            out_specs=pl.BlockSpec((None, 2, a_n, bg, td), lambda g, b, j: (g, 0, 0, b, j)),
            out_shape=jax.ShapeDtypeStruct((nb, 2, a_n, bf, hyw), F32),
            compiler_params=_cparams(3),
            name="hyena_fft_s3",
        )(q, q, g2, z5, skip3, u5)

    z_mid = conv(u5, 0, 0, 1)
    y = conv(z_mid, 0, 1, 2)
    return y.reshape(2 * nb, ls, hyw)[:nb_in]


def _moe_up_kernel(x_ref, wg_ref, wu_ref, comb_ref, o_ref, *, es):
    x = x_ref[...]
    g = jnp.dot(x, wg_ref[...], preferred_element_type=F32)
    u = jnp.dot(x, wu_ref[...], preferred_element_type=F32)
    comb = comb_ref[...]
    lane = lax.broadcasted_iota(jnp.int32, comb.shape, 1)
    f = g.shape[1] // es
    col = lax.broadcasted_iota(jnp.int32, (1, g.shape[1]), 1)
    scale = jnp.zeros_like(g)
    for s in range(es):
        e = pl.program_id(0) * es + s
        sc = jnp.sum(jnp.where(lane == e, comb, 0.0), axis=1, keepdims=True)
        scale = jnp.where((col >= s * f) & (col < (s + 1) * f), sc, scale)
    o_ref[...] = (g * jax.nn.sigmoid(g) * u * scale).astype(o_ref.dtype)


MOE_EXPERTS_PER_STEP = 4


def _moe_group_weights(w):
    nl, ne, d, f = w.shape
    es = MOE_EXPERTS_PER_STEP
    w = w.reshape(nl, ne // es, es, d, f).transpose(0, 1, 3, 2, 4)
    return w.reshape(nl, ne // es, d, es * f).astype(BF)


def _moe_up(u, w_gate, w_up, comb, l):
    m, d = u.shape
    _, ng, _, nf = w_gate.shape
    tm = _pick(m, (544, 512, 256))
    wspec = pl.BlockSpec((None, None, d, nf), lambda e, i: (l, e, 0, 0))
    return pl.pallas_call(
        functools.partial(_moe_up_kernel, es=MOE_EXPERTS_PER_STEP),
        grid=(ng, m // tm),
        in_specs=[pl.BlockSpec((tm, d), lambda e, i: (i, 0)), wspec, wspec,
                  pl.BlockSpec((tm, LANES), lambda e, i: (i, 0))],
        out_specs=pl.BlockSpec((tm, nf), lambda e, i: (i, e)),
        out_shape=jax.ShapeDtypeStruct((m, ng * nf), BF),
        compiler_params=_cparams(2),
        name="moe_up",
    )(u, w_gate, w_up, comb)


def _row_gate(gt_ref, tm, t_len, ctx_len, n_batch):
    r = lax.broadcasted_iota(jnp.int32, (tm, 1), 0) + pl.program_id(2) * tm
    gate = jnp.zeros((tm, gt_ref.shape[1]), F32)
    for bi in range(n_batch):
        in_b = (r >= bi * t_len) & (r < (bi + 1) * t_len)
        is_c = r < bi * t_len + ctx_len
        gate = jnp.where(in_b & is_c, gt_ref[0:1, :], gate)
        gate = jnp.where(in_b & ~is_c, gt_ref[1 + bi:2 + bi, :], gate)
    return gate


def _final_norm_kernel(h_ref, g_ref, o_ref):
    xf = h_ref[...]
    o_ref[...] = xf * lax.rsqrt(jnp.mean(xf * xf, axis=-1, keepdims=True) + NORM_EPS) * g_ref[...]


def _final_norm(h, gain, n_batch, t_len, ctx_len):
    m, d = h.shape
    tt = TOKEN_TILE
    n_t, n_c = t_len // tt, ctx_len // tt
    n_x = n_t - n_c
    return pl.pallas_call(
        _final_norm_kernel,
        grid=(n_batch, n_x),
        in_specs=[pl.BlockSpec((tt, d), lambda bi, i: (bi * n_t + n_c + i, 0)),
                  pl.BlockSpec((1, d), lambda bi, i: (0, 0))],
        out_specs=pl.BlockSpec((None, tt, d), lambda bi, i: (bi, i, 0)),
        out_shape=jax.ShapeDtypeStruct((n_batch, t_len - ctx_len, d), F32),
        compiler_params=_cparams(2),
        name="final_norm",
    )(h, gain.reshape(1, d))


def kernel(x, c, ctx, c_ctx, ada_down, ada_up, ada_bias, norm_mix, norm_ffn, w_in, qkv_conv, a_log, dt_bias, o_norm, hy_conv, hy_w1, hy_b1, hy_f1, hy_w2, hy_b2, hy_f2, hy_w3, hy_decay, hy_skip, w_br_a, w_br_b, w_out, router_group, router_group_bias, router_expert, router_expert_bias, exp_gate, exp_up, exp_down, final_norm):
    n_batch, seq, d = x.shape
    ctx_len = ctx.shape[1]
    t_len = ctx_len + seq
    m = n_batch * t_len
    depth = w_in.shape[0]
    n_heads = a_log.shape[2]
    a_width = n_heads * LANES
    hyw = hy_skip.shape[2]
    n_exp = exp_gate.shape[1]
    dg_off = 4 * a_width
    hy_off = dg_off + 4 * n_heads
    tail_w = 3 * hyw + 2 * d
    n_t, n_c = t_len // TOKEN_TILE, ctx_len // TOKEN_TILE
    nc = t_len // CHUNK
    ncp = -(-nc // 8) * 8

    h = jnp.concatenate([ctx, x], axis=1).reshape(m, d)

    conds = jnp.zeros((8, d), F32).at[0].set(c_ctx).at[1:1 + n_batch].set(c)
    mod_all = _adaln_all(conds, ada_down, ada_up, ada_bias)
    mod_all = mod_all.reshape(depth, 8, 6, d)[:, :1 + n_batch]
    mod_all = jnp.pad(mod_all, ((0, 0), (0, 0), (0, 2), (0, 0)))

    tm = _pick(m, (1088, 512, 256))
    tmp = tm
    tn = 512

    w_in_b = w_in.astype(BF)
    tail_base = (hy_off // LANES) * LANES
    exp_gate_b = _moe_group_weights(exp_gate)
    exp_up_b = _moe_group_weights(exp_up)

    def gate_table(mod_l, idx):
        return jnp.pad(mod_l[:, idx, :], ((0, 8 - (1 + n_batch)), (0, 0)))

    def residual_ep(acc, extras, outs):
        outs[0][...] = extras[0][...] + _row_gate(extras[1], acc.shape[0], t_len, ctx_len, n_batch) * acc

    for l in range(depth):
        mod_l = mod_all[l]
        u = _norm_mod(h, norm_mix[l], mod_l, 0, n_t, n_c)
        (p1,) = _matmul(
            u, w_in_b,
            pl.BlockSpec((tmp, d), lambda g, j, i: (i, 0)),
            pl.BlockSpec((None, d, tn), lambda g, j, i, l=l: (l, 0, j)),
            (1, dg_off // tn, m // tmp),
            [(jax.ShapeDtypeStruct((m, dg_off), BF), pl.BlockSpec((tmp, tn), lambda g, j, i: (i, j)))],
            _ep_store(BF), name="proj_qkvz")
        (p_tail,) = _matmul(
            u, w_in_b,
            pl.BlockSpec((tmp, d), lambda g, j, i: (i, 0)),
            pl.BlockSpec((None, d, tn), lambda g, j, i, l=l: (l, 0, tail_base // tn + j)),
            (1, tail_w // tn, m // tmp),
            [(jax.ShapeDtypeStruct((m, tail_w), BF), pl.BlockSpec((tmp, tn), lambda g, j, i: (i, j)))],
            _ep_store(BF), name="proj_tail",
            w_next_spec=pl.BlockSpec((None, d, LANES),
                                     lambda g, j, i, l=l: (l, 0, (tail_base + (j + 1) * tn) // LANES)),
            lane_shift=hy_off - tail_base)
        w_dg_t = w_in[l, :, dg_off:hy_off].T
        a_exp = jnp.concatenate([jnp.exp(a_log[l].reshape(-1)), jnp.zeros((2 * n_heads,), F32)])[:, None]
        dt_b = jnp.concatenate([dt_bias[l].reshape(-1), jnp.zeros((2 * n_heads,), F32)])[:, None]
        gates = _gates(u, w_dg_t, a_exp, dt_b)

        g5 = gates.reshape(4, n_heads, n_batch, nc, CHUNK).transpose(2, 1, 0, 3, 4)
        g_rows = jnp.pad(g5, ((0, 0), (0, 0), (0, 0), (0, ncp - nc), (0, 0)))
        g_cols = jnp.pad(jnp.swapaxes(g5, 3, 4), ((0, 0), (0, 0), (0, 0), (0, 0), (0, LANES - nc)))

        p1_3 = p1.reshape(n_batch, t_len, dg_off)
        conv9 = qkv_conv[l].reshape(9, 3 * a_width)
        du, dw, dq, dk, da, dgl = _delta_prep(p1_3, conv9, g_rows, g_cols, n_heads, ctx_len)
        o_f, o_b = _delta_scan(du, dw, dq, dk, da, dgl, ctx_len)
        o_a = _delta_out(o_f, o_b, p1_3, o_norm[l], n_heads).reshape(m, a_width)

        p_tail3 = p_tail.reshape(n_batch, t_len, tail_w)
        u_c, u_x = _hy_conv(p_tail3, hy_conv[l], ctx_len, 3 * hyw)
        filt_args = (hy_w1[l], hy_b1[l], hy_f1[l], hy_w2[l], hy_b2[l], hy_f2[l], hy_w3[l], hy_decay[l])
        y_x = _hy_long_convs_fft(u_x, _hy_filter_stack(seq, *filt_args, hyw, True), hy_skip[l], hyw)
        y_c = _hy_long_convs(u_c, _hy_filter_stack(ctx_len, *filt_args, hyw, False), hy_skip[l], hyw)
        y_b = jnp.concatenate([y_c, y_x.astype(BF)], axis=1).reshape(m, hyw)

        ga_blk = (3 * hyw) // tn
        gb_blk = (3 * hyw + d) // tn

        merged = pl.pallas_call(
            _merge_kernel,
            grid=(d // tn, m // tm),
            in_specs=[pl.BlockSpec((tm, a_width), lambda j, i: (i, 0)),
                      pl.BlockSpec((tm, hyw), lambda j, i: (i, 0)),
                      pl.BlockSpec((None, a_width, tn), lambda j, i, l=l: (l, 0, j)),
                      pl.BlockSpec((None, hyw, tn), lambda j, i, l=l: (l, 0, j)),
                      pl.BlockSpec((tm, tn), lambda j, i: (i, ga_blk + j)),
                      pl.BlockSpec((tm, tn), lambda j, i: (i, gb_blk + j))],
            out_specs=pl.BlockSpec((tm, tn), lambda j, i: (i, j)),
            out_shape=jax.ShapeDtypeStruct((m, d), BF),
            scratch_shapes=[pltpu.VMEM((a_width, tn), BF), pltpu.VMEM((hyw, tn), BF)],
            compiler_params=_cparams(2),
            name="branch_merge",
        )(o_a, y_b, w_br_a, w_br_b, p_tail, p_tail)

        (h,) = _matmul(
            merged, w_out,
            pl.BlockSpec((tm, d), lambda g, j, i: (i, 0)),
            pl.BlockSpec((None, d, tn), lambda g, j, i, l=l: (l, 0, j)),
            (1, d // tn, m // tm),
            [(jax.ShapeDtypeStruct((m, d), F32), pl.BlockSpec((tm, tn), lambda g, j, i: (i, j)))],
            residual_ep,
            extras=[(h, pl.BlockSpec((tm, tn), lambda g, j, i: (i, j))),
                    (gate_table(mod_l, 2), pl.BlockSpec((8, tn), lambda g, j, i: (0, j)))],
            name="out_proj")

        w_router = jnp.pad(jnp.concatenate([router_expert[l], router_group[l]], axis=1),
                           ((0, 0), (0, LANES - n_exp - N_GROUPS)))
        b_router = jnp.pad(jnp.concatenate([router_expert_bias[l], router_group_bias[l]]),
                           (0, LANES - n_exp - N_GROUPS)).reshape(1, LANES)
        u2, comb = _norm_mod_router(h, norm_ffn[l], mod_l, 3, n_t, n_c, w_router, b_router, n_exp, N_GROUPS)
        hid = _moe_up(u2, exp_gate_b, exp_up_b, comb, l)
        kf = hid.shape[1]
        tnd = _pick(d, (512, 256, 128))
        tmd = _pick(m, (544, 512, 256))
        (h,) = _matmul(
            hid, exp_down.reshape(depth, kf, d),
            pl.BlockSpec((tmd, kf), lambda g, j, i: (i, 0)),
            pl.BlockSpec((None, kf, tnd), lambda g, j, i, l=l: (l, 0, j)),
            (1, d // tnd, m // tmd),
            [(jax.ShapeDtypeStruct((m, d), F32), pl.BlockSpec((tmd, tnd), lambda g, j, i: (i, j)))],
            residual_ep,
            extras=[(h, pl.BlockSpec((tmd, tnd), lambda g, j, i: (i, j))),
                    (gate_table(mod_l, 5), pl.BlockSpec((8, tnd), lambda g, j, i: (0, j)))],
            name="moe_down")

    return _final_norm(h, final_norm, n_batch, t_len, ctx_len)
```

```python
import functools
import math

import jax
import jax.numpy as jnp
from jax import lax
from jax.experimental import pallas as pl
from jax.experimental.pallas import tpu as pltpu

F32 = jnp.float32
BF = jnp.bfloat16

GRID_W = 64
CHUNK = 64
N_GROUPS = 4
EXPERTS_PER_GROUP = 8
HY_N_FILT = 4
NORM_EPS = 1e-6
TOKEN_TILE = 256
LANES = 128
CONV_PAD = 72
VMEM_LIMIT = 56 * 1024 * 1024


def _cparams(n_grid):
    return pltpu.CompilerParams(dimension_semantics=("arbitrary",) * n_grid,
                                vmem_limit_bytes=VMEM_LIMIT)


def _split_bf16(a):
    hi = a.astype(BF)
    lo = (a - hi.astype(F32)).astype(BF)
    return hi, lo


def _dot3(a, b):
    ah, al = _split_bf16(a)
    bh, bl = _split_bf16(b)
    d = functools.partial(jnp.dot, preferred_element_type=F32)
    return d(ah, bh) + (d(ah, bl) + d(al, bh))


def _dotx(a, b):
    return jnp.dot(a, b, precision=lax.Precision.HIGHEST, preferred_element_type=F32)


def _bdot(a, b):
    return jnp.dot(a.astype(BF), b.astype(BF), preferred_element_type=F32)


def _mm_kernel(*refs, n_extra, n_out, stage_w, lane_shift, prologue, epilogue):
    x_ref, w_ref = refs[0], refs[1]
    n_w = 3 if lane_shift else 2
    extras = refs[n_w:n_w + n_extra]
    outs = refs[n_w + n_extra:n_w + n_extra + n_out]
    scratch = refs[n_w + n_extra + n_out:]
    if stage_w:
        wbf = scratch[0]

        @pl.when(pl.program_id(2) == 0)
        def _():
            k = w_ref.shape[0]
            step = 512 if k % 512 == 0 else k

            def load(r0):
                wa = w_ref[pl.ds(r0, step), :]
                if lane_shift:
                    wa = jnp.concatenate([wa[:, lane_shift:], refs[2][pl.ds(r0, step), :lane_shift]], axis=1)
                return wa.astype(BF)

            if k == step:
                wbf[...] = load(0)
            else:
                def it(r, c):
                    r0 = pl.multiple_of(r * step, step)
                    wbf[pl.ds(r0, step), :] = load(r0)
                    return c
                lax.fori_loop(0, k // step, it, 0)
        w = wbf[...]
    else:
        w = w_ref[...]
    x = x_ref[...]
    if prologue is not None:
        x = prologue(x)
    if x.dtype != BF:
        x = x.astype(BF)
    acc = jnp.dot(x, w, preferred_element_type=F32)
    epilogue(acc, extras, outs)


def _matmul(x, w, x_spec, w_spec, grid, outs, epilogue, extras=(), prologue=None, name=None,
            w_next_spec=None, lane_shift=0):
    stage_w = w.dtype != BF or lane_shift > 0
    kw, tn = [d for d in w_spec.block_shape if d is not None]
    scratch = [pltpu.VMEM((kw, tn), BF)] if stage_w else []
    body = functools.partial(_mm_kernel, n_extra=len(extras), n_out=len(outs), stage_w=stage_w,
                             lane_shift=lane_shift, prologue=prologue, epilogue=epilogue)
    w_ops, w_specs = ([w, w], [w_spec, w_next_spec]) if lane_shift else ([w], [w_spec])
    res = pl.pallas_call(
        body,
        grid=grid,
        in_specs=[x_spec] + w_specs + [s for _, s in extras],
        out_specs=[s for _, s in outs],
        out_shape=[o for o, _ in outs],
        scratch_shapes=scratch,
        compiler_params=_cparams(3),
        name=name,
    )(x, *w_ops, *[a for a, _ in extras])
    return res


def _merge_kernel(xa_ref, xb_ref, wa_ref, wb_ref, ga_ref, gb_ref, o_ref, wa_bf, wb_bf):
    @pl.when(pl.program_id(1) == 0)
    def _():
        wa_bf[...] = wa_ref[...].astype(BF)
        wb_bf[...] = wb_ref[...].astype(BF)
    a = jnp.dot(xa_ref[...], wa_bf[...], preferred_element_type=F32)
    b = jnp.dot(xb_ref[...], wb_bf[...], preferred_element_type=F32)
    o_ref[...] = (jax.nn.sigmoid(ga_ref[...].astype(F32)) * a
                  + jax.nn.sigmoid(gb_ref[...].astype(F32)) * b).astype(o_ref.dtype)


def _ep_store(dtype):
    def ep(acc, extras, outs):
        outs[0][...] = acc.astype(dtype)
    return ep


def _pick(n, prefs):
    for p in prefs:
        if n % p == 0:
            return p
    return n


def _adaln_all(conds, ada_down, ada_up, ada_bias):
    nl, d, r = ada_down.shape
    n6 = ada_up.shape[2]
    (t,) = _matmul(
        conds, ada_down,
        pl.BlockSpec((8, d), lambda g, j, i: (0, 0)),
        pl.BlockSpec((None, d, r), lambda g, j, i: (g, 0, 0)),
        (nl, 1, 1),
        [(jax.ShapeDtypeStruct((nl, 8, r), F32), pl.BlockSpec((None, 8, r), lambda g, j, i: (g, 0, 0)))],
        _ep_store(F32), prologue=lambda v: v * jax.nn.sigmoid(v), name="adaln_down")
    tn = _pick(n6, (2048, 1024, 512, 256, 128))

    def ep(acc, extras, outs):
        outs[0][...] = acc + extras[0][...]

    (mod,) = _matmul(
        t, ada_up,
        pl.BlockSpec((None, 8, r), lambda g, j, i: (g, 0, 0)),
        pl.BlockSpec((None, r, tn), lambda g, j, i: (g, 0, j)),
        (nl, n6 // tn, 1),
        [(jax.ShapeDtypeStruct((nl, 8, n6), F32), pl.BlockSpec((None, 8, tn), lambda g, j, i: (g, 0, j)))],
        ep, extras=[(ada_bias.reshape(nl, 1, n6), pl.BlockSpec((None, 1, tn), lambda g, j, i: (g, 0, j)))],
        name="adaln_up")
    return mod


def _mod_row_map(n_t, n_c):
    def m(i):
        return (jnp.where(i % n_t < n_c, 0, 1 + i // n_t), 0, 0)
    return m


def _norm_mod_kernel(h_ref, g_ref, mod_ref, u_ref, *, off):
    xf = h_ref[...]
    y = xf * lax.rsqrt(jnp.mean(xf * xf, axis=-1, keepdims=True) + NORM_EPS) * g_ref[...]
    sh = mod_ref[off:off + 1, :]
    sc = mod_ref[off + 1:off + 2, :]
    u_ref[...] = (y * (1.0 + sc) + sh).astype(u_ref.dtype)


def _norm_mod(h, gain, mod_l, off, n_t, n_c):
    m, d = h.shape
    tt = TOKEN_TILE
    return pl.pallas_call(
        functools.partial(_norm_mod_kernel, off=off),
        grid=(m // tt,),
        in_specs=[pl.BlockSpec((tt, d), lambda i: (i, 0)),
                  pl.BlockSpec((1, d), lambda i: (0, 0)),
                  pl.BlockSpec((None, 8, d), _mod_row_map(n_t, n_c))],
        out_specs=pl.BlockSpec((tt, d), lambda i: (i, 0)),
        out_shape=jax.ShapeDtypeStruct((m, d), BF),
        compiler_params=_cparams(1),
        name="norm_mod",
    )(h, gain.reshape(1, d), mod_l)


def _router_kernel(h_ref, g_ref, mod_ref, wr_ref, br_ref, u_ref, comb_ref, *, off, n_exp, n_grp):
    xf = h_ref[...]
    y = xf * lax.rsqrt(jnp.mean(xf * xf, axis=-1, keepdims=True) + NORM_EPS) * g_ref[...]
    sh = mod_ref[off:off + 1, :]
    sc = mod_ref[off + 1:off + 2, :]
    u = y * (1.0 + sc) + sh
    u_ref[...] = u.astype(u_ref.dtype)
    logits = _dot3(u, wr_ref[...]) + br_ref[...]
    lane = lax.broadcasted_iota(jnp.int32, logits.shape, 1)
    neg = jnp.float32(-jnp.inf)
    big = jnp.int32(1 << 20)
    is_g = (lane >= n_exp) & (lane < n_exp + n_grp)
    lg = jnp.where(is_g, logits, neg)
    eg = jnp.exp(lg - jnp.max(lg, axis=1, keepdims=True))
    pg = eg / jnp.sum(eg, axis=1, keepdims=True)
    p_top = jnp.max(pg, axis=1, keepdims=True)
    g_idx = jnp.min(jnp.where(is_g & (pg == p_top), lane, big), axis=1, keepdims=True) - n_exp
    per = n_exp // n_grp
    in_grp = (lane >= g_idx * per) & (lane < (g_idx + 1) * per)
    le = jnp.where(in_grp, logits, neg)
    v1 = jnp.max(le, axis=1, keepdims=True)
    i1 = jnp.min(jnp.where(le == v1, lane, big), axis=1, keepdims=True)
    le2 = jnp.where(lane == i1, neg, le)
    v2 = jnp.max(le2, axis=1, keepdims=True)
    i2 = jnp.min(jnp.where(le2 == v2, lane, big), axis=1, keepdims=True)
    e2 = jnp.exp(v2 - v1)
    w1 = p_top / (1.0 + e2)
    w2 = p_top * e2 / (1.0 + e2)
    comb_ref[...] = jnp.where(lane == i1, w1, 0.0) + jnp.where(lane == i2, w2, 0.0)


def _norm_mod_router(h, gain, mod_l, off, n_t, n_c, w_router, b_router, n_exp, n_grp):
    m, d = h.shape
    tt = TOKEN_TILE
    return pl.pallas_call(
        functools.partial(_router_kernel, off=off, n_exp=n_exp, n_grp=n_grp),
        grid=(m // tt,),
        in_specs=[pl.BlockSpec((tt, d), lambda i: (i, 0)),
                  pl.BlockSpec((1, d), lambda i: (0, 0)),
                  pl.BlockSpec((None, 8, d), _mod_row_map(n_t, n_c)),
                  pl.BlockSpec((d, LANES), lambda i: (0, 0)),
                  pl.BlockSpec((1, LANES), lambda i: (0, 0))],
        out_specs=[pl.BlockSpec((tt, d), lambda i: (i, 0)),
                   pl.BlockSpec((tt, LANES), lambda i: (i, 0))],
        out_shape=[jax.ShapeDtypeStruct((m, d), BF), jax.ShapeDtypeStruct((m, LANES), F32)],
        compiler_params=_cparams(1),
        name="norm_mod_router",
    )(h, gain.reshape(1, d), mod_l, w_router, b_router)


def _gates_kernel(w_ref, u_ref, aexp_ref, dtb_ref, o_ref, *, n_decay):
    acc = lax.dot_general(w_ref[...].astype(BF), u_ref[...], (((1,), (1,)), ((), ())),
                          preferred_element_type=F32)
    row = lax.broadcasted_iota(jnp.int32, acc.shape, 0)
    z = acc + dtb_ref[...]
    softplus = jnp.maximum(z, 0.0) + jnp.log(1.0 + jnp.exp(-jnp.abs(z)))
    g = -aexp_ref[...] * softplus
    o_ref[...] = jnp.where(row < n_decay, g, jax.nn.sigmoid(acc))


def _gates(u, w_dg_t, a_exp, dt_b):
    m, d = u.shape
    r = w_dg_t.shape[0]
    tm = _pick(m, (512, 256))
    return pl.pallas_call(
        functools.partial(_gates_kernel, n_decay=r // 2),
        grid=(m // tm,),
        in_specs=[pl.BlockSpec((r, d), lambda i: (0, 0)),
                  pl.BlockSpec((tm, d), lambda i: (i, 0)),
                  pl.BlockSpec((r, 1), lambda i: (0, 0)),
                  pl.BlockSpec((r, 1), lambda i: (0, 0))],
        out_specs=pl.BlockSpec((r, tm), lambda i: (0, i)),
        out_shape=jax.ShapeDtypeStruct((r, m), F32),
        compiler_params=_cparams(1),
        name="delta_gates",
    )(w_dg_t, u, a_exp, dt_b)


def _conv_silu_tile(pad_ref, cw_ref, t0, n, seq_start, seq_len, width):
    one_row = width == seq_len
    r = lax.broadcasted_iota(jnp.int32, (n, 1), 0) + (t0 - seq_start)
    col = lax.rem(r, width)
    acc = jnp.zeros((n, LANES), F32)
    for dj in (-1, 0, 1):
        part = jnp.zeros((n, LANES), F32)
        for di in (-1, 0, 1):
            if one_row and di != 0:
                continue
            off = di * width + dj
            tap = pad_ref[pl.ds(CONV_PAD + t0 + off, n), :]
            wrow = cw_ref[(di + 1) * 3 + (dj + 1):(di + 1) * 3 + (dj + 1) + 1, :]
            lo, hi = t0 - seq_start + off, t0 - seq_start + n - 1 + off
            if lo < 0 or hi >= seq_len:
                tap = jnp.where((r + off >= 0) & (r + off < seq_len), tap, 0.0)
            part = part + tap * wrow
        ok_c = (col + dj >= 0) & (col + dj < width)
        acc = acc + jnp.where(ok_c, part, 0.0)
    return acc * jax.nn.sigmoid(acc)


def _bmm(a, b):
    return jnp.einsum('gik,gkj->gij', a.astype(BF), b.astype(BF), preferred_element_type=F32)


def _bmm_nt(a, b):
    return jnp.einsum('gik,gjk->gij', a.astype(BF), b.astype(BF), preferred_element_type=F32)


TRI_BASE = 4


def _tri_inverse(lmat, eye, ii, jj):
    n = lmat.shape[-1]
    same = lambda s: (ii >> (s.bit_length() - 1)) == (jj >> (s.bit_length() - 1))
    x = -jnp.where(same(TRI_BASE), lmat, 0.0)
    t = eye + x
    t = t + _bmm(t, _bmm(x, x))
    s = TRI_BASE
    while s < n:
        e = jnp.where(same(2 * s) & ~same(s), lmat, 0.0)
        t = t - _bmm(t, _bmm(e, t))
        s *= 2
    return t


def _delta_prep_kernel(pq_ref, pk_ref, pv_ref, cq_ref, ck_ref, cv_ref, g_ref, gt_ref,
                       u_ref, w_ref, qd_ref, kd_ref, a_ref, gl_ref,
                       pad_ref, q_s, k_s, v_s, gcr_s, gct_s,
                       *, t_len, ctx_len, n_chunks, group):
    hd = q_s.shape[1]
    tt = TOKEN_TILE
    d = pl.program_id(2)

    @pl.when(d == 0)
    def _():
        zpad = jnp.zeros((CONV_PAD, LANES), F32)
        pad_ref[pl.ds(0, CONV_PAD), :] = zpad
        pad_ref[pl.ds(CONV_PAD + t_len, CONV_PAD), :] = zpad
        for src, cw, dst, kind in ((pq_ref, cq_ref, q_s, "q"), (pk_ref, ck_ref, k_s, "k"), (pv_ref, cv_ref, v_s, "v")):
            pad_ref[pl.ds(CONV_PAD, t_len), :] = src[...].astype(F32)
            for ti in range(t_len // tt):
                t0 = ti * tt
                if t0 < ctx_len:
                    a = _conv_silu_tile(pad_ref, cw, t0, tt, 0, ctx_len, ctx_len)
                else:
                    a = _conv_silu_tile(pad_ref, cw, t0, tt, ctx_len, t_len - ctx_len, GRID_W)
                if kind != "v":
                    a = a * lax.rsqrt(jnp.sum(a * a, axis=-1, keepdims=True) + NORM_EPS)
                if kind == "q":
                    a = a * (hd ** -0.5)
                dst[pl.ds(t0, tt), :] = a

    ii = lax.broadcasted_iota(jnp.int32, (CHUNK, CHUNK), 0)
    jj = lax.broadcasted_iota(jnp.int32, (CHUNK, CHUNK), 1)
    sdiff = (ii - jj) * (1 - 2 * d)
    eye = (ii == jj).astype(F32)
    incl = sdiff >= 0
    strict = sdiff > 0
    gcr_s[...] = _dotx(g_ref[d], (sdiff <= 0).astype(F32))
    gct_s[...] = _dotx(incl.astype(F32), gt_ref[d])
    gtot = jnp.where(d == 0, gcr_s[:, CHUNK - 1:CHUNK], gcr_s[:, 0:1])
    gl_ref[...] = jnp.broadcast_to(jnp.exp(gtot), gl_ref.shape)

    gc = group
    rows = gc * CHUNK
    lane_c = lax.broadcasted_iota(jnp.int32, (CHUNK, LANES), 1)
    eye_hd = (lax.broadcasted_iota(jnp.int32, (hd, hd), 0)
              == lax.broadcasted_iota(jnp.int32, (hd, hd), 1)).astype(BF)[None]

    def body(it, carry):
        c0 = it * gc
        t0 = pl.multiple_of(c0 * CHUNK, rows)
        q = q_s[pl.ds(t0, rows), :].reshape(gc, CHUNK, hd)
        k = k_s[pl.ds(t0, rows), :].reshape(gc, CHUNK, hd)
        v = v_s[pl.ds(t0, rows), :].reshape(gc, CHUNK, hd)
        gct = gct_s[...]
        bt = gt_ref[2 + d]
        gcols, bcols, grows, glasts = [], [], [], []
        for g in range(gc):
            sel = lane_c == c0 + g
            gcols.append(jnp.sum(jnp.where(sel, gct, 0.0), axis=1, keepdims=True))
            bcols.append(jnp.sum(jnp.where(sel, bt, 0.0), axis=1, keepdims=True))
            grow_g = gcr_s[pl.ds(c0 + g, 1), :]
            grows.append(grow_g)
            glasts.append(jnp.where(d == 0, grow_g[:, CHUNK - 1:CHUNK], grow_g[:, 0:1]))
        gcol = jnp.broadcast_to(jnp.stack(gcols), (gc, CHUNK, hd))
        bcol = jnp.broadcast_to(jnp.stack(bcols), (gc, CHUNK, hd))
        grow = jnp.stack(grows)
        glast = jnp.stack(glasts)
        decay = jnp.where(incl, jnp.exp(jnp.where(incl, gcol[:, :, :CHUNK] - grow, 0.0)), 0.0)
        kb = k * bcol
        lmat = jnp.where(strict, _bmm_nt(kb, k) * decay, 0.0)
        tinv = _tri_inverse(lmat, eye[None], ii, jj)
        egc = jnp.exp(gcol)
        uw = _bmm(tinv, jnp.concatenate([v * bcol, kb * egc], axis=2))
        attn = (_bmm_nt(q, k) * decay).astype(BF)
        kdec = k * jnp.exp(glast - gcol)
        kdec_t = _bmm_nt(jnp.broadcast_to(eye_hd, (gc, hd, hd)), kdec).astype(BF)
        u_ref[pl.ds(t0, rows), :] = uw[:, :, :hd].reshape(rows, hd)
        w_ref[pl.ds(t0, rows), :] = uw[:, :, hd:].astype(BF).reshape(rows, hd)
        qd_ref[pl.ds(t0, rows), :] = (q * egc).astype(BF).reshape(rows, hd)
        a_ref[pl.ds(t0, rows), :] = attn.reshape(rows, CHUNK)
        kd_ref[pl.ds(c0, gc)] = kdec_t
        return carry

    lax.fori_loop(0, n_chunks // group, body, 0)


def _delta_prep(p1, conv_w9, g_rows, g_cols, n_heads, ctx_len):
    b, t, _ = p1.shape
    hd = LANES
    nc = t // CHUNK
    ncp = g_rows.shape[3]
    group = max(g for g in range(1, 35) if nc % g == 0)
    h = n_heads
    kern = functools.partial(_delta_prep_kernel, t_len=t, ctx_len=ctx_len, n_chunks=nc, group=group)
    tok = lambda off: pl.BlockSpec((None, t, hd), lambda bi, hi, di: (bi, 0, off + hi))
    cws = lambda off: pl.BlockSpec((9, hd), lambda bi, hi, di: (0, off + hi))
    o5 = lambda last: pl.BlockSpec((None, None, None, t, last), lambda bi, hi, di: (di, bi, hi, 0, 0))
    return pl.pallas_call(
        kern,
        grid=(b, h, 2),
        in_specs=[tok(0), tok(h), tok(2 * h), cws(0), cws(h), cws(2 * h),
                  pl.BlockSpec((None, None, 4, ncp, CHUNK), lambda bi, hi, di: (bi, hi, 0, 0, 0)),
                  pl.BlockSpec((None, None, 4, CHUNK, LANES), lambda bi, hi, di: (bi, hi, 0, 0, 0))],
        out_specs=[o5(hd), o5(hd), o5(hd),
                   pl.BlockSpec((None, None, None, nc, hd, CHUNK), lambda bi, hi, di: (di, bi, hi, 0, 0, 0)),
                   o5(CHUNK),
                   pl.BlockSpec((None, None, None, ncp, LANES), lambda bi, hi, di: (di, bi, hi, 0, 0))],
        out_shape=[jax.ShapeDtypeStruct((2, b, h, t, hd), F32),
                   jax.ShapeDtypeStruct((2, b, h, t, hd), BF),
                   jax.ShapeDtypeStruct((2, b, h, t, hd), BF),
                   jax.ShapeDtypeStruct((2, b, h, nc, hd, CHUNK), BF),
                   jax.ShapeDtypeStruct((2, b, h, t, CHUNK), BF),
                   jax.ShapeDtypeStruct((2, b, h, ncp, LANES), F32)],
        scratch_shapes=[pltpu.VMEM((t + 2 * CONV_PAD, LANES), F32),
                        pltpu.VMEM((t, hd), F32), pltpu.VMEM((t, hd), F32), pltpu.VMEM((t, hd), F32),
                        pltpu.VMEM((ncp, CHUNK), F32), pltpu.VMEM((CHUNK, LANES), F32)],
        compiler_params=_cparams(3),
        name="delta_prep",
    )(p1, p1, p1, conv_w9, conv_w9, conv_w9, g_rows, g_cols)


def _delta_scan_kernel(uf, wf, qf, kf, af, gf, ub, wb, qb, kb, ab, gb, of_ref, ob_ref, s_ref, *, hg):
    @pl.when(pl.program_id(2) == 0)
    def _():
        s_ref[...] = jnp.zeros_like(s_ref)

    cs = uf.shape[1]
    s = s_ref[...]
    for t in range(cs):
        tb = cs - 1 - t
        cat = lambda a, b: jnp.concatenate([a[:, t], b[:, tb]], axis=0)
        sb = s.astype(BF)
        vn = cat(uf, ub) - _bmm(cat(wf, wb), sb)
        vnb = vn.astype(BF)
        o = _bmm(cat(qf, qb), sb) + _bmm(cat(af, ab), vnb)
        s = s * cat(gf, gb) + _bmm(cat(kf, kb), vnb)
        of_ref[:, t] = o[:hg]
        ob_ref[:, tb] = o[hg:]
    s_ref[...] = s


def _delta_scan(u, w, qd, kd, a, gl, ctx_len):
    _, b, h, t, hd = u.shape
    nc = t // CHUNK
    ncc = ctx_len // CHUNK
    hg = _pick(h, (8, 4, 2, 1))
    cs = max(c for c in (4, 2, 1) if ncc % c == 0 and nc % c == 0)
    nblk, nblk_c = nc // cs, ncc // cs
    r6 = lambda arr: arr.reshape(2, b, h, nc, CHUNK, arr.shape[-1])
    u6, w6, q6, k6, a6 = r6(u), r6(w), r6(qd), kd, r6(a)
    gl6 = gl[:, :, :, :nc].reshape(2, b, h, nc, 1, LANES)

    def cf(si):
        return si

    def cb(si):
        return jnp.where(si < nblk_c, nblk_c - 1 - si, nblk - 1 - (si - nblk_c))

    def spec(d, cmap, rows, last):
        return pl.BlockSpec((None, None, hg, cs, rows, last),
                            lambda bi, gi, si: (d, bi, gi, cmap(si), 0, 0))

    ins, specs = [], []
    for d, cmap in ((0, cf), (1, cb)):
        ins += [u6, w6, q6, k6, a6, gl6]
        specs += [spec(d, cmap, CHUNK, hd), spec(d, cmap, CHUNK, hd), spec(d, cmap, CHUNK, hd),
                  spec(d, cmap, hd, CHUNK), spec(d, cmap, CHUNK, CHUNK), spec(d, cmap, 1, LANES)]
    o_shape = jax.ShapeDtypeStruct((b, h, nc, CHUNK, hd), F32)
    ospec = lambda cmap: pl.BlockSpec((None, hg, cs, CHUNK, hd), lambda bi, gi, si: (bi, gi, cmap(si), 0, 0))
    of, ob = pl.pallas_call(
        functools.partial(_delta_scan_kernel, hg=hg),
        grid=(b, h // hg, nblk),
        in_specs=specs,
        out_specs=[ospec(cf), ospec(cb)],
        out_shape=[o_shape, o_shape],
        scratch_shapes=[pltpu.VMEM((2 * hg, hd, hd), F32)],
        compiler_params=_cparams(3),
        name="delta_scan",
    )(*ins)
    return of.reshape(b, h, t, hd), ob.reshape(b, h, t, hd)


def _delta_out_kernel(of_ref, ob_ref, z_ref, g_ref, o_ref):
    hd = of_ref.shape[2]
    for h in range(of_ref.shape[0]):
        o = of_ref[h] + ob_ref[h]
        o = o * lax.rsqrt(jnp.mean(o * o, axis=-1, keepdims=True) + NORM_EPS)
        z = z_ref[:, h * hd:(h + 1) * hd].astype(F32)
        o_ref[:, h * hd:(h + 1) * hd] = (o * g_ref[...] * (z * jax.nn.sigmoid(z))).astype(o_ref.dtype)


def _delta_out(of, ob, p1, o_gain, n_heads):
    b, h, t, hd = of.shape
    tt = TOKEN_TILE
    return pl.pallas_call(
        _delta_out_kernel,
        grid=(b, t // tt),
        in_specs=[pl.BlockSpec((None, h, tt, hd), lambda bi, ti: (bi, 0, ti, 0)),
                  pl.BlockSpec((None, h, tt, hd), lambda bi, ti: (bi, 0, ti, 0)),
                  pl.BlockSpec((None, tt, h * hd), lambda bi, ti: (bi, ti, 3)),
                  pl.BlockSpec((1, hd), lambda bi, ti: (0, 0))],
        out_specs=pl.BlockSpec((None, tt, h * hd), lambda bi, ti: (bi, ti, 0)),
        out_shape=jax.ShapeDtypeStruct((b, t, h * hd), BF),
        compiler_params=_cparams(2),
        name="delta_out",
    )(of, ob, p1, o_gain.reshape(1, hd))


def _hy_conv_kernel(p_ref, w_ref, oc_ref, ox_ref, pad_ref, *, t_len, ctx_len):
    tt = TOKEN_TILE
    z8 = jnp.zeros((8, pad_ref.shape[1]), F32)
    pad_ref[pl.ds(0, 8), :] = z8
    pad_ref[pl.ds(8 + t_len, 8), :] = z8
    pad_ref[pl.ds(8, t_len), :] = p_ref[...].astype(F32)
    w0, w1, w2 = w_ref[0:1, :], w_ref[1:2, :], w_ref[2:3, :]
    for ti in range(t_len // tt):
        t0 = ti * tt
        r = lax.broadcasted_iota(jnp.int32, (tt, 1), 0) + t0
        left = pad_ref[pl.ds(8 + t0 - 1, tt), :]
        mid = pad_ref[pl.ds(8 + t0, tt), :]
        right = pad_ref[pl.ds(8 + t0 + 1, tt), :]
        ok_l = (r != 0) & (r != ctx_len)
        ok_r = (r != ctx_len - 1) & (r != t_len - 1)
        y = jnp.where(ok_l, left, 0.0) * w0 + mid * w1 + jnp.where(ok_r, right, 0.0) * w2
        if t0 < ctx_len:
            oc_ref[pl.ds(t0, tt), :] = y.astype(oc_ref.dtype)
        else:
            ox_ref[pl.ds(t0 - ctx_len, tt), :] = y.astype(ox_ref.dtype)


def _hy_conv(p_tail, hy_conv_w, ctx_len, width3):
    b, t, _ = p_tail.shape
    tc = _pick(width3, (512, 256, 128))
    return pl.pallas_call(
        functools.partial(_hy_conv_kernel, t_len=t, ctx_len=ctx_len),
        grid=(b, width3 // tc),
        in_specs=[pl.BlockSpec((None, t, tc), lambda bi, j: (bi, 0, j)),
                  pl.BlockSpec((3, tc), lambda bi, j: (0, j))],
        out_specs=[pl.BlockSpec((None, ctx_len, tc), lambda bi, j: (bi, 0, j)),
                   pl.BlockSpec((None, t - ctx_len, tc), lambda bi, j: (bi, 0, j))],
        out_shape=[jax.ShapeDtypeStruct((b, ctx_len, width3), BF),
                   jax.ShapeDtypeStruct((b, t - ctx_len, width3), F32)],
        scratch_shapes=[pltpu.VMEM((t + 16, tc), F32)],
        compiler_params=_cparams(2),
        name="hyena_short_conv",
    )(p_tail, hy_conv_w)


def _hy_filter_kernel(ft_ref, w1_ref, b1_ref, f1_ref, w2_ref, b2_ref, f2_ref, w3_ref, dec_ref, o_ref, hid_s,
                      *, mask_lag0):
    part = pl.program_id(0)
    feats = ft_ref[...]

    @pl.when((pl.program_id(2) == 0) & (pl.program_id(3) == 0))
    def _():
        hid = jnp.sin(f1_ref[...] * (_dot3(feats, w1_ref[...]) + b1_ref[...]))
        hid_s[...] = jnp.sin(f2_ref[...] * (_dot3(hid, w2_ref[...]) + b2_ref[...]))

    filt = _bdot(hid_s[...], w3_ref[...])
    tpos = feats[:, 0:1]
    val = filt * jnp.exp(-tpos * jnp.abs(dec_ref[...]))
    if mask_lag0:
        row = lax.broadcasted_iota(jnp.int32, (val.shape[0], 1), 0) + pl.program_id(1) * val.shape[0]
        val = jnp.where((part == 1) & (row == 0), 0.0, val)
    o_ref[...] = val.astype(o_ref.dtype)


def _features(length, n_emb, k_pad, reverse_tail):
    t = jnp.linspace(0.0, 1.0, length, dtype=F32)
    n_bands = (n_emb - 1) // 2
    omega = 2.0 * math.pi * jnp.arange(length, dtype=F32) / length
    bands = jnp.linspace(1e-4, n_bands - 1, n_bands, dtype=F32)
    ang = omega[:, None] * bands[None, :]
    feats = jnp.concatenate([t[:, None], jnp.cos(ang), -jnp.sin(ang)], axis=-1)
    feats = jnp.pad(feats, ((0, 0), (0, k_pad - n_emb)))
    if reverse_tail:
        tail = feats[::-1]
    else:
        tail = jnp.concatenate([jnp.zeros((1, k_pad), F32), feats[:-1]], axis=0)
    return jnp.stack([feats, tail])


def _hy_filter_stack(length, w1, b1, f1, w2, b2, f2, w3, decay, hyw, reverse_tail):
    n_emb, fh = w1.shape
    k_pad = 64 if n_emb <= 64 else _pick(n_emb, (128,))
    feats = _features(length, n_emb, k_pad, reverse_tail)
    w1p = jnp.pad(w1, ((0, k_pad - n_emb), (0, 0)))
    tr = _pick(length, (512, 256, 128))
    tc = _pick(hyw, (2048, 1024, 512, 256, 128))
    nj = hyw // tc
    ni = length // tr
    return pl.pallas_call(
        functools.partial(_hy_filter_kernel, mask_lag0=not reverse_tail),
        grid=(2, ni, 2, nj),
        in_specs=[pl.BlockSpec((None, tr, k_pad), lambda p, i, c, j: (p, i, 0)),
                  pl.BlockSpec((k_pad, fh), lambda p, i, c, j: (0, 0)),
                  pl.BlockSpec((1, fh), lambda p, i, c, j: (0, 0)),
                  pl.BlockSpec((1, fh), lambda p, i, c, j: (0, 0)),
                  pl.BlockSpec((fh, fh), lambda p, i, c, j: (0, 0)),
                  pl.BlockSpec((1, fh), lambda p, i, c, j: (0, 0)),
                  pl.BlockSpec((1, fh), lambda p, i, c, j: (0, 0)),
                  pl.BlockSpec((fh, tc), lambda p, i, c, j: (0, (2 * c + p) * nj + j)),
                  pl.BlockSpec((1, tc), lambda p, i, c, j: (0, (2 * c + p) * nj + j))],
        out_specs=pl.BlockSpec((tr, tc), lambda p, i, c, j: (p * ni + i, c * nj + j)),
        out_shape=jax.ShapeDtypeStruct((2 * length, 2 * hyw), F32 if reverse_tail else BF),
        scratch_shapes=[pltpu.VMEM((tr, fh), F32)],
        compiler_params=_cparams(4),
        name="hyena_filters",
    )(feats, w1p, b1.reshape(1, fh), f1.reshape(1, fh), w2, b2.reshape(1, fh), f2.reshape(1, fh),
      w3, decay.reshape(1, HY_N_FILT * hyw))


def _dft_tables(length, tile):
    n = 2 * length
    half = tile // 2
    rows = jnp.arange(n, dtype=jnp.int32)
    tile_i, r = rows // tile, rows % tile
    is_im = r >= half
    k = tile_i * half + jnp.where(is_im, r - half, r)
    t = jnp.arange(length, dtype=jnp.int32)
    ang = (2.0 * math.pi / n) * ((k[:, None] * t[None, :]) % n).astype(F32)
    nyq = jnp.where(t % 2 == 0, 1.0, -1.0).astype(F32)[None, :]
    is_nyq = (is_im & (k == 0))[:, None]
    fwd = jnp.where(is_im[:, None], -jnp.sin(ang), jnp.cos(ang))
    fwd = jnp.where(is_nyq, nyq, fwd)
    sgn = jnp.where(is_im[:, None] & ~is_nyq, -1.0, 1.0)
    fker = jnp.concatenate([fwd, fwd * sgn], axis=1)
    scale = jnp.where(k == 0, 1.0 / n, 2.0 / n)[None, :]
    inv = jnp.where(is_im[None, :], -jnp.sin(ang.T), jnp.cos(ang.T))
    inv = jnp.where(is_nyq.T, nyq.T, inv) * scale
    return fwd.astype(BF), fker.astype(BF), inv.astype(BF)


def _hy_long_convs(u_seq, kstack, hy_skip, hyw):
    b, ls, _ = u_seq.shape
    n = 2 * ls
    tmf = _pick(n, (512,))
    half = tmf // 2
    tn = _pick(hyw, (512, 256, 128))
    nj = hyw // tn
    fwd, fker, inv = _dft_tables(ls, tmf)

    (hspec,) = _matmul(
        fker, kstack,
        pl.BlockSpec((tmf, n), lambda g, j, i: (i, 0)),
        pl.BlockSpec((n, tn), lambda g, j, i: (0, j)),
        (1, 2 * nj, n // tmf),
        [(jax.ShapeDtypeStruct((n, 2 * hyw), F32), pl.BlockSpec((tmf, tn), lambda g, j, i: (i, j)))],
        _ep_store(F32), name="hyena_filter_dft")

    def spectrum_product(acc, extras, outs):
        hs = extras[0][...]
        re, im = acc[:half], acc[half:]
        hre, him = hs[:half], hs[half:]
        first = (lax.broadcasted_iota(jnp.int32, (half, 1), 0) == 0) & (pl.program_id(2) == 0)
        yre = jnp.where(first, re * hre, re * hre - im * him)
        yim = jnp.where(first, im * him, re * him + im * hre)
        outs[0][pl.ds(0, half), :] = yre.astype(BF)
        outs[0][pl.ds(half, half), :] = yim.astype(BF)

    tmi = _pick(ls, (TOKEN_TILE,))
    skip3 = hy_skip.reshape(2, 1, hyw)

    def conv(z_arr, z_off, conv_idx, mul_off):
        (spec,) = _matmul(
            fwd, z_arr,
            pl.BlockSpec((tmf, ls), lambda g, j, i: (i, 0)),
            pl.BlockSpec((None, ls, tn), lambda g, j, i: (g, 0, z_off + j)),
            (b, nj, n // tmf),
            [(jax.ShapeDtypeStruct((b, n, hyw), BF), pl.BlockSpec((None, tmf, tn), lambda g, j, i: (g, i, j)))],
            spectrum_product,
            extras=[(hspec, pl.BlockSpec((tmf, tn), lambda g, j, i: (i, conv_idx * nj + j)))],
            name="hyena_dft_fwd")

        def finish(acc, extras, outs):
            z = extras[0][...].astype(F32)
            xm = extras[2][...].astype(F32)
            outs[0][...] = (xm * (acc + z * extras[1][...])).astype(BF)

        (y,) = _matmul(
            inv, spec,
            pl.BlockSpec((tmi, n), lambda g, j, i: (i, 0)),
            pl.BlockSpec((None, n, tn), lambda g, j, i: (g, 0, j)),
            (b, nj, ls // tmi),
            [(jax.ShapeDtypeStruct((b, ls, hyw), BF), pl.BlockSpec((None, tmi, tn), lambda g, j, i: (g, i, j)))],
            finish,
            extras=[(z_arr, pl.BlockSpec((None, tmi, tn), lambda g, j, i: (g, i, z_off + j))),
                    (skip3, pl.BlockSpec((None, 1, tn), lambda g, j, i: (conv_idx, 0, j))),
                    (u_seq, pl.BlockSpec((None, tmi, tn), lambda g, j, i: (g, i, mul_off * nj + j)))],
            name="hyena_dft_inv")
        return y

    z2 = conv(u_seq, 0, 0, 1)
    return conv(z2, 0, 1, 2)


FFT_FAST = 64


def _fft_tables(length):
    bf = FFT_FAST
    a_n = length // bf
    na = 2 * a_n
    n = 2 * length
    b = jnp.arange(bf, dtype=jnp.int32)
    k1 = jnp.arange(na, dtype=jnp.int32)
    a = jnp.arange(na, dtype=jnp.int32)
    nn = bf * a[None, None, :] + b[:, None, None]
    ang = (2.0 * math.pi / n) * ((k1[None, :, None] * nn) % n).astype(F32)
    cs, sn = jnp.cos(ang), jnp.sin(ang)
    m1k = jnp.concatenate([cs, -sn], axis=1)
    cd, sd = cs[:, :, :a_n], sn[:, :, :a_n]
    m1 = jnp.concatenate([jnp.concatenate([cd, sd], axis=2),
                          jnp.concatenate([-sd, cd], axis=2)], axis=1)
    ct, st = jnp.swapaxes(cd, 1, 2) / n, jnp.swapaxes(sd, 1, 2) / n
    g2 = jnp.concatenate([jnp.concatenate([ct, -st], axis=2),
                          jnp.concatenate([st, ct], axis=2)], axis=1)
    k2 = jnp.arange(bf, dtype=jnp.int32)
    ang2 = (2.0 * math.pi / bf) * ((k2[:, None] * b[None, :]) % bf).astype(F32)
    c2, s2 = jnp.cos(ang2), jnp.sin(ang2)
    f2 = jnp.concatenate([jnp.concatenate([c2, s2], axis=1), jnp.concatenate([-s2, c2], axis=1)], axis=0)
    i1 = jnp.concatenate([jnp.concatenate([c2.T, -s2.T], axis=1), jnp.concatenate([s2.T, c2.T], axis=1)], axis=0)
    return m1.astype(BF), m1k.astype(BF), f2.astype(BF), i1.astype(BF), g2.astype(BF)


def _fft_s1_kernel(z_ref, m_ref, o_ref):
    for bl in range(m_ref.shape[0]):
        z = jnp.concatenate([z_ref[0, :, bl, :], z_ref[1, :, bl, :]], axis=0).astype(BF)
        o_ref[bl] = jnp.dot(m_ref[bl], z, preferred_element_type=F32)


def _fft_mid_kernel(pr_ref, pi_ref, h_ref, f2_ref, i1_ref, o_ref):
    bf = f2_ref.shape[0] // 2
    for kl in range(h_ref.shape[0]):
        p = jnp.concatenate([pr_ref[:, kl, :], pi_ref[:, kl, :]], axis=0).astype(BF)
        x = jnp.dot(f2_ref[...], p, preferred_element_type=F32)
        hs = h_ref[kl].astype(F32)
        xr, xi = x[:bf], x[bf:]
        hr, hi = hs[:bf], hs[bf:]
        y = jnp.concatenate([xr * hr - xi * hi, xr * hi + xi * hr], axis=0).astype(BF)
        o_ref[kl] = jnp.dot(i1_ref[...], y, preferred_element_type=F32)


def _fft_s3_kernel(qr_ref, qi_ref, g_ref, z_ref, skip_ref, xm_ref, o_ref):
    a_n = z_ref.shape[1]
    for bl in range(g_ref.shape[0]):
        q = jnp.concatenate([qr_ref[:, bl, :], qi_ref[:, bl, :]], axis=0).astype(BF)
        y = jnp.dot(g_ref[bl], q, preferred_element_type=F32)
        for s in range(2):
            ys = y[s * a_n:(s + 1) * a_n]
            o_ref[s, :, bl, :] = xm_ref[s, :, bl, :] * (ys + z_ref[s, :, bl, :] * skip_ref[...])


def _fft_kern_s1_kernel(k_ref, m_ref, o_ref):
    for bl in range(m_ref.shape[0]):
        o_ref[bl] = jnp.dot(m_ref[bl], k_ref[:, bl, :].astype(BF), preferred_element_type=F32)


def _fft_spec_kernel(pr_ref, pi_ref, f2_ref, o_ref):
    for kl in range(o_ref.shape[0]):
        p = jnp.concatenate([pr_ref[:, kl, :], pi_ref[:, kl, :]], axis=0).astype(BF)
        o_ref[kl] = jnp.dot(f2_ref[...], p, preferred_element_type=F32).astype(o_ref.dtype)


def _hy_long_convs_fft(u_seq, kern, hy_skip, hyw):
    nb_in, ls, _ = u_seq.shape
    if nb_in % 2:
        u_seq = jnp.concatenate([u_seq, jnp.zeros_like(u_seq[:1])], axis=0)
    nb = u_seq.shape[0] // 2
    bf = FFT_FAST
    a_n = ls // bf
    na = 2 * a_n
    ra = 2 * a_n
    m1, m1k, f2, i1, g2 = _fft_tables(ls)
    bg = 8
    td = _pick(hyw, (1024, 512, 256, 128))
    ndt = hyw // td

    ph = pl.pallas_call(
        _fft_kern_s1_kernel,
        grid=(bf // bg, 2 * ndt),
        in_specs=[pl.BlockSpec((na, bg, td), lambda b, j: (0, b, j)),
                  pl.BlockSpec((bg, 2 * na, na), lambda b, j: (b, 0, 0))],
        out_specs=pl.BlockSpec((bg, 2 * na, td), lambda b, j: (b, 0, j)),
        out_shape=jax.ShapeDtypeStruct((bf, 2 * na, 2 * hyw), F32),
        compiler_params=_cparams(2),
        name="hyena_fft_kern_s1",
    )(kern.reshape(na, bf, 2 * hyw), m1k)
    hspec = pl.pallas_call(
        _fft_spec_kernel,
        grid=(na // bg, 2 * ndt),
        in_specs=[pl.BlockSpec((bf, bg, td), lambda k, j: (0, k, j)),
                  pl.BlockSpec((bf, bg, td), lambda k, j: (0, na // bg + k, j)),
                  pl.BlockSpec((2 * bf, 2 * bf), lambda k, j: (0, 0))],
        out_specs=pl.BlockSpec((bg, 2 * bf, td), lambda k, j: (k, 0, j)),
        out_shape=jax.ShapeDtypeStruct((na, 2 * bf, 2 * hyw), BF),
        compiler_params=_cparams(2),
        name="hyena_fft_kern_s2",
    )(ph, ph, f2)

    skip3 = hy_skip.reshape(2, 1, hyw)
    u5 = u_seq.reshape(nb, 2, a_n, bf, 3 * hyw)

    def conv(z5, z_sec, conv_idx, mul_sec):
        zspec = pl.BlockSpec((None, 2, a_n, bg, td), lambda g, b, j: (g, 0, 0, b, z_sec * ndt + j))
        p = pl.pallas_call(
            _fft_s1_kernel,
            grid=(nb, bf // bg, ndt),
            in_specs=[zspec, pl.BlockSpec((bg, 2 * na, ra), lambda g, b, j: (b, 0, 0))],
            out_specs=pl.BlockSpec((None, bg, 2 * na, td), lambda g, b, j: (g, b, 0, j)),
            out_shape=jax.ShapeDtypeStruct((nb, bf, 2 * na, hyw), F32),
            compiler_params=_cparams(3),
            name="hyena_fft_s1",
        )(z5, m1)
        q = pl.pallas_call(
            _fft_mid_kernel,
            grid=(na // bg, ndt, nb),
            in_specs=[pl.BlockSpec((None, bf, bg, td), lambda k, j, g: (g, 0, k, j)),
                      pl.BlockSpec((None, bf, bg, td), lambda k, j, g: (g, 0, na // bg + k, j)),
                      pl.BlockSpec((bg, 2 * bf, td), lambda k, j, g: (k, 0, conv_idx * ndt + j)),
                      pl.BlockSpec((2 * bf, 2 * bf), lambda k, j, g: (0, 0)),
                      pl.BlockSpec((2 * bf, 2 * bf), lambda k, j, g: (0, 0))],
            out_specs=pl.BlockSpec((None, bg, 2 * bf, td), lambda k, j, g: (g, k, 0, j)),
            out_shape=jax.ShapeDtypeStruct((nb, na, 2 * bf, hyw), F32),
            compiler_params=_cparams(3),
            name="hyena_fft_mid",
        )(p, p, hspec, f2, i1)
        return pl.pallas_call(
            _fft_s3_kernel,
            grid=(nb, bf // bg, ndt),
            in_specs=[pl.BlockSpec((None, na, bg, td), lambda g, b, j: (g, 0, b, j)),
                      pl.BlockSpec((None, na, bg, td), lambda g, b, j: (g, 0, bf // bg + b, j)),
                      pl.BlockSpec((bg, ra, 2 * na), lambda g, b, j: (b, 0, 0)),
                      zspec,
                      pl.BlockSpec((None, 1, td), lambda g, b, j: (conv_idx, 0, j)),
                      pl.BlockSpec((None, 2, a_n, bg, td), lambda g, b, j: (g, 0, 0, b, mul_sec * ndt + j))],
            out_specs=pl.BlockSpec((None, 2, a_n, bg, td), lambda g, b, j: (g, 0, 0, b, j)),
            out_shape=jax.ShapeDtypeStruct((nb, 2, a_n, bf, hyw), F32),
            compiler_params=_cparams(3),
            name="hyena_fft_s3",
        )(q, q, g2, z5, skip3, u5)

    z_mid = conv(u5, 0, 0, 1)
    y = conv(z_mid, 0, 1, 2)
    return y.reshape(2 * nb, ls, hyw)[:nb_in]


def _moe_up_kernel(x_ref, wg_ref, wu_ref, comb_ref, o_ref, *, es):
    x = x_ref[...]
    g = jnp.dot(x, wg_ref[...], preferred_element_type=F32)
    u = jnp.dot(x, wu_ref[...], preferred_element_type=F32)
    comb = comb_ref[...]
    lane = lax.broadcasted_iota(jnp.int32, comb.shape, 1)
    f = g.shape[1] // es
    col = lax.broadcasted_iota(jnp.int32, (1, g.shape[1]), 1)
    scale = jnp.zeros_like(g)
    for s in range(es):
        e = pl.program_id(0) * es + s
        sc = jnp.sum(jnp.where(lane == e, comb, 0.0), axis=1, keepdims=True)
        scale = jnp.where((col >= s * f) & (col < (s + 1) * f), sc, scale)
    o_ref[...] = (g * jax.nn.sigmoid(g) * u * scale).astype(o_ref.dtype)


MOE_EXPERTS_PER_STEP = 4


def _moe_group_weights(w):
    nl, ne, d, f = w.shape
    es = MOE_EXPERTS_PER_STEP
    w = w.reshape(nl, ne // es, es, d, f).transpose(0, 1, 3, 2, 4)
    return w.reshape(nl, ne // es, d, es * f).astype(BF)


def _moe_up(u, w_gate, w_up, comb, l):
    m, d = u.shape
    _, ng, _, nf = w_gate.shape
    tm = _pick(m, (544, 512, 256))
    wspec = pl.BlockSpec((None, None, d, nf), lambda e, i: (l, e, 0, 0))
    return pl.pallas_call(
        functools.partial(_moe_up_kernel, es=MOE_EXPERTS_PER_STEP),
        grid=(ng, m // tm),
        in_specs=[pl.BlockSpec((tm, d), lambda e, i: (i, 0)), wspec, wspec,
                  pl.BlockSpec((tm, LANES), lambda e, i: (i, 0))],
        out_specs=pl.BlockSpec((tm, nf), lambda e, i: (i, e)),
        out_shape=jax.ShapeDtypeStruct((m, ng * nf), BF),
        compiler_params=_cparams(2),
        name="moe_up",
    )(u, w_gate, w_up, comb)


def _row_gate(gt_ref, tm, t_len, ctx_len, n_batch):
    r = lax.broadcasted_iota(jnp.int32, (tm, 1), 0) + pl.program_id(2) * tm
    gate = jnp.zeros((tm, gt_ref.shape[1]), F32)
    for bi in range(n_batch):
        in_b = (r >= bi * t_len) & (r < (bi + 1) * t_len)
        is_c = r < bi * t_len + ctx_len
        gate = jnp.where(in_b & is_c, gt_ref[0:1, :], gate)
        gate = jnp.where(in_b & ~is_c, gt_ref[1 + bi:2 + bi, :], gate)
    return gate


def _final_norm_kernel(h_ref, g_ref, o_ref):
    xf = h_ref[...]
    o_ref[...] = xf * lax.rsqrt(jnp.mean(xf * xf, axis=-1, keepdims=True) + NORM_EPS) * g_ref[...]


def _final_norm(h, gain, n_batch, t_len, ctx_len):
    m, d = h.shape
    tt = TOKEN_TILE
    n_t, n_c = t_len // tt, ctx_len // tt
    n_x = n_t - n_c
    return pl.pallas_call(
        _final_norm_kernel,
        grid=(n_batch, n_x),
        in_specs=[pl.BlockSpec((tt, d), lambda bi, i: (bi * n_t + n_c + i, 0)),
                  pl.BlockSpec((1, d), lambda bi, i: (0, 0))],
        out_specs=pl.BlockSpec((None, tt, d), lambda bi, i: (bi, i, 0)),
        out_shape=jax.ShapeDtypeStruct((n_batch, t_len - ctx_len, d), F32),
        compiler_params=_cparams(2),
        name="final_norm",
    )(h, gain.reshape(1, d))


def kernel(x, c, ctx, c_ctx, ada_down, ada_up, ada_bias, norm_mix, norm_ffn, w_in, qkv_conv, a_log, dt_bias, o_norm, hy_conv, hy_w1, hy_b1, hy_f1, hy_w2, hy_b2, hy_f2, hy_w3, hy_decay, hy_skip, w_br_a, w_br_b, w_out, router_group, router_group_bias, router_expert, router_expert_bias, exp_gate, exp_up, exp_down, final_norm):
    n_batch, seq, d = x.shape
    ctx_len = ctx.shape[1]
    t_len = ctx_len + seq
    m = n_batch * t_len
    depth = w_in.shape[0]
    n_heads = a_log.shape[2]
    a_width = n_heads * LANES
    hyw = hy_skip.shape[2]
    n_exp = exp_gate.shape[1]
    dg_off = 4 * a_width
    hy_off = dg_off + 4 * n_heads
    tail_w = 3 * hyw + 2 * d
    n_t, n_c = t_len // TOKEN_TILE, ctx_len // TOKEN_TILE
    nc = t_len // CHUNK
    ncp = -(-nc // 8) * 8

    h = jnp.concatenate([ctx, x], axis=1).reshape(m, d)

    conds = jnp.zeros((8, d), F32).at[0].set(c_ctx).at[1:1 + n_batch].set(c)
    mod_all = _adaln_all(conds, ada_down, ada_up, ada_bias)
    mod_all = mod_all.reshape(depth, 8, 6, d)[:, :1 + n_batch]
    mod_all = jnp.pad(mod_all, ((0, 0), (0, 0), (0, 2), (0, 0)))

    tm = _pick(m, (1088, 512, 256))
    tmp = tm
    tn = 512

    w_in_b = w_in.astype(BF)
    tail_base = (hy_off // LANES) * LANES
    exp_gate_b = _moe_group_weights(exp_gate)
    exp_up_b = _moe_group_weights(exp_up)

    def gate_table(mod_l, idx):
        return jnp.pad(mod_l[:, idx, :], ((0, 8 - (1 + n_batch)), (0, 0)))

    def residual_ep(acc, extras, outs):
        outs[0][...] = extras[0][...] + _row_gate(extras[1], acc.shape[0], t_len, ctx_len, n_batch) * acc

    for l in range(depth):
        mod_l = mod_all[l]
        u = _norm_mod(h, norm_mix[l], mod_l, 0, n_t, n_c)
        (p1,) = _matmul(
            u, w_in_b,
            pl.BlockSpec((tmp, d), lambda g, j, i: (i, 0)),
            pl.BlockSpec((None, d, tn), lambda g, j, i, l=l: (l, 0, j)),
            (1, dg_off // tn, m // tmp),
            [(jax.ShapeDtypeStruct((m, dg_off), BF), pl.BlockSpec((tmp, tn), lambda g, j, i: (i, j)))],
            _ep_store(BF), name="proj_qkvz")
        (p_tail,) = _matmul(
            u, w_in_b,
            pl.BlockSpec((tmp, d), lambda g, j, i: (i, 0)),
            pl.BlockSpec((None, d, tn), lambda g, j, i, l=l: (l, 0, tail_base // tn + j)),
            (1, tail_w // tn, m // tmp),
            [(jax.ShapeDtypeStruct((m, tail_w), BF), pl.BlockSpec((tmp, tn), lambda g, j, i: (i, j)))],
            _ep_store(BF), name="proj_tail",
            w_next_spec=pl.BlockSpec((None, d, LANES),
                                     lambda g, j, i, l=l: (l, 0, (tail_base + (j + 1) * tn) // LANES)),
            lane_shift=hy_off - tail_base)
        w_dg_t = w_in[l, :, dg_off:hy_off].T
        a_exp = jnp.concatenate([jnp.exp(a_log[l].reshape(-1)), jnp.zeros((2 * n_heads,), F32)])[:, None]
        dt_b = jnp.concatenate([dt_bias[l].reshape(-1), jnp.zeros((2 * n_heads,), F32)])[:, None]
        gates = _gates(u, w_dg_t, a_exp, dt_b)

        g5 = gates.reshape(4, n_heads, n_batch, nc, CHUNK).transpose(2, 1, 0, 3, 4)
        g_rows = jnp.pad(g5, ((0, 0), (0, 0), (0, 0), (0, ncp - nc), (0, 0)))
        g_cols = jnp.pad(jnp.swapaxes(g5, 3, 4), ((0, 0), (0, 0), (0, 0), (0, 0), (0, LANES - nc)))

        p1_3 = p1.reshape(n_batch, t_len, dg_off)
        conv9 = qkv_conv[l].reshape(9, 3 * a_width)
        du, dw, dq, dk, da, dgl = _delta_prep(p1_3, conv9, g_rows, g_cols, n_heads, ctx_len)
        o_f, o_b = _delta_scan(du, dw, dq, dk, da, dgl, ctx_len)
        o_a = _delta_out(o_f, o_b, p1_3, o_norm[l], n_heads).reshape(m, a_width)

        p_tail3 = p_tail.reshape(n_batch, t_len, tail_w)
        u_c, u_x = _hy_conv(p_tail3, hy_conv[l], ctx_len, 3 * hyw)
        filt_args = (hy_w1[l], hy_b1[l], hy_f1[l], hy_w2[l], hy_b2[l], hy_f2[l], hy_w3[l], hy_decay[l])
        y_x = _hy_long_convs_fft(u_x, _hy_filter_stack(seq, *filt_args, hyw, True), hy_skip[l], hyw)
        y_c = _hy_long_convs(u_c, _hy_filter_stack(ctx_len, *filt_args, hyw, False), hy_skip[l], hyw)
        y_b = jnp.concatenate([y_c, y_x.astype(BF)], axis=1).reshape(m, hyw)

        ga_blk = (3 * hyw) // tn
        gb_blk = (3 * hyw + d) // tn

        merged = pl.pallas_call(
            _merge_kernel,
            grid=(d // tn, m // tm),
            in_specs=[pl.BlockSpec((tm, a_width), lambda j, i: (i, 0)),
                      pl.BlockSpec((tm, hyw), lambda j, i: (i, 0)),
                      pl.BlockSpec((None, a_width, tn), lambda j, i, l=l: (l, 0, j)),
                      pl.BlockSpec((None, hyw, tn), lambda j, i, l=l: (l, 0, j)),
                      pl.BlockSpec((tm, tn), lambda j, i: (i, ga_blk + j)),
                      pl.BlockSpec((tm, tn), lambda j, i: (i, gb_blk + j))],
            out_specs=pl.BlockSpec((tm, tn), lambda j, i: (i, j)),
            out_shape=jax.ShapeDtypeStruct((m, d), BF),
            scratch_shapes=[pltpu.VMEM((a_width, tn), BF), pltpu.VMEM((hyw, tn), BF)],
            compiler_params=_cparams(2),
            name="branch_merge",
        )(o_a, y_b, w_br_a, w_br_b, p_tail, p_tail)

        (h,) = _matmul(
            merged, w_out,
            pl.BlockSpec((tm, d), lambda g, j, i: (i, 0)),
            pl.BlockSpec((None, d, tn), lambda g, j, i, l=l: (l, 0, j)),
            (1, d // tn, m // tm),
            [(jax.ShapeDtypeStruct((m, d), F32), pl.BlockSpec((tm, tn), lambda g, j, i: (i, j)))],
            residual_ep,
            extras=[(h, pl.BlockSpec((tm, tn), lambda g, j, i: (i, j))),
                    (gate_table(mod_l, 2), pl.BlockSpec((8, tn), lambda g, j, i: (0, j)))],
            name="out_proj")

        w_router = jnp.pad(jnp.concatenate([router_expert[l], router_group[l]], axis=1),
                           ((0, 0), (0, LANES - n_exp - N_GROUPS)))
        b_router = jnp.pad(jnp.concatenate([router_expert_bias[l], router_group_bias[l]]),
                           (0, LANES - n_exp - N_GROUPS)).reshape(1, LANES)
        u2, comb = _norm_mod_router(h, norm_ffn[l], mod_l, 3, n_t, n_c, w_router, b_router, n_exp, N_GROUPS)
        hid = _moe_up(u2, exp_gate_b, exp_up_b, comb, l)
        kf = hid.shape[1]
        tnd = _pick(d, (512, 256, 128))
        tmd = _pick(m, (544, 512, 256))
        (h,) = _matmul(
            hid, exp_down.reshape(depth, kf, d),
            pl.BlockSpec((tmd, kf), lambda g, j, i: (i, 0)),
            pl.BlockSpec((None, kf, tnd), lambda g, j, i, l=l: (l, 0, j)),
            (1, d // tnd, m // tmd),
            [(jax.ShapeDtypeStruct((m, d), F32), pl.BlockSpec((tmd, tnd), lambda g, j, i: (i, j)))],
            residual_ep,
            extras=[(h, pl.BlockSpec((tmd, tnd), lambda g, j, i: (i, j))),
                    (gate_table(mod_l, 5), pl.BlockSpec((8, tnd), lambda g, j, i: (0, j)))],
            name="moe_down")

    return _final_norm(h, final_norm, n_batch, t_len, ctx_len)
```

```python
import functools
import math

import jax
import jax.numpy as jnp
from jax import lax
from jax.experimental import pallas as pl
from jax.experimental.pallas import tpu as pltpu

F32 = jnp.float32
BF = jnp.bfloat16

GRID_W = 64
CHUNK = 64
N_GROUPS = 4
EXPERTS_PER_GROUP = 8
HY_N_FILT = 4
NORM_EPS = 1e-6
TOKEN_TILE = 256
LANES = 128
CONV_PAD = 72
VMEM_LIMIT = 56 * 1024 * 1024


def _cparams(n_grid):
    return pltpu.CompilerParams(dimension_semantics=("arbitrary",) * n_grid,
                                vmem_limit_bytes=VMEM_LIMIT)


def _split_bf16(a):
    hi = a.astype(BF)
    lo = (a - hi.astype(F32)).astype(BF)
    return hi, lo


def _dot3(a, b):
    ah, al = _split_bf16(a)
    bh, bl = _split_bf16(b)
    d = functools.partial(jnp.dot, preferred_element_type=F32)
    return d(ah, bh) + (d(ah, bl) + d(al, bh))


def _dotx(a, b):
    return jnp.dot(a, b, precision=lax.Precision.HIGHEST, preferred_element_type=F32)


def _bdot(a, b):
    return jnp.dot(a.astype(BF), b.astype(BF), preferred_element_type=F32)


def _mm_kernel(*refs, n_extra, n_out, stage_w, lane_shift, prologue, epilogue):
    x_ref, w_ref = refs[0], refs[1]
    n_w = 3 if lane_shift else 2
    extras = refs[n_w:n_w + n_extra]
    outs = refs[n_w + n_extra:n_w + n_extra + n_out]
    scratch = refs[n_w + n_extra + n_out:]
    if stage_w:
        wbf = scratch[0]

        @pl.when(pl.program_id(2) == 0)
        def _():
            k = w_ref.shape[0]
            step = 512 if k % 512 == 0 else k

            def load(r0):
                wa = w_ref[pl.ds(r0, step), :]
                if lane_shift:
                    wa = jnp.concatenate([wa[:, lane_shift:], refs[2][pl.ds(r0, step), :lane_shift]], axis=1)
                return wa.astype(BF)

            if k == step:
                wbf[...] = load(0)
            else:
                def it(r, c):
                    r0 = pl.multiple_of(r * step, step)
                    wbf[pl.ds(r0, step), :] = load(r0)
                    return c
                lax.fori_loop(0, k // step, it, 0)
        w = wbf[...]
    else:
        w = w_ref[...]
    x = x_ref[...]
    if prologue is not None:
        x = prologue(x)
    if x.dtype != BF:
        x = x.astype(BF)
    acc = jnp.dot(x, w, preferred_element_type=F32)
    epilogue(acc, extras, outs)


def _matmul(x, w, x_spec, w_spec, grid, outs, epilogue, extras=(), prologue=None, name=None,
            w_next_spec=None, lane_shift=0):
    stage_w = w.dtype != BF or lane_shift > 0
    kw, tn = [d for d in w_spec.block_shape if d is not None]
    scratch = [pltpu.VMEM((kw, tn), BF)] if stage_w else []
    body = functools.partial(_mm_kernel, n_extra=len(extras), n_out=len(outs), stage_w=stage_w,
                             lane_shift=lane_shift, prologue=prologue, epilogue=epilogue)
    w_ops, w_specs = ([w, w], [w_spec, w_next_spec]) if lane_shift else ([w], [w_spec])
    res = pl.pallas_call(
        body,
        grid=grid,
        in_specs=[x_spec] + w_specs + [s for _, s in extras],
        out_specs=[s for _, s in outs],
        out_shape=[o for o, _ in outs],
        scratch_shapes=scratch,
        compiler_params=_cparams(3),
        name=name,
    )(x, *w_ops, *[a for a, _ in extras])
    return res


def _merge_kernel(xa_ref, xb_ref, wa_ref, wb_ref, ga_ref, gb_ref, o_ref, wa_bf, wb_bf):
    @pl.when(pl.program_id(1) == 0)
    def _():
        wa_bf[...] = wa_ref[...].astype(BF)
        wb_bf[...] = wb_ref[...].astype(BF)
    a = jnp.dot(xa_ref[...], wa_bf[...], preferred_element_type=F32)
    b = jnp.dot(xb_ref[...], wb_bf[...], preferred_element_type=F32)
    o_ref[...] = (jax.nn.sigmoid(ga_ref[...].astype(F32)) * a
                  + jax.nn.sigmoid(gb_ref[...].astype(F32)) * b).astype(o_ref.dtype)


def _ep_store(dtype):
    def ep(acc, extras, outs):
        outs[0][...] = acc.astype(dtype)
    return ep


def _pick(n, prefs):
    for p in prefs:
        if n % p == 0:
            return p
    return n


def _adaln_all(conds, ada_down, ada_up, ada_bias):
    nl, d, r = ada_down.shape
    n6 = ada_up.shape[2]
    (t,) = _matmul(
        conds, ada_down,
        pl.BlockSpec((8, d), lambda g, j, i: (0, 0)),
        pl.BlockSpec((None, d, r), lambda g, j, i: (g, 0, 0)),
        (nl, 1, 1),
        [(jax.ShapeDtypeStruct((nl, 8, r), F32), pl.BlockSpec((None, 8, r), lambda g, j, i: (g, 0, 0)))],
        _ep_store(F32), prologue=lambda v: v * jax.nn.sigmoid(v), name="adaln_down")
    tn = _pick(n6, (2048, 1024, 512, 256, 128))

    def ep(acc, extras, outs):
        outs[0][...] = acc + extras[0][...]

    (mod,) = _matmul(
        t, ada_up,
        pl.BlockSpec((None, 8, r), lambda g, j, i: (g, 0, 0)),
        pl.BlockSpec((None, r, tn), lambda g, j, i: (g, 0, j)),
        (nl, n6 // tn, 1),
        [(jax.ShapeDtypeStruct((nl, 8, n6), F32), pl.BlockSpec((None, 8, tn), lambda g, j, i: (g, 0, j)))],
        ep, extras=[(ada_bias.reshape(nl, 1, n6), pl.BlockSpec((None, 1, tn), lambda g, j, i: (g, 0, j)))],
        name="adaln_up")
    return mod


def _mod_row_map(n_t, n_c):
    def m(i):
        return (jnp.where(i % n_t < n_c, 0, 1 + i // n_t), 0, 0)
    return m


def _norm_mod_kernel(h_ref, g_ref, mod_ref, u_ref, *, off):
    xf = h_ref[...]
    y = xf * lax.rsqrt(jnp.mean(xf * xf, axis=-1, keepdims=True) + NORM_EPS) * g_ref[...]
    sh = mod_ref[off:off + 1, :]
    sc = mod_ref[off + 1:off + 2, :]
    u_ref[...] = (y * (1.0 + sc) + sh).astype(u_ref.dtype)


def _norm_mod(h, gain, mod_l, off, n_t, n_c):
    m, d = h.shape
    tt = TOKEN_TILE
    return pl.pallas_call(
        functools.partial(_norm_mod_kernel, off=off),
        grid=(m // tt,),
        in_specs=[pl.BlockSpec((tt, d), lambda i: (i, 0)),
                  pl.BlockSpec((1, d), lambda i: (0, 0)),
                  pl.BlockSpec((None, 8, d), _mod_row_map(n_t, n_c))],
        out_specs=pl.BlockSpec((tt, d), lambda i: (i, 0)),
        out_shape=jax.ShapeDtypeStruct((m, d), BF),
        compiler_params=_cparams(1),
        name="norm_mod",
    )(h, gain.reshape(1, d), mod_l)


def _router_kernel(h_ref, g_ref, mod_ref, wr_ref, br_ref, u_ref, comb_ref, *, off, n_exp, n_grp):
    xf = h_ref[...]
    y = xf * lax.rsqrt(jnp.mean(xf * xf, axis=-1, keepdims=True) + NORM_EPS) * g_ref[...]
    sh = mod_ref[off:off + 1, :]
    sc = mod_ref[off + 1:off + 2, :]
    u = y * (1.0 + sc) + sh
    u_ref[...] = u.astype(u_ref.dtype)
    logits = _dot3(u, wr_ref[...]) + br_ref[...]
    lane = lax.broadcasted_iota(jnp.int32, logits.shape, 1)
    neg = jnp.float32(-jnp.inf)
    big = jnp.int32(1 << 20)
    is_g = (lane >= n_exp) & (lane < n_exp + n_grp)
    lg = jnp.where(is_g, logits, neg)
    eg = jnp.exp(lg - jnp.max(lg, axis=1, keepdims=True))
    pg = eg / jnp.sum(eg, axis=1, keepdims=True)
    p_top = jnp.max(pg, axis=1, keepdims=True)
    g_idx = jnp.min(jnp.where(is_g & (pg == p_top), lane, big), axis=1, keepdims=True) - n_exp
    per = n_exp // n_grp
    in_grp = (lane >= g_idx * per) & (lane < (g_idx + 1) * per)
    le = jnp.where(in_grp, logits, neg)
    v1 = jnp.max(le, axis=1, keepdims=True)
    i1 = jnp.min(jnp.where(le == v1, lane, big), axis=1, keepdims=True)
    le2 = jnp.where(lane == i1, neg, le)
    v2 = jnp.max(le2, axis=1, keepdims=True)
    i2 = jnp.min(jnp.where(le2 == v2, lane, big), axis=1, keepdims=True)
    e2 = jnp.exp(v2 - v1)
    w1 = p_top / (1.0 + e2)
    w2 = p_top * e2 / (1.0 + e2)
    comb_ref[...] = jnp.where(lane == i1, w1, 0.0) + jnp.where(lane == i2, w2, 0.0)


def _norm_mod_router(h, gain, mod_l, off, n_t, n_c, w_router, b_router, n_exp, n_grp):
    m, d = h.shape
    tt = TOKEN_TILE
    return pl.pallas_call(
        functools.partial(_router_kernel, off=off, n_exp=n_exp, n_grp=n_grp),
        grid=(m // tt,),
        in_specs=[pl.BlockSpec((tt, d), lambda i: (i, 0)),
                  pl.BlockSpec((1, d), lambda i: (0, 0)),
                  pl.BlockSpec((None, 8, d), _mod_row_map(n_t, n_c)),
                  pl.BlockSpec((d, LANES), lambda i: (0, 0)),
                  pl.BlockSpec((1, LANES), lambda i: (0, 0))],
        out_specs=[pl.BlockSpec((tt, d), lambda i: (i, 0)),
                   pl.BlockSpec((tt, LANES), lambda i: (i, 0))],
        out_shape=[jax.ShapeDtypeStruct((m, d), BF), jax.ShapeDtypeStruct((m, LANES), F32)],
        compiler_params=_cparams(1),
        name="norm_mod_router",
    )(h, gain.reshape(1, d), mod_l, w_router, b_router)


def _gates_kernel(w_ref, u_ref, aexp_ref, dtb_ref, o_ref, *, n_decay):
    acc = lax.dot_general(w_ref[...].astype(BF), u_ref[...], (((1,), (1,)), ((), ())),
                          preferred_element_type=F32)
    row = lax.broadcasted_iota(jnp.int32, acc.shape, 0)
    z = acc + dtb_ref[...]
    softplus = jnp.maximum(z, 0.0) + jnp.log(1.0 + jnp.exp(-jnp.abs(z)))
    g = -aexp_ref[...] * softplus
    o_ref[...] = jnp.where(row < n_decay, g, jax.nn.sigmoid(acc))


def _gates(u, w_dg_t, a_exp, dt_b):
    m, d = u.shape
    r = w_dg_t.shape[0]
    tm = _pick(m, (512, 256))
    return pl.pallas_call(
        functools.partial(_gates_kernel, n_decay=r // 2),
        grid=(m // tm,),
        in_specs=[pl.BlockSpec((r, d), lambda i: (0, 0)),
                  pl.BlockSpec((tm, d), lambda i: (i, 0)),
                  pl.BlockSpec((r, 1), lambda i: (0, 0)),
                  pl.BlockSpec((r, 1), lambda i: (0, 0))],
        out_specs=pl.BlockSpec((r, tm), lambda i: (0, i)),
        out_shape=jax.ShapeDtypeStruct((r, m), F32),
        compiler_params=_cparams(1),
        name="delta_gates",
    )(w_dg_t, u, a_exp, dt_b)


def _conv_silu_tile(pad_ref, cw_ref, t0, n, seq_start, seq_len, width):
    one_row = width == seq_len
    r = lax.broadcasted_iota(jnp.int32, (n, 1), 0) + (t0 - seq_start)
    col = lax.rem(r, width)
    acc = jnp.zeros((n, LANES), F32)
    for dj in (-1, 0, 1):
        part = jnp.zeros((n, LANES), F32)
        for di in (-1, 0, 1):
            if one_row and di != 0:
                continue
            off = di * width + dj
            tap = pad_ref[pl.ds(CONV_PAD + t0 + off, n), :]
            wrow = cw_ref[(di + 1) * 3 + (dj + 1):(di + 1) * 3 + (dj + 1) + 1, :]
            lo, hi = t0 - seq_start + off, t0 - seq_start + n - 1 + off
            if lo < 0 or hi >= seq_len:
                tap = jnp.where((r + off >= 0) & (r + off < seq_len), tap, 0.0)
            part = part + tap * wrow
        ok_c = (col + dj >= 0) & (col + dj < width)
        acc = acc + jnp.where(ok_c, part, 0.0)
    return acc * jax.nn.sigmoid(acc)


def _bmm(a, b):
    return jnp.einsum('gik,gkj->gij', a.astype(BF), b.astype(BF), preferred_element_type=F32)


def _bmm_nt(a, b):
    return jnp.einsum('gik,gjk->gij', a.astype(BF), b.astype(BF), preferred_element_type=F32)


TRI_BASE = 4


def _tri_inverse(lmat, eye, ii, jj):
    n = lmat.shape[-1]
    same = lambda s: (ii >> (s.bit_length() - 1)) == (jj >> (s.bit_length() - 1))
    x = -jnp.where(same(TRI_BASE), lmat, 0.0)
    t = eye + x
    t = t + _bmm(t, _bmm(x, x))
    s = TRI_BASE
    while s < n:
        e = jnp.where(same(2 * s) & ~same(s), lmat, 0.0)
        t = t - _bmm(t, _bmm(e, t))
        s *= 2
    return t


def _delta_prep_kernel(pq_ref, pk_ref, pv_ref, cq_ref, ck_ref, cv_ref, g_ref, gt_ref,
                       u_ref, w_ref, qd_ref, kd_ref, a_ref, gl_ref,
                       pad_ref, q_s, k_s, v_s, gcr_s, gct_s,
                       *, t_len, ctx_len, n_chunks, group):
    hd = q_s.shape[1]
    tt = TOKEN_TILE
    d = pl.program_id(2)

    @pl.when(d == 0)
    def _():
        zpad = jnp.zeros((CONV_PAD, LANES), F32)
        pad_ref[pl.ds(0, CONV_PAD), :] = zpad
        pad_ref[pl.ds(CONV_PAD + t_len, CONV_PAD), :] = zpad
        for src, cw, dst, kind in ((pq_ref, cq_ref, q_s, "q"), (pk_ref, ck_ref, k_s, "k"), (pv_ref, cv_ref, v_s, "v")):
            pad_ref[pl.ds(CONV_PAD, t_len), :] = src[...].astype(F32)
            for ti in range(t_len // tt):
                t0 = ti * tt
                if t0 < ctx_len:
                    a = _conv_silu_tile(pad_ref, cw, t0, tt, 0, ctx_len, ctx_len)
                else:
                    a = _conv_silu_tile(pad_ref, cw, t0, tt, ctx_len, t_len - ctx_len, GRID_W)
                if kind != "v":
                    a = a * lax.rsqrt(jnp.sum(a * a, axis=-1, keepdims=True) + NORM_EPS)
                if kind == "q":
                    a = a * (hd ** -0.5)
                dst[pl.ds(t0, tt), :] = a

    ii = lax.broadcasted_iota(jnp.int32, (CHUNK, CHUNK), 0)
    jj = lax.broadcasted_iota(jnp.int32, (CHUNK, CHUNK), 1)
    sdiff = (ii - jj) * (1 - 2 * d)
    eye = (ii == jj).astype(F32)
    incl = sdiff >= 0
    strict = sdiff > 0
    gcr_s[...] = _dotx(g_ref[d], (sdiff <= 0).astype(F32))
    gct_s[...] = _dotx(incl.astype(F32), gt_ref[d])
    gtot = jnp.where(d == 0, gcr_s[:, CHUNK - 1:CHUNK], gcr_s[:, 0:1])
    gl_ref[...] = jnp.broadcast_to(jnp.exp(gtot), gl_ref.shape)

    gc = group
    rows = gc * CHUNK
    lane_c = lax.broadcasted_iota(jnp.int32, (CHUNK, LANES), 1)
    eye_hd = (lax.broadcasted_iota(jnp.int32, (hd, hd), 0)
              == lax.broadcasted_iota(jnp.int32, (hd, hd), 1)).astype(BF)[None]

    def body(it, carry):
        c0 = it * gc
        t0 = pl.multiple_of(c0 * CHUNK, rows)
        q = q_s[pl.ds(t0, rows), :].reshape(gc, CHUNK, hd)
        k = k_s[pl.ds(t0, rows), :].reshape(gc, CHUNK, hd)
        v = v_s[pl.ds(t0, rows), :].reshape(gc, CHUNK, hd)
        gct = gct_s[...]
        bt = gt_ref[2 + d]
        gcols, bcols, grows, glasts = [], [], [], []
        for g in range(gc):
            sel = lane_c == c0 + g
            gcols.append(jnp.sum(jnp.where(sel, gct, 0.0), axis=1, keepdims=True))
            bcols.append(jnp.sum(jnp.where(sel, bt, 0.0), axis=1, keepdims=True))
            grow_g = gcr_s[pl.ds(c0 + g, 1), :]
            grows.append(grow_g)
            glasts.append(jnp.where(d == 0, grow_g[:, CHUNK - 1:CHUNK], grow_g[:, 0:1]))
        gcol = jnp.broadcast_to(jnp.stack(gcols), (gc, CHUNK, hd))
        bcol = jnp.broadcast_to(jnp.stack(bcols), (gc, CHUNK, hd))
        grow = jnp.stack(grows)
        glast = jnp.stack(glasts)
        decay = jnp.where(incl, jnp.exp(jnp.where(incl, gcol[:, :, :CHUNK] - grow, 0.0)), 0.0)
        kb = k * bcol
        lmat = jnp.where(strict, _bmm_nt(kb, k) * decay, 0.0)
        tinv = _tri_inverse(lmat, eye[None], ii, jj)
        egc = jnp.exp(gcol)
        uw = _bmm(tinv, jnp.concatenate([v * bcol, kb * egc], axis=2))
        attn = (_bmm_nt(q, k) * decay).astype(BF)
        kdec = k * jnp.exp(glast - gcol)
        kdec_t = _bmm_nt(jnp.broadcast_to(eye_hd, (gc, hd, hd)), kdec).astype(BF)
        u_ref[pl.ds(t0, rows), :] = uw[:, :, :hd].reshape(rows, hd)
        w_ref[pl.ds(t0, rows), :] = uw[:, :, hd:].astype(BF).reshape(rows, hd)
        qd_ref[pl.ds(t0, rows), :] = (q * egc).astype(BF).reshape(rows, hd)
        a_ref[pl.ds(t0, rows), :] = attn.reshape(rows, CHUNK)
        kd_ref[pl.ds(c0, gc)] = kdec_t
        return carry

    lax.fori_loop(0, n_chunks // group, body, 0)


def _delta_prep(p1, conv_w9, g_rows, g_cols, n_heads, ctx_len):
    b, t, _ = p1.shape
    hd = LANES
    nc = t // CHUNK
    ncp = g_rows.shape[3]
    group = max(g for g in range(1, 35) if nc % g == 0)
    h = n_heads
    kern = functools.partial(_delta_prep_kernel, t_len=t, ctx_len=ctx_len, n_chunks=nc, group=group)
    tok = lambda off: pl.BlockSpec((None, t, hd), lambda bi, hi, di: (bi, 0, off + hi))
    cws = lambda off: pl.BlockSpec((9, hd), lambda bi, hi, di: (0, off + hi))
    o5 = lambda last: pl.BlockSpec((None, None, None, t, last), lambda bi, hi, di: (di, bi, hi, 0, 0))
    return pl.pallas_call(
        kern,
        grid=(b, h, 2),
        in_specs=[tok(0), tok(h), tok(2 * h), cws(0), cws(h), cws(2 * h),
                  pl.BlockSpec((None, None, 4, ncp, CHUNK), lambda bi, hi, di: (bi, hi, 0, 0, 0)),
                  pl.BlockSpec((None, None, 4, CHUNK, LANES), lambda bi, hi, di: (bi, hi, 0, 0, 0))],
        out_specs=[o5(hd), o5(hd), o5(hd),
                   pl.BlockSpec((None, None, None, nc, hd, CHUNK), lambda bi, hi, di: (di, bi, hi, 0, 0, 0)),
                   o5(CHUNK),
                   pl.BlockSpec((None, None, None, ncp, LANES), lambda bi, hi, di: (di, bi, hi, 0, 0))],
        out_shape=[jax.ShapeDtypeStruct((2, b, h, t, hd), F32),
                   jax.ShapeDtypeStruct((2, b, h, t, hd), BF),
                   jax.ShapeDtypeStruct((2, b, h, t, hd), BF),
                   jax.ShapeDtypeStruct((2, b, h, nc, hd, CHUNK), BF),
                   jax.ShapeDtypeStruct((2, b, h, t, CHUNK), BF),
                   jax.ShapeDtypeStruct((2, b, h, ncp, LANES), F32)],
        scratch_shapes=[pltpu.VMEM((t + 2 * CONV_PAD, LANES), F32),
                        pltpu.VMEM((t, hd), F32), pltpu.VMEM((t, hd), F32), pltpu.VMEM((t, hd), F32),
                        pltpu.VMEM((ncp, CHUNK), F32), pltpu.VMEM((CHUNK, LANES), F32)],
        compiler_params=_cparams(3),
        name="delta_prep",
    )(p1, p1, p1, conv_w9, conv_w9, conv_w9, g_rows, g_cols)


def _delta_scan_kernel(uf, wf, qf, kf, af, gf, ub, wb, qb, kb, ab, gb, of_ref, ob_ref, s_ref, *, hg):
    @pl.when(pl.program_id(2) == 0)
    def _():
        s_ref[...] = jnp.zeros_like(s_ref)

    cs = uf.shape[1]
    s = s_ref[...]
    for t in range(cs):
        tb = cs - 1 - t
        cat = lambda a, b: jnp.concatenate([a[:, t], b[:, tb]], axis=0)
        sb = s.astype(BF)
        vn = cat(uf, ub) - _bmm(cat(wf, wb), sb)
        vnb = vn.astype(BF)
        o = _bmm(cat(qf, qb), sb) + _bmm(cat(af, ab), vnb)
        s = s * cat(gf, gb) + _bmm(cat(kf, kb), vnb)
        of_ref[:, t] = o[:hg]
        ob_ref[:, tb] = o[hg:]
    s_ref[...] = s


def _delta_scan(u, w, qd, kd, a, gl, ctx_len):
    _, b, h, t, hd = u.shape
    nc = t // CHUNK
    ncc = ctx_len // CHUNK
    hg = _pick(h, (8, 4, 2, 1))
    cs = max(c for c in (4, 2, 1) if ncc % c == 0 and nc % c == 0)
    nblk, nblk_c = nc // cs, ncc // cs
    r6 = lambda arr: arr.reshape(2, b, h, nc, CHUNK, arr.shape[-1])
    u6, w6, q6, k6, a6 = r6(u), r6(w), r6(qd), kd, r6(a)
    gl6 = gl[:, :, :, :nc].reshape(2, b, h, nc, 1, LANES)

    def cf(si):
        return si

    def cb(si):
        return jnp.where(si < nblk_c, nblk_c - 1 - si, nblk - 1 - (si - nblk_c))

    def spec(d, cmap, rows, last):
        return pl.BlockSpec((None, None, hg, cs, rows, last),
                            lambda bi, gi, si: (d, bi, gi, cmap(si), 0, 0))

    ins, specs = [], []
    for d, cmap in ((0, cf), (1, cb)):
        ins += [u6, w6, q6, k6, a6, gl6]
        specs += [spec(d, cmap, CHUNK, hd), spec(d, cmap, CHUNK, hd), spec(d, cmap, CHUNK, hd),
                  spec(d, cmap, hd, CHUNK), spec(d, cmap, CHUNK, CHUNK), spec(d, cmap, 1, LANES)]
    o_shape = jax.ShapeDtypeStruct((b, h, nc, CHUNK, hd), F32)
    ospec = lambda cmap: pl.BlockSpec((None, hg, cs, CHUNK, hd), lambda bi, gi, si: (bi, gi, cmap(si), 0, 0))
    of, ob = pl.pallas_call(
        functools.partial(_delta_scan_kernel, hg=hg),
        grid=(b, h // hg, nblk),
        in_specs=specs,
        out_specs=[ospec(cf), ospec(cb)],
        out_shape=[o_shape, o_shape],
        scratch_shapes=[pltpu.VMEM((2 * hg, hd, hd), F32)],
        compiler_params=_cparams(3),
        name="delta_scan",
    )(*ins)
    return of.reshape(b, h, t, hd), ob.reshape(b, h, t, hd)


def _delta_out_kernel(of_ref, ob_ref, z_ref, g_ref, o_ref):
    hd = of_ref.shape[2]
    for h in range(of_ref.shape[0]):
        o = of_ref[h] + ob_ref[h]
        o = o * lax.rsqrt(jnp.mean(o * o, axis=-1, keepdims=True) + NORM_EPS)
        z = z_ref[:, h * hd:(h + 1) * hd].astype(F32)
        o_ref[:, h * hd:(h + 1) * hd] = (o * g_ref[...] * (z * jax.nn.sigmoid(z))).astype(o_ref.dtype)


def _delta_out(of, ob, p1, o_gain, n_heads):
    b, h, t, hd = of.shape
    tt = TOKEN_TILE
    return pl.pallas_call(
        _delta_out_kernel,
        grid=(b, t // tt),
        in_specs=[pl.BlockSpec((None, h, tt, hd), lambda bi, ti: (bi, 0, ti, 0)),
                  pl.BlockSpec((None, h, tt, hd), lambda bi, ti: (bi, 0, ti, 0)),
                  pl.BlockSpec((None, tt, h * hd), lambda bi, ti: (bi, ti, 3)),
                  pl.BlockSpec((1, hd), lambda bi, ti: (0, 0))],
        out_specs=pl.BlockSpec((None, tt, h * hd), lambda bi, ti: (bi, ti, 0)),
        out_shape=jax.ShapeDtypeStruct((b, t, h * hd), BF),
        compiler_params=_cparams(2),
        name="delta_out",
    )(of, ob, p1, o_gain.reshape(1, hd))


def _hy_conv_kernel(p_ref, w_ref, oc_ref, ox_ref, pad_ref, *, t_len, ctx_len):
    tt = TOKEN_TILE
    z8 = jnp.zeros((8, pad_ref.shape[1]), F32)
    pad_ref[pl.ds(0, 8), :] = z8
    pad_ref[pl.ds(8 + t_len, 8), :] = z8
    pad_ref[pl.ds(8, t_len), :] = p_ref[...].astype(F32)
    w0, w1, w2 = w_ref[0:1, :], w_ref[1:2, :], w_ref[2:3, :]
    for ti in range(t_len // tt):
        t0 = ti * tt
        r = lax.broadcasted_iota(jnp.int32, (tt, 1), 0) + t0
        left = pad_ref[pl.ds(8 + t0 - 1, tt), :]
        mid = pad_ref[pl.ds(8 + t0, tt), :]
        right = pad_ref[pl.ds(8 + t0 + 1, tt), :]
        ok_l = (r != 0) & (r != ctx_len)
        ok_r = (r != ctx_len - 1) & (r != t_len - 1)
        y = jnp.where(ok_l, left, 0.0) * w0 + mid * w1 + jnp.where(ok_r, right, 0.0) * w2
        if t0 < ctx_len:
            oc_ref[pl.ds(t0, tt), :] = y.astype(oc_ref.dtype)
        else:
            ox_ref[pl.ds(t0 - ctx_len, tt), :] = y.astype(ox_ref.dtype)


def _hy_conv(p_tail, hy_conv_w, ctx_len, width3):
    b, t, _ = p_tail.shape
    tc = _pick(width3, (512, 256, 128))
    return pl.pallas_call(
        functools.partial(_hy_conv_kernel, t_len=t, ctx_len=ctx_len),
        grid=(b, width3 // tc),
        in_specs=[pl.BlockSpec((None, t, tc), lambda bi, j: (bi, 0, j)),
                  pl.BlockSpec((3, tc), lambda bi, j: (0, j))],
        out_specs=[pl.BlockSpec((None, ctx_len, tc), lambda bi, j: (bi, 0, j)),
                   pl.BlockSpec((None, t - ctx_len, tc), lambda bi, j: (bi, 0, j))],
        out_shape=[jax.ShapeDtypeStruct((b, ctx_len, width3), BF),
                   jax.ShapeDtypeStruct((b, t - ctx_len, width3), F32)],
        scratch_shapes=[pltpu.VMEM((t + 16, tc), F32)],
        compiler_params=_cparams(2),
        name="hyena_short_conv",
    )(p_tail, hy_conv_w)


def _hy_filter_kernel(ft_ref, w1_ref, b1_ref, f1_ref, w2_ref, b2_ref, f2_ref, w3_ref, dec_ref, o_ref, hid_s,
                      *, mask_lag0):
    part = pl.program_id(0)
    feats = ft_ref[...]

    @pl.when((pl.program_id(2) == 0) & (pl.program_id(3) == 0))
    def _():
        hid = jnp.sin(f1_ref[...] * (_dot3(feats, w1_ref[...]) + b1_ref[...]))
        hid_s[...] = jnp.sin(f2_ref[...] * (_dot3(hid, w2_ref[...]) + b2_ref[...]))

    filt = _bdot(hid_s[...], w3_ref[...])
    tpos = feats[:, 0:1]
    val = filt * jnp.exp(-tpos * jnp.abs(dec_ref[...]))
    if mask_lag0:
        row = lax.broadcasted_iota(jnp.int32, (val.shape[0], 1), 0) + pl.program_id(1) * val.shape[0]
        val = jnp.where((part == 1) & (row == 0), 0.0, val)
    o_ref[...] = val.astype(o_ref.dtype)


def _features(length, n_emb, k_pad, reverse_tail):
    t = jnp.linspace(0.0, 1.0, length, dtype=F32)
    n_bands = (n_emb - 1) // 2
    omega = 2.0 * math.pi * jnp.arange(length, dtype=F32) / length
    bands = jnp.linspace(1e-4, n_bands - 1, n_bands, dtype=F32)
    ang = omega[:, None] * bands[None, :]
    feats = jnp.concatenate([t[:, None], jnp.cos(ang), -jnp.sin(ang)], axis=-1)
    feats = jnp.pad(feats, ((0, 0), (0, k_pad - n_emb)))
    if reverse_tail:
        tail = feats[::-1]
    else:
        tail = jnp.concatenate([jnp.zeros((1, k_pad), F32), feats[:-1]], axis=0)
    return jnp.stack([feats, tail])


def _hy_filter_stack(length, w1, b1, f1, w2, b2, f2, w3, decay, hyw, reverse_tail):
    n_emb, fh = w1.shape
    k_pad = 64 if n_emb <= 64 else _pick(n_emb, (128,))
    feats = _features(length, n_emb, k_pad, reverse_tail)
    w1p = jnp.pad(w1, ((0, k_pad - n_emb), (0, 0)))
    tr = _pick(length, (512, 256, 128))
    tc = _pick(hyw, (2048, 1024, 512, 256, 128))
    nj = hyw // tc
    ni = length // tr
    return pl.pallas_call(
        functools.partial(_hy_filter_kernel, mask_lag0=not reverse_tail),
        grid=(2, ni, 2, nj),
        in_specs=[pl.BlockSpec((None, tr, k_pad), lambda p, i, c, j: (p, i, 0)),
                  pl.BlockSpec((k_pad, fh), lambda p, i, c, j: (0, 0)),
                  pl.BlockSpec((1, fh), lambda p, i, c, j: (0, 0)),
                  pl.BlockSpec((1, fh), lambda p, i, c, j: (0, 0)),
                  pl.BlockSpec((fh, fh), lambda p, i, c, j: (0, 0)),
                  pl.BlockSpec((1, fh), lambda p, i, c, j: (0, 0)),
                  pl.BlockSpec((1, fh), lambda p, i, c, j: (0, 0)),
                  pl.BlockSpec((fh, tc), lambda p, i, c, j: (0, (2 * c + p) * nj + j)),
                  pl.BlockSpec((1, tc), lambda p, i, c, j: (0, (2 * c + p) * nj + j))],
        out_specs=pl.BlockSpec((tr, tc), lambda p, i, c, j: (p * ni + i, c * nj + j)),
        out_shape=jax.ShapeDtypeStruct((2 * length, 2 * hyw), F32 if reverse_tail else BF),
        scratch_shapes=[pltpu.VMEM((tr, fh), F32)],
        compiler_params=_cparams(4),
        name="hyena_filters",
    )(feats, w1p, b1.reshape(1, fh), f1.reshape(1, fh), w2, b2.reshape(1, fh), f2.reshape(1, fh),
      w3, decay.reshape(1, HY_N_FILT * hyw))


def _dft_tables(length, tile):
    n = 2 * length
    half = tile // 2
    rows = jnp.arange(n, dtype=jnp.int32)
    tile_i, r = rows // tile, rows % tile
    is_im = r >= half
    k = tile_i * half + jnp.where(is_im, r - half, r)
    t = jnp.arange(length, dtype=jnp.int32)
    ang = (2.0 * math.pi / n) * ((k[:, None] * t[None, :]) % n).astype(F32)
    nyq = jnp.where(t % 2 == 0, 1.0, -1.0).astype(F32)[None, :]
    is_nyq = (is_im & (k == 0))[:, None]
    fwd = jnp.where(is_im[:, None], -jnp.sin(ang), jnp.cos(ang))
    fwd = jnp.where(is_nyq, nyq, fwd)
    sgn = jnp.where(is_im[:, None] & ~is_nyq, -1.0, 1.0)
    fker = jnp.concatenate([fwd, fwd * sgn], axis=1)
    scale = jnp.where(k == 0, 1.0 / n, 2.0 / n)[None, :]
    inv = jnp.where(is_im[None, :], -jnp.sin(ang.T), jnp.cos(ang.T))
    inv = jnp.where(is_nyq.T, nyq.T, inv) * scale
    return fwd.astype(BF), fker.astype(BF), inv.astype(BF)


def _hy_long_convs(u_seq, kstack, hy_skip, hyw):
    b, ls, _ = u_seq.shape
    n = 2 * ls
    tmf = _pick(n, (512,))
    half = tmf // 2
    tn = _pick(hyw, (512, 256, 128))
    nj = hyw // tn
    fwd, fker, inv = _dft_tables(ls, tmf)

    (hspec,) = _matmul(
        fker, kstack,
        pl.BlockSpec((tmf, n), lambda g, j, i: (i, 0)),
        pl.BlockSpec((n, tn), lambda g, j, i: (0, j)),
        (1, 2 * nj, n // tmf),
        [(jax.ShapeDtypeStruct((n, 2 * hyw), F32), pl.BlockSpec((tmf, tn), lambda g, j, i: (i, j)))],
        _ep_store(F32), name="hyena_filter_dft")

    def spectrum_product(acc, extras, outs):
        hs = extras[0][...]
        re, im = acc[:half], acc[half:]
        hre, him = hs[:half], hs[half:]
        first = (lax.broadcasted_iota(jnp.int32, (half, 1), 0) == 0) & (pl.program_id(2) == 0)
        yre = jnp.where(first, re * hre, re * hre - im * him)
        yim = jnp.where(first, im * him, re * him + im * hre)
        outs[0][pl.ds(0, half), :] = yre.astype(BF)
        outs[0][pl.ds(half, half), :] = yim.astype(BF)

    tmi = _pick(ls, (TOKEN_TILE,))
    skip3 = hy_skip.reshape(2, 1, hyw)

    def conv(z_arr, z_off, conv_idx, mul_off):
        (spec,) = _matmul(
            fwd, z_arr,
            pl.BlockSpec((tmf, ls), lambda g, j, i: (i, 0)),
            pl.BlockSpec((None, ls, tn), lambda g, j, i: (g, 0, z_off + j)),
            (b, nj, n // tmf),
            [(jax.ShapeDtypeStruct((b, n, hyw), BF), pl.BlockSpec((None, tmf, tn), lambda g, j, i: (g, i, j)))],
            spectrum_product,
            extras=[(hspec, pl.BlockSpec((tmf, tn), lambda g, j, i: (i, conv_idx * nj + j)))],
            name="hyena_dft_fwd")

        def finish(acc, extras, outs):
            z = extras[0][...].astype(F32)
            xm = extras[2][...].astype(F32)
            outs[0][...] = (xm * (acc + z * extras[1][...])).astype(BF)

        (y,) = _matmul(
            inv, spec,
            pl.BlockSpec((tmi, n), lambda g, j, i: (i, 0)),
            pl.BlockSpec((None, n, tn), lambda g, j, i: (g, 0, j)),
            (b, nj, ls // tmi),
            [(jax.ShapeDtypeStruct((b, ls, hyw), BF), pl.BlockSpec((None, tmi, tn), lambda g, j, i: (g, i, j)))],
            finish,
            extras=[(z_arr, pl.BlockSpec((None, tmi, tn), lambda g, j, i: (g, i, z_off + j))),
                    (skip3, pl.BlockSpec((None, 1, tn), lambda g, j, i: (conv_idx, 0, j))),
                    (u_seq, pl.BlockSpec((None, tmi, tn), lambda g, j, i: (g, i, mul_off * nj + j)))],
            name="hyena_dft_inv")
        return y

    z2 = conv(u_seq, 0, 0, 1)
    return conv(z2, 0, 1, 2)


FFT_FAST = 64


def _fft_tables(length):
    bf = FFT_FAST
    a_n = length // bf
    na = 2 * a_n
    n = 2 * length
    b = jnp.arange(bf, dtype=jnp.int32)
    k1 = jnp.arange(na, dtype=jnp.int32)
    a = jnp.arange(na, dtype=jnp.int32)
    nn = bf * a[None, None, :] + b[:, None, None]
    ang = (2.0 * math.pi / n) * ((k1[None, :, None] * nn) % n).astype(F32)
    cs, sn = jnp.cos(ang), jnp.sin(ang)
    m1k = jnp.concatenate([cs, -sn], axis=1)
    cd, sd = cs[:, :, :a_n], sn[:, :, :a_n]
    m1 = jnp.concatenate([jnp.concatenate([cd, sd], axis=2),
                          jnp.concatenate([-sd, cd], axis=2)], axis=1)
    ct, st = jnp.swapaxes(cd, 1, 2) / n, jnp.swapaxes(sd, 1, 2) / n
    g2 = jnp.concatenate([jnp.concatenate([ct, -st], axis=2),
                          jnp.concatenate([st, ct], axis=2)], axis=1)
    k2 = jnp.arange(bf, dtype=jnp.int32)
    ang2 = (2.0 * math.pi / bf) * ((k2[:, None] * b[None, :]) % bf).astype(F32)
    c2, s2 = jnp.cos(ang2), jnp.sin(ang2)
    f2 = jnp.concatenate([jnp.concatenate([c2, s2], axis=1), jnp.concatenate([-s2, c2], axis=1)], axis=0)
    i1 = jnp.concatenate([jnp.concatenate([c2.T, -s2.T], axis=1), jnp.concatenate([s2.T, c2.T], axis=1)], axis=0)
    return m1.astype(BF), m1k.astype(BF), f2.astype(BF), i1.astype(BF), g2.astype(BF)


def _fft_s1_kernel(z_ref, m_ref, o_ref):
    for bl in range(m_ref.shape[0]):
        z = jnp.concatenate([z_ref[0, :, bl, :], z_ref[1, :, bl, :]], axis=0).astype(BF)
        o_ref[bl] = jnp.dot(m_ref[bl], z, preferred_element_type=F32)


def _fft_mid_kernel(pr_ref, pi_ref, h_ref, f2_ref, i1_ref, o_ref):
    bf = f2_ref.shape[0] // 2
    for kl in range(h_ref.shape[0]):
        p = jnp.concatenate([pr_ref[:, kl, :], pi_ref[:, kl, :]], axis=0).astype(BF)
        x = jnp.dot(f2_ref[...], p, preferred_element_type=F32)
        hs = h_ref[kl].astype(F32)
        xr, xi = x[:bf], x[bf:]
        hr, hi = hs[:bf], hs[bf:]
        y = jnp.concatenate([xr * hr - xi * hi, xr * hi + xi * hr], axis=0).astype(BF)
        o_ref[kl] = jnp.dot(i1_ref[...], y, preferred_element_type=F32)


def _fft_s3_kernel(qr_ref, qi_ref, g_ref, z_ref, skip_ref, xm_ref, o_ref):
    a_n = z_ref.shape[1]
    for bl in range(g_ref.shape[0]):
        q = jnp.concatenate([qr_ref[:, bl, :], qi_ref[:, bl, :]], axis=0).astype(BF)
        y = jnp.dot(g_ref[bl], q, preferred_element_type=F32)
        for s in range(2):
            ys = y[s * a_n:(s + 1) * a_n]
            o_ref[s, :, bl, :] = xm_ref[s, :, bl, :] * (ys + z_ref[s, :, bl, :] * skip_ref[...])


def _fft_kern_s1_kernel(k_ref, m_ref, o_ref):
    for bl in range(m_ref.shape[0]):
        o_ref[bl] = jnp.dot(m_ref[bl], k_ref[:, bl, :].astype(BF), preferred_element_type=F32)


def _fft_spec_kernel(pr_ref, pi_ref, f2_ref, o_ref):
    for kl in range(o_ref.shape[0]):
        p = jnp.concatenate([pr_ref[:, kl, :], pi_ref[:, kl, :]], axis=0).astype(BF)
        o_ref[kl] = jnp.dot(f2_ref[...], p, preferred_element_type=F32).astype(o_ref.dtype)


def _hy_long_convs_fft(u_seq, kern, hy_skip, hyw):
    nb_in, ls, _ = u_seq.shape
    if nb_in % 2:
        u_seq = jnp.concatenate([u_seq, jnp.zeros_like(u_seq[:1])], axis=0)
    nb = u_seq.shape[0] // 2
    bf = FFT_FAST
    a_n = ls // bf
    na = 2 * a_n
    ra = 2 * a_n
    m1, m1k, f2, i1, g2 = _fft_tables(ls)
    bg = 8
    td = _pick(hyw, (1024, 512, 256, 128))
    ndt = hyw // td

    ph = pl.pallas_call(
        _fft_kern_s1_kernel,
        grid=(bf // bg, 2 * ndt),
        in_specs=[pl.BlockSpec((na, bg, td), lambda b, j: (0, b, j)),
                  pl.BlockSpec((bg, 2 * na, na), lambda b, j: (b, 0, 0))],
        out_specs=pl.BlockSpec((bg, 2 * na, td), lambda b, j: (b, 0, j)),
        out_shape=jax.ShapeDtypeStruct((bf, 2 * na, 2 * hyw), F32),
        compiler_params=_cparams(2),
        name="hyena_fft_kern_s1",
    )(kern.reshape(na, bf, 2 * hyw), m1k)
    hspec = pl.pallas_call(
        _fft_spec_kernel,
        grid=(na // bg, 2 * ndt),
        in_specs=[pl.BlockSpec((bf, bg, td), lambda k, j: (0, k, j)),
                  pl.BlockSpec((bf, bg, td), lambda k, j: (0, na // bg + k, j)),
                  pl.BlockSpec((2 * bf, 2 * bf), lambda k, j: (0, 0))],
        out_specs=pl.BlockSpec((bg, 2 * bf, td), lambda k, j: (k, 0, j)),
        out_shape=jax.ShapeDtypeStruct((na, 2 * bf, 2 * hyw), BF),
        compiler_params=_cparams(2),
        name="hyena_fft_kern_s2",
    )(ph, ph, f2)

    skip3 = hy_skip.reshape(2, 1, hyw)
    u5 = u_seq.reshape(nb, 2, a_n, bf, 3 * hyw)

    def conv(z5, z_sec, conv_idx, mul_sec):
        zspec = pl.BlockSpec((None, 2, a_n, bg, td), lambda g, b, j: (g, 0, 0, b, z_sec * ndt + j))
        p = pl.pallas_call(
            _fft_s1_kernel,
            grid=(nb, bf // bg, ndt),
            in_specs=[zspec, pl.BlockSpec((bg, 2 * na, ra), lambda g, b, j: (b, 0, 0))],
            out_specs=pl.BlockSpec((None, bg, 2 * na, td), lambda g, b, j: (g, b, 0, j)),
            out_shape=jax.ShapeDtypeStruct((nb, bf, 2 * na, hyw), F32),
            compiler_params=_cparams(3),
            name="hyena_fft_s1",
        )(z5, m1)
        q = pl.pallas_call(
            _fft_mid_kernel,
            grid=(na // bg, ndt, nb),
            in_specs=[pl.BlockSpec((None, bf, bg, td), lambda k, j, g: (g, 0, k, j)),
                      pl.BlockSpec((None, bf, bg, td), lambda k, j, g: (g, 0, na // bg + k, j)),
                      pl.BlockSpec((bg, 2 * bf, td), lambda k, j, g: (k, 0, conv_idx * ndt + j)),
                      pl.BlockSpec((2 * bf, 2 * bf), lambda k, j, g: (0, 0)),
                      pl.BlockSpec((2 * bf, 2 * bf), lambda k, j, g: (0, 0))],
            out_specs=pl.BlockSpec((None, bg, 2 * bf, td), lambda k, j, g: (g, k, 0, j)),
            out_shape=jax.ShapeDtypeStruct((nb, na, 2 * bf, hyw), F32),
            compiler_params=_cparams(3),
            name="hyena_fft_mid",
        )(p, p, hspec, f2, i1)
        return pl.pallas_call(
            _fft_s3_kernel,
            grid=(nb, bf // bg, ndt),
            in_specs=[pl.BlockSpec((None, na, bg, td), lambda g, b, j: (g, 0, b, j)),
                      pl.BlockSpec((None, na, bg, td), lambda g, b, j: (g, 0, bf // bg + b, j)),
                      pl.BlockSpec((bg, ra, 2 * na), lambda g, b, j: (b, 0, 0)),
                      zspec,
                      pl.BlockSpec((None, 1, td), lambda g, b, j: (conv_idx, 0, j)),
                      pl.BlockSpec((None, 2, a_n, bg, td), lambda g, b, j: (g, 0, 0, b, mul_sec * ndt + j))],
            out_specs=pl.BlockSpec((None, 2, a_n, bg, td), lambda g, b, j: (g, 0, 0, b, j)),
            out_shape=jax.ShapeDtypeStruct((nb, 2, a_n, bf, hyw), F32),
            compiler_params=_cparams(3),
            name="hyena_fft_s3",
        )(q, q, g2, z5, skip3, u5)

    z_mid = conv(u5, 0, 0, 1)
    y = conv(z_mid, 0, 1, 2)
    return y.reshape(2 * nb, ls, hyw)[:nb_in]


def _moe_up_kernel(x_ref, wg_ref, wu_ref, comb_ref, o_ref, *, es):
    x = x_ref[...]
    g = jnp.dot(x, wg_ref[...], preferred_element_type=F32)
    u = jnp.dot(x, wu_ref[...], preferred_element_type=F32)
    comb = comb_ref[...]
    lane = lax.broadcasted_iota(jnp.int32, comb.shape, 1)
    f = g.shape[1] // es
    col = lax.broadcasted_iota(jnp.int32, (1, g.shape[1]), 1)
    scale = jnp.zeros_like(g)
    for s in range(es):
        e = pl.program_id(0) * es + s
        sc = jnp.sum(jnp.where(lane == e, comb, 0.0), axis=1, keepdims=True)
        scale = jnp.where((col >= s * f) & (col < (s + 1) * f), sc, scale)
    o_ref[...] = (g * jax.nn.sigmoid(g) * u * scale).astype(o_ref.dtype)


MOE_EXPERTS_PER_STEP = 4


def _moe_group_weights(w):
    nl, ne, d, f = w.shape
    es = MOE_EXPERTS_PER_STEP
    w = w.reshape(nl, ne // es, es, d, f).transpose(0, 1, 3, 2, 4)
    return w.reshape(nl, ne // es, d, es * f).astype(BF)


def _moe_up(u, w_gate, w_up, comb, l):
    m, d = u.shape
    _, ng, _, nf = w_gate.shape
    tm = _pick(m, (544, 512, 256))
    wspec = pl.BlockSpec((None, None, d, nf), lambda e, i: (l, e, 0, 0))
    return pl.pallas_call(
        functools.partial(_moe_up_kernel, es=MOE_EXPERTS_PER_STEP),
        grid=(ng, m // tm),
        in_specs=[pl.BlockSpec((tm, d), lambda e, i: (i, 0)), wspec, wspec,
                  pl.BlockSpec((tm, LANES), lambda e, i: (i, 0))],
        out_specs=pl.BlockSpec((tm, nf), lambda e, i: (i, e)),
        out_shape=jax.ShapeDtypeStruct((m, ng * nf), BF),
        compiler_params=_cparams(2),
        name="moe_up",
    )(u, w_gate, w_up, comb)


def _row_gate(gt_ref, tm, t_len, ctx_len, n_batch):
    r = lax.broadcasted_iota(jnp.int32, (tm, 1), 0) + pl.program_id(2) * tm
    gate = jnp.zeros((tm, gt_ref.shape[1]), F32)
    for bi in range(n_batch):
        in_b = (r >= bi * t_len) & (r < (bi + 1) * t_len)
        is_c = r < bi * t_len + ctx_len
        gate = jnp.where(in_b & is_c, gt_ref[0:1, :], gate)
        gate = jnp.where(in_b & ~is_c, gt_ref[1 + bi:2 + bi, :], gate)
    return gate


def _final_norm_kernel(h_ref, g_ref, o_ref):
    xf = h_ref[...]
    o_ref[...] = xf * lax.rsqrt(jnp.mean(xf * xf, axis=-1, keepdims=True) + NORM_EPS) * g_ref[...]


def _final_norm(h, gain, n_batch, t_len, ctx_len):
    m, d = h.shape
    tt = TOKEN_TILE
    n_t, n_c = t_len // tt, ctx_len // tt
    n_x = n_t - n_c
    return pl.pallas_call(
        _final_norm_kernel,
        grid=(n_batch, n_x),
        in_specs=[pl.BlockSpec((tt, d), lambda bi, i: (bi * n_t + n_c + i, 0)),
                  pl.BlockSpec((1, d), lambda bi, i: (0, 0))],
        out_specs=pl.BlockSpec((None, tt, d), lambda bi, i: (bi, i, 0)),
        out_shape=jax.ShapeDtypeStruct((n_batch, t_len - ctx_len, d), F32),
        compiler_params=_cparams(2),
        name="final_norm",
    )(h, gain.reshape(1, d))


def kernel(x, c, ctx, c_ctx, ada_down, ada_up, ada_bias, norm_mix, norm_ffn, w_in, qkv_conv, a_log, dt_bias, o_norm, hy_conv, hy_w1, hy_b1, hy_f1, hy_w2, hy_b2, hy_f2, hy_w3, hy_decay, hy_skip, w_br_a, w_br_b, w_out, router_group, router_group_bias, router_expert, router_expert_bias, exp_gate, exp_up, exp_down, final_norm):
    n_batch, seq, d = x.shape
    ctx_len = ctx.shape[1]
    t_len = ctx_len + seq
    m = n_batch * t_len
    depth = w_in.shape[0]
    n_heads = a_log.shape[2]
    a_width = n_heads * LANES
    hyw = hy_skip.shape[2]
    n_exp = exp_gate.shape[1]
    dg_off = 4 * a_width
    hy_off = dg_off + 4 * n_heads
    tail_w = 3 * hyw + 2 * d
    n_t, n_c = t_len // TOKEN_TILE, ctx_len // TOKEN_TILE
    nc = t_len // CHUNK
    ncp = -(-nc // 8) * 8

    h = jnp.concatenate([ctx, x], axis=1).reshape(m, d)

    conds = jnp.zeros((8, d), F32).at[0].set(c_ctx).at[1:1 + n_batch].set(c)
    mod_all = _adaln_all(conds, ada_down, ada_up, ada_bias)
    mod_all = mod_all.reshape(depth, 8, 6, d)[:, :1 + n_batch]
    mod_all = jnp.pad(mod_all, ((0, 0), (0, 0), (0, 2), (0, 0)))

    tm = _pick(m, (1088, 512, 256))
    tn = 512

    w_in_b = w_in.astype(BF)
    tail_base = (hy_off // LANES) * LANES
    tnp = 1024 if (dg_off % 1024 == 0 and tail_w % 1024 == 0 and tail_base % 1024 == 0) else tn
    exp_gate_b = _moe_group_weights(exp_gate)
    exp_up_b = _moe_group_weights(exp_up)

    def gate_table(mod_l, idx):
        return jnp.pad(mod_l[:, idx, :], ((0, 8 - (1 + n_batch)), (0, 0)))

    def residual_ep(acc, extras, outs):
        outs[0][...] = extras[0][...] + _row_gate(extras[1], acc.shape[0], t_len, ctx_len, n_batch) * acc

    for l in range(depth):
        mod_l = mod_all[l]
        u = _norm_mod(h, norm_mix[l], mod_l, 0, n_t, n_c)
        (p1,) = _matmul(
            u, w_in_b,
            pl.BlockSpec((tm, d), lambda g, j, i: (i, 0)),
            pl.BlockSpec((None, d, tnp), lambda g, j, i, l=l: (l, 0, j)),
            (1, dg_off // tnp, m // tm),
            [(jax.ShapeDtypeStruct((m, dg_off), BF), pl.BlockSpec((tm, tnp), lambda g, j, i: (i, j)))],
            _ep_store(BF), name="proj_qkvz")
        tmt = _pick(m, (544, 512, 256))
        (p_tail,) = _matmul(
            u, w_in_b,
            pl.BlockSpec((tmt, d), lambda g, j, i: (i, 0)),
            pl.BlockSpec((None, d, tnp), lambda g, j, i, l=l: (l, 0, tail_base // tnp + j)),
            (1, tail_w // tnp, m // tmt),
            [(jax.ShapeDtypeStruct((m, tail_w), BF), pl.BlockSpec((tmt, tnp), lambda g, j, i: (i, j)))],
            _ep_store(BF), name="proj_tail",
            w_next_spec=pl.BlockSpec((None, d, LANES),
                                     lambda g, j, i, l=l: (l, 0, (tail_base + (j + 1) * tnp) // LANES)),
            lane_shift=hy_off - tail_base)
        w_dg_t = w_in[l, :, dg_off:hy_off].T
        a_exp = jnp.concatenate([jnp.exp(a_log[l].reshape(-1)), jnp.zeros((2 * n_heads,), F32)])[:, None]
        dt_b = jnp.concatenate([dt_bias[l].reshape(-1), jnp.zeros((2 * n_heads,), F32)])[:, None]
        gates = _gates(u, w_dg_t, a_exp, dt_b)

        g5 = gates.reshape(4, n_heads, n_batch, nc, CHUNK).transpose(2, 1, 0, 3, 4)
        g_rows = jnp.pad(g5, ((0, 0), (0, 0), (0, 0), (0, ncp - nc), (0, 0)))
        g_cols = jnp.pad(jnp.swapaxes(g5, 3, 4), ((0, 0), (0, 0), (0, 0), (0, 0), (0, LANES - nc)))

        p1_3 = p1.reshape(n_batch, t_len, dg_off)
        conv9 = qkv_conv[l].reshape(9, 3 * a_width)
        du, dw, dq, dk, da, dgl = _delta_prep(p1_3, conv9, g_rows, g_cols, n_heads, ctx_len)
        o_f, o_b = _delta_scan(du, dw, dq, dk, da, dgl, ctx_len)
        o_a = _delta_out(o_f, o_b, p1_3, o_norm[l], n_heads).reshape(m, a_width)

        p_tail3 = p_tail.reshape(n_batch, t_len, tail_w)
        u_c, u_x = _hy_conv(p_tail3, hy_conv[l], ctx_len, 3 * hyw)
        filt_args = (hy_w1[l], hy_b1[l], hy_f1[l], hy_w2[l], hy_b2[l], hy_f2[l], hy_w3[l], hy_decay[l])
        y_x = _hy_long_convs_fft(u_x, _hy_filter_stack(seq, *filt_args, hyw, True), hy_skip[l], hyw)
        if l < depth - 1:
            y_c = _hy_long_convs(u_c, _hy_filter_stack(ctx_len, *filt_args, hyw, False), hy_skip[l], hyw)
        else:
            y_c = jnp.zeros((n_batch, ctx_len, hyw), BF)
        y_b = jnp.concatenate([y_c, y_x.astype(BF)], axis=1).reshape(m, hyw)

        ga_blk = (3 * hyw) // tn
        gb_blk = (3 * hyw + d) // tn

        merged = pl.pallas_call(
            _merge_kernel,
            grid=(d // tn, m // tm),
            in_specs=[pl.BlockSpec((tm, a_width), lambda j, i: (i, 0)),
                      pl.BlockSpec((tm, hyw), lambda j, i: (i, 0)),
                      pl.BlockSpec((None, a_width, tn), lambda j, i, l=l: (l, 0, j)),
                      pl.BlockSpec((None, hyw, tn), lambda j, i, l=l: (l, 0, j)),
                      pl.BlockSpec((tm, tn), lambda j, i: (i, ga_blk + j)),
                      pl.BlockSpec((tm, tn), lambda j, i: (i, gb_blk + j))],
            out_specs=pl.BlockSpec((tm, tn), lambda j, i: (i, j)),
            out_shape=jax.ShapeDtypeStruct((m, d), BF),
            scratch_shapes=[pltpu.VMEM((a_width, tn), BF), pltpu.VMEM((hyw, tn), BF)],
            compiler_params=_cparams(2),
            name="branch_merge",
        )(o_a, y_b, w_br_a, w_br_b, p_tail, p_tail)

        (h,) = _matmul(
            merged, w_out,
            pl.BlockSpec((tm, d), lambda g, j, i: (i, 0)),
            pl.BlockSpec((None, d, tn), lambda g, j, i, l=l: (l, 0, j)),
            (1, d // tn, m // tm),
            [(jax.ShapeDtypeStruct((m, d), F32), pl.BlockSpec((tm, tn), lambda g, j, i: (i, j)))],
            residual_ep,
            extras=[(h, pl.BlockSpec((tm, tn), lambda g, j, i: (i, j))),
                    (gate_table(mod_l, 2), pl.BlockSpec((8, tn), lambda g, j, i: (0, j)))],
            name="out_proj")

        w_router = jnp.pad(jnp.concatenate([router_expert[l], router_group[l]], axis=1),
                           ((0, 0), (0, LANES - n_exp - N_GROUPS)))
        b_router = jnp.pad(jnp.concatenate([router_expert_bias[l], router_group_bias[l]]),
                           (0, LANES - n_exp - N_GROUPS)).reshape(1, LANES)
        u2, comb = _norm_mod_router(h, norm_ffn[l], mod_l, 3, n_t, n_c, w_router, b_router, n_exp, N_GROUPS)
        hid = _moe_up(u2, exp_gate_b, exp_up_b, comb, l)
        kf = hid.shape[1]
        tnd = _pick(d, (512, 256, 128))
        tmd = _pick(m, (544, 512, 256))
        (h,) = _matmul(
            hid, exp_down.reshape(depth, kf, d),
            pl.BlockSpec((tmd, kf), lambda g, j, i: (i, 0)),
            pl.BlockSpec((None, kf, tnd), lambda g, j, i, l=l: (l, 0, j)),
            (1, d // tnd, m // tmd),
            [(jax.ShapeDtypeStruct((m, d), F32), pl.BlockSpec((tmd, tnd), lambda g, j, i: (i, j)))],
            residual_ep,
            extras=[(h, pl.BlockSpec((tmd, tnd), lambda g, j, i: (i, j))),
                    (gate_table(mod_l, 5), pl.BlockSpec((8, tnd), lambda g, j, i: (0, j)))],
            name="moe_down")

    return _final_norm(h, final_norm, n_batch, t_len, ctx_len)
```

```python
import functools
import math

import jax
import jax.numpy as jnp
from jax import lax
from jax.experimental import pallas as pl
from jax.experimental.pallas import tpu as pltpu

F32 = jnp.float32
BF = jnp.bfloat16

GRID_W = 64
CHUNK = 64
N_GROUPS = 4
EXPERTS_PER_GROUP = 8
HY_N_FILT = 4
NORM_EPS = 1e-6
TOKEN_TILE = 256
LANES = 128
CONV_PAD = 72
VMEM_LIMIT = 56 * 1024 * 1024


def _cparams(n_grid):
    return pltpu.CompilerParams(dimension_semantics=("arbitrary",) * n_grid,
                                vmem_limit_bytes=VMEM_LIMIT)


def _split_bf16(a):
    hi = a.astype(BF)
    lo = (a - hi.astype(F32)).astype(BF)
    return hi, lo


def _dot3(a, b):
    ah, al = _split_bf16(a)
    bh, bl = _split_bf16(b)
    d = functools.partial(jnp.dot, preferred_element_type=F32)
    return d(ah, bh) + (d(ah, bl) + d(al, bh))


def _dotx(a, b):
    return jnp.dot(a, b, precision=lax.Precision.HIGHEST, preferred_element_type=F32)


def _bdot(a, b):
    return jnp.dot(a.astype(BF), b.astype(BF), preferred_element_type=F32)


def _mm_kernel(*refs, n_extra, n_out, stage_w, lane_shift, prologue, epilogue):
    x_ref, w_ref = refs[0], refs[1]
    n_w = 3 if lane_shift else 2
    extras = refs[n_w:n_w + n_extra]
    outs = refs[n_w + n_extra:n_w + n_extra + n_out]
    scratch = refs[n_w + n_extra + n_out:]
    if stage_w:
        wbf = scratch[0]

        @pl.when(pl.program_id(2) == 0)
        def _():
            k = w_ref.shape[0]
            step = 512 if k % 512 == 0 else k

            def load(r0):
                wa = w_ref[pl.ds(r0, step), :]
                if lane_shift:
                    wa = jnp.concatenate([wa[:, lane_shift:], refs[2][pl.ds(r0, step), :lane_shift]], axis=1)
                return wa.astype(BF)

            if k == step:
                wbf[...] = load(0)
            else:
                def it(r, c):
                    r0 = pl.multiple_of(r * step, step)
                    wbf[pl.ds(r0, step), :] = load(r0)
                    return c
                lax.fori_loop(0, k // step, it, 0)
        w = wbf[...]
    else:
        w = w_ref[...]
    x = x_ref[...]
    if prologue is not None:
        x = prologue(x)
    if x.dtype != BF:
        x = x.astype(BF)
    acc = jnp.dot(x, w, preferred_element_type=F32)
    epilogue(acc, extras, outs)


def _matmul(x, w, x_spec, w_spec, grid, outs, epilogue, extras=(), prologue=None, name=None,
            w_next_spec=None, lane_shift=0):
    stage_w = w.dtype != BF or lane_shift > 0
    kw, tn = [d for d in w_spec.block_shape if d is not None]
    scratch = [pltpu.VMEM((kw, tn), BF)] if stage_w else []
    body = functools.partial(_mm_kernel, n_extra=len(extras), n_out=len(outs), stage_w=stage_w,
                             lane_shift=lane_shift, prologue=prologue, epilogue=epilogue)
    w_ops, w_specs = ([w, w], [w_spec, w_next_spec]) if lane_shift else ([w], [w_spec])
    res = pl.pallas_call(
        body,
        grid=grid,
        in_specs=[x_spec] + w_specs + [s for _, s in extras],
        out_specs=[s for _, s in outs],
        out_shape=[o for o, _ in outs],
        scratch_shapes=scratch,
        compiler_params=_cparams(3),
        name=name,
    )(x, *w_ops, *[a for a, _ in extras])
    return res


def _merge_kernel(xa_ref, xb_ref, wa_ref, wb_ref, ga_ref, gb_ref, o_ref, wa_bf, wb_bf):
    @pl.when(pl.program_id(1) == 0)
    def _():
        wa_bf[...] = wa_ref[...].astype(BF)
        wb_bf[...] = wb_ref[...].astype(BF)
    a = jnp.dot(xa_ref[...], wa_bf[...], preferred_element_type=F32)
    b = jnp.dot(xb_ref[...], wb_bf[...], preferred_element_type=F32)
    o_ref[...] = (jax.nn.sigmoid(ga_ref[...].astype(F32)) * a
                  + jax.nn.sigmoid(gb_ref[...].astype(F32)) * b).astype(o_ref.dtype)


def _ep_store(dtype):
    def ep(acc, extras, outs):
        outs[0][...] = acc.astype(dtype)
    return ep


def _pick(n, prefs):
    for p in prefs:
        if n % p == 0:
            return p
    return n


def _adaln_all(conds, ada_down, ada_up, ada_bias):
    nl, d, r = ada_down.shape
    n6 = ada_up.shape[2]
    (t,) = _matmul(
        conds, ada_down,
        pl.BlockSpec((8, d), lambda g, j, i: (0, 0)),
        pl.BlockSpec((None, d, r), lambda g, j, i: (g, 0, 0)),
        (nl, 1, 1),
        [(jax.ShapeDtypeStruct((nl, 8, r), F32), pl.BlockSpec((None, 8, r), lambda g, j, i: (g, 0, 0)))],
        _ep_store(F32), prologue=lambda v: v * jax.nn.sigmoid(v), name="adaln_down")
    tn = _pick(n6, (2048, 1024, 512, 256, 128))

    def ep(acc, extras, outs):
        outs[0][...] = acc + extras[0][...]

    (mod,) = _matmul(
        t, ada_up,
        pl.BlockSpec((None, 8, r), lambda g, j, i: (g, 0, 0)),
        pl.BlockSpec((None, r, tn), lambda g, j, i: (g, 0, j)),
        (nl, n6 // tn, 1),
        [(jax.ShapeDtypeStruct((nl, 8, n6), F32), pl.BlockSpec((None, 8, tn), lambda g, j, i: (g, 0, j)))],
        ep, extras=[(ada_bias.reshape(nl, 1, n6), pl.BlockSpec((None, 1, tn), lambda g, j, i: (g, 0, j)))],
        name="adaln_up")
    return mod


def _mod_row_map(n_t, n_c):
    def m(i):
        return (jnp.where(i % n_t < n_c, 0, 1 + i // n_t), 0, 0)
    return m


def _norm_mod_kernel(h_ref, g_ref, mod_ref, u_ref, *, off):
    xf = h_ref[...]
    y = xf * lax.rsqrt(jnp.mean(xf * xf, axis=-1, keepdims=True) + NORM_EPS) * g_ref[...]
    sh = mod_ref[off:off + 1, :]
    sc = mod_ref[off + 1:off + 2, :]
    u_ref[...] = (y * (1.0 + sc) + sh).astype(u_ref.dtype)


def _norm_mod(h, gain, mod_l, off, n_t, n_c):
    m, d = h.shape
    tt = TOKEN_TILE
    return pl.pallas_call(
        functools.partial(_norm_mod_kernel, off=off),
        grid=(m // tt,),
        in_specs=[pl.BlockSpec((tt, d), lambda i: (i, 0)),
                  pl.BlockSpec((1, d), lambda i: (0, 0)),
                  pl.BlockSpec((None, 8, d), _mod_row_map(n_t, n_c))],
        out_specs=pl.BlockSpec((tt, d), lambda i: (i, 0)),
        out_shape=jax.ShapeDtypeStruct((m, d), BF),
        compiler_params=_cparams(1),
        name="norm_mod",
    )(h, gain.reshape(1, d), mod_l)


def _router_kernel(h_ref, g_ref, mod_ref, wr_ref, br_ref, u_ref, comb_ref, *, off, n_exp, n_grp):
    xf = h_ref[...]
    y = xf * lax.rsqrt(jnp.mean(xf * xf, axis=-1, keepdims=True) + NORM_EPS) * g_ref[...]
    sh = mod_ref[off:off + 1, :]
    sc = mod_ref[off + 1:off + 2, :]
    u = y * (1.0 + sc) + sh
    u_ref[...] = u.astype(u_ref.dtype)
    logits = _dot3(u, wr_ref[...]) + br_ref[...]
    lane = lax.broadcasted_iota(jnp.int32, logits.shape, 1)
    neg = jnp.float32(-jnp.inf)
    big = jnp.int32(1 << 20)
    is_g = (lane >= n_exp) & (lane < n_exp + n_grp)
    lg = jnp.where(is_g, logits, neg)
    eg = jnp.exp(lg - jnp.max(lg, axis=1, keepdims=True))
    pg = eg / jnp.sum(eg, axis=1, keepdims=True)
    p_top = jnp.max(pg, axis=1, keepdims=True)
    g_idx = jnp.min(jnp.where(is_g & (pg == p_top), lane, big), axis=1, keepdims=True) - n_exp
    per = n_exp // n_grp
    in_grp = (lane >= g_idx * per) & (lane < (g_idx + 1) * per)
    le = jnp.where(in_grp, logits, neg)
    v1 = jnp.max(le, axis=1, keepdims=True)
    i1 = jnp.min(jnp.where(le == v1, lane, big), axis=1, keepdims=True)
    le2 = jnp.where(lane == i1, neg, le)
    v2 = jnp.max(le2, axis=1, keepdims=True)
    i2 = jnp.min(jnp.where(le2 == v2, lane, big), axis=1, keepdims=True)
    e2 = jnp.exp(v2 - v1)
    w1 = p_top / (1.0 + e2)
    w2 = p_top * e2 / (1.0 + e2)
    comb_ref[...] = jnp.where(lane == i1, w1, 0.0) + jnp.where(lane == i2, w2, 0.0)


def _norm_mod_router(h, gain, mod_l, off, n_t, n_c, w_router, b_router, n_exp, n_grp):
    m, d = h.shape
    tt = TOKEN_TILE
    return pl.pallas_call(
        functools.partial(_router_kernel, off=off, n_exp=n_exp, n_grp=n_grp),
        grid=(m // tt,),
        in_specs=[pl.BlockSpec((tt, d), lambda i: (i, 0)),
                  pl.BlockSpec((1, d), lambda i: (0, 0)),
                  pl.BlockSpec((None, 8, d), _mod_row_map(n_t, n_c)),
                  pl.BlockSpec((d, LANES), lambda i: (0, 0)),
                  pl.BlockSpec((1, LANES), lambda i: (0, 0))],
        out_specs=[pl.BlockSpec((tt, d), lambda i: (i, 0)),
                   pl.BlockSpec((tt, LANES), lambda i: (i, 0))],
        out_shape=[jax.ShapeDtypeStruct((m, d), BF), jax.ShapeDtypeStruct((m, LANES), F32)],
        compiler_params=_cparams(1),
        name="norm_mod_router",
    )(h, gain.reshape(1, d), mod_l, w_router, b_router)


def _gates_kernel(w_ref, u_ref, aexp_ref, dtb_ref, o_ref, *, n_decay):
    acc = lax.dot_general(w_ref[...].astype(BF), u_ref[...], (((1,), (1,)), ((), ())),
                          preferred_element_type=F32)
    row = lax.broadcasted_iota(jnp.int32, acc.shape, 0)
    z = acc + dtb_ref[...]
    softplus = jnp.maximum(z, 0.0) + jnp.log(1.0 + jnp.exp(-jnp.abs(z)))
    g = -aexp_ref[...] * softplus
    o_ref[...] = jnp.where(row < n_decay, g, jax.nn.sigmoid(acc))


def _gates(u, w_dg_t, a_exp, dt_b):
    m, d = u.shape
    r = w_dg_t.shape[0]
    tm = _pick(m, (512, 256))
    return pl.pallas_call(
        functools.partial(_gates_kernel, n_decay=r // 2),
        grid=(m // tm,),
        in_specs=[pl.BlockSpec((r, d), lambda i: (0, 0)),
                  pl.BlockSpec((tm, d), lambda i: (i, 0)),
                  pl.BlockSpec((r, 1), lambda i: (0, 0)),
                  pl.BlockSpec((r, 1), lambda i: (0, 0))],
        out_specs=pl.BlockSpec((r, tm), lambda i: (0, i)),
        out_shape=jax.ShapeDtypeStruct((r, m), F32),
        compiler_params=_cparams(1),
        name="delta_gates",
    )(w_dg_t, u, a_exp, dt_b)


def _conv_silu_tile(pad_ref, cw_ref, t0, n, seq_start, seq_len, width):
    one_row = width == seq_len
    r = lax.broadcasted_iota(jnp.int32, (n, 1), 0) + (t0 - seq_start)
    col = lax.rem(r, width)
    acc = jnp.zeros((n, LANES), F32)
    for dj in (-1, 0, 1):
        part = jnp.zeros((n, LANES), F32)
        for di in (-1, 0, 1):
            if one_row and di != 0:
                continue
            off = di * width + dj
            tap = pad_ref[pl.ds(CONV_PAD + t0 + off, n), :]
            wrow = cw_ref[(di + 1) * 3 + (dj + 1):(di + 1) * 3 + (dj + 1) + 1, :]
            lo, hi = t0 - seq_start + off, t0 - seq_start + n - 1 + off
            if lo < 0 or hi >= seq_len:
                tap = jnp.where((r + off >= 0) & (r + off < seq_len), tap, 0.0)
            part = part + tap * wrow
        ok_c = (col + dj >= 0) & (col + dj < width)
        acc = acc + jnp.where(ok_c, part, 0.0)
    return acc * jax.nn.sigmoid(acc)


def _bmm(a, b):
    return jnp.einsum('gik,gkj->gij', a.astype(BF), b.astype(BF), preferred_element_type=F32)


def _bmm_nt(a, b):
    return jnp.einsum('gik,gjk->gij', a.astype(BF), b.astype(BF), preferred_element_type=F32)


TRI_BASE = 4


def _tri_inverse(lmat, eye, ii, jj):
    n = lmat.shape[-1]
    same = lambda s: (ii >> (s.bit_length() - 1)) == (jj >> (s.bit_length() - 1))
    x = -jnp.where(same(TRI_BASE), lmat, 0.0)
    t = eye + x
    t = t + _bmm(t, _bmm(x, x))
    s = TRI_BASE
    while s < n:
        e = jnp.where(same(2 * s) & ~same(s), lmat, 0.0)
        t = t - _bmm(t, _bmm(e, t))
        s *= 2
    return t


def _delta_prep_kernel(pq_ref, pk_ref, pv_ref, cq_ref, ck_ref, cv_ref, g_ref, gt_ref,
                       u_ref, w_ref, qd_ref, kd_ref, a_ref, gl_ref,
                       pad_ref, q_s, k_s, v_s, gcr_s, gct_s,
                       *, t_len, ctx_len, n_chunks, group):
    hd = q_s.shape[1]
    tt = TOKEN_TILE
    d = pl.program_id(2)

    @pl.when(d == 0)
    def _():
        zpad = jnp.zeros((CONV_PAD, LANES), F32)
        pad_ref[pl.ds(0, CONV_PAD), :] = zpad
        pad_ref[pl.ds(CONV_PAD + t_len, CONV_PAD), :] = zpad
        for src, cw, dst, kind in ((pq_ref, cq_ref, q_s, "q"), (pk_ref, ck_ref, k_s, "k"), (pv_ref, cv_ref, v_s, "v")):
            pad_ref[pl.ds(CONV_PAD, t_len), :] = src[...].astype(F32)
            for ti in range(t_len // tt):
                t0 = ti * tt
                if t0 < ctx_len:
                    a = _conv_silu_tile(pad_ref, cw, t0, tt, 0, ctx_len, ctx_len)
                else:
                    a = _conv_silu_tile(pad_ref, cw, t0, tt, ctx_len, t_len - ctx_len, GRID_W)
                if kind != "v":
                    a = a * lax.rsqrt(jnp.sum(a * a, axis=-1, keepdims=True) + NORM_EPS)
                if kind == "q":
                    a = a * (hd ** -0.5)
                dst[pl.ds(t0, tt), :] = a

    ii = lax.broadcasted_iota(jnp.int32, (CHUNK, CHUNK), 0)
    jj = lax.broadcasted_iota(jnp.int32, (CHUNK, CHUNK), 1)
    sdiff = (ii - jj) * (1 - 2 * d)
    eye = (ii == jj).astype(F32)
    incl = sdiff >= 0
    strict = sdiff > 0
    gcr_s[...] = _dotx(g_ref[d], (sdiff <= 0).astype(F32))
    gct_s[...] = _dotx(incl.astype(F32), gt_ref[d])
    gtot = jnp.where(d == 0, gcr_s[:, CHUNK - 1:CHUNK], gcr_s[:, 0:1])
    gl_ref[...] = jnp.broadcast_to(jnp.exp(gtot), gl_ref.shape)

    gc = group
    rows = gc * CHUNK
    lane_c = lax.broadcasted_iota(jnp.int32, (CHUNK, LANES), 1)
    eye_hd = (lax.broadcasted_iota(jnp.int32, (hd, hd), 0)
              == lax.broadcasted_iota(jnp.int32, (hd, hd), 1)).astype(BF)[None]

    def body(it, carry):
        c0 = it * gc
        t0 = pl.multiple_of(c0 * CHUNK, rows)
        q = q_s[pl.ds(t0, rows), :].reshape(gc, CHUNK, hd)
        k = k_s[pl.ds(t0, rows), :].reshape(gc, CHUNK, hd)
        v = v_s[pl.ds(t0, rows), :].reshape(gc, CHUNK, hd)
        gct = gct_s[...]
        bt = gt_ref[2 + d]
        gcols, bcols, grows, glasts = [], [], [], []
        for g in range(gc):
            sel = lane_c == c0 + g
            gcols.append(jnp.sum(jnp.where(sel, gct, 0.0), axis=1, keepdims=True))
            bcols.append(jnp.sum(jnp.where(sel, bt, 0.0), axis=1, keepdims=True))
            grow_g = gcr_s[pl.ds(c0 + g, 1), :]
            grows.append(grow_g)
            glasts.append(jnp.where(d == 0, grow_g[:, CHUNK - 1:CHUNK], grow_g[:, 0:1]))
        gcol = jnp.broadcast_to(jnp.stack(gcols), (gc, CHUNK, hd))
        bcol = jnp.broadcast_to(jnp.stack(bcols), (gc, CHUNK, hd))
        grow = jnp.stack(grows)
        glast = jnp.stack(glasts)
        decay = jnp.where(incl, jnp.exp(jnp.where(incl, gcol[:, :, :CHUNK] - grow, 0.0)), 0.0)
        kb = k * bcol
        lmat = jnp.where(strict, _bmm_nt(kb, k) * decay, 0.0)
        tinv = _tri_inverse(lmat, eye[None], ii, jj)
        egc = jnp.exp(gcol)
        uw = _bmm(tinv, jnp.concatenate([v * bcol, kb * egc], axis=2))
        attn = (_bmm_nt(q, k) * decay).astype(BF)
        kdec = k * jnp.exp(glast - gcol)
        kdec_t = _bmm_nt(jnp.broadcast_to(eye_hd, (gc, hd, hd)), kdec).astype(BF)
        u_ref[pl.ds(t0, rows), :] = uw[:, :, :hd].reshape(rows, hd)
        w_ref[pl.ds(t0, rows), :] = uw[:, :, hd:].astype(BF).reshape(rows, hd)
        qd_ref[pl.ds(t0, rows), :] = (q * egc).astype(BF).reshape(rows, hd)
        a_ref[pl.ds(t0, rows), :] = attn.reshape(rows, CHUNK)
        kd_ref[pl.ds(c0, gc)] = kdec_t
        return carry

    lax.fori_loop(0, n_chunks // group, body, 0)


def _delta_prep(p1, conv_w9, g_rows, g_cols, n_heads, ctx_len):
    b, t, _ = p1.shape
    hd = LANES
    nc = t // CHUNK
    ncp = g_rows.shape[3]
    group = max(g for g in range(1, 35) if nc % g == 0)
    h = n_heads
    kern = functools.partial(_delta_prep_kernel, t_len=t, ctx_len=ctx_len, n_chunks=nc, group=group)
    tok = lambda off: pl.BlockSpec((None, t, hd), lambda bi, hi, di: (bi, 0, off + hi))
    cws = lambda off: pl.BlockSpec((9, hd), lambda bi, hi, di: (0, off + hi))
    o5 = lambda last: pl.BlockSpec((None, None, None, t, last), lambda bi, hi, di: (di, bi, hi, 0, 0))
    return pl.pallas_call(
        kern,
        grid=(b, h, 2),
        in_specs=[tok(0), tok(h), tok(2 * h), cws(0), cws(h), cws(2 * h),
                  pl.BlockSpec((None, None, 4, ncp, CHUNK), lambda bi, hi, di: (bi, hi, 0, 0, 0)),
                  pl.BlockSpec((None, None, 4, CHUNK, LANES), lambda bi, hi, di: (bi, hi, 0, 0, 0))],
        out_specs=[o5(hd), o5(hd), o5(hd),
                   pl.BlockSpec((None, None, None, nc, hd, CHUNK), lambda bi, hi, di: (di, bi, hi, 0, 0, 0)),
                   o5(CHUNK),
                   pl.BlockSpec((None, None, None, ncp, LANES), lambda bi, hi, di: (di, bi, hi, 0, 0))],
        out_shape=[jax.ShapeDtypeStruct((2, b, h, t, hd), F32),
                   jax.ShapeDtypeStruct((2, b, h, t, hd), BF),
                   jax.ShapeDtypeStruct((2, b, h, t, hd), BF),
                   jax.ShapeDtypeStruct((2, b, h, nc, hd, CHUNK), BF),
                   jax.ShapeDtypeStruct((2, b, h, t, CHUNK), BF),
                   jax.ShapeDtypeStruct((2, b, h, ncp, LANES), F32)],
        scratch_shapes=[pltpu.VMEM((t + 2 * CONV_PAD, LANES), F32),
                        pltpu.VMEM((t, hd), F32), pltpu.VMEM((t, hd), F32), pltpu.VMEM((t, hd), F32),
                        pltpu.VMEM((ncp, CHUNK), F32), pltpu.VMEM((CHUNK, LANES), F32)],
        compiler_params=_cparams(3),
        name="delta_prep",
    )(p1, p1, p1, conv_w9, conv_w9, conv_w9, g_rows, g_cols)


def _delta_scan_kernel(uf, wf, qf, kf, af, gf, ub, wb, qb, kb, ab, gb, of_ref, ob_ref, s_ref, *, hg):
    @pl.when(pl.program_id(2) == 0)
    def _():
        s_ref[...] = jnp.zeros_like(s_ref)

    cs = uf.shape[1]
    s = s_ref[...]
    for t in range(cs):
        tb = cs - 1 - t
        cat = lambda a, b: jnp.concatenate([a[:, t], b[:, tb]], axis=0)
        sb = s.astype(BF)
        vn = cat(uf, ub) - _bmm(cat(wf, wb), sb)
        vnb = vn.astype(BF)
        o = _bmm(cat(qf, qb), sb) + _bmm(cat(af, ab), vnb)
        s = s * cat(gf, gb) + _bmm(cat(kf, kb), vnb)
        of_ref[:, t] = o[:hg]
        ob_ref[:, tb] = o[hg:]
    s_ref[...] = s


def _delta_scan(u, w, qd, kd, a, gl, ctx_len):
    _, b, h, t, hd = u.shape
    nc = t // CHUNK
    ncc = ctx_len // CHUNK
    hg = _pick(h, (8, 4, 2, 1))
    cs = max(c for c in (4, 2, 1) if ncc % c == 0 and nc % c == 0)
    nblk, nblk_c = nc // cs, ncc // cs
    r6 = lambda arr: arr.reshape(2, b, h, nc, CHUNK, arr.shape[-1])
    u6, w6, q6, k6, a6 = r6(u), r6(w), r6(qd), kd, r6(a)
    gl6 = gl[:, :, :, :nc].reshape(2, b, h, nc, 1, LANES)

    def cf(si):
        return si

    def cb(si):
        return jnp.where(si < nblk_c, nblk_c - 1 - si, nblk - 1 - (si - nblk_c))

    def spec(d, cmap, rows, last):
        return pl.BlockSpec((None, None, hg, cs, rows, last),
                            lambda bi, gi, si: (d, bi, gi, cmap(si), 0, 0))

    ins, specs = [], []
    for d, cmap in ((0, cf), (1, cb)):
        ins += [u6, w6, q6, k6, a6, gl6]
        specs += [spec(d, cmap, CHUNK, hd), spec(d, cmap, CHUNK, hd), spec(d, cmap, CHUNK, hd),
                  spec(d, cmap, hd, CHUNK), spec(d, cmap, CHUNK, CHUNK), spec(d, cmap, 1, LANES)]
    o_shape = jax.ShapeDtypeStruct((b, h, nc, CHUNK, hd), F32)
    ospec = lambda cmap: pl.BlockSpec((None, hg, cs, CHUNK, hd), lambda bi, gi, si: (bi, gi, cmap(si), 0, 0))
    of, ob = pl.pallas_call(
        functools.partial(_delta_scan_kernel, hg=hg),
        grid=(b, h // hg, nblk),
        in_specs=specs,
        out_specs=[ospec(cf), ospec(cb)],
        out_shape=[o_shape, o_shape],
        scratch_shapes=[pltpu.VMEM((2 * hg, hd, hd), F32)],
        compiler_params=_cparams(3),
        name="delta_scan",
    )(*ins)
    return of.reshape(b, h, t, hd), ob.reshape(b, h, t, hd)


def _delta_out_kernel(of_ref, ob_ref, z_ref, g_ref, o_ref):
    hd = of_ref.shape[2]
    for h in range(of_ref.shape[0]):
        o = of_ref[h] + ob_ref[h]
        o = o * lax.rsqrt(jnp.mean(o * o, axis=-1, keepdims=True) + NORM_EPS)
        z = z_ref[:, h * hd:(h + 1) * hd].astype(F32)
        o_ref[:, h * hd:(h + 1) * hd] = (o * g_ref[...] * (z * jax.nn.sigmoid(z))).astype(o_ref.dtype)


def _delta_out(of, ob, p1, o_gain, n_heads):
    b, h, t, hd = of.shape
    tt = TOKEN_TILE
    return pl.pallas_call(
        _delta_out_kernel,
        grid=(b, t // tt),
        in_specs=[pl.BlockSpec((None, h, tt, hd), lambda bi, ti: (bi, 0, ti, 0)),
                  pl.BlockSpec((None, h, tt, hd), lambda bi, ti: (bi, 0, ti, 0)),
                  pl.BlockSpec((None, tt, h * hd), lambda bi, ti: (bi, ti, 3)),
                  pl.BlockSpec((1, hd), lambda bi, ti: (0, 0))],
        out_specs=pl.BlockSpec((None, tt, h * hd), lambda bi, ti: (bi, ti, 0)),
        out_shape=jax.ShapeDtypeStruct((b, t, h * hd), BF),
        compiler_params=_cparams(2),
        name="delta_out",
    )(of, ob, p1, o_gain.reshape(1, hd))


def _hy_conv_kernel(p_ref, w_ref, oc_ref, ox_ref, pad_ref, *, t_len, ctx_len):
    tt = TOKEN_TILE
    z8 = jnp.zeros((8, pad_ref.shape[1]), F32)
    pad_ref[pl.ds(0, 8), :] = z8
    pad_ref[pl.ds(8 + t_len, 8), :] = z8
    pad_ref[pl.ds(8, t_len), :] = p_ref[...].astype(F32)
    w0, w1, w2 = w_ref[0:1, :], w_ref[1:2, :], w_ref[2:3, :]
    for ti in range(t_len // tt):
        t0 = ti * tt
        r = lax.broadcasted_iota(jnp.int32, (tt, 1), 0) + t0
        left = pad_ref[pl.ds(8 + t0 - 1, tt), :]
        mid = pad_ref[pl.ds(8 + t0, tt), :]
        right = pad_ref[pl.ds(8 + t0 + 1, tt), :]
        ok_l = (r != 0) & (r != ctx_len)
        ok_r = (r != ctx_len - 1) & (r != t_len - 1)
        y = jnp.where(ok_l, left, 0.0) * w0 + mid * w1 + jnp.where(ok_r, right, 0.0) * w2
        if t0 < ctx_len:
            oc_ref[pl.ds(t0, tt), :] = y.astype(oc_ref.dtype)
        else:
            ox_ref[pl.ds(t0 - ctx_len, tt), :] = y.astype(ox_ref.dtype)


def _hy_conv(p_tail, hy_conv_w, ctx_len, width3):
    b, t, _ = p_tail.shape
    tc = _pick(width3, (512, 256, 128))
    return pl.pallas_call(
        functools.partial(_hy_conv_kernel, t_len=t, ctx_len=ctx_len),
        grid=(b, width3 // tc),
        in_specs=[pl.BlockSpec((None, t, tc), lambda bi, j: (bi, 0, j)),
                  pl.BlockSpec((3, tc), lambda bi, j: (0, j))],
        out_specs=[pl.BlockSpec((None, ctx_len, tc), lambda bi, j: (bi, 0, j)),
                   pl.BlockSpec((None, t - ctx_len, tc), lambda bi, j: (bi, 0, j))],
        out_shape=[jax.ShapeDtypeStruct((b, ctx_len, width3), BF),
                   jax.ShapeDtypeStruct((b, t - ctx_len, width3), F32)],
        scratch_shapes=[pltpu.VMEM((t + 16, tc), F32)],
        compiler_params=_cparams(2),
        name="hyena_short_conv",
    )(p_tail, hy_conv_w)


def _hy_filter_kernel(ft_ref, w1_ref, b1_ref, f1_ref, w2_ref, b2_ref, f2_ref, w3_ref, dec_ref, o_ref, hid_s,
                      *, mask_lag0):
    part = pl.program_id(0)
    feats = ft_ref[...]

    @pl.when((pl.program_id(2) == 0) & (pl.program_id(3) == 0))
    def _():
        hid = jnp.sin(f1_ref[...] * (_dot3(feats, w1_ref[...]) + b1_ref[...]))
        hid_s[...] = jnp.sin(f2_ref[...] * (_dot3(hid, w2_ref[...]) + b2_ref[...]))

    filt = _bdot(hid_s[...], w3_ref[...])
    tpos = feats[:, 0:1]
    val = filt * jnp.exp(-tpos * jnp.abs(dec_ref[...]))
    if mask_lag0:
        row = lax.broadcasted_iota(jnp.int32, (val.shape[0], 1), 0) + pl.program_id(1) * val.shape[0]
        val = jnp.where((part == 1) & (row == 0), 0.0, val)
    o_ref[...] = val.astype(o_ref.dtype)


def _features(length, n_emb, k_pad, reverse_tail):
    t = jnp.linspace(0.0, 1.0, length, dtype=F32)
    n_bands = (n_emb - 1) // 2
    omega = 2.0 * math.pi * jnp.arange(length, dtype=F32) / length
    bands = jnp.linspace(1e-4, n_bands - 1, n_bands, dtype=F32)
    ang = omega[:, None] * bands[None, :]
    feats = jnp.concatenate([t[:, None], jnp.cos(ang), -jnp.sin(ang)], axis=-1)
    feats = jnp.pad(feats, ((0, 0), (0, k_pad - n_emb)))
    if reverse_tail:
        tail = feats[::-1]
    else:
        tail = jnp.concatenate([jnp.zeros((1, k_pad), F32), feats[:-1]], axis=0)
    return jnp.stack([feats, tail])


def _hy_filter_stack(length, w1, b1, f1, w2, b2, f2, w3, decay, hyw, reverse_tail):
    n_emb, fh = w1.shape
    k_pad = 64 if n_emb <= 64 else _pick(n_emb, (128,))
    feats = _features(length, n_emb, k_pad, reverse_tail)
    w1p = jnp.pad(w1, ((0, k_pad - n_emb), (0, 0)))
    tr = _pick(length, (512, 256, 128))
    tc = _pick(hyw, (2048, 1024, 512, 256, 128))
    nj = hyw // tc
    ni = length // tr
    return pl.pallas_call(
        functools.partial(_hy_filter_kernel, mask_lag0=not reverse_tail),
        grid=(2, ni, 2, nj),
        in_specs=[pl.BlockSpec((None, tr, k_pad), lambda p, i, c, j: (p, i, 0)),
                  pl.BlockSpec((k_pad, fh), lambda p, i, c, j: (0, 0)),
                  pl.BlockSpec((1, fh), lambda p, i, c, j: (0, 0)),
                  pl.BlockSpec((1, fh), lambda p, i, c, j: (0, 0)),
                  pl.BlockSpec((fh, fh), lambda p, i, c, j: (0, 0)),
                  pl.BlockSpec((1, fh), lambda p, i, c, j: (0, 0)),
                  pl.BlockSpec((1, fh), lambda p, i, c, j: (0, 0)),
                  pl.BlockSpec((fh, tc), lambda p, i, c, j: (0, (2 * c + p) * nj + j)),
                  pl.BlockSpec((1, tc), lambda p, i, c, j: (0, (2 * c + p) * nj + j))],
        out_specs=pl.BlockSpec((tr, tc), lambda p, i, c, j: (p * ni + i, c * nj + j)),
        out_shape=jax.ShapeDtypeStruct((2 * length, 2 * hyw), F32 if reverse_tail else BF),
        scratch_shapes=[pltpu.VMEM((tr, fh), F32)],
        compiler_params=_cparams(4),
        name="hyena_filters",
    )(feats, w1p, b1.reshape(1, fh), f1.reshape(1, fh), w2, b2.reshape(1, fh), f2.reshape(1, fh),
      w3, decay.reshape(1, HY_N_FILT * hyw))


def _dft_tables(length, tile):
    n = 2 * length
    half = tile // 2
    rows = jnp.arange(n, dtype=jnp.int32)
    tile_i, r = rows // tile, rows % tile
    is_im = r >= half
    k = tile_i * half + jnp.where(is_im, r - half, r)
    t = jnp.arange(length, dtype=jnp.int32)
    ang = (2.0 * math.pi / n) * ((k[:, None] * t[None, :]) % n).astype(F32)
    nyq = jnp.where(t % 2 == 0, 1.0, -1.0).astype(F32)[None, :]
    is_nyq = (is_im & (k == 0))[:, None]
    fwd = jnp.where(is_im[:, None], -jnp.sin(ang), jnp.cos(ang))
    fwd = jnp.where(is_nyq, nyq, fwd)
    sgn = jnp.where(is_im[:, None] & ~is_nyq, -1.0, 1.0)
    fker = jnp.concatenate([fwd, fwd * sgn], axis=1)
    scale = jnp.where(k == 0, 1.0 / n, 2.0 / n)[None, :]
    inv = jnp.where(is_im[None, :], -jnp.sin(ang.T), jnp.cos(ang.T))
    inv = jnp.where(is_nyq.T, nyq.T, inv) * scale
    return fwd.astype(BF), fker.astype(BF), inv.astype(BF)


def _hy_long_convs(u_seq, kstack, hy_skip, hyw):
    b, ls, _ = u_seq.shape
    n = 2 * ls
    tmf = _pick(n, (512,))
    half = tmf // 2
    tn = _pick(hyw, (512, 256, 128))
    nj = hyw // tn
    fwd, fker, inv = _dft_tables(ls, tmf)

    (hspec,) = _matmul(
        fker, kstack,
        pl.BlockSpec((tmf, n), lambda g, j, i: (i, 0)),
        pl.BlockSpec((n, tn), lambda g, j, i: (0, j)),
        (1, 2 * nj, n // tmf),
        [(jax.ShapeDtypeStruct((n, 2 * hyw), F32), pl.BlockSpec((tmf, tn), lambda g, j, i: (i, j)))],
        _ep_store(F32), name="hyena_filter_dft")

    def spectrum_product(acc, extras, outs):
        hs = extras[0][...]
        re, im = acc[:half], acc[half:]
        hre, him = hs[:half], hs[half:]
        first = (lax.broadcasted_iota(jnp.int32, (half, 1), 0) == 0) & (pl.program_id(2) == 0)
        yre = jnp.where(first, re * hre, re * hre - im * him)
        yim = jnp.where(first, im * him, re * him + im * hre)
        outs[0][pl.ds(0, half), :] = yre.astype(BF)
        outs[0][pl.ds(half, half), :] = yim.astype(BF)

    tmi = _pick(ls, (TOKEN_TILE,))
    skip3 = hy_skip.reshape(2, 1, hyw)

    def conv(z_arr, z_off, conv_idx, mul_off):
        (spec,) = _matmul(
            fwd, z_arr,
            pl.BlockSpec((tmf, ls), lambda g, j, i: (i, 0)),
            pl.BlockSpec((None, ls, tn), lambda g, j, i: (g, 0, z_off + j)),
            (b, nj, n // tmf),
            [(jax.ShapeDtypeStruct((b, n, hyw), BF), pl.BlockSpec((None, tmf, tn), lambda g, j, i: (g, i, j)))],
            spectrum_product,
            extras=[(hspec, pl.BlockSpec((tmf, tn), lambda g, j, i: (i, conv_idx * nj + j)))],
            name="hyena_dft_fwd")

        def finish(acc, extras, outs):
            z = extras[0][...].astype(F32)
            xm = extras[2][...].astype(F32)
            outs[0][...] = (xm * (acc + z * extras[1][...])).astype(BF)

        (y,) = _matmul(
            inv, spec,
            pl.BlockSpec((tmi, n), lambda g, j, i: (i, 0)),
            pl.BlockSpec((None, n, tn), lambda g, j, i: (g, 0, j)),
            (b, nj, ls // tmi),
            [(jax.ShapeDtypeStruct((b, ls, hyw), BF), pl.BlockSpec((None, tmi, tn), lambda g, j, i: (g, i, j)))],
            finish,
            extras=[(z_arr, pl.BlockSpec((None, tmi, tn), lambda g, j, i: (g, i, z_off + j))),
                    (skip3, pl.BlockSpec((None, 1, tn), lambda g, j, i: (conv_idx, 0, j))),
                    (u_seq, pl.BlockSpec((None, tmi, tn), lambda g, j, i: (g, i, mul_off * nj + j)))],
            name="hyena_dft_inv")
        return y

    z2 = conv(u_seq, 0, 0, 1)
    return conv(z2, 0, 1, 2)


FFT_FAST = 64


def _fft_tables(length):
    bf = FFT_FAST
    a_n = length // bf
    na = 2 * a_n
    n = 2 * length
    b = jnp.arange(bf, dtype=jnp.int32)
    k1 = jnp.arange(na, dtype=jnp.int32)
    a = jnp.arange(na, dtype=jnp.int32)
    nn = bf * a[None, None, :] + b[:, None, None]
    ang = (2.0 * math.pi / n) * ((k1[None, :, None] * nn) % n).astype(F32)
    cs, sn = jnp.cos(ang), jnp.sin(ang)
    m1k = jnp.concatenate([cs, -sn], axis=1)
    cd, sd = cs[:, :, :a_n], sn[:, :, :a_n]
    m1 = jnp.concatenate([jnp.concatenate([cd, sd], axis=2),
                          jnp.concatenate([-sd, cd], axis=2)], axis=1)
    ct, st = jnp.swapaxes(cd, 1, 2) / n, jnp.swapaxes(sd, 1, 2) / n
    g2 = jnp.concatenate([jnp.concatenate([ct, -st], axis=2),
                          jnp.concatenate([st, ct], axis=2)], axis=1)
    k2 = jnp.arange(bf, dtype=jnp.int32)
    ang2 = (2.0 * math.pi / bf) * ((k2[:, None] * b[None, :]) % bf).astype(F32)
    c2, s2 = jnp.cos(ang2), jnp.sin(ang2)
    f2 = jnp.concatenate([jnp.concatenate([c2, s2], axis=1), jnp.concatenate([-s2, c2], axis=1)], axis=0)
    i1 = jnp.concatenate([jnp.concatenate([c2.T, -s2.T], axis=1), jnp.concatenate([s2.T, c2.T], axis=1)], axis=0)
    return m1.astype(BF), m1k.astype(BF), f2.astype(BF), i1.astype(BF), g2.astype(BF)


def _split_rows8(blk):
    g, _, c = blk.shape
    x = blk.reshape(g * 8, c).astype(BF)
    r = lax.broadcasted_iota(jnp.int32, (64, 64), 0)
    s = lax.broadcasted_iota(jnp.int32, (64, 64), 1)
    perm = (((r >> 3) == (s & 7)) & ((r & 7) == (s >> 3))).astype(BF)
    slabs = [jnp.dot(perm, x[n * 64:(n + 1) * 64], preferred_element_type=F32) for n in range(g // 8)]
    return [jnp.concatenate([sl[j * 8:(j + 1) * 8] for sl in slabs], axis=0).astype(BF) for j in range(8)]


def _fft_s1_kernel(z_ref, m_ref, o_ref):
    for bl in range(m_ref.shape[0]):
        z = jnp.concatenate([z_ref[0, :, bl, :], z_ref[1, :, bl, :]], axis=0).astype(BF)
        o_ref[bl] = jnp.dot(m_ref[bl], z, preferred_element_type=F32)


def _fft_mid_kernel(pr_ref, pi_ref, h_ref, f2_ref, i1_ref, o_ref):
    bf = f2_ref.shape[0] // 2
    pr, pi = _split_rows8(pr_ref[...]), _split_rows8(pi_ref[...])
    for kl in range(h_ref.shape[0]):
        p = jnp.concatenate([pr[kl], pi[kl]], axis=0)
        x = jnp.dot(f2_ref[...], p, preferred_element_type=F32)
        hs = h_ref[kl].astype(F32)
        xr, xi = x[:bf], x[bf:]
        hr, hi = hs[:bf], hs[bf:]
        y = jnp.concatenate([xr * hr - xi * hi, xr * hi + xi * hr], axis=0).astype(BF)
        o_ref[kl] = jnp.dot(i1_ref[...], y, preferred_element_type=F32)


def _fft_s3_kernel(qr_ref, qi_ref, g_ref, z_ref, skip_ref, xm_ref, o_ref):
    a_n = z_ref.shape[1]
    qr, qi = _split_rows8(qr_ref[...]), _split_rows8(qi_ref[...])
    for bl in range(g_ref.shape[0]):
        q = jnp.concatenate([qr[bl], qi[bl]], axis=0)
        y = jnp.dot(g_ref[bl], q, preferred_element_type=F32)
        for s in range(2):
            ys = y[s * a_n:(s + 1) * a_n]
            o_ref[s, :, bl, :] = xm_ref[s, :, bl, :] * (ys + z_ref[s, :, bl, :] * skip_ref[...])


def _fft_kern_s1_kernel(k_ref, m_ref, o_ref):
    for bl in range(m_ref.shape[0]):
        o_ref[bl] = jnp.dot(m_ref[bl], k_ref[:, bl, :].astype(BF), preferred_element_type=F32)


def _fft_spec_kernel(pr_ref, pi_ref, f2_ref, o_ref):
    for kl in range(o_ref.shape[0]):
        p = jnp.concatenate([pr_ref[:, kl, :], pi_ref[:, kl, :]], axis=0).astype(BF)
        o_ref[kl] = jnp.dot(f2_ref[...], p, preferred_element_type=F32).astype(o_ref.dtype)


def _hy_long_convs_fft(u_seq, kern, hy_skip, hyw):
    nb_in, ls, _ = u_seq.shape
    if nb_in % 2:
        u_seq = jnp.concatenate([u_seq, jnp.zeros_like(u_seq[:1])], axis=0)
    nb = u_seq.shape[0] // 2
    bf = FFT_FAST
    a_n = ls // bf
    na = 2 * a_n
    ra = 2 * a_n
    m1, m1k, f2, i1, g2 = _fft_tables(ls)
    bg = 8
    td = _pick(hyw, (1024, 512, 256, 128))
    ndt = hyw // td

    ph = pl.pallas_call(
        _fft_kern_s1_kernel,
        grid=(bf // bg, 2 * ndt),
        in_specs=[pl.BlockSpec((na, bg, td), lambda b, j: (0, b, j)),
                  pl.BlockSpec((bg, 2 * na, na), lambda b, j: (b, 0, 0))],
        out_specs=pl.BlockSpec((bg, 2 * na, td), lambda b, j: (b, 0, j)),
        out_shape=jax.ShapeDtypeStruct((bf, 2 * na, 2 * hyw), F32),
        compiler_params=_cparams(2),
        name="hyena_fft_kern_s1",
    )(kern.reshape(na, bf, 2 * hyw), m1k)
    hspec = pl.pallas_call(
        _fft_spec_kernel,
        grid=(na // bg, 2 * ndt),
        in_specs=[pl.BlockSpec((bf, bg, td), lambda k, j: (0, k, j)),
                  pl.BlockSpec((bf, bg, td), lambda k, j: (0, na // bg + k, j)),
                  pl.BlockSpec((2 * bf, 2 * bf), lambda k, j: (0, 0))],
        out_specs=pl.BlockSpec((bg, 2 * bf, td), lambda k, j: (k, 0, j)),
        out_shape=jax.ShapeDtypeStruct((na, 2 * bf, 2 * hyw), BF),
        compiler_params=_cparams(2),
        name="hyena_fft_kern_s2",
    )(ph, ph, f2)

    skip3 = hy_skip.reshape(2, 1, hyw)
    u5 = u_seq.reshape(nb, 2, a_n, bf, 3 * hyw)

    def conv(z5, z_sec, conv_idx, mul_sec):
        zspec = pl.BlockSpec((None, 2, a_n, bg, td), lambda g, b, j: (g, 0, 0, b, z_sec * ndt + j))
        p = pl.pallas_call(
            _fft_s1_kernel,
            grid=(nb, bf // bg, ndt),
            in_specs=[zspec, pl.BlockSpec((bg, 2 * na, ra), lambda g, b, j: (b, 0, 0))],
            out_specs=pl.BlockSpec((None, bg, 2 * na, td), lambda g, b, j: (g, b, 0, j)),
            out_shape=jax.ShapeDtypeStruct((nb, bf, 2 * na, hyw), F32),
            compiler_params=_cparams(3),
            name="hyena_fft_s1",
        )(z5, m1)
        q = pl.pallas_call(
            _fft_mid_kernel,
            grid=(na // bg, ndt, nb),
            in_specs=[pl.BlockSpec((None, bf, bg, td), lambda k, j, g: (g, 0, k, j)),
                      pl.BlockSpec((None, bf, bg, td), lambda k, j, g: (g, 0, na // bg + k, j)),
                      pl.BlockSpec((bg, 2 * bf, td), lambda k, j, g: (k, 0, conv_idx * ndt + j)),
                      pl.BlockSpec((2 * bf, 2 * bf), lambda k, j, g: (0, 0)),
                      pl.BlockSpec((2 * bf, 2 * bf), lambda k, j, g: (0, 0))],
            out_specs=pl.BlockSpec((None, bg, 2 * bf, td), lambda k, j, g: (g, k, 0, j)),
            out_shape=jax.ShapeDtypeStruct((nb, na, 2 * bf, hyw), F32),
            compiler_params=_cparams(3),
            name="hyena_fft_mid",
        )(p, p, hspec, f2, i1)
        return pl.pallas_call(
            _fft_s3_kernel,
            grid=(nb, bf // bg, ndt),
            in_specs=[pl.BlockSpec((None, na, bg, td), lambda g, b, j: (g, 0, b, j)),
                      pl.BlockSpec((None, na, bg, td), lambda g, b, j: (g, 0, bf // bg + b, j)),
                      pl.BlockSpec((bg, ra, 2 * na), lambda g, b, j: (b, 0, 0)),
                      zspec,
                      pl.BlockSpec((None, 1, td), lambda g, b, j: (conv_idx, 0, j)),
                      pl.BlockSpec((None, 2, a_n, bg, td), lambda g, b, j: (g, 0, 0, b, mul_sec * ndt + j))],
            out_specs=pl.BlockSpec((None, 2, a_n, bg, td), lambda g, b, j: (g, 0, 0, b, j)),
            out_shape=jax.ShapeDtypeStruct((nb, 2, a_n, bf, hyw), F32),
            compiler_params=_cparams(3),
            name="hyena_fft_s3",
        )(q, q, g2, z5, skip3, u5)

    z_mid = conv(u5, 0, 0, 1)
    y = conv(z_mid, 0, 1, 2)
    return y.reshape(2 * nb, ls, hyw)[:nb_in]


def _moe_up_kernel(x_ref, wg_ref, wu_ref, comb_ref, o_ref, *, es):
    x = x_ref[...]
    g = jnp.dot(x, wg_ref[...], preferred_element_type=F32)
    u = jnp.dot(x, wu_ref[...], preferred_element_type=F32)
    comb = comb_ref[...]
    lane = lax.broadcasted_iota(jnp.int32, comb.shape, 1)
    f = g.shape[1] // es
    col = lax.broadcasted_iota(jnp.int32, (1, g.shape[1]), 1)
    scale = jnp.zeros_like(g)
    for s in range(es):
        e = pl.program_id(0) * es + s
        sc = jnp.sum(jnp.where(lane == e, comb, 0.0), axis=1, keepdims=True)
        scale = jnp.where((col >= s * f) & (col < (s + 1) * f), sc, scale)
    o_ref[...] = (g * jax.nn.sigmoid(g) * u * scale).astype(o_ref.dtype)


MOE_EXPERTS_PER_STEP = 4


def _moe_group_weights(w):
    nl, ne, d, f = w.shape
    es = MOE_EXPERTS_PER_STEP
    w = w.reshape(nl, ne // es, es, d, f).transpose(0, 1, 3, 2, 4)
    return w.reshape(nl, ne // es, d, es * f).astype(BF)


def _moe_up(u, w_gate, w_up, comb, l):
    m, d = u.shape
    _, ng, _, nf = w_gate.shape
    tm = _pick(m, (544, 512, 256))
    wspec = pl.BlockSpec((None, None, d, nf), lambda e, i: (l, e, 0, 0))
    return pl.pallas_call(
        functools.partial(_moe_up_kernel, es=MOE_EXPERTS_PER_STEP),
        grid=(ng, m // tm),
        in_specs=[pl.BlockSpec((tm, d), lambda e, i: (i, 0)), wspec, wspec,
                  pl.BlockSpec((tm, LANES), lambda e, i: (i, 0))],
        out_specs=pl.BlockSpec((tm, nf), lambda e, i: (i, e)),
        out_shape=jax.ShapeDtypeStruct((m, ng * nf), BF),
        compiler_params=_cparams(2),
        name="moe_up",
    )(u, w_gate, w_up, comb)


def _row_gate(gt_ref, tm, t_len, ctx_len, n_batch):
    r = lax.broadcasted_iota(jnp.int32, (tm, 1), 0) + pl.program_id(2) * tm
    gate = jnp.zeros((tm, gt_ref.shape[1]), F32)
    for bi in range(n_batch):
        in_b = (r >= bi * t_len) & (r < (bi + 1) * t_len)
        is_c = r < bi * t_len + ctx_len
        gate = jnp.where(in_b & is_c, gt_ref[0:1, :], gate)
        gate = jnp.where(in_b & ~is_c, gt_ref[1 + bi:2 + bi, :], gate)
    return gate


def _final_norm_kernel(h_ref, g_ref, o_ref):
    xf = h_ref[...]
    o_ref[...] = xf * lax.rsqrt(jnp.mean(xf * xf, axis=-1, keepdims=True) + NORM_EPS) * g_ref[...]


def _final_norm(h, gain, n_batch, t_len, ctx_len):
    m, d = h.shape
    tt = TOKEN_TILE
    n_t, n_c = t_len // tt, ctx_len // tt
    n_x = n_t - n_c
    return pl.pallas_call(
        _final_norm_kernel,
        grid=(n_batch, n_x),
        in_specs=[pl.BlockSpec((tt, d), lambda bi, i: (bi * n_t + n_c + i, 0)),
                  pl.BlockSpec((1, d), lambda bi, i: (0, 0))],
        out_specs=pl.BlockSpec((None, tt, d), lambda bi, i: (bi, i, 0)),
        out_shape=jax.ShapeDtypeStruct((n_batch, t_len - ctx_len, d), F32),
        compiler_params=_cparams(2),
        name="final_norm",
    )(h, gain.reshape(1, d))


def kernel(x, c, ctx, c_ctx, ada_down, ada_up, ada_bias, norm_mix, norm_ffn, w_in, qkv_conv, a_log, dt_bias, o_norm, hy_conv, hy_w1, hy_b1, hy_f1, hy_w2, hy_b2, hy_f2, hy_w3, hy_decay, hy_skip, w_br_a, w_br_b, w_out, router_group, router_group_bias, router_expert, router_expert_bias, exp_gate, exp_up, exp_down, final_norm):
    n_batch, seq, d = x.shape
    ctx_len = ctx.shape[1]
    t_len = ctx_len + seq
    m = n_batch * t_len
    depth = w_in.shape[0]
    n_heads = a_log.shape[2]
    a_width = n_heads * LANES
    hyw = hy_skip.shape[2]
    n_exp = exp_gate.shape[1]
    dg_off = 4 * a_width
    hy_off = dg_off + 4 * n_heads
    tail_w = 3 * hyw + 2 * d
    n_t, n_c = t_len // TOKEN_TILE, ctx_len // TOKEN_TILE
    nc = t_len // CHUNK
    ncp = -(-nc // 8) * 8

    h = jnp.concatenate([ctx, x], axis=1).reshape(m, d)

    conds = jnp.zeros((8, d), F32).at[0].set(c_ctx).at[1:1 + n_batch].set(c)
    mod_all = _adaln_all(conds, ada_down, ada_up, ada_bias)
    mod_all = mod_all.reshape(depth, 8, 6, d)[:, :1 + n_batch]
    mod_all = jnp.pad(mod_all, ((0, 0), (0, 0), (0, 2), (0, 0)))

    tm = _pick(m, (1088, 512, 256))
    tn = 512

    w_in_b = w_in.astype(BF)
    tail_base = (hy_off // LANES) * LANES
    tnp = 1024 if (dg_off % 1024 == 0 and tail_w % 1024 == 0 and tail_base % 1024 == 0) else tn
    exp_gate_b = _moe_group_weights(exp_gate)
    exp_up_b = _moe_group_weights(exp_up)

    def gate_table(mod_l, idx):
        return jnp.pad(mod_l[:, idx, :], ((0, 8 - (1 + n_batch)), (0, 0)))

    def residual_ep(acc, extras, outs):
        outs[0][...] = extras[0][...] + _row_gate(extras[1], acc.shape[0], t_len, ctx_len, n_batch) * acc

    for l in range(depth):
        mod_l = mod_all[l]
        u = _norm_mod(h, norm_mix[l], mod_l, 0, n_t, n_c)
        (p1,) = _matmul(
            u, w_in_b,
            pl.BlockSpec((tm, d), lambda g, j, i: (i, 0)),
            pl.BlockSpec((None, d, tnp), lambda g, j, i, l=l: (l, 0, j)),
            (1, dg_off // tnp, m // tm),
            [(jax.ShapeDtypeStruct((m, dg_off), BF), pl.BlockSpec((tm, tnp), lambda g, j, i: (i, j)))],
            _ep_store(BF), name="proj_qkvz")
        tmt = _pick(m, (544, 512, 256))
        (p_tail,) = _matmul(
            u, w_in_b,
            pl.BlockSpec((tmt, d), lambda g, j, i: (i, 0)),
            pl.BlockSpec((None, d, tnp), lambda g, j, i, l=l: (l, 0, tail_base // tnp + j)),
            (1, tail_w // tnp, m // tmt),
            [(jax.ShapeDtypeStruct((m, tail_w), BF), pl.BlockSpec((tmt, tnp), lambda g, j, i: (i, j)))],
            _ep_store(BF), name="proj_tail",
            w_next_spec=pl.BlockSpec((None, d, LANES),
                                     lambda g, j, i, l=l: (l, 0, (tail_base + (j + 1) * tnp) // LANES)),
            lane_shift=hy_off - tail_base)
        w_dg_t = w_in[l, :, dg_off:hy_off].T
        a_exp = jnp.concatenate([jnp.exp(a_log[l].reshape(-1)), jnp.zeros((2 * n_heads,), F32)])[:, None]
        dt_b = jnp.concatenate([dt_bias[l].reshape(-1), jnp.zeros((2 * n_heads,), F32)])[:, None]
        gates = _gates(u, w_dg_t, a_exp, dt_b)

        g5 = gates.reshape(4, n_heads, n_batch, nc, CHUNK).transpose(2, 1, 0, 3, 4)
        g_rows = jnp.pad(g5, ((0, 0), (0, 0), (0, 0), (0, ncp - nc), (0, 0)))
        g_cols = jnp.pad(jnp.swapaxes(g5, 3, 4), ((0, 0), (0, 0), (0, 0), (0, 0), (0, LANES - nc)))

        p1_3 = p1.reshape(n_batch, t_len, dg_off)
        conv9 = qkv_conv[l].reshape(9, 3 * a_width)
        du, dw, dq, dk, da, dgl = _delta_prep(p1_3, conv9, g_rows, g_cols, n_heads, ctx_len)
        o_f, o_b = _delta_scan(du, dw, dq, dk, da, dgl, ctx_len)
        o_a = _delta_out(o_f, o_b, p1_3, o_norm[l], n_heads).reshape(m, a_width)

        p_tail3 = p_tail.reshape(n_batch, t_len, tail_w)
        u_c, u_x = _hy_conv(p_tail3, hy_conv[l], ctx_len, 3 * hyw)
        filt_args = (hy_w1[l], hy_b1[l], hy_f1[l], hy_w2[l], hy_b2[l], hy_f2[l], hy_w3[l], hy_decay[l])
        y_x = _hy_long_convs_fft(u_x, _hy_filter_stack(seq, *filt_args, hyw, True), hy_skip[l], hyw)
        if l < depth - 1:
            y_c = _hy_long_convs(u_c, _hy_filter_stack(ctx_len, *filt_args, hyw, False), hy_skip[l], hyw)
        else:
            y_c = jnp.zeros((n_batch, ctx_len, hyw), BF)
        y_b = jnp.concatenate([y_c, y_x.astype(BF)], axis=1).reshape(m, hyw)

        ga_blk = (3 * hyw) // tn
        gb_blk = (3 * hyw + d) // tn

        merged = pl.pallas_call(
            _merge_kernel,
            grid=(d // tn, m // tm),
            in_specs=[pl.BlockSpec((tm, a_width), lambda j, i: (i, 0)),
                      pl.BlockSpec((tm, hyw), lambda j, i: (i, 0)),
                      pl.BlockSpec((None, a_width, tn), lambda j, i, l=l: (l, 0, j)),
                      pl.BlockSpec((None, hyw, tn), lambda j, i, l=l: (l, 0, j)),
                      pl.BlockSpec((tm, tn), lambda j, i: (i, ga_blk + j)),
                      pl.BlockSpec((tm, tn), lambda j, i: (i, gb_blk + j))],
            out_specs=pl.BlockSpec((tm, tn), lambda j, i: (i, j)),
            out_shape=jax.ShapeDtypeStruct((m, d), BF),
            scratch_shapes=[pltpu.VMEM((a_width, tn), BF), pltpu.VMEM((hyw, tn), BF)],
            compiler_params=_cparams(2),
            name="branch_merge",
        )(o_a, y_b, w_br_a, w_br_b, p_tail, p_tail)

        (h,) = _matmul(
            merged, w_out,
            pl.BlockSpec((tm, d), lambda g, j, i: (i, 0)),
            pl.BlockSpec((None, d, tn), lambda g, j, i, l=l: (l, 0, j)),
            (1, d // tn, m // tm),
            [(jax.ShapeDtypeStruct((m, d), F32), pl.BlockSpec((tm, tn), lambda g, j, i: (i, j)))],
            residual_ep,
            extras=[(h, pl.BlockSpec((tm, tn), lambda g, j, i: (i, j))),
                    (gate_table(mod_l, 2), pl.BlockSpec((8, tn), lambda g, j, i: (0, j)))],
            name="out_proj")

        w_router = jnp.pad(jnp.concatenate([router_expert[l], router_group[l]], axis=1),
                           ((0, 0), (0, LANES - n_exp - N_GROUPS)))
        b_router = jnp.pad(jnp.concatenate([router_expert_bias[l], router_group_bias[l]]),
                           (0, LANES - n_exp - N_GROUPS)).reshape(1, LANES)
        u2, comb = _norm_mod_router(h, norm_ffn[l], mod_l, 3, n_t, n_c, w_router, b_router, n_exp, N_GROUPS)
        hid = _moe_up(u2, exp_gate_b, exp_up_b, comb, l)
        kf = hid.shape[1]
        tnd = _pick(d, (512, 256, 128))
        tmd = _pick(m, (544, 512, 256))
        (h,) = _matmul(
            hid, exp_down.reshape(depth, kf, d),
            pl.BlockSpec((tmd, kf), lambda g, j, i: (i, 0)),
            pl.BlockSpec((None, kf, tnd), lambda g, j, i, l=l: (l, 0, j)),
            (1, d // tnd, m // tmd),
            [(jax.ShapeDtypeStruct((m, d), F32), pl.BlockSpec((tmd, tnd), lambda g, j, i: (i, j)))],
            residual_ep,
            extras=[(h, pl.BlockSpec((tmd, tnd), lambda g, j, i: (i, j))),
                    (gate_table(mod_l, 5), pl.BlockSpec((8, tnd), lambda g, j, i: (0, j)))],
            name="moe_down")

    return _final_norm(h, final_norm, n_batch, t_len, ctx_len)
```

```python
import functools
import math

import jax
import jax.numpy as jnp
from jax import lax
from jax.experimental import pallas as pl
from jax.experimental.pallas import tpu as pltpu

F32 = jnp.float32
BF = jnp.bfloat16

GRID_W = 64
CHUNK = 64
N_GROUPS = 4
EXPERTS_PER_GROUP = 8
HY_N_FILT = 4
NORM_EPS = 1e-6
TOKEN_TILE = 256
LANES = 128
CONV_PAD = 72
VMEM_LIMIT = 56 * 1024 * 1024


def _cparams(n_grid):
    return pltpu.CompilerParams(dimension_semantics=("arbitrary",) * n_grid,
                                vmem_limit_bytes=VMEM_LIMIT)


def _split_bf16(a):
    hi = a.astype(BF)
    lo = (a - hi.astype(F32)).astype(BF)
    return hi, lo


def _dot3(a, b):
    ah, al = _split_bf16(a)
    bh, bl = _split_bf16(b)
    d = functools.partial(jnp.dot, preferred_element_type=F32)
    return d(ah, bh) + (d(ah, bl) + d(al, bh))


def _dotx(a, b):
    return jnp.dot(a, b, precision=lax.Precision.HIGHEST, preferred_element_type=F32)


def _bdot(a, b):
    return jnp.dot(a.astype(BF), b.astype(BF), preferred_element_type=F32)


def _mm_kernel(*refs, n_extra, n_out, stage_w, lane_shift, prologue, epilogue):
    x_ref, w_ref = refs[0], refs[1]
    n_w = 3 if lane_shift else 2
    extras = refs[n_w:n_w + n_extra]
    outs = refs[n_w + n_extra:n_w + n_extra + n_out]
    scratch = refs[n_w + n_extra + n_out:]
    if stage_w:
        wbf = scratch[0]

        @pl.when(pl.program_id(2) == 0)
        def _():
            k = w_ref.shape[0]
            step = 512 if k % 512 == 0 else k

            def load(r0):
                wa = w_ref[pl.ds(r0, step), :]
                if lane_shift:
                    wa = jnp.concatenate([wa[:, lane_shift:], refs[2][pl.ds(r0, step), :lane_shift]], axis=1)
                return wa.astype(BF)

            if k == step:
                wbf[...] = load(0)
            else:
                def it(r, c):
                    r0 = pl.multiple_of(r * step, step)
                    wbf[pl.ds(r0, step), :] = load(r0)
                    return c
                lax.fori_loop(0, k // step, it, 0)
        w = wbf[...]
    else:
        w = w_ref[...]
    x = x_ref[...]
    if prologue is not None:
        x = prologue(x)
    if x.dtype != BF:
        x = x.astype(BF)
    acc = jnp.dot(x, w, preferred_element_type=F32)
    epilogue(acc, extras, outs)


def _matmul(x, w, x_spec, w_spec, grid, outs, epilogue, extras=(), prologue=None, name=None,
            w_next_spec=None, lane_shift=0):
    stage_w = w.dtype != BF or lane_shift > 0
    kw, tn = [d for d in w_spec.block_shape if d is not None]
    scratch = [pltpu.VMEM((kw, tn), BF)] if stage_w else []
    body = functools.partial(_mm_kernel, n_extra=len(extras), n_out=len(outs), stage_w=stage_w,
                             lane_shift=lane_shift, prologue=prologue, epilogue=epilogue)
    w_ops, w_specs = ([w, w], [w_spec, w_next_spec]) if lane_shift else ([w], [w_spec])
    res = pl.pallas_call(
        body,
        grid=grid,
        in_specs=[x_spec] + w_specs + [s for _, s in extras],
        out_specs=[s for _, s in outs],
        out_shape=[o for o, _ in outs],
        scratch_shapes=scratch,
        compiler_params=_cparams(3),
        name=name,
    )(x, *w_ops, *[a for a, _ in extras])
    return res


def _merge_kernel(xa_ref, xb_ref, wa_ref, wb_ref, ga_ref, gb_ref, o_ref, wa_bf, wb_bf):
    @pl.when(pl.program_id(1) == 0)
    def _():
        wa_bf[...] = wa_ref[...].astype(BF)
        wb_bf[...] = wb_ref[...].astype(BF)
    a = jnp.dot(xa_ref[...], wa_bf[...], preferred_element_type=F32)
    b = jnp.dot(xb_ref[...], wb_bf[...], preferred_element_type=F32)
    o_ref[...] = (jax.nn.sigmoid(ga_ref[...].astype(F32)) * a
                  + jax.nn.sigmoid(gb_ref[...].astype(F32)) * b).astype(o_ref.dtype)


def _ep_store(dtype):
    def ep(acc, extras, outs):
        outs[0][...] = acc.astype(dtype)
    return ep


def _pick(n, prefs):
    for p in prefs:
        if n % p == 0:
            return p
    return n


def _adaln_all(conds, ada_down, ada_up, ada_bias):
    nl, d, r = ada_down.shape
    n6 = ada_up.shape[2]
    (t,) = _matmul(
        conds, ada_down,
        pl.BlockSpec((8, d), lambda g, j, i: (0, 0)),
        pl.BlockSpec((None, d, r), lambda g, j, i: (g, 0, 0)),
        (nl, 1, 1),
        [(jax.ShapeDtypeStruct((nl, 8, r), F32), pl.BlockSpec((None, 8, r), lambda g, j, i: (g, 0, 0)))],
        _ep_store(F32), prologue=lambda v: v * jax.nn.sigmoid(v), name="adaln_down")
    tn = _pick(n6, (2048, 1024, 512, 256, 128))

    def ep(acc, extras, outs):
        outs[0][...] = acc + extras[0][...]

    (mod,) = _matmul(
        t, ada_up,
        pl.BlockSpec((None, 8, r), lambda g, j, i: (g, 0, 0)),
        pl.BlockSpec((None, r, tn), lambda g, j, i: (g, 0, j)),
        (nl, n6 // tn, 1),
        [(jax.ShapeDtypeStruct((nl, 8, n6), F32), pl.BlockSpec((None, 8, tn), lambda g, j, i: (g, 0, j)))],
        ep, extras=[(ada_bias.reshape(nl, 1, n6), pl.BlockSpec((None, 1, tn), lambda g, j, i: (g, 0, j)))],
        name="adaln_up")
    return mod


def _mod_row_map(n_t, n_c):
    def m(i):
        return (jnp.where(i % n_t < n_c, 0, 1 + i // n_t), 0, 0)
    return m


def _norm_mod_kernel(h_ref, g_ref, mod_ref, u_ref, *, off):
    xf = h_ref[...]
    y = xf * lax.rsqrt(jnp.mean(xf * xf, axis=-1, keepdims=True) + NORM_EPS) * g_ref[...]
    sh = mod_ref[off:off + 1, :]
    sc = mod_ref[off + 1:off + 2, :]
    u_ref[...] = (y * (1.0 + sc) + sh).astype(u_ref.dtype)


def _norm_mod(h, gain, mod_l, off, n_t, n_c):
    m, d = h.shape
    tt = TOKEN_TILE
    return pl.pallas_call(
        functools.partial(_norm_mod_kernel, off=off),
        grid=(m // tt,),
        in_specs=[pl.BlockSpec((tt, d), lambda i: (i, 0)),
                  pl.BlockSpec((1, d), lambda i: (0, 0)),
                  pl.BlockSpec((None, 8, d), _mod_row_map(n_t, n_c))],
        out_specs=pl.BlockSpec((tt, d), lambda i: (i, 0)),
        out_shape=jax.ShapeDtypeStruct((m, d), BF),
        compiler_params=_cparams(1),
        name="norm_mod",
    )(h, gain.reshape(1, d), mod_l)


def _router_kernel(h_ref, g_ref, mod_ref, wr_ref, br_ref, u_ref, comb_ref, *, off, n_exp, n_grp):
    xf = h_ref[...]
    y = xf * lax.rsqrt(jnp.mean(xf * xf, axis=-1, keepdims=True) + NORM_EPS) * g_ref[...]
    sh = mod_ref[off:off + 1, :]
    sc = mod_ref[off + 1:off + 2, :]
    u = y * (1.0 + sc) + sh
    u_ref[...] = u.astype(u_ref.dtype)
    logits = _dot3(u, wr_ref[...]) + br_ref[...]
    lane = lax.broadcasted_iota(jnp.int32, logits.shape, 1)
    neg = jnp.float32(-jnp.inf)
    big = jnp.int32(1 << 20)
    is_g = (lane >= n_exp) & (lane < n_exp + n_grp)
    lg = jnp.where(is_g, logits, neg)
    eg = jnp.exp(lg - jnp.max(lg, axis=1, keepdims=True))
    pg = eg / jnp.sum(eg, axis=1, keepdims=True)
    p_top = jnp.max(pg, axis=1, keepdims=True)
    g_idx = jnp.min(jnp.where(is_g & (pg == p_top), lane, big), axis=1, keepdims=True) - n_exp
    per = n_exp // n_grp
    in_grp = (lane >= g_idx * per) & (lane < (g_idx + 1) * per)
    le = jnp.where(in_grp, logits, neg)
    v1 = jnp.max(le, axis=1, keepdims=True)
    i1 = jnp.min(jnp.where(le == v1, lane, big), axis=1, keepdims=True)
    le2 = jnp.where(lane == i1, neg, le)
    v2 = jnp.max(le2, axis=1, keepdims=True)
    i2 = jnp.min(jnp.where(le2 == v2, lane, big), axis=1, keepdims=True)
    e2 = jnp.exp(v2 - v1)
    w1 = p_top / (1.0 + e2)
    w2 = p_top * e2 / (1.0 + e2)
    comb_ref[...] = jnp.where(lane == i1, w1, 0.0) + jnp.where(lane == i2, w2, 0.0)


def _norm_mod_router(h, gain, mod_l, off, n_t, n_c, w_router, b_router, n_exp, n_grp):
    m, d = h.shape
    tt = TOKEN_TILE
    return pl.pallas_call(
        functools.partial(_router_kernel, off=off, n_exp=n_exp, n_grp=n_grp),
        grid=(m // tt,),
        in_specs=[pl.BlockSpec((tt, d), lambda i: (i, 0)),
                  pl.BlockSpec((1, d), lambda i: (0, 0)),
                  pl.BlockSpec((None, 8, d), _mod_row_map(n_t, n_c)),
                  pl.BlockSpec((d, LANES), lambda i: (0, 0)),
                  pl.BlockSpec((1, LANES), lambda i: (0, 0))],
        out_specs=[pl.BlockSpec((tt, d), lambda i: (i, 0)),
                   pl.BlockSpec((tt, LANES), lambda i: (i, 0))],
        out_shape=[jax.ShapeDtypeStruct((m, d), BF), jax.ShapeDtypeStruct((m, LANES), F32)],
        compiler_params=_cparams(1),
        name="norm_mod_router",
    )(h, gain.reshape(1, d), mod_l, w_router, b_router)


def _gates_kernel(w_ref, u_ref, aexp_ref, dtb_ref, o_ref, *, n_decay):
    acc = lax.dot_general(w_ref[...].astype(BF), u_ref[...], (((1,), (1,)), ((), ())),
                          preferred_element_type=F32)
    row = lax.broadcasted_iota(jnp.int32, acc.shape, 0)
    z = acc + dtb_ref[...]
    softplus = jnp.maximum(z, 0.0) + jnp.log(1.0 + jnp.exp(-jnp.abs(z)))
    g = -aexp_ref[...] * softplus
    o_ref[...] = jnp.where(row < n_decay, g, jax.nn.sigmoid(acc))


def _gates(u, w_dg_t, a_exp, dt_b):
    m, d = u.shape
    r = w_dg_t.shape[0]
    tm = _pick(m, (512, 256))
    return pl.pallas_call(
        functools.partial(_gates_kernel, n_decay=r // 2),
        grid=(m // tm,),
        in_specs=[pl.BlockSpec((r, d), lambda i: (0, 0)),
                  pl.BlockSpec((tm, d), lambda i: (i, 0)),
                  pl.BlockSpec((r, 1), lambda i: (0, 0)),
                  pl.BlockSpec((r, 1), lambda i: (0, 0))],
        out_specs=pl.BlockSpec((r, tm), lambda i: (0, i)),
        out_shape=jax.ShapeDtypeStruct((r, m), F32),
        compiler_params=_cparams(1),
        name="delta_gates",
    )(w_dg_t, u, a_exp, dt_b)


def _conv_silu_tile(pad_ref, cw_ref, t0, n, seq_start, seq_len, width):
    one_row = width == seq_len
    r = lax.broadcasted_iota(jnp.int32, (n, 1), 0) + (t0 - seq_start)
    col = lax.rem(r, width)
    acc = jnp.zeros((n, LANES), F32)
    for dj in (-1, 0, 1):
        part = jnp.zeros((n, LANES), F32)
        for di in (-1, 0, 1):
            if one_row and di != 0:
                continue
            off = di * width + dj
            tap = pad_ref[pl.ds(CONV_PAD + t0 + off, n), :]
            wrow = cw_ref[(di + 1) * 3 + (dj + 1):(di + 1) * 3 + (dj + 1) + 1, :]
            lo, hi = t0 - seq_start + off, t0 - seq_start + n - 1 + off
            if lo < 0 or hi >= seq_len:
                tap = jnp.where((r + off >= 0) & (r + off < seq_len), tap, 0.0)
            part = part + tap * wrow
        ok_c = (col + dj >= 0) & (col + dj < width)
        acc = acc + jnp.where(ok_c, part, 0.0)
    return acc * jax.nn.sigmoid(acc)


def _bmm(a, b):
    return jnp.einsum('gik,gkj->gij', a.astype(BF), b.astype(BF), preferred_element_type=F32)


def _bmm_nt(a, b):
    return jnp.einsum('gik,gjk->gij', a.astype(BF), b.astype(BF), preferred_element_type=F32)


TRI_BASE = 4


def _tri_inverse(lmat, eye, ii, jj):
    n = lmat.shape[-1]
    same = lambda s: (ii >> (s.bit_length() - 1)) == (jj >> (s.bit_length() - 1))
    x = -jnp.where(same(TRI_BASE), lmat, 0.0)
    t = eye + x
    t = t + _bmm(t, _bmm(x, x))
    s = TRI_BASE
    while s < n:
        e = jnp.where(same(2 * s) & ~same(s), lmat, 0.0)
        t = t - _bmm(t, _bmm(e, t))
        s *= 2
    return t


def _delta_prep_kernel(pq_ref, pk_ref, pv_ref, cq_ref, ck_ref, cv_ref, g_ref, gt_ref,
                       u_ref, w_ref, qd_ref, kd_ref, a_ref, gl_ref,
                       pad_ref, q_s, k_s, v_s, gcr_s, gct_s,
                       *, t_len, ctx_len, n_chunks, group):
    hd = q_s.shape[1]
    tt = TOKEN_TILE
    d = pl.program_id(2)

    @pl.when(d == 0)
    def _():
        zpad = jnp.zeros((CONV_PAD, LANES), F32)
        pad_ref[pl.ds(0, CONV_PAD), :] = zpad
        pad_ref[pl.ds(CONV_PAD + t_len, CONV_PAD), :] = zpad
        for src, cw, dst, kind in ((pq_ref, cq_ref, q_s, "q"), (pk_ref, ck_ref, k_s, "k"), (pv_ref, cv_ref, v_s, "v")):
            pad_ref[pl.ds(CONV_PAD, t_len), :] = src[...].astype(F32)
            for ti in range(t_len // tt):
                t0 = ti * tt
                if t0 < ctx_len:
                    a = _conv_silu_tile(pad_ref, cw, t0, tt, 0, ctx_len, ctx_len)
                else:
                    a = _conv_silu_tile(pad_ref, cw, t0, tt, ctx_len, t_len - ctx_len, GRID_W)
                if kind != "v":
                    a = a * lax.rsqrt(jnp.sum(a * a, axis=-1, keepdims=True) + NORM_EPS)
                if kind == "q":
                    a = a * (hd ** -0.5)
                dst[pl.ds(t0, tt), :] = a

    ii = lax.broadcasted_iota(jnp.int32, (CHUNK, CHUNK), 0)
    jj = lax.broadcasted_iota(jnp.int32, (CHUNK, CHUNK), 1)
    sdiff = (ii - jj) * (1 - 2 * d)
    eye = (ii == jj).astype(F32)
    incl = sdiff >= 0
    strict = sdiff > 0
    gcr_s[...] = _dotx(g_ref[d], (sdiff <= 0).astype(F32))
    gct_s[...] = _dotx(incl.astype(F32), gt_ref[d])
    gtot = jnp.where(d == 0, gcr_s[:, CHUNK - 1:CHUNK], gcr_s[:, 0:1])
    gl_ref[...] = jnp.broadcast_to(jnp.exp(gtot), gl_ref.shape)

    gc = group
    rows = gc * CHUNK
    lane_c = lax.broadcasted_iota(jnp.int32, (CHUNK, LANES), 1)
    eye_hd = (lax.broadcasted_iota(jnp.int32, (hd, hd), 0)
              == lax.broadcasted_iota(jnp.int32, (hd, hd), 1)).astype(BF)[None]

    def body(it, carry):
        c0 = it * gc
        t0 = pl.multiple_of(c0 * CHUNK, rows)
        q = q_s[pl.ds(t0, rows), :].reshape(gc, CHUNK, hd)
        k = k_s[pl.ds(t0, rows), :].reshape(gc, CHUNK, hd)
        v = v_s[pl.ds(t0, rows), :].reshape(gc, CHUNK, hd)
        gct = gct_s[...]
        bt = gt_ref[2 + d]
        gcols, bcols, grows, glasts = [], [], [], []
        for g in range(gc):
            sel = lane_c == c0 + g
            gcols.append(jnp.sum(jnp.where(sel, gct, 0.0), axis=1, keepdims=True))
            bcols.append(jnp.sum(jnp.where(sel, bt, 0.0), axis=1, keepdims=True))
            grow_g = gcr_s[pl.ds(c0 + g, 1), :]
            grows.append(grow_g)
            glasts.append(jnp.where(d == 0, grow_g[:, CHUNK - 1:CHUNK], grow_g[:, 0:1]))
        gcol = jnp.broadcast_to(jnp.stack(gcols), (gc, CHUNK, hd))
        bcol = jnp.broadcast_to(jnp.stack(bcols), (gc, CHUNK, hd))
        grow = jnp.stack(grows)
        glast = jnp.stack(glasts)
        decay = jnp.where(incl, jnp.exp(jnp.where(incl, gcol[:, :, :CHUNK] - grow, 0.0)), 0.0)
        kb = k * bcol
        lmat = jnp.where(strict, _bmm_nt(kb, k) * decay, 0.0)
        tinv = _tri_inverse(lmat, eye[None], ii, jj)
        egc = jnp.exp(gcol)
        uw = _bmm(tinv, jnp.concatenate([v * bcol, kb * egc], axis=2))
        attn = (_bmm_nt(q, k) * decay).astype(BF)
        kdec = k * jnp.exp(glast - gcol)
        kdec_t = _bmm_nt(jnp.broadcast_to(eye_hd, (gc, hd, hd)), kdec).astype(BF)
        u_ref[pl.ds(t0, rows), :] = uw[:, :, :hd].reshape(rows, hd)
        w_ref[pl.ds(t0, rows), :] = uw[:, :, hd:].astype(BF).reshape(rows, hd)
        qd_ref[pl.ds(t0, rows), :] = (q * egc).astype(BF).reshape(rows, hd)
        a_ref[pl.ds(t0, rows), :] = attn.reshape(rows, CHUNK)
        kd_ref[pl.ds(c0, gc)] = kdec_t
        return carry

    lax.fori_loop(0, n_chunks // group, body, 0)


def _delta_prep(p1, conv_w9, g_rows, g_cols, n_heads, ctx_len):
    b, t, _ = p1.shape
    hd = LANES
    nc = t // CHUNK
    ncp = g_rows.shape[3]
    group = max(g for g in range(1, 35) if nc % g == 0)
    h = n_heads
    kern = functools.partial(_delta_prep_kernel, t_len=t, ctx_len=ctx_len, n_chunks=nc, group=group)
    tok = lambda off: pl.BlockSpec((None, t, hd), lambda bi, hi, di: (bi, 0, off + hi))
    cws = lambda off: pl.BlockSpec((9, hd), lambda bi, hi, di: (0, off + hi))
    o5 = lambda last: pl.BlockSpec((None, None, None, t, last), lambda bi, hi, di: (di, bi, hi, 0, 0))
    return pl.pallas_call(
        kern,
        grid=(b, h, 2),
        in_specs=[tok(0), tok(h), tok(2 * h), cws(0), cws(h), cws(2 * h),
                  pl.BlockSpec((None, None, 4, ncp, CHUNK), lambda bi, hi, di: (bi, hi, 0, 0, 0)),
                  pl.BlockSpec((None, None, 4, CHUNK, LANES), lambda bi, hi, di: (bi, hi, 0, 0, 0))],
        out_specs=[o5(hd), o5(hd), o5(hd),
                   pl.BlockSpec((None, None, None, nc, hd, CHUNK), lambda bi, hi, di: (di, bi, hi, 0, 0, 0)),
                   o5(CHUNK),
                   pl.BlockSpec((None, None, None, ncp, LANES), lambda bi, hi, di: (di, bi, hi, 0, 0))],
        out_shape=[jax.ShapeDtypeStruct((2, b, h, t, hd), F32),
                   jax.ShapeDtypeStruct((2, b, h, t, hd), BF),
                   jax.ShapeDtypeStruct((2, b, h, t, hd), BF),
                   jax.ShapeDtypeStruct((2, b, h, nc, hd, CHUNK), BF),
                   jax.ShapeDtypeStruct((2, b, h, t, CHUNK), BF),
                   jax.ShapeDtypeStruct((2, b, h, ncp, LANES), F32)],
        scratch_shapes=[pltpu.VMEM((t + 2 * CONV_PAD, LANES), F32),
                        pltpu.VMEM((t, hd), F32), pltpu.VMEM((t, hd), F32), pltpu.VMEM((t, hd), F32),
                        pltpu.VMEM((ncp, CHUNK), F32), pltpu.VMEM((CHUNK, LANES), F32)],
        compiler_params=_cparams(3),
        name="delta_prep",
    )(p1, p1, p1, conv_w9, conv_w9, conv_w9, g_rows, g_cols)


def _delta_scan_kernel(uf, wf, qf, kf, af, gf, ub, wb, qb, kb, ab, gb, of_ref, ob_ref, s_ref, *, hg):
    @pl.when(pl.program_id(2) == 0)
    def _():
        s_ref[...] = jnp.zeros_like(s_ref)

    cs = uf.shape[1]
    s = s_ref[...]
    for t in range(cs):
        tb = cs - 1 - t
        cat = lambda a, b: jnp.concatenate([a[:, t], b[:, tb]], axis=0)
        sb = s.astype(BF)
        vn = cat(uf, ub) - _bmm(cat(wf, wb), sb)
        vnb = vn.astype(BF)
        o = _bmm(cat(qf, qb), sb) + _bmm(cat(af, ab), vnb)
        s = s * cat(gf, gb) + _bmm(cat(kf, kb), vnb)
        of_ref[:, t] = o[:hg]
        ob_ref[:, tb] = o[hg:]
    s_ref[...] = s


def _delta_scan(u, w, qd, kd, a, gl, ctx_len):
    _, b, h, t, hd = u.shape
    nc = t // CHUNK
    ncc = ctx_len // CHUNK
    hg = _pick(h, (8, 4, 2, 1))
    cs = max(c for c in (4, 2, 1) if ncc % c == 0 and nc % c == 0)
    nblk, nblk_c = nc // cs, ncc // cs
    r6 = lambda arr: arr.reshape(2, b, h, nc, CHUNK, arr.shape[-1])
    u6, w6, q6, k6, a6 = r6(u), r6(w), r6(qd), kd, r6(a)
    gl6 = gl[:, :, :, :nc].reshape(2, b, h, nc, 1, LANES)

    def cf(si):
        return si

    def cb(si):
        return jnp.where(si < nblk_c, nblk_c - 1 - si, nblk - 1 - (si - nblk_c))

    def spec(d, cmap, rows, last):
        return pl.BlockSpec((None, None, hg, cs, rows, last),
                            lambda bi, gi, si: (d, bi, gi, cmap(si), 0, 0))

    ins, specs = [], []
    for d, cmap in ((0, cf), (1, cb)):
        ins += [u6, w6, q6, k6, a6, gl6]
        specs += [spec(d, cmap, CHUNK, hd), spec(d, cmap, CHUNK, hd), spec(d, cmap, CHUNK, hd),
                  spec(d, cmap, hd, CHUNK), spec(d, cmap, CHUNK, CHUNK), spec(d, cmap, 1, LANES)]
    o_shape = jax.ShapeDtypeStruct((b, h, nc, CHUNK, hd), F32)
    ospec = lambda cmap: pl.BlockSpec((None, hg, cs, CHUNK, hd), lambda bi, gi, si: (bi, gi, cmap(si), 0, 0))
    of, ob = pl.pallas_call(
        functools.partial(_delta_scan_kernel, hg=hg),
        grid=(b, h // hg, nblk),
        in_specs=specs,
        out_specs=[ospec(cf), ospec(cb)],
        out_shape=[o_shape, o_shape],
        scratch_shapes=[pltpu.VMEM((2 * hg, hd, hd), F32)],
        compiler_params=_cparams(3),
        name="delta_scan",
    )(*ins)
    return of.reshape(b, h, t, hd), ob.reshape(b, h, t, hd)


def _delta_out_kernel(of_ref, ob_ref, z_ref, g_ref, o_ref):
    hd = of_ref.shape[2]
    for h in range(of_ref.shape[0]):
        o = of_ref[h] + ob_ref[h]
        o = o * lax.rsqrt(jnp.mean(o * o, axis=-1, keepdims=True) + NORM_EPS)
        z = z_ref[:, h * hd:(h + 1) * hd].astype(F32)
        o_ref[:, h * hd:(h + 1) * hd] = (o * g_ref[...] * (z * jax.nn.sigmoid(z))).astype(o_ref.dtype)


def _delta_out(of, ob, p1, o_gain, n_heads):
    b, h, t, hd = of.shape
    tt = TOKEN_TILE
    return pl.pallas_call(
        _delta_out_kernel,
        grid=(b, t // tt),
        in_specs=[pl.BlockSpec((None, h, tt, hd), lambda bi, ti: (bi, 0, ti, 0)),
                  pl.BlockSpec((None, h, tt, hd), lambda bi, ti: (bi, 0, ti, 0)),
                  pl.BlockSpec((None, tt, h * hd), lambda bi, ti: (bi, ti, 3)),
                  pl.BlockSpec((1, hd), lambda bi, ti: (0, 0))],
        out_specs=pl.BlockSpec((None, tt, h * hd), lambda bi, ti: (bi, ti, 0)),
        out_shape=jax.ShapeDtypeStruct((b, t, h * hd), BF),
        compiler_params=_cparams(2),
        name="delta_out",
    )(of, ob, p1, o_gain.reshape(1, hd))


def _hy_conv_kernel(p_ref, w_ref, oc_ref, ox_ref, pad_ref, *, t_len, ctx_len):
    tt = TOKEN_TILE
    z8 = jnp.zeros((8, pad_ref.shape[1]), F32)
    pad_ref[pl.ds(0, 8), :] = z8
    pad_ref[pl.ds(8 + t_len, 8), :] = z8
    pad_ref[pl.ds(8, t_len), :] = p_ref[...].astype(F32)
    w0, w1, w2 = w_ref[0:1, :], w_ref[1:2, :], w_ref[2:3, :]
    for ti in range(t_len // tt):
        t0 = ti * tt
        r = lax.broadcasted_iota(jnp.int32, (tt, 1), 0) + t0
        left = pad_ref[pl.ds(8 + t0 - 1, tt), :]
        mid = pad_ref[pl.ds(8 + t0, tt), :]
        right = pad_ref[pl.ds(8 + t0 + 1, tt), :]
        ok_l = (r != 0) & (r != ctx_len)
        ok_r = (r != ctx_len - 1) & (r != t_len - 1)
        y = jnp.where(ok_l, left, 0.0) * w0 + mid * w1 + jnp.where(ok_r, right, 0.0) * w2
        if t0 < ctx_len:
            oc_ref[pl.ds(t0, tt), :] = y.astype(oc_ref.dtype)
        else:
            ox_ref[pl.ds(t0 - ctx_len, tt), :] = y.astype(ox_ref.dtype)


def _hy_conv(p_tail, hy_conv_w, ctx_len, width3):
    b, t, _ = p_tail.shape
    tc = _pick(width3, (512, 256, 128))
    return pl.pallas_call(
        functools.partial(_hy_conv_kernel, t_len=t, ctx_len=ctx_len),
        grid=(b, width3 // tc),
        in_specs=[pl.BlockSpec((None, t, tc), lambda bi, j: (bi, 0, j)),
                  pl.BlockSpec((3, tc), lambda bi, j: (0, j))],
        out_specs=[pl.BlockSpec((None, ctx_len, tc), lambda bi, j: (bi, 0, j)),
                   pl.BlockSpec((None, t - ctx_len, tc), lambda bi, j: (bi, 0, j))],
        out_shape=[jax.ShapeDtypeStruct((b, ctx_len, width3), BF),
                   jax.ShapeDtypeStruct((b, t - ctx_len, width3), F32)],
        scratch_shapes=[pltpu.VMEM((t + 16, tc), F32)],
        compiler_params=_cparams(2),
        name="hyena_short_conv",
    )(p_tail, hy_conv_w)


def _hy_filter_kernel(ft_ref, w1_ref, b1_ref, f1_ref, w2_ref, b2_ref, f2_ref, w3_ref, dec_ref, o_ref, hid_s,
                      *, mask_lag0):
    part = pl.program_id(0)
    feats = ft_ref[...]

    @pl.when((pl.program_id(2) == 0) & (pl.program_id(3) == 0))
    def _():
        hid = jnp.sin(f1_ref[...] * (_dot3(feats, w1_ref[...]) + b1_ref[...]))
        hid_s[...] = jnp.sin(f2_ref[...] * (_dot3(hid, w2_ref[...]) + b2_ref[...]))

    filt = _bdot(hid_s[...], w3_ref[...])
    tpos = feats[:, 0:1]
    val = filt * jnp.exp(-tpos * jnp.abs(dec_ref[...]))
    if mask_lag0:
        row = lax.broadcasted_iota(jnp.int32, (val.shape[0], 1), 0) + pl.program_id(1) * val.shape[0]
        val = jnp.where((part == 1) & (row == 0), 0.0, val)
    o_ref[...] = val.astype(o_ref.dtype)


def _features(length, n_emb, k_pad, reverse_tail):
    t = jnp.linspace(0.0, 1.0, length, dtype=F32)
    n_bands = (n_emb - 1) // 2
    omega = 2.0 * math.pi * jnp.arange(length, dtype=F32) / length
    bands = jnp.linspace(1e-4, n_bands - 1, n_bands, dtype=F32)
    ang = omega[:, None] * bands[None, :]
    feats = jnp.concatenate([t[:, None], jnp.cos(ang), -jnp.sin(ang)], axis=-1)
    feats = jnp.pad(feats, ((0, 0), (0, k_pad - n_emb)))
    if reverse_tail:
        tail = feats[::-1]
    else:
        tail = jnp.concatenate([jnp.zeros((1, k_pad), F32), feats[:-1]], axis=0)
    return jnp.stack([feats, tail])


def _hy_filter_stack(length, w1, b1, f1, w2, b2, f2, w3, decay, hyw, reverse_tail):
    n_emb, fh = w1.shape
    k_pad = 64 if n_emb <= 64 else _pick(n_emb, (128,))
    feats = _features(length, n_emb, k_pad, reverse_tail)
    w1p = jnp.pad(w1, ((0, k_pad - n_emb), (0, 0)))
    tr = _pick(length, (512, 256, 128))
    tc = _pick(hyw, (2048, 1024, 512, 256, 128))
    nj = hyw // tc
    ni = length // tr
    return pl.pallas_call(
        functools.partial(_hy_filter_kernel, mask_lag0=not reverse_tail),
        grid=(2, ni, 2, nj),
        in_specs=[pl.BlockSpec((None, tr, k_pad), lambda p, i, c, j: (p, i, 0)),
                  pl.BlockSpec((k_pad, fh), lambda p, i, c, j: (0, 0)),
                  pl.BlockSpec((1, fh), lambda p, i, c, j: (0, 0)),
                  pl.BlockSpec((1, fh), lambda p, i, c, j: (0, 0)),
                  pl.BlockSpec((fh, fh), lambda p, i, c, j: (0, 0)),
                  pl.BlockSpec((1, fh), lambda p, i, c, j: (0, 0)),
                  pl.BlockSpec((1, fh), lambda p, i, c, j: (0, 0)),
                  pl.BlockSpec((fh, tc), lambda p, i, c, j: (0, (2 * c + p) * nj + j)),
                  pl.BlockSpec((1, tc), lambda p, i, c, j: (0, (2 * c + p) * nj + j))],
        out_specs=pl.BlockSpec((tr, tc), lambda p, i, c, j: (p * ni + i, c * nj + j)),
        out_shape=jax.ShapeDtypeStruct((2 * length, 2 * hyw), F32 if reverse_tail else BF),
        scratch_shapes=[pltpu.VMEM((tr, fh), F32)],
        compiler_params=_cparams(4),
        name="hyena_filters",
    )(feats, w1p, b1.reshape(1, fh), f1.reshape(1, fh), w2, b2.reshape(1, fh), f2.reshape(1, fh),
      w3, decay.reshape(1, HY_N_FILT * hyw))


def _dft_tables(length, tile):
    n = 2 * length
    half = tile // 2
    rows = jnp.arange(n, dtype=jnp.int32)
    tile_i, r = rows // tile, rows % tile
    is_im = r >= half
    k = tile_i * half + jnp.where(is_im, r - half, r)
    t = jnp.arange(length, dtype=jnp.int32)
    ang = (2.0 * math.pi / n) * ((k[:, None] * t[None, :]) % n).astype(F32)
    nyq = jnp.where(t % 2 == 0, 1.0, -1.0).astype(F32)[None, :]
    is_nyq = (is_im & (k == 0))[:, None]
    fwd = jnp.where(is_im[:, None], -jnp.sin(ang), jnp.cos(ang))
    fwd = jnp.where(is_nyq, nyq, fwd)
    sgn = jnp.where(is_im[:, None] & ~is_nyq, -1.0, 1.0)
    fker = jnp.concatenate([fwd, fwd * sgn], axis=1)
    scale = jnp.where(k == 0, 1.0 / n, 2.0 / n)[None, :]
    inv = jnp.where(is_im[None, :], -jnp.sin(ang.T), jnp.cos(ang.T))
    inv = jnp.where(is_nyq.T, nyq.T, inv) * scale
    return fwd.astype(BF), fker.astype(BF), inv.astype(BF)


def _hy_long_convs(u_seq, kstack, hy_skip, hyw):
    b, ls, _ = u_seq.shape
    n = 2 * ls
    tmf = _pick(n, (512,))
    half = tmf // 2
    tn = _pick(hyw, (512, 256, 128))
    nj = hyw // tn
    fwd, fker, inv = _dft_tables(ls, tmf)

    (hspec,) = _matmul(
        fker, kstack,
        pl.BlockSpec((tmf, n), lambda g, j, i: (i, 0)),
        pl.BlockSpec((n, tn), lambda g, j, i: (0, j)),
        (1, 2 * nj, n // tmf),
        [(jax.ShapeDtypeStruct((n, 2 * hyw), F32), pl.BlockSpec((tmf, tn), lambda g, j, i: (i, j)))],
        _ep_store(F32), name="hyena_filter_dft")

    def spectrum_product(acc, extras, outs):
        hs = extras[0][...]
        re, im = acc[:half], acc[half:]
        hre, him = hs[:half], hs[half:]
        first = (lax.broadcasted_iota(jnp.int32, (half, 1), 0) == 0) & (pl.program_id(2) == 0)
        yre = jnp.where(first, re * hre, re * hre - im * him)
        yim = jnp.where(first, im * him, re * him + im * hre)
        outs[0][pl.ds(0, half), :] = yre.astype(BF)
        outs[0][pl.ds(half, half), :] = yim.astype(BF)

    tmi = _pick(ls, (TOKEN_TILE,))
    skip3 = hy_skip.reshape(2, 1, hyw)

    def conv(z_arr, z_off, conv_idx, mul_off):
        (spec,) = _matmul(
            fwd, z_arr,
            pl.BlockSpec((tmf, ls), lambda g, j, i: (i, 0)),
            pl.BlockSpec((None, ls, tn), lambda g, j, i: (g, 0, z_off + j)),
            (b, nj, n // tmf),
            [(jax.ShapeDtypeStruct((b, n, hyw), BF), pl.BlockSpec((None, tmf, tn), lambda g, j, i: (g, i, j)))],
            spectrum_product,
            extras=[(hspec, pl.BlockSpec((tmf, tn), lambda g, j, i: (i, conv_idx * nj + j)))],
            name="hyena_dft_fwd")

        def finish(acc, extras, outs):
            z = extras[0][...].astype(F32)
            xm = extras[2][...].astype(F32)
            outs[0][...] = (xm * (acc + z * extras[1][...])).astype(BF)

        (y,) = _matmul(
            inv, spec,
            pl.BlockSpec((tmi, n), lambda g, j, i: (i, 0)),
            pl.BlockSpec((None, n, tn), lambda g, j, i: (g, 0, j)),
            (b, nj, ls // tmi),
            [(jax.ShapeDtypeStruct((b, ls, hyw), BF), pl.BlockSpec((None, tmi, tn), lambda g, j, i: (g, i, j)))],
            finish,
            extras=[(z_arr, pl.BlockSpec((None, tmi, tn), lambda g, j, i: (g, i, z_off + j))),
                    (skip3, pl.BlockSpec((None, 1, tn), lambda g, j, i: (conv_idx, 0, j))),
                    (u_seq, pl.BlockSpec((None, tmi, tn), lambda g, j, i: (g, i, mul_off * nj + j)))],
            name="hyena_dft_inv")
        return y

    z2 = conv(u_seq, 0, 0, 1)
    return conv(z2, 0, 1, 2)


FFT_FAST = 64


def _fft_tables(length):
    bf = FFT_FAST
    a_n = length // bf
    na = 2 * a_n
    n = 2 * length
    b = jnp.arange(bf, dtype=jnp.int32)
    k1 = jnp.arange(na, dtype=jnp.int32)
    a = jnp.arange(na, dtype=jnp.int32)
    nn = bf * a[None, None, :] + b[:, None, None]
    ang = (2.0 * math.pi / n) * ((k1[None, :, None] * nn) % n).astype(F32)
    cs, sn = jnp.cos(ang), jnp.sin(ang)
    m1k = jnp.concatenate([cs, -sn], axis=1)
    cd, sd = cs[:, :, :a_n], sn[:, :, :a_n]
    m1 = jnp.concatenate([jnp.concatenate([cd, sd], axis=2),
                          jnp.concatenate([-sd, cd], axis=2)], axis=1)
    ct, st = jnp.swapaxes(cd, 1, 2) / n, jnp.swapaxes(sd, 1, 2) / n
    g2 = jnp.concatenate([jnp.concatenate([ct, -st], axis=2),
                          jnp.concatenate([st, ct], axis=2)], axis=1)
    k2 = jnp.arange(bf, dtype=jnp.int32)
    ang2 = (2.0 * math.pi / bf) * ((k2[:, None] * b[None, :]) % bf).astype(F32)
    c2, s2 = jnp.cos(ang2), jnp.sin(ang2)
    f2 = jnp.concatenate([jnp.concatenate([c2, s2], axis=1), jnp.concatenate([-s2, c2], axis=1)], axis=0)
    i1 = jnp.concatenate([jnp.concatenate([c2.T, -s2.T], axis=1), jnp.concatenate([s2.T, c2.T], axis=1)], axis=0)
    return m1.astype(BF), m1k.astype(BF), f2.astype(BF), i1.astype(BF), g2.astype(BF)


def _split_rows8(blk):
    g, _, c = blk.shape
    x = blk.reshape(g * 8, c).astype(BF)
    r = lax.broadcasted_iota(jnp.int32, (64, 64), 0)
    s = lax.broadcasted_iota(jnp.int32, (64, 64), 1)
    perm = (((r >> 3) == (s & 7)) & ((r & 7) == (s >> 3))).astype(BF)
    slabs = [jnp.dot(perm, x[n * 64:(n + 1) * 64], preferred_element_type=F32) for n in range(g // 8)]
    return [jnp.concatenate([sl[j * 8:(j + 1) * 8] for sl in slabs], axis=0).astype(BF) for j in range(8)]


def _fft_s1_kernel(z_ref, m_ref, o_ref):
    for bl in range(m_ref.shape[0]):
        z = jnp.concatenate([z_ref[0, :, bl, :], z_ref[1, :, bl, :]], axis=0).astype(BF)
        o_ref[bl] = jnp.dot(m_ref[bl], z, preferred_element_type=F32)


def _fft_mid_kernel(pr_ref, pi_ref, h_ref, f2_ref, i1_ref, o_ref):
    bf = f2_ref.shape[0] // 2
    pr, pi = _split_rows8(pr_ref[...]), _split_rows8(pi_ref[...])
    for kl in range(h_ref.shape[0]):
        p = jnp.concatenate([pr[kl], pi[kl]], axis=0)
        x = jnp.dot(f2_ref[...], p, preferred_element_type=F32)
        hs = h_ref[kl].astype(F32)
        xr, xi = x[:bf], x[bf:]
        hr, hi = hs[:bf], hs[bf:]
        y = jnp.concatenate([xr * hr - xi * hi, xr * hi + xi * hr], axis=0).astype(BF)
        o_ref[kl] = jnp.dot(i1_ref[...], y, preferred_element_type=F32)


def _fft_s3_kernel(qr_ref, qi_ref, g_ref, z_ref, skip_ref, xm_ref, o_ref):
    a_n = z_ref.shape[1]
    qr, qi = _split_rows8(qr_ref[...]), _split_rows8(qi_ref[...])
    for bl in range(g_ref.shape[0]):
        q = jnp.concatenate([qr[bl], qi[bl]], axis=0)
        y = jnp.dot(g_ref[bl], q, preferred_element_type=F32)
        for s in range(2):
            ys = y[s * a_n:(s + 1) * a_n]
            o_ref[s, :, bl, :] = xm_ref[s, :, bl, :] * (ys + z_ref[s, :, bl, :] * skip_ref[...])


def _fft_kern_s1_kernel(k_ref, m_ref, o_ref):
    k = _split_rows8(k_ref[...])
    for bl in range(m_ref.shape[0]):
        o_ref[bl] = jnp.dot(m_ref[bl], k[bl], preferred_element_type=F32)


def _fft_spec_kernel(pr_ref, pi_ref, f2_ref, o_ref):
    pr, pi = _split_rows8(pr_ref[...]), _split_rows8(pi_ref[...])
    for kl in range(o_ref.shape[0]):
        p = jnp.concatenate([pr[kl], pi[kl]], axis=0)
        o_ref[kl] = jnp.dot(f2_ref[...], p, preferred_element_type=F32).astype(o_ref.dtype)


def _hy_long_convs_fft(u_seq, kern, hy_skip, hyw):
    nb_in, ls, _ = u_seq.shape
    if nb_in % 2:
        u_seq = jnp.concatenate([u_seq, jnp.zeros_like(u_seq[:1])], axis=0)
    nb = u_seq.shape[0] // 2
    bf = FFT_FAST
    a_n = ls // bf
    na = 2 * a_n
    ra = 2 * a_n
    m1, m1k, f2, i1, g2 = _fft_tables(ls)
    bg = 8
    td = _pick(hyw, (1024, 512, 256, 128))
    ndt = hyw // td

    ph = pl.pallas_call(
        _fft_kern_s1_kernel,
        grid=(bf // bg, 2 * ndt),
        in_specs=[pl.BlockSpec((na, bg, td), lambda b, j: (0, b, j)),
                  pl.BlockSpec((bg, 2 * na, na), lambda b, j: (b, 0, 0))],
        out_specs=pl.BlockSpec((bg, 2 * na, td), lambda b, j: (b, 0, j)),
        out_shape=jax.ShapeDtypeStruct((bf, 2 * na, 2 * hyw), F32),
        compiler_params=_cparams(2),
        name="hyena_fft_kern_s1",
    )(kern.reshape(na, bf, 2 * hyw), m1k)
    hspec = pl.pallas_call(
        _fft_spec_kernel,
        grid=(na // bg, 2 * ndt),
        in_specs=[pl.BlockSpec((bf, bg, td), lambda k, j: (0, k, j)),
                  pl.BlockSpec((bf, bg, td), lambda k, j: (0, na // bg + k, j)),
                  pl.BlockSpec((2 * bf, 2 * bf), lambda k, j: (0, 0))],
        out_specs=pl.BlockSpec((bg, 2 * bf, td), lambda k, j: (k, 0, j)),
        out_shape=jax.ShapeDtypeStruct((na, 2 * bf, 2 * hyw), BF),
        compiler_params=_cparams(2),
        name="hyena_fft_kern_s2",
    )(ph, ph, f2)

    skip3 = hy_skip.reshape(2, 1, hyw)
    u5 = u_seq.reshape(nb, 2, a_n, bf, 3 * hyw)

    def conv(z5, z_sec, conv_idx, mul_sec):
        zspec = pl.BlockSpec((None, 2, a_n, bg, td), lambda g, b, j: (g, 0, 0, b, z_sec * ndt + j))
        p = pl.pallas_call(
            _fft_s1_kernel,
            grid=(nb, bf // bg, ndt),
            in_specs=[zspec, pl.BlockSpec((bg, 2 * na, ra), lambda g, b, j: (b, 0, 0))],
            out_specs=pl.BlockSpec((None, bg, 2 * na, td), lambda g, b, j: (g, b, 0, j)),
            out_shape=jax.ShapeDtypeStruct((nb, bf, 2 * na, hyw), F32),
            compiler_params=_cparams(3),
            name="hyena_fft_s1",
        )(z5, m1)
        q = pl.pallas_call(
            _fft_mid_kernel,
            grid=(na // bg, ndt, nb),
            in_specs=[pl.BlockSpec((None, bf, bg, td), lambda k, j, g: (g, 0, k, j)),
                      pl.BlockSpec((None, bf, bg, td), lambda k, j, g: (g, 0, na // bg + k, j)),
                      pl.BlockSpec((bg, 2 * bf, td), lambda k, j, g: (k, 0, conv_idx * ndt + j)),
                      pl.BlockSpec((2 * bf, 2 * bf), lambda k, j, g: (0, 0)),
                      pl.BlockSpec((2 * bf, 2 * bf), lambda k, j, g: (0, 0))],
            out_specs=pl.BlockSpec((None, bg, 2 * bf, td), lambda k, j, g: (g, k, 0, j)),
            out_shape=jax.ShapeDtypeStruct((nb, na, 2 * bf, hyw), F32),
            compiler_params=_cparams(3),
            name="hyena_fft_mid",
        )(p, p, hspec, f2, i1)
        return pl.pallas_call(
            _fft_s3_kernel,
            grid=(nb, bf // bg, ndt),
            in_specs=[pl.BlockSpec((None, na, bg, td), lambda g, b, j: (g, 0, b, j)),
                      pl.BlockSpec((None, na, bg, td), lambda g, b, j: (g, 0, bf // bg + b, j)),
                      pl.BlockSpec((bg, ra, 2 * na), lambda g, b, j: (b, 0, 0)),
                      zspec,
                      pl.BlockSpec((None, 1, td), lambda g, b, j: (conv_idx, 0, j)),
                      pl.BlockSpec((None, 2, a_n, bg, td), lambda g, b, j: (g, 0, 0, b, mul_sec * ndt + j))],
            out_specs=pl.BlockSpec((None, 2, a_n, bg, td), lambda g, b, j: (g, 0, 0, b, j)),
            out_shape=jax.ShapeDtypeStruct((nb, 2, a_n, bf, hyw), F32),
            compiler_params=_cparams(3),
            name="hyena_fft_s3",
        )(q, q, g2, z5, skip3, u5)

    z_mid = conv(u5, 0, 0, 1)
    y = conv(z_mid, 0, 1, 2)
    return y.reshape(2 * nb, ls, hyw)[:nb_in]


def _moe_up_kernel(x_ref, wg_ref, wu_ref, comb_ref, o_ref, *, es):
    x = x_ref[...]
    g = jnp.dot(x, wg_ref[...], preferred_element_type=F32)
    u = jnp.dot(x, wu_ref[...], preferred_element_type=F32)
    comb = comb_ref[...]
    lane = lax.broadcasted_iota(jnp.int32, comb.shape, 1)
    f = g.shape[1] // es
    col = lax.broadcasted_iota(jnp.int32, (1, g.shape[1]), 1)
    scale = jnp.zeros_like(g)
    for s in range(es):
        e = pl.program_id(0) * es + s
        sc = jnp.sum(jnp.where(lane == e, comb, 0.0), axis=1, keepdims=True)
        scale = jnp.where((col >= s * f) & (col < (s + 1) * f), sc, scale)
    o_ref[...] = (g * jax.nn.sigmoid(g) * u * scale).astype(o_ref.dtype)


MOE_EXPERTS_PER_STEP = 4


def _moe_group_weights(w):
    nl, ne, d, f = w.shape
    es = MOE_EXPERTS_PER_STEP
    w = w.reshape(nl, ne // es, es, d, f).transpose(0, 1, 3, 2, 4)
    return w.reshape(nl, ne // es, d, es * f).astype(BF)


def _moe_up(u, w_gate, w_up, comb, l):
    m, d = u.shape
    _, ng, _, nf = w_gate.shape
    tm = _pick(m, (544, 512, 256))
    wspec = pl.BlockSpec((None, None, d, nf), lambda e, i: (l, e, 0, 0))
    return pl.pallas_call(
        functools.partial(_moe_up_kernel, es=MOE_EXPERTS_PER_STEP),
        grid=(ng, m // tm),
        in_specs=[pl.BlockSpec((tm, d), lambda e, i: (i, 0)), wspec, wspec,
                  pl.BlockSpec((tm, LANES), lambda e, i: (i, 0))],
        out_specs=pl.BlockSpec((tm, nf), lambda e, i: (i, e)),
        out_shape=jax.ShapeDtypeStruct((m, ng * nf), BF),
        compiler_params=_cparams(2),
        name="moe_up",
    )(u, w_gate, w_up, comb)


def _row_gate(gt_ref, tm, t_len, ctx_len, n_batch):
    r = lax.broadcasted_iota(jnp.int32, (tm, 1), 0) + pl.program_id(2) * tm
    gate = jnp.zeros((tm, gt_ref.shape[1]), F32)
    for bi in range(n_batch):
        in_b = (r >= bi * t_len) & (r < (bi + 1) * t_len)
        is_c = r < bi * t_len + ctx_len
        gate = jnp.where(in_b & is_c, gt_ref[0:1, :], gate)
        gate = jnp.where(in_b & ~is_c, gt_ref[1 + bi:2 + bi, :], gate)
    return gate


def _final_norm_kernel(h_ref, g_ref, o_ref):
    xf = h_ref[...]
    o_ref[...] = xf * lax.rsqrt(jnp.mean(xf * xf, axis=-1, keepdims=True) + NORM_EPS) * g_ref[...]


def _final_norm(h, gain, n_batch, t_len, ctx_len):
    m, d = h.shape
    tt = TOKEN_TILE
    n_t, n_c = t_len // tt, ctx_len // tt
    n_x = n_t - n_c
    return pl.pallas_call(
        _final_norm_kernel,
        grid=(n_batch, n_x),
        in_specs=[pl.BlockSpec((tt, d), lambda bi, i: (bi * n_t + n_c + i, 0)),
                  pl.BlockSpec((1, d), lambda bi, i: (0, 0))],
        out_specs=pl.BlockSpec((None, tt, d), lambda bi, i: (bi, i, 0)),
        out_shape=jax.ShapeDtypeStruct((n_batch, t_len - ctx_len, d), F32),
        compiler_params=_cparams(2),
        name="final_norm",
    )(h, gain.reshape(1, d))


def kernel(x, c, ctx, c_ctx, ada_down, ada_up, ada_bias, norm_mix, norm_ffn, w_in, qkv_conv, a_log, dt_bias, o_norm, hy_conv, hy_w1, hy_b1, hy_f1, hy_w2, hy_b2, hy_f2, hy_w3, hy_decay, hy_skip, w_br_a, w_br_b, w_out, router_group, router_group_bias, router_expert, router_expert_bias, exp_gate, exp_up, exp_down, final_norm):
    n_batch, seq, d = x.shape
    ctx_len = ctx.shape[1]
    t_len = ctx_len + seq
    m = n_batch * t_len
    depth = w_in.shape[0]
    n_heads = a_log.shape[2]
    a_width = n_heads * LANES
    hyw = hy_skip.shape[2]
    n_exp = exp_gate.shape[1]
    dg_off = 4 * a_width
    hy_off = dg_off + 4 * n_heads
    tail_w = 3 * hyw + 2 * d
    n_t, n_c = t_len // TOKEN_TILE, ctx_len // TOKEN_TILE
    nc = t_len // CHUNK
    ncp = -(-nc // 8) * 8

    h = jnp.concatenate([ctx, x], axis=1).reshape(m, d)

    conds = jnp.zeros((8, d), F32).at[0].set(c_ctx).at[1:1 + n_batch].set(c)
    mod_all = _adaln_all(conds, ada_down, ada_up, ada_bias)
    mod_all = mod_all.reshape(depth, 8, 6, d)[:, :1 + n_batch]
    mod_all = jnp.pad(mod_all, ((0, 0), (0, 0), (0, 2), (0, 0)))

    tm = _pick(m, (1088, 512, 256))
    tn = 512

    w_in_b = w_in.astype(BF)
    tail_base = (hy_off // LANES) * LANES
    tnp = 1024 if (dg_off % 1024 == 0 and tail_w % 1024 == 0 and tail_base % 1024 == 0) else tn
    exp_gate_b = _moe_group_weights(exp_gate)
    exp_up_b = _moe_group_weights(exp_up)

    def gate_table(mod_l, idx):
        return jnp.pad(mod_l[:, idx, :], ((0, 8 - (1 + n_batch)), (0, 0)))

    def residual_ep(acc, extras, outs):
        outs[0][...] = extras[0][...] + _row_gate(extras[1], acc.shape[0], t_len, ctx_len, n_batch) * acc

    for l in range(depth):
        mod_l = mod_all[l]
        u = _norm_mod(h, norm_mix[l], mod_l, 0, n_t, n_c)
        (p1,) = _matmul(
            u, w_in_b,
            pl.BlockSpec((tm, d), lambda g, j, i: (i, 0)),
            pl.BlockSpec((None, d, tnp), lambda g, j, i, l=l: (l, 0, j)),
            (1, dg_off // tnp, m // tm),
            [(jax.ShapeDtypeStruct((m, dg_off), BF), pl.BlockSpec((tm, tnp), lambda g, j, i: (i, j)))],
            _ep_store(BF), name="proj_qkvz")
        tmt = _pick(m, (544, 512, 256))
        (p_tail,) = _matmul(
            u, w_in_b,
            pl.BlockSpec((tmt, d), lambda g, j, i: (i, 0)),
            pl.BlockSpec((None, d, tnp), lambda g, j, i, l=l: (l, 0, tail_base // tnp + j)),
            (1, tail_w // tnp, m // tmt),
            [(jax.ShapeDtypeStruct((m, tail_w), BF), pl.BlockSpec((tmt, tnp), lambda g, j, i: (i, j)))],
            _ep_store(BF), name="proj_tail",
            w_next_spec=pl.BlockSpec((None, d, LANES),
                                     lambda g, j, i, l=l: (l, 0, (tail_base + (j + 1) * tnp) // LANES)),
            lane_shift=hy_off - tail_base)
        w_dg_t = w_in[l, :, dg_off:hy_off].T
        a_exp = jnp.concatenate([jnp.exp(a_log[l].reshape(-1)), jnp.zeros((2 * n_heads,), F32)])[:, None]
        dt_b = jnp.concatenate([dt_bias[l].reshape(-1), jnp.zeros((2 * n_heads,), F32)])[:, None]
        gates = _gates(u, w_dg_t, a_exp, dt_b)

        g5 = gates.reshape(4, n_heads, n_batch, nc, CHUNK).transpose(2, 1, 0, 3, 4)
        g_rows = jnp.pad(g5, ((0, 0), (0, 0), (0, 0), (0, ncp - nc), (0, 0)))
        g_cols = jnp.pad(jnp.swapaxes(g5, 3, 4), ((0, 0), (0, 0), (0, 0), (0, 0), (0, LANES - nc)))

        p1_3 = p1.reshape(n_batch, t_len, dg_off)
        conv9 = qkv_conv[l].reshape(9, 3 * a_width)
        du, dw, dq, dk, da, dgl = _delta_prep(p1_3, conv9, g_rows, g_cols, n_heads, ctx_len)
        o_f, o_b = _delta_scan(du, dw, dq, dk, da, dgl, ctx_len)
        o_a = _delta_out(o_f, o_b, p1_3, o_norm[l], n_heads).reshape(m, a_width)

        p_tail3 = p_tail.reshape(n_batch, t_len, tail_w)
        u_c, u_x = _hy_conv(p_tail3, hy_conv[l], ctx_len, 3 * hyw)
        filt_args = (hy_w1[l], hy_b1[l], hy_f1[l], hy_w2[l], hy_b2[l], hy_f2[l], hy_w3[l], hy_decay[l])
        y_x = _hy_long_convs_fft(u_x, _hy_filter_stack(seq, *filt_args, hyw, True), hy_skip[l], hyw)
        if l < depth - 1:
            y_c = _hy_long_convs(u_c, _hy_filter_stack(ctx_len, *filt_args, hyw, False), hy_skip[l], hyw)
        else:
            y_c = jnp.zeros((n_batch, ctx_len, hyw), BF)
        y_b = jnp.concatenate([y_c, y_x.astype(BF)], axis=1).reshape(m, hyw)

        ga_blk = (3 * hyw) // tn
        gb_blk = (3 * hyw + d) // tn

        merged = pl.pallas_call(
            _merge_kernel,
            grid=(d // tn, m // tm),
            in_specs=[pl.BlockSpec((tm, a_width), lambda j, i: (i, 0)),
                      pl.BlockSpec((tm, hyw), lambda j, i: (i, 0)),
                      pl.BlockSpec((None, a_width, tn), lambda j, i, l=l: (l, 0, j)),
                      pl.BlockSpec((None, hyw, tn), lambda j, i, l=l: (l, 0, j)),
                      pl.BlockSpec((tm, tn), lambda j, i: (i, ga_blk + j)),
                      pl.BlockSpec((tm, tn), lambda j, i: (i, gb_blk + j))],
            out_specs=pl.BlockSpec((tm, tn), lambda j, i: (i, j)),
            out_shape=jax.ShapeDtypeStruct((m, d), BF),
            scratch_shapes=[pltpu.VMEM((a_width, tn), BF), pltpu.VMEM((hyw, tn), BF)],
            compiler_params=_cparams(2),
            name="branch_merge",
        )(o_a, y_b, w_br_a, w_br_b, p_tail, p_tail)

        (h,) = _matmul(
            merged, w_out,
            pl.BlockSpec((tm, d), lambda g, j, i: (i, 0)),
            pl.BlockSpec((None, d, tn), lambda g, j, i, l=l: (l, 0, j)),
            (1, d // tn, m // tm),
            [(jax.ShapeDtypeStruct((m, d), F32), pl.BlockSpec((tm, tn), lambda g, j, i: (i, j)))],
            residual_ep,
            extras=[(h, pl.BlockSpec((tm, tn), lambda g, j, i: (i, j))),
                    (gate_table(mod_l, 2), pl.BlockSpec((8, tn), lambda g, j, i: (0, j)))],
            name="out_proj")

        w_router = jnp.pad(jnp.concatenate([router_expert[l], router_group[l]], axis=1),
                           ((0, 0), (0, LANES - n_exp - N_GROUPS)))
        b_router = jnp.pad(jnp.concatenate([router_expert_bias[l], router_group_bias[l]]),
                           (0, LANES - n_exp - N_GROUPS)).reshape(1, LANES)
        u2, comb = _norm_mod_router(h, norm_ffn[l], mod_l, 3, n_t, n_c, w_router, b_router, n_exp, N_GROUPS)
        hid = _moe_up(u2, exp_gate_b, exp_up_b, comb, l)
        kf = hid.shape[1]
        tnd = _pick(d, (512, 256, 128))
        tmd = _pick(m, (544, 512, 256))
        (h,) = _matmul(
            hid, exp_down.reshape(depth, kf, d),
            pl.BlockSpec((tmd, kf), lambda g, j, i: (i, 0)),
            pl.BlockSpec((None, kf, tnd), lambda g, j, i, l=l: (l, 0, j)),
            (1, d // tnd, m // tmd),
            [(jax.ShapeDtypeStruct((m, d), F32), pl.BlockSpec((tmd, tnd), lambda g, j, i: (i, j)))],
            residual_ep,
            extras=[(h, pl.BlockSpec((tmd, tnd), lambda g, j, i: (i, j))),
                    (gate_table(mod_l, 5), pl.BlockSpec((8, tnd), lambda g, j, i: (0, j)))],
            name="moe_down")

    return _final_norm(h, final_norm, n_batch, t_len, ctx_len)
```
